```python
import math
import jax, jax.numpy as jnp
from jax import lax
import numpy as np

D_MODEL = 1024
BATCH = 2
SEQ = 8192
DEPTH = 4

N_MIXERS = 3
N_S5_LAYERS = (DEPTH + 2) // 3
N_HG_LAYERS = (DEPTH + 1) // 3
N_AT_LAYERS = DEPTH // 3

S5_WIDTH = D_MODEL
S5_GROUP = 16
S5_GROUPS = S5_WIDTH // S5_GROUP
S5_STATE = 64
DT_MIN = 1e-3
DT_MAX = 1e-1

HG_WIDTH = D_MODEL
HG_HEAD_DIM = 128
HG_HEADS = HG_WIDTH // HG_HEAD_DIM
HG_CHUNK = 64

AT_HEAD_DIM = 64
AT_Q_HEADS = D_MODEL // AT_HEAD_DIM
AT_KV_HEADS = 4
AT_GROUP = AT_Q_HEADS // AT_KV_HEADS
AT_Q_WIDTH = AT_Q_HEADS * AT_HEAD_DIM
AT_KV_WIDTH = AT_KV_HEADS * AT_HEAD_DIM
AT_QKV_WIDTH = AT_Q_WIDTH + 2 * AT_KV_WIDTH
WINDOW = 128
BLOCK = 128
ROPE_THETA = 10000.0

NORM_EPS = 1e-6

kernel_name = 'hybrid_s5_hgrn2_swa_trunk'


def rms_norm(x, g):
    xf = x.astype(jnp.float32)
    y = xf * lax.rsqrt(jnp.mean(xf * xf, axis=-1, keepdims=True) + NORM_EPS)
    return (y * g.astype(jnp.float32)).astype(x.dtype)


def _complex_scan_combine(e1, e2):
    a1r, a1i, b1r, b1i = e1
    a2r, a2i, b2r, b2i = e2
    return (a2r * a1r - a2i * a1i,
            a2r * a1i + a2i * a1r,
            a2r * b1r - a2i * b1i + b2r,
            a2r * b1i + a2i * b1r + b2i)


def s5_branch(h, w_in, lam_re, lam_im, log_dt, b_re, b_im, c_re, c_im, d_skip, w_glu, b_glu, w_out):
    bsz, seq, _ = h.shape
    proj = h @ w_in
    u, z = proj[..., :S5_WIDTH], proj[..., S5_WIDTH:]
    uf = u.astype(jnp.float32).reshape(bsz, seq, S5_GROUPS, S5_GROUP)
    lr = lam_re.astype(jnp.float32)
    li = lam_im.astype(jnp.float32)
    dt = jnp.exp(log_dt.astype(jnp.float32))[:, None]
    mag = jnp.exp(lr * dt)
    ar = mag * jnp.cos(li * dt)
    ai = mag * jnp.sin(li * dt)
    den = lr * lr + li * li
    qr = ((ar - 1.0) * lr + ai * li) / den
    qi = (ai * lr - (ar - 1.0) * li) / den
    br = b_re.astype(jnp.float32)
    bi = b_im.astype(jnp.float32)
    bbr = qr[..., None] * br - qi[..., None] * bi
    bbi = qr[..., None] * bi + qi[..., None] * br
    xr = jnp.einsum('blgh,gph->blgp', uf, bbr)
    xi = jnp.einsum('blgh,gph->blgp', uf, bbi)
    a_r = jnp.broadcast_to(ar, (1, seq) + ar.shape)
    a_i = jnp.broadcast_to(ai, (1, seq) + ai.shape)
    _, _, sr, si = lax.associative_scan(_complex_scan_combine, (a_r, a_i, xr, xi), axis=1)
    y = (jnp.einsum('ghp,blgp->blgh', c_re.astype(jnp.float32), sr)
         - jnp.einsum('ghp,blgp->blgh', c_im.astype(jnp.float32), si))
    y = y.reshape(bsz, seq, S5_WIDTH) + d_skip.astype(jnp.float32) * u.astype(jnp.float32)
    y = jax.nn.gelu(y).astype(h.dtype)
    y = y * jax.nn.sigmoid(y @ w_glu + b_glu)
    return (y * jax.nn.silu(z)) @ w_out


def hgrn2_lower_bounds(lb_logits):
    p = jax.nn.softmax(lb_logits.astype(jnp.float32), axis=0)
    return jnp.cumsum(p, axis=0) - p[0]


def _to_chunks(t):
    bsz, seq, nh, dh = t.shape
    return t.reshape(bsz, seq // HG_CHUNK, HG_CHUNK, nh, dh).transpose(1, 0, 3, 2, 4)


def _hgrn2_chunk_step(state, inp):
    q, k, v, logf = inp
    b = jnp.cumsum(logf, axis=2)
    o_inter = jnp.einsum('bhtk,bhkv->bhtv', q * jnp.exp(b), state)
    causal = jnp.tril(jnp.ones((HG_CHUNK, HG_CHUNK), dtype=bool))
    diff = b[:, :, :, None, :] - b[:, :, None, :, :]
    decay = jnp.exp(jnp.where(causal[:, :, None], diff, -jnp.inf))
    scores = jnp.einsum('bhtk,bhtsk,bhsk->bhts', q, decay, k)
    o_intra = jnp.einsum('bhts,bhsv->bhtv', scores, v)
    b_last = b[:, :, -1:, :]
    new_state = (jnp.exp(b_last[:, :, 0, :])[..., None] * state
                 + jnp.einsum('bhsk,bhsv->bhkv', k * jnp.exp(b_last - b), v))
    return new_state, o_inter + o_intra


def hgrn2_branch(h, w_in, lb, norm_g, w_out):
    bsz, seq, _ = h.shape
    proj = h @ w_in
    q, fz, i_in, z = jnp.split(proj, 4, axis=-1)
    f = lb + (1.0 - lb) * jax.nn.sigmoid(fz.astype(jnp.float32))
    logf = jnp.log(f)
    k = 1.0 - f
    heads = lambda t: _to_chunks(t.astype(jnp.float32).reshape(bsz, seq, HG_HEADS, HG_HEAD_DIM))
    s0 = jnp.zeros((bsz, HG_HEADS, HG_HEAD_DIM, HG_HEAD_DIM), jnp.float32)
    _, o = lax.scan(_hgrn2_chunk_step, s0, (heads(q), heads(k), heads(i_in), heads(logf)))
    o = o.transpose(1, 0, 3, 2, 4).reshape(bsz, seq, HG_HEADS, HG_HEAD_DIM)
    o = rms_norm(o, norm_g.reshape(HG_HEADS, HG_HEAD_DIM)).reshape(bsz, seq, HG_WIDTH).astype(h.dtype)
    return (o * jax.nn.silu(z)) @ w_out


def _rope(t, cos, sin):
    half = AT_HEAD_DIM // 2
    t1 = t[..., :half].astype(jnp.float32)
    t2 = t[..., half:].astype(jnp.float32)
    return jnp.concatenate([t1 * cos - t2 * sin, t2 * cos + t1 * sin], axis=-1).astype(t.dtype)


def swa_branch(h, positions, w_in, b_in, sinks, w_out):
    bsz, seq, _ = h.shape
    nb = seq // BLOCK
    proj = h @ w_in
    qkv = proj[..., :AT_QKV_WIDTH] + b_in
    z = proj[..., AT_QKV_WIDTH:]
    q = qkv[..., :AT_Q_WIDTH].reshape(bsz, seq, AT_Q_HEADS, AT_HEAD_DIM)
    k = qkv[..., AT_Q_WIDTH:AT_Q_WIDTH + AT_KV_WIDTH].reshape(bsz, seq, AT_KV_HEADS, AT_HEAD_DIM)
    v = qkv[..., AT_Q_WIDTH + AT_KV_WIDTH:].reshape(bsz, seq, AT_KV_HEADS, AT_HEAD_DIM)
    inv_freq = ROPE_THETA ** (-jnp.arange(0, AT_HEAD_DIM, 2, dtype=jnp.float32) / AT_HEAD_DIM)
    ang = positions.astype(jnp.float32)[..., None] * inv_freq
    cos = jnp.cos(ang)[:, :, None, :]
    sin = jnp.sin(ang)[:, :, None, :]
    q = _rope(q, cos, sin)
    k = _rope(k, cos, sin)
    qb = q.reshape(bsz, nb, BLOCK, AT_KV_HEADS, AT_GROUP, AT_HEAD_DIM)

    def band(t):
        t = t.reshape(bsz, nb, BLOCK, AT_KV_HEADS, AT_HEAD_DIM)
        prev = jnp.pad(t[:, :-1], ((0, 0), (1, 0), (0, 0), (0, 0), (0, 0)))
        return jnp.concatenate([prev, t], axis=2)

    kb, vb = band(k), band(v)
    s = jnp.einsum('bnqhgd,bnkhd->bnhgqk', qb, kb).astype(jnp.float32) * (AT_HEAD_DIM ** -0.5)
    qi = jnp.arange(BLOCK)[:, None]
    kj = jnp.arange(2 * BLOCK)[None, :]
    dist = qi + BLOCK - kj
    in_window = (dist >= 0) & (dist < WINDOW)
    has_prev = (jnp.arange(nb)[:, None, None] > 0) | (kj >= BLOCK)[None]
    mask = in_window[None] & has_prev
    s = jnp.where(mask[None, :, None, None], s, -jnp.inf)
    sink = sinks.astype(jnp.float32).reshape(AT_KV_HEADS, AT_GROUP)[None, None, :, :, None, None]
    m = jnp.maximum(jnp.max(s, axis=-1, keepdims=True), sink)
    e = jnp.exp(s - m)
    p = e / (jnp.sum(e, axis=-1, keepdims=True) + jnp.exp(sink - m))
    o = jnp.einsum('bnhgqk,bnkhd->bnqhgd', p.astype(vb.dtype), vb).reshape(bsz, seq, AT_Q_WIDTH)
    return (o * jax.nn.silu(z)) @ w_out


def setup_inputs(seed: int = 0) -> dict:
    key = jax.random.key(seed)
    ks = iter(jax.random.split(key, 32))

    def nrm(shape, scale):
        return jax.random.normal(next(ks), shape, jnp.float32) * scale

    x = nrm((BATCH, SEQ, D_MODEL), 1.0)
    offset = jax.random.randint(next(ks), (BATCH, 1), 0, 4096, dtype=jnp.int32)
    positions = (offset + jnp.arange(SEQ, dtype=jnp.int32)[None, :]).astype(jnp.int32)
    norm_pre = 1.0 + nrm((DEPTH, D_MODEL), 0.02)
    norm_post = 1.0 + nrm((DEPTH, D_MODEL), 0.02)

    nA = N_S5_LAYERS
    s5_w_in = nrm((nA, D_MODEL, 2 * S5_WIDTH), D_MODEL ** -0.5)
    s5_lambda_re = -0.5 + nrm((nA, S5_GROUPS, S5_STATE), 0.01)
    s5_lambda_im = (math.pi * jnp.arange(S5_STATE, dtype=jnp.float32))[None, None, :] + nrm((nA, S5_GROUPS, S5_STATE), 0.01)
    s5_log_dt = jax.random.uniform(next(ks), (nA, S5_GROUPS), jnp.float32, math.log(DT_MIN), math.log(DT_MAX))
    s5_b_re = nrm((nA, S5_GROUPS, S5_STATE, S5_GROUP), (2.0 * S5_GROUP) ** -0.5)
    s5_b_im = nrm((nA, S5_GROUPS, S5_STATE, S5_GROUP), (2.0 * S5_GROUP) ** -0.5)
    s5_c_re = nrm((nA, S5_GROUPS, S5_GROUP, S5_STATE), (2.0 * S5_STATE) ** -0.5)
    s5_c_im = nrm((nA, S5_GROUPS, S5_GROUP, S5_STATE), (2.0 * S5_STATE) ** -0.5)
    s5_d = nrm((nA, S5_WIDTH), 1.0)
    s5_w_glu = nrm((nA, S5_WIDTH, S5_WIDTH), S5_WIDTH ** -0.5)
    s5_b_glu = nrm((nA, S5_WIDTH), 0.02)
    s5_w_out = nrm((nA, S5_WIDTH, D_MODEL), S5_WIDTH ** -0.5)

    nB = N_HG_LAYERS
    hg_w_in = nrm((nB, D_MODEL, 4 * HG_WIDTH), D_MODEL ** -0.5)
    hg_lb_logits = nrm((DEPTH, HG_WIDTH), 0.1)
    hg_norm = 1.0 + nrm((nB, HG_WIDTH), 0.02)
    hg_w_out = nrm((nB, HG_WIDTH, D_MODEL), HG_WIDTH ** -0.5)

    nC = N_AT_LAYERS
    at_w_in = nrm((nC, D_MODEL, AT_QKV_WIDTH + AT_Q_WIDTH), D_MODEL ** -0.5)
    at_b_in = nrm((nC, AT_QKV_WIDTH), 0.02)
    at_sinks = nrm((nC, AT_Q_HEADS), 0.5)
    at_w_out = nrm((nC, AT_Q_WIDTH, D_MODEL), AT_Q_WIDTH ** -0.5)

    return {'x': x, 'positions': positions, 'norm_pre': norm_pre, 'norm_post': norm_post,
            's5_w_in': s5_w_in, 's5_lambda_re': s5_lambda_re, 's5_lambda_im': s5_lambda_im,
            's5_log_dt': s5_log_dt, 's5_b_re': s5_b_re, 's5_b_im': s5_b_im,
            's5_c_re': s5_c_re, 's5_c_im': s5_c_im, 's5_d': s5_d,
            's5_w_glu': s5_w_glu, 's5_b_glu': s5_b_glu, 's5_w_out': s5_w_out,
            'hg_w_in': hg_w_in, 'hg_lb_logits': hg_lb_logits, 'hg_norm': hg_norm, 'hg_w_out': hg_w_out,
            'at_w_in': at_w_in, 'at_b_in': at_b_in, 'at_sinks': at_sinks, 'at_w_out': at_w_out}


def reference(x, positions, norm_pre, norm_post,
              s5_w_in, s5_lambda_re, s5_lambda_im, s5_log_dt, s5_b_re, s5_b_im,
              s5_c_re, s5_c_im, s5_d, s5_w_glu, s5_b_glu, s5_w_out,
              hg_w_in, hg_lb_logits, hg_norm, hg_w_out,
              at_w_in, at_b_in, at_sinks, at_w_out):
    lower_bounds = hgrn2_lower_bounds(hg_lb_logits)
    h = x
    for i in range(DEPTH):
        kind, j = i % N_MIXERS, i // N_MIXERS
        u = rms_norm(h, norm_pre[i])
        if kind == 0:
            y = s5_branch(u, s5_w_in[j], s5_lambda_re[j], s5_lambda_im[j], s5_log_dt[j],
                          s5_b_re[j], s5_b_im[j], s5_c_re[j], s5_c_im[j], s5_d[j],
                          s5_w_glu[j], s5_b_glu[j], s5_w_out[j])
        elif kind == 1:
            y = hgrn2_branch(u, hg_w_in[j], lower_bounds[i], hg_norm[j], hg_w_out[j])
        else:
            y = swa_branch(u, positions, at_w_in[j], at_b_in[j], at_sinks[j], at_w_out[j])
        h = h + rms_norm(y, norm_post[i])
    return h
```

```python
import functools
import math

import jax
import jax.numpy as jnp
from jax import lax
from jax.experimental import pallas as pl
from jax.experimental.pallas import tpu as pltpu

F32 = jnp.float32
BF16 = jnp.bfloat16

NORM_EPS = 1e-6
LANES = 128
SUBLANES = 8
VMEM_LIMIT = 56 * 1024 * 1024

S5_GROUP = 16
S5_STATE = 64
S5_TILE_GROUPS = 16
S5_TILE_CH = S5_TILE_GROUPS * S5_GROUP
S5_TILE_ST = S5_TILE_GROUPS * S5_STATE
S5_STEPS = 64
HG_HEAD = 128
HG_CHUNK = 64
HG_SUB = 16
HG_BLOCK = 512
AT_HEAD = 64
AT_KV_HEADS = 4
AT_BLOCK = 128
ROPE_THETA = 10000.0

ROW_TILE = 512


def _cparams(sem):
    return pltpu.CompilerParams(dimension_semantics=sem, vmem_limit_bytes=VMEM_LIMIT)


def _rms(x, g):
    return x * lax.rsqrt(jnp.mean(x * x, axis=-1, keepdims=True) + NORM_EPS) * g


def _sigmoid(x):
    return 1.0 / (1.0 + jnp.exp(-x))


def _silu(x):
    return x * _sigmoid(x)


def _gelu_tanh(x):
    c = math.sqrt(2.0 / math.pi)
    return 0.5 * x * (1.0 + jnp.tanh(c * (x + 0.044715 * (x * x * x))))


def _dot(a, b):
    return jnp.dot(a, b, preferred_element_type=F32)


def _dot_nt(a, b):
    return lax.dot_general(a, b, (((1,), (1,)), ((), ())), preferred_element_type=F32)


def _dot_tn(a, b):
    return lax.dot_general(a, b, (((0,), (0,)), ((), ())), preferred_element_type=F32)


def _norm_proj_kernel(h_ref, g_ref, w_ref, *out_refs, widths, slab):
    y = _rms(h_ref[0], g_ref[...]).astype(BF16)
    off = 0
    for o_ref, width, as_slab in zip(out_refs, widths, slab):
        r = _dot(y, w_ref[:, off:off + width])
        if as_slab:
            for j in range(width // LANES):
                o_ref[0, j] = r[:, j * LANES:(j + 1) * LANES]
        else:
            o_ref[0] = r
        off += width


def _norm_proj(h, g, w_bf16, widths, slab):
    bsz, seq, d = h.shape
    tm = min(ROW_TILE, seq)
    out_shape, out_specs = [], []
    for width, as_slab in zip(widths, slab):
        if as_slab:
            ns = width // LANES
            out_shape.append(jax.ShapeDtypeStruct((bsz, ns, seq, LANES), F32))
            out_specs.append(pl.BlockSpec((1, ns, tm, LANES), lambda b, m: (b, 0, m, 0)))
        else:
            out_shape.append(jax.ShapeDtypeStruct((bsz, seq, width), F32))
            out_specs.append(pl.BlockSpec((1, tm, width), lambda b, m: (b, m, 0)))
    return pl.pallas_call(
        functools.partial(_norm_proj_kernel, widths=tuple(widths), slab=tuple(slab)),
        out_shape=out_shape,
        grid=(bsz, seq // tm),
        in_specs=[pl.BlockSpec((1, tm, d), lambda b, m: (b, m, 0)),
                  pl.BlockSpec((1, d), lambda b, m: (0, 0)),
                  pl.BlockSpec(w_bf16.shape, lambda b, m: (0, 0))],
        out_specs=out_specs,
        compiler_params=_cparams(("parallel", "parallel")),
        name="norm_proj",
    )(h, g.reshape(1, d), w_bf16)


def _s5_prep_kernel(lr_ref, li_ref, ldt_ref, br_ref, bi_ref, ar_ref, ai_ref, bbr_ref, bbi_ref):
    lr = lr_ref[...]
    li = li_ref[...]
    dt = jnp.exp(ldt_ref[...])
    mag = jnp.exp(lr * dt)
    ar = mag * jnp.cos(li * dt)
    ai = mag * jnp.sin(li * dt)
    den = lr * lr + li * li
    qr = ((ar - 1.0) * lr + ai * li) / den
    qi = (ai * lr - (ar - 1.0) * li) / den
    br = br_ref[...]
    bi = bi_ref[...]
    ar_ref[...] = ar
    ai_ref[...] = ai
    bbr_ref[...] = qr * br - qi * bi
    bbi_ref[...] = qr * bi + qi * br


def _s5_prep(lam_re, lam_im, log_dt, b_re, b_im):
    g, p, hh = b_re.shape
    rep = lambda t: jnp.repeat(t, hh, axis=1)
    shp = jax.ShapeDtypeStruct((g, p * hh), F32)
    ar, ai, bbr, bbi = pl.pallas_call(
        _s5_prep_kernel, out_shape=[shp, shp, shp, shp], name="s5_prep",
    )(rep(lam_re), rep(lam_im), log_dt.reshape(g, 1), b_re.reshape(g, p * hh), b_im.reshape(g, p * hh))
    return ar[:, ::hh], ai[:, ::hh], bbr.reshape(g, p, hh), bbi.reshape(g, p, hh)


def _s5_core_kernel(u_ref, wb_ref, wc_ref, a_ref, y_ref, lhs_ref, x_ref, s_ref, carry_ref, *, steps):
    nst = S5_TILE_ST
    rows = SUBLANES * steps

    @pl.when(pl.program_id(2) == 0)
    def _():
        carry_ref[...] = jnp.zeros_like(carry_ref)

    for k in range(steps):
        for j in range(S5_TILE_CH // LANES):
            lhs_ref[k * SUBLANES:(k + 1) * SUBLANES, j * LANES:(j + 1) * LANES] = (
                u_ref[0, j, pl.ds(k, SUBLANES, stride=steps), :])

    x_ref[...] = _dot(lhs_ref[...].astype(BF16), wb_ref[0])

    ar1 = a_ref[0, 0:1, :]
    ai1 = a_ref[0, 1:2, :]
    ar = jnp.broadcast_to(ar1, (SUBLANES, nst))
    ai = jnp.broadcast_to(ai1, (SUBLANES, nst))

    def step(k, carry, store):
        sr, si = carry
        row = pl.multiple_of(k * SUBLANES, SUBLANES)
        xr = x_ref[pl.ds(row, SUBLANES), 0:nst]
        xi = x_ref[pl.ds(row, SUBLANES), nst:2 * nst]
        nr = ar * sr - ai * si + xr
        ni = ar * si + ai * sr + xi
        if store:
            s_ref[pl.ds(row, SUBLANES), 0:nst] = nr
            s_ref[pl.ds(row, SUBLANES), nst:2 * nst] = ni
        return nr, ni

    zero = jnp.zeros((SUBLANES, nst), F32)
    er, ei = lax.fori_loop(0, steps, functools.partial(step, store=False), (zero, zero))

    pr, pi = ar1, ai1
    for _ in range(int(math.log2(steps))):
        pr, pi = pr * pr - pi * pi, 2.0 * pr * pi

    rid = lax.broadcasted_iota(jnp.int32, (SUBLANES, nst), 0)
    cr = carry_ref[0:1, :]
    ci = carry_ref[1:2, :]
    init_r = jnp.zeros((SUBLANES, nst), F32)
    init_i = jnp.zeros((SUBLANES, nst), F32)
    for r in range(SUBLANES):
        init_r = jnp.where(rid == r, jnp.broadcast_to(cr, (SUBLANES, nst)), init_r)
        init_i = jnp.where(rid == r, jnp.broadcast_to(ci, (SUBLANES, nst)), init_i)
        e_r = er[r:r + 1, :]
        e_i = ei[r:r + 1, :]
        cr, ci = pr * cr - pi * ci + e_r, pr * ci + pi * cr + e_i
    carry_ref[0:1, :] = cr
    carry_ref[1:2, :] = ci

    lax.fori_loop(0, steps, functools.partial(step, store=True), (init_r, init_i))

    y = _dot(s_ref[...].astype(BF16), wc_ref[0])
    for k in range(steps):
        for j in range(S5_TILE_CH // LANES):
            y_ref[0, j, pl.ds(k, SUBLANES, stride=steps), :] = (
                y[k * SUBLANES:(k + 1) * SUBLANES, j * LANES:(j + 1) * LANES])
    assert rows == lhs_ref.shape[0]


def _s5_core(u_slab, wb, wc, a_tiles, steps):
    bsz, nslab, seq, _ = u_slab.shape
    ntile = wb.shape[0]
    spt = S5_TILE_CH // LANES
    rows = SUBLANES * steps
    return pl.pallas_call(
        functools.partial(_s5_core_kernel, steps=steps),
        out_shape=jax.ShapeDtypeStruct(u_slab.shape, F32),
        grid=(bsz, ntile, seq // rows),
        in_specs=[pl.BlockSpec((1, spt, rows, LANES), lambda b, i, n: (b, i, n, 0)),
                  pl.BlockSpec((1, S5_TILE_CH, 2 * S5_TILE_ST), lambda b, i, n: (i, 0, 0)),
                  pl.BlockSpec((1, 2 * S5_TILE_ST, S5_TILE_CH), lambda b, i, n: (i, 0, 0)),
                  pl.BlockSpec((1, 2, S5_TILE_ST), lambda b, i, n: (i, 0, 0))],
        out_specs=pl.BlockSpec((1, spt, rows, LANES), lambda b, i, n: (b, i, n, 0)),
        scratch_shapes=[pltpu.VMEM((rows, S5_TILE_CH), F32),
                        pltpu.VMEM((rows, 2 * S5_TILE_ST), F32),
                        pltpu.VMEM((rows, 2 * S5_TILE_ST), F32),
                        pltpu.VMEM((2, S5_TILE_ST), F32)],
        compiler_params=_cparams(("parallel", "parallel", "arbitrary")),
        name="s5_core",
    )(u_slab, wb, wc, a_tiles)


def _s5_post_kernel(y_ref, u_ref, z_ref, h_ref, d_ref, wg_ref, bg_ref, wo_ref, gp_ref, o_ref):
    nslab = y_ref.shape[1]
    y = jnp.concatenate([y_ref[0, j] for j in range(nslab)], axis=-1)
    u = jnp.concatenate([u_ref[0, j] for j in range(nslab)], axis=-1)
    y = _gelu_tanh(y + d_ref[...] * u)
    y = y * _sigmoid(_dot(y.astype(BF16), wg_ref[...]) + bg_ref[...])
    t = y * _silu(z_ref[0])
    o = _dot(t.astype(BF16), wo_ref[...])
    o_ref[0] = h_ref[0] + _rms(o, gp_ref[...])


def _s5_post(y_slab, u_slab, z, h, d_skip, w_glu, b_glu, w_out, g_post):
    bsz, seq, d = h.shape
    nslab = y_slab.shape[1]
    tm = min(ROW_TILE, seq)
    row = lambda t: t.reshape(1, -1)
    vec = pl.BlockSpec((1, d), lambda b, m: (0, 0))
    mat = pl.BlockSpec((d, d), lambda b, m: (0, 0))
    slab = pl.BlockSpec((1, nslab, tm, LANES), lambda b, m: (b, 0, m, 0))
    tok = pl.BlockSpec((1, tm, d), lambda b, m: (b, m, 0))
    return pl.pallas_call(
        _s5_post_kernel,
        out_shape=jax.ShapeDtypeStruct(h.shape, F32),
        grid=(bsz, seq // tm),
        in_specs=[slab, slab, tok, tok, vec, mat, vec, mat, vec],
        out_specs=tok,
        compiler_params=_cparams(("parallel", "parallel")),
        name="s5_post",
    )(y_slab, u_slab, z, h, row(d_skip), w_glu.astype(BF16), row(b_glu), w_out.astype(BF16), row(g_post))


def _s5_layer(h, g_pre, g_post, w_in, lam_re, lam_im, log_dt, b_re, b_im, c_re, c_im, d_skip,
              w_glu, b_glu, w_out):
    bsz, seq, d = h.shape
    ngroup = lam_re.shape[0]
    ntile = ngroup // S5_TILE_GROUPS
    ar, ai, bbr, bbi = _s5_prep(lam_re, lam_im, log_dt, b_re, b_im)
    eye = jnp.eye(S5_TILE_GROUPS, dtype=F32)

    def in_tile(bb):
        t = bb.reshape(ntile, S5_TILE_GROUPS, S5_STATE, S5_GROUP)
        return jnp.einsum('igph,gk->ighkp', t, eye).reshape(ntile, S5_TILE_CH, S5_TILE_ST)

    def out_tile(cc):
        t = cc.reshape(ntile, S5_TILE_GROUPS, S5_GROUP, S5_STATE)
        return jnp.einsum('ighp,gk->ikpgh', t, eye).reshape(ntile, S5_TILE_ST, S5_TILE_CH)

    wb = jnp.concatenate([in_tile(bbr), in_tile(bbi)], axis=2).astype(BF16)
    wc = jnp.concatenate([out_tile(c_re), -out_tile(c_im)], axis=1).astype(BF16)
    a_tiles = jnp.stack([ar.reshape(ntile, S5_TILE_ST), ai.reshape(ntile, S5_TILE_ST)], axis=1)

    steps = min(S5_STEPS, seq // SUBLANES)
    u_slab, z = _norm_proj(h, g_pre, w_in.astype(BF16), (d, d), (True, False))
    y_slab = _s5_core(u_slab, wb, wc, a_tiles, steps)
    return _s5_post(y_slab, u_slab, z, h, d_skip, w_glu, b_glu, w_out, g_post)


def _split3(x):
    hi = x.astype(BF16)
    r1 = x - hi.astype(F32)
    mid = r1.astype(BF16)
    lo = (r1 - mid.astype(F32)).astype(BF16)
    return hi, mid, lo


def _hg_core_kernel(q_ref, f_ref, v_ref, lbl_ref, o_ref, st_ref, *, layer, chunk, sub):
    dk = HG_HEAD
    nsub = chunk // sub

    @pl.when(pl.program_id(2) == 0)
    def _():
        st_ref[...] = jnp.zeros_like(st_ref)

    lg = lbl_ref[...]
    ex = jnp.exp(lg - jnp.max(lg, axis=0, keepdims=True))
    p = ex / jnp.sum(ex, axis=0, keepdims=True)
    lb = jnp.zeros((1, dk), F32)
    for j in range(1, layer + 1):
        lb = lb + p[j:j + 1, :]

    ri = lax.broadcasted_iota(jnp.int32, (chunk, chunk), 0)
    ci = lax.broadcasted_iota(jnp.int32, (chunk, chunk), 1)
    tri = jnp.where(ci <= ri, 1.0, 0.0).astype(BF16)
    off_mask = (ci // sub) < (ri // sub)
    diag_mask = ((ci // sub) == (ri // sub)) & (ci <= ri)
    sel_r = lax.broadcasted_iota(jnp.int32, (sub * dk, chunk), 0) // dk
    sel_c = lax.broadcasted_iota(jnp.int32, (sub * dk, chunk), 1) % sub
    sel = jnp.where(sel_r == sel_c, 1.0, 0.0).astype(BF16)

    def chunk_body(c, _):
        row = pl.multiple_of(c * chunk, chunk)
        q = q_ref[0, pl.ds(row, chunk), :]
        v = v_ref[0, pl.ds(row, chunk), :]
        f = lb + (1.0 - lb) * _sigmoid(f_ref[0, pl.ds(row, chunk), :])
        logf = jnp.log(f)
        kk = 1.0 - f
        hi, mid, lo = _split3(logf)
        b = _dot(tri, hi) + _dot(tri, mid) + _dot(tri, lo)
        b_last = b[chunk - 1:chunk, :]
        st = st_ref[...]
        vb = v.astype(BF16)

        o = _dot_nt((q * jnp.exp(b)).astype(BF16), st.astype(BF16))

        blocks = [jnp.zeros((sub, chunk), F32)]
        for i in range(1, nsub):
            bref = b[i * sub - 1:i * sub, :]
            qi = q[i * sub:(i + 1) * sub, :] * jnp.exp(b[i * sub:(i + 1) * sub, :] - bref)
            kj = kk * jnp.exp(jnp.minimum(bref - b, 0.0))
            blocks.append(_dot_nt(qi.astype(BF16), kj.astype(BF16)))
        scores = jnp.where(off_mask, jnp.concatenate(blocks, axis=0), 0.0)

        k3 = kk.reshape(nsub, sub, dk)
        b3 = b.reshape(nsub, sub, dk)
        terms = []
        for s in range(sub):
            ks = jnp.broadcast_to(k3[:, s:s + 1, :], (nsub, sub, dk)).reshape(chunk, dk)
            bs = jnp.broadcast_to(b3[:, s:s + 1, :], (nsub, sub, dk)).reshape(chunk, dk)
            terms.append((q * ks * jnp.exp(jnp.minimum(b - bs, 0.0))).astype(BF16))
        diag = _dot(jnp.concatenate(terms, axis=1), sel)
        scores = scores + jnp.where(diag_mask, diag, 0.0)

        o = o + _dot(scores.astype(BF16), vb)
        o_ref[0, pl.ds(row, chunk), :] = o

        kd = (kk * jnp.exp(b_last - b)).astype(BF16)
        st_ref[...] = jnp.exp(b_last) * st + _dot_tn(vb, kd)
        return 0

    lax.fori_loop(0, q_ref.shape[1] // chunk, chunk_body, 0)


def _hg_core(q, fz, v, lb_logits, layer):
    bsz, seq, width = q.shape
    nhead = width // HG_HEAD
    blk = min(HG_BLOCK, seq)
    chunk = min(HG_CHUNK, blk)
    tok = pl.BlockSpec((1, blk, HG_HEAD), lambda b, hd, n: (b, n, hd))
    return pl.pallas_call(
        functools.partial(_hg_core_kernel, layer=layer, chunk=chunk, sub=HG_SUB),
        out_shape=jax.ShapeDtypeStruct(q.shape, F32),
        grid=(bsz, nhead, seq // blk),
        in_specs=[tok, tok, tok,
                  pl.BlockSpec((lb_logits.shape[0], HG_HEAD), lambda b, hd, n: (0, hd))],
        out_specs=tok,
        scratch_shapes=[pltpu.VMEM((HG_HEAD, HG_HEAD), F32)],
        compiler_params=_cparams(("parallel", "parallel", "arbitrary")),
        name="hgrn2_core",
    )(q, fz, v, lb_logits)


def _gate_out_kernel(o_ref, z_ref, h_ref, gn_ref, wo_ref, gp_ref, out_ref, *, head_norm):
    o = o_ref[0]
    if head_norm:
        parts = []
        for j in range(o.shape[-1] // HG_HEAD):
            oj = o[:, j * HG_HEAD:(j + 1) * HG_HEAD]
            parts.append(oj * lax.rsqrt(jnp.mean(oj * oj, axis=-1, keepdims=True) + NORM_EPS))
        o = jnp.concatenate(parts, axis=-1) * gn_ref[...]
    t = o * _silu(z_ref[0])
    r = _dot(t.astype(BF16), wo_ref[...])
    out_ref[0] = h_ref[0] + _rms(r, gp_ref[...])


def _gate_out(o, z, h, g_head, w_out, g_post, head_norm):
    bsz, seq, d = h.shape
    tm = min(ROW_TILE, seq)
    row = lambda t: t.reshape(1, -1)
    vec = pl.BlockSpec((1, d), lambda b, m: (0, 0))
    tok = pl.BlockSpec((1, tm, d), lambda b, m: (b, m, 0))
    return pl.pallas_call(
        functools.partial(_gate_out_kernel, head_norm=head_norm),
        out_shape=jax.ShapeDtypeStruct(h.shape, F32),
        grid=(bsz, seq // tm),
        in_specs=[tok, tok, tok, vec, pl.BlockSpec((d, d), lambda b, m: (0, 0)), vec],
        out_specs=tok,
        compiler_params=_cparams(("parallel", "parallel")),
        name="gate_out",
    )(o, z, h, row(g_head), w_out.astype(BF16), row(g_post))


def _hg_layer(h, g_pre, g_post, w_in, lb_logits, layer, norm_g, w_out):
    d = h.shape[-1]
    q, fz, v, z = _norm_proj(h, g_pre, w_in.astype(BF16), (d, d, d, d), (False,) * 4)
    o = _hg_core(q, fz, v, lb_logits, layer)
    return _gate_out(o, z, h, norm_g, w_out, g_post, True)


def _at_proj_kernel(h_ref, g_ref, w_ref, bias_ref, pos_ref, invf_ref, q_ref, k_ref, v_ref, z_ref,
                    *, qw, kvw):
    y = _rms(h_ref[0], g_ref[...]).astype(BF16)
    ang = pos_ref[0].astype(F32) * invf_ref[...]
    lane = lax.broadcasted_iota(jnp.int32, ang.shape, 1)
    first = (lane % AT_HEAD) < (AT_HEAD // 2)
    cosf = jnp.cos(ang)
    sinf = jnp.where(first, -jnp.sin(ang), jnp.sin(ang))

    def rope(t):
        partner = jnp.where(first, pltpu.roll(t, LANES - AT_HEAD // 2, 1), pltpu.roll(t, AT_HEAD // 2, 1))
        return t * cosf + partner * sinf

    for j in range(qw // LANES):
        sl = slice(j * LANES, (j + 1) * LANES)
        q_ref[0, :, sl] = rope(_dot(y, w_ref[:, sl]) + bias_ref[:, sl])
    for j in range(kvw // LANES):
        sl = slice(qw + j * LANES, qw + (j + 1) * LANES)
        k_ref[0, :, j * LANES:(j + 1) * LANES] = rope(_dot(y, w_ref[:, sl]) + bias_ref[:, sl])
    off = qw + kvw
    v_ref[0] = _dot(y, w_ref[:, off:off + kvw]) + bias_ref[:, off:off + kvw]
    off += kvw
    z_ref[0] = _dot(y, w_ref[:, off:off + qw])


def _at_proj(h, g, w_bf16, bias, positions, qw, kvw):
    bsz, seq, d = h.shape
    tm = min(ROW_TILE, seq)
    half = AT_HEAD // 2
    inv_freq = ROPE_THETA ** (-jnp.arange(0, AT_HEAD, 2, dtype=F32) / AT_HEAD)
    invf = jnp.tile(inv_freq, LANES // half).reshape(1, LANES)
    tokspec = lambda w: pl.BlockSpec((1, tm, w), lambda b, m: (b, m, 0))
    const = lambda shp: pl.BlockSpec(shp, lambda b, m: (0, 0))
    return pl.pallas_call(
        functools.partial(_at_proj_kernel, qw=qw, kvw=kvw),
        out_shape=[jax.ShapeDtypeStruct((bsz, seq, qw), F32), jax.ShapeDtypeStruct((bsz, seq, kvw), F32),
                   jax.ShapeDtypeStruct((bsz, seq, kvw), F32), jax.ShapeDtypeStruct((bsz, seq, qw), F32)],
        grid=(bsz, seq // tm),
        in_specs=[tokspec(d), const((1, d)), const(w_bf16.shape), const((1, qw + 2 * kvw)),
                  tokspec(1), const((1, LANES))],
        out_specs=[tokspec(qw), tokspec(kvw), tokspec(kvw), tokspec(qw)],
        compiler_params=_cparams(("parallel", "parallel")),
        name="attn_proj",
    )(h, g.reshape(1, d), w_bf16, bias.reshape(1, -1), positions.reshape(bsz, seq, 1), invf)


def _at_core_kernel(q_ref, kc_ref, kp_ref, vc_ref, vp_ref, sink_ref, o_ref, *, group):
    blk = q_ref.shape[1]
    n = pl.program_id(1)
    qi = lax.broadcasted_iota(jnp.int32, (blk, 2 * blk), 0)
    kj = lax.broadcasted_iota(jnp.int32, (blk, 2 * blk), 1)
    dist = qi + blk - kj
    mask = (dist >= 0) & (dist < blk) & ((kj >= blk) | (n > 0))
    scale = AT_HEAD ** -0.5
    for hk in range(AT_KV_HEADS):
        ksl = slice(hk * AT_HEAD, (hk + 1) * AT_HEAD)
        kband = jnp.concatenate([kp_ref[0, :, ksl], kc_ref[0, :, ksl]], axis=0).astype(BF16)
        vband = jnp.concatenate([vp_ref[0, :, ksl], vc_ref[0, :, ksl]], axis=0).astype(BF16)
        for g in range(group):
            hq = hk * group + g
            qsl = slice(hq * AT_HEAD, (hq + 1) * AT_HEAD)
            s = _dot_nt(q_ref[0, :, qsl].astype(BF16), kband) * scale
            s = jnp.where(mask, s, -jnp.inf)
            sink = sink_ref[0:1, hq:hq + 1]
            m = jnp.maximum(jnp.max(s, axis=-1, keepdims=True), sink)
            e = jnp.exp(s - m)
            p = e / (jnp.sum(e, axis=-1, keepdims=True) + jnp.exp(sink - m))
            o_ref[0, :, qsl] = _dot(p.astype(BF16), vband)


def _at_core(q, k, v, sinks):
    bsz, seq, qw = q.shape
    kvw = k.shape[-1]
    blk = AT_BLOCK
    nq = qw // AT_HEAD
    cur = lambda w: pl.BlockSpec((1, blk, w), lambda b, n: (b, n, 0))
    prev = lambda w: pl.BlockSpec((1, blk, w), lambda b, n: (b, jnp.maximum(n - 1, 0), 0))
    return pl.pallas_call(
        functools.partial(_at_core_kernel, group=nq // AT_KV_HEADS),
        out_shape=jax.ShapeDtypeStruct(q.shape, F32),
        grid=(bsz, seq // blk),
        in_specs=[cur(qw), cur(kvw), prev(kvw), cur(kvw), prev(kvw),
                  pl.BlockSpec((1, nq), lambda b, n: (0, 0))],
        out_specs=cur(qw),
        compiler_params=_cparams(("parallel", "arbitrary")),
        name="attn_core",
    )(q, k, k, v, v, sinks.reshape(1, nq))


def _at_layer(h, positions, g_pre, g_post, w_in, b_in, sinks, w_out):
    d = h.shape[-1]
    qw = w_out.shape[0]
    kvw = (b_in.shape[0] - qw) // 2
    q, k, v, z = _at_proj(h, g_pre, w_in.astype(BF16), b_in, positions, qw, kvw)
    o = _at_core(q, k, v, sinks)
    return _gate_out(o, z, h, jnp.ones((d,), F32), w_out, g_post, False)


def kernel(x, positions, norm_pre, norm_post, s5_w_in, s5_lambda_re, s5_lambda_im, s5_log_dt, s5_b_re, s5_b_im, s5_c_re, s5_c_im, s5_d, s5_w_glu, s5_b_glu, s5_w_out, hg_w_in, hg_lb_logits, hg_norm, hg_w_out, at_w_in, at_b_in, at_sinks, at_w_out):
    depth = norm_pre.shape[0]
    h = x
    for i in range(depth):
        kind, j = i % 3, i // 3
        if kind == 0:
            h = _s5_layer(h, norm_pre[i], norm_post[i], s5_w_in[j], s5_lambda_re[j], s5_lambda_im[j],
                          s5_log_dt[j], s5_b_re[j], s5_b_im[j], s5_c_re[j], s5_c_im[j], s5_d[j],
                          s5_w_glu[j], s5_b_glu[j], s5_w_out[j])
        elif kind == 1:
            h = _hg_layer(h, norm_pre[i], norm_post[i], hg_w_in[j], hg_lb_logits, i, hg_norm[j], hg_w_out[j])
        else:
            h = _at_layer(h, positions, norm_pre[i], norm_post[i], at_w_in[j], at_b_in[j], at_sinks[j],
                          at_w_out[j])
    return h
```

```python
import functools
import math

import jax
import jax.numpy as jnp
from jax import lax
from jax.experimental import pallas as pl
from jax.experimental.pallas import tpu as pltpu

F32 = jnp.float32
BF16 = jnp.bfloat16

NORM_EPS = 1e-6
LANES = 128
SUBLANES = 8
VMEM_LIMIT = 56 * 1024 * 1024

S5_GROUP = 16
S5_STATE = 64
S5_TILE_GROUPS = 16
S5_TILE_CH = S5_TILE_GROUPS * S5_GROUP
S5_TILE_ST = S5_TILE_GROUPS * S5_STATE
S5_STEPS = 64
HG_HEAD = 128
HG_CHUNK = 64
HG_SUB = 16
HG_BLOCK = 512
AT_HEAD = 64
AT_KV_HEADS = 4
AT_BLOCK = 128
ROPE_THETA = 10000.0

ROW_TILE = 512


def _cparams(sem):
    return pltpu.CompilerParams(dimension_semantics=sem, vmem_limit_bytes=VMEM_LIMIT)


def _rms(x, g):
    return x * lax.rsqrt(jnp.mean(x * x, axis=-1, keepdims=True) + NORM_EPS) * g


def _sigmoid(x):
    return 1.0 / (1.0 + jnp.exp(-x))


def _silu(x):
    return x * _sigmoid(x)


def _gelu_tanh(x):
    c = math.sqrt(2.0 / math.pi)
    return 0.5 * x * (1.0 + jnp.tanh(c * (x + 0.044715 * (x * x * x))))


def _dot(a, b):
    return jnp.dot(a, b, preferred_element_type=F32)


def _dot_nt(a, b):
    return lax.dot_general(a, b, (((1,), (1,)), ((), ())), preferred_element_type=F32)


def _dot_tn(a, b):
    return lax.dot_general(a, b, (((0,), (0,)), ((), ())), preferred_element_type=F32)


def _chunk_row_groups(rows, steps):
    for base in range(0, rows, SUBLANES * steps):
        for r in range(SUBLANES):
            for k0 in range(0, steps, SUBLANES):
                yield base + r * steps + k0, base + k0 * SUBLANES + r


def _norm_proj_kernel(h_ref, g_ref, w_ref, *out_refs, widths, slab_steps):
    y = _rms(h_ref[0], g_ref[...]).astype(BF16)
    off = 0
    for o_ref, width, steps in zip(out_refs, widths, slab_steps):
        r = _dot(y, w_ref[:, off:off + width])
        if steps:
            for nat, cm in _chunk_row_groups(r.shape[0], steps):
                for j in range(width // LANES):
                    o_ref[0, j, pl.ds(cm, SUBLANES, stride=SUBLANES), :] = (
                        r[nat:nat + SUBLANES, j * LANES:(j + 1) * LANES])
        else:
            o_ref[0] = r
        off += width


def _norm_proj(h, g, w_bf16, widths, slab_steps):
    bsz, seq, d = h.shape
    tm = min(ROW_TILE, seq)
    out_shape, out_specs = [], []
    for width, as_slab in zip(widths, slab_steps):
        if as_slab:
            assert tm % (SUBLANES * as_slab) == 0
            ns = width // LANES
            out_shape.append(jax.ShapeDtypeStruct((bsz, ns, seq, LANES), F32))
            out_specs.append(pl.BlockSpec((1, ns, tm, LANES), lambda b, m: (b, 0, m, 0)))
        else:
            out_shape.append(jax.ShapeDtypeStruct((bsz, seq, width), F32))
            out_specs.append(pl.BlockSpec((1, tm, width), lambda b, m: (b, m, 0)))
    return pl.pallas_call(
        functools.partial(_norm_proj_kernel, widths=tuple(widths), slab_steps=tuple(slab_steps)),
        out_shape=out_shape,
        grid=(bsz, seq // tm),
        in_specs=[pl.BlockSpec((1, tm, d), lambda b, m: (b, m, 0)),
                  pl.BlockSpec((1, d), lambda b, m: (0, 0)),
                  pl.BlockSpec(w_bf16.shape, lambda b, m: (0, 0))],
        out_specs=out_specs,
        compiler_params=_cparams(("parallel", "parallel")),
        name="norm_proj",
    )(h, g.reshape(1, d), w_bf16)


def _s5_prep_kernel(lr_ref, li_ref, ldt_ref, br_ref, bi_ref, ar_ref, ai_ref, bbr_ref, bbi_ref):
    lr = lr_ref[...]
    li = li_ref[...]
    dt = jnp.exp(ldt_ref[...])
    mag = jnp.exp(lr * dt)
    ar = mag * jnp.cos(li * dt)
    ai = mag * jnp.sin(li * dt)
    den = lr * lr + li * li
    qr = ((ar - 1.0) * lr + ai * li) / den
    qi = (ai * lr - (ar - 1.0) * li) / den
    br = br_ref[...]
    bi = bi_ref[...]
    ar_ref[...] = ar
    ai_ref[...] = ai
    bbr_ref[...] = qr * br - qi * bi
    bbi_ref[...] = qr * bi + qi * br


def _s5_prep(lam_re, lam_im, log_dt, b_re, b_im):
    g, p, hh = b_re.shape
    rep = lambda t: jnp.repeat(t, hh, axis=1)
    shp = jax.ShapeDtypeStruct((g, p * hh), F32)
    ar, ai, bbr, bbi = pl.pallas_call(
        _s5_prep_kernel, out_shape=[shp, shp, shp, shp], name="s5_prep",
    )(rep(lam_re), rep(lam_im), log_dt.reshape(g, 1), b_re.reshape(g, p * hh), b_im.reshape(g, p * hh))
    return ar[:, ::hh], ai[:, ::hh], bbr.reshape(g, p, hh), bbi.reshape(g, p, hh)


def _s5_core_kernel(u_ref, wb_ref, wc_ref, a_ref, d_ref, y_ref, x_ref, s_ref, carry_ref, *, steps):
    nst = S5_TILE_ST
    nslab = S5_TILE_CH // LANES

    @pl.when(pl.program_id(2) == 0)
    def _():
        carry_ref[...] = jnp.zeros_like(carry_ref)

    u = jnp.concatenate([u_ref[0, j] for j in range(nslab)], axis=-1)
    ub = u.astype(BF16)
    x_ref[:, 0:nst] = _dot(ub, wb_ref[0, :, 0:nst])
    x_ref[:, nst:2 * nst] = _dot(ub, wb_ref[0, :, nst:2 * nst])

    ar1 = a_ref[0, 0:1, :]
    ai1 = a_ref[0, 1:2, :]
    ar = jnp.broadcast_to(ar1, (SUBLANES, nst))
    ai = jnp.broadcast_to(ai1, (SUBLANES, nst))

    def advance(k, sr, si):
        row = pl.multiple_of(k * SUBLANES, SUBLANES)
        xr = x_ref[pl.ds(row, SUBLANES), 0:nst]
        xi = x_ref[pl.ds(row, SUBLANES), nst:2 * nst]
        return ar * sr - ai * si + xr, ar * si + ai * sr + xi

    zero = jnp.zeros((SUBLANES, nst), F32)
    er, ei = lax.fori_loop(0, steps, lambda k, c: advance(k, *c), (zero, zero))

    pr, pi = ar1, ai1
    for _ in range(int(math.log2(steps))):
        pr, pi = pr * pr - pi * pi, 2.0 * pr * pi

    rid = lax.broadcasted_iota(jnp.int32, (SUBLANES, nst), 0)
    cr = carry_ref[0:1, :]
    ci = carry_ref[1:2, :]
    init_r = jnp.zeros((SUBLANES, nst), F32)
    init_i = jnp.zeros((SUBLANES, nst), F32)
    for r in range(SUBLANES):
        init_r = jnp.where(rid == r, jnp.broadcast_to(cr, (SUBLANES, nst)), init_r)
        init_i = jnp.where(rid == r, jnp.broadcast_to(ci, (SUBLANES, nst)), init_i)
        e_r = er[r:r + 1, :]
        e_i = ei[r:r + 1, :]
        cr, ci = pr * cr - pi * ci + e_r, pr * ci + pi * cr + e_i
    carry_ref[0:1, :] = cr
    carry_ref[1:2, :] = ci

    def final_pair(kk, carry):
        r0, i0 = advance(2 * kk, *carry)
        r1, i1 = advance(2 * kk + 1, r0, i0)
        row = pl.multiple_of(kk * 2 * SUBLANES, 2 * SUBLANES)
        s_ref[pl.ds(row, 2 * SUBLANES), 0:nst] = jnp.concatenate([r0, r1], axis=0).astype(BF16)
        s_ref[pl.ds(row, 2 * SUBLANES), nst:2 * nst] = jnp.concatenate([i0, i1], axis=0).astype(BF16)
        return r1, i1

    lax.fori_loop(0, steps // 2, final_pair, (init_r, init_i))

    y = (_dot(s_ref[:, 0:nst], wc_ref[0, 0:nst, :]) + _dot(s_ref[:, nst:2 * nst], wc_ref[0, nst:2 * nst, :])
         + d_ref[0] * u)
    for j in range(nslab):
        y_ref[0, j] = y[:, j * LANES:(j + 1) * LANES]


def _s5_core(u_slab, wb, wc, a_tiles, d_tiles, steps):
    bsz, nslab, seq, _ = u_slab.shape
    ntile = wb.shape[0]
    spt = S5_TILE_CH // LANES
    rows = SUBLANES * steps
    return pl.pallas_call(
        functools.partial(_s5_core_kernel, steps=steps),
        out_shape=jax.ShapeDtypeStruct(u_slab.shape, F32),
        grid=(bsz, ntile, seq // rows),
        in_specs=[pl.BlockSpec((1, spt, rows, LANES), lambda b, i, n: (b, i, n, 0)),
                  pl.BlockSpec((1, S5_TILE_CH, 2 * S5_TILE_ST), lambda b, i, n: (i, 0, 0)),
                  pl.BlockSpec((1, 2 * S5_TILE_ST, S5_TILE_CH), lambda b, i, n: (i, 0, 0)),
                  pl.BlockSpec((1, 2, S5_TILE_ST), lambda b, i, n: (i, 0, 0)),
                  pl.BlockSpec((1, 1, S5_TILE_CH), lambda b, i, n: (i, 0, 0))],
        out_specs=pl.BlockSpec((1, spt, rows, LANES), lambda b, i, n: (b, i, n, 0)),
        scratch_shapes=[pltpu.VMEM((rows, 2 * S5_TILE_ST), F32),
                        pltpu.VMEM((rows, 2 * S5_TILE_ST), BF16),
                        pltpu.VMEM((2, S5_TILE_ST), F32)],
        compiler_params=_cparams(("parallel", "parallel", "arbitrary")),
        name="s5_core",
    )(u_slab, wb, wc, a_tiles, d_tiles)


def _s5_post_kernel(y_ref, z_ref, h_ref, wg_ref, bg_ref, wo_ref, gp_ref, o_ref, *, steps):
    nslab = y_ref.shape[1]
    groups = list(_chunk_row_groups(y_ref.shape[2], steps))
    y = jnp.concatenate(
        [jnp.concatenate([y_ref[0, j, pl.ds(cm, SUBLANES, stride=SUBLANES), :] for _, cm in groups], axis=0)
         for j in range(nslab)], axis=-1)
    y = _gelu_tanh(y)
    y = y * _sigmoid(_dot(y.astype(BF16), wg_ref[...]) + bg_ref[...])
    t = y * _silu(z_ref[0])
    o = _dot(t.astype(BF16), wo_ref[...])
    o_ref[0] = h_ref[0] + _rms(o, gp_ref[...])


def _s5_post(y_slab, z, h, w_glu, b_glu, w_out, g_post, steps):
    bsz, seq, d = h.shape
    nslab = y_slab.shape[1]
    tm = min(ROW_TILE, seq)
    assert tm % (SUBLANES * steps) == 0
    row = lambda t: t.reshape(1, -1)
    vec = pl.BlockSpec((1, d), lambda b, m: (0, 0))
    mat = pl.BlockSpec((d, d), lambda b, m: (0, 0))
    slab = pl.BlockSpec((1, nslab, tm, LANES), lambda b, m: (b, 0, m, 0))
    tok = pl.BlockSpec((1, tm, d), lambda b, m: (b, m, 0))
    return pl.pallas_call(
        functools.partial(_s5_post_kernel, steps=steps),
        out_shape=jax.ShapeDtypeStruct(h.shape, F32),
        grid=(bsz, seq // tm),
        in_specs=[slab, tok, tok, mat, vec, mat, vec],
        out_specs=tok,
        compiler_params=_cparams(("parallel", "parallel")),
        name="s5_post",
    )(y_slab, z, h, w_glu.astype(BF16), row(b_glu), w_out.astype(BF16), row(g_post))


def _s5_layer(h, g_pre, g_post, w_in, lam_re, lam_im, log_dt, b_re, b_im, c_re, c_im, d_skip,
              w_glu, b_glu, w_out):
    bsz, seq, d = h.shape
    ngroup = lam_re.shape[0]
    ntile = ngroup // S5_TILE_GROUPS
    ar, ai, bbr, bbi = _s5_prep(lam_re, lam_im, log_dt, b_re, b_im)
    eye = jnp.eye(S5_TILE_GROUPS, dtype=F32)

    def in_tile(bb):
        t = bb.reshape(ntile, S5_TILE_GROUPS, S5_STATE, S5_GROUP)
        return jnp.einsum('igph,gk->ighkp', t, eye).reshape(ntile, S5_TILE_CH, S5_TILE_ST)

    def out_tile(cc):
        t = cc.reshape(ntile, S5_TILE_GROUPS, S5_GROUP, S5_STATE)
        return jnp.einsum('ighp,gk->ikpgh', t, eye).reshape(ntile, S5_TILE_ST, S5_TILE_CH)

    wb = jnp.concatenate([in_tile(bbr), in_tile(bbi)], axis=2).astype(BF16)
    wc = jnp.concatenate([out_tile(c_re), -out_tile(c_im)], axis=1).astype(BF16)
    a_tiles = jnp.stack([ar.reshape(ntile, S5_TILE_ST), ai.reshape(ntile, S5_TILE_ST)], axis=1)
    d_tiles = d_skip.reshape(ntile, 1, S5_TILE_CH)

    steps = min(S5_STEPS, seq // SUBLANES)
    u_slab, z = _norm_proj(h, g_pre, w_in.astype(BF16), (d, d), (steps, 0))
    y_slab = _s5_core(u_slab, wb, wc, a_tiles, d_tiles, steps)
    return _s5_post(y_slab, z, h, w_glu, b_glu, w_out, g_post, steps)


def _split3(x):
    hi = x.astype(BF16)
    r1 = x - hi.astype(F32)
    mid = r1.astype(BF16)
    lo = (r1 - mid.astype(F32)).astype(BF16)
    return hi, mid, lo


def _hg_core_kernel(q_ref, f_ref, v_ref, lbl_ref, o_ref, st_ref, *, layer, chunk, sub):
    dk = HG_HEAD
    nsub = chunk // sub

    @pl.when(pl.program_id(2) == 0)
    def _():
        st_ref[...] = jnp.zeros_like(st_ref)

    lg = lbl_ref[...]
    ex = jnp.exp(lg - jnp.max(lg, axis=0, keepdims=True))
    p = ex / jnp.sum(ex, axis=0, keepdims=True)
    lb = jnp.zeros((1, dk), F32)
    for j in range(1, layer + 1):
        lb = lb + p[j:j + 1, :]

    ri = lax.broadcasted_iota(jnp.int32, (chunk, chunk), 0)
    ci = lax.broadcasted_iota(jnp.int32, (chunk, chunk), 1)
    tri = jnp.where(ci <= ri, 1.0, 0.0).astype(BF16)
    off_mask = (ci // sub) < (ri // sub)
    diag_mask = ((ci // sub) == (ri // sub)) & (ci <= ri)
    sel_r = lax.broadcasted_iota(jnp.int32, (sub * dk, chunk), 0) // dk
    sel_c = lax.broadcasted_iota(jnp.int32, (sub * dk, chunk), 1) % sub
    sel = jnp.where(sel_r == sel_c, 1.0, 0.0).astype(BF16)

    def chunk_body(c, _):
        row = pl.multiple_of(c * chunk, chunk)
        q = q_ref[0, pl.ds(row, chunk), :]
        v = v_ref[0, pl.ds(row, chunk), :]
        f = lb + (1.0 - lb) * _sigmoid(f_ref[0, pl.ds(row, chunk), :])
        logf = jnp.log(f)
        kk = 1.0 - f
        hi, mid, lo = _split3(logf)
        b = _dot(tri, hi) + _dot(tri, mid) + _dot(tri, lo)
        b_last = b[chunk - 1:chunk, :]
        st = st_ref[...]
        vb = v.astype(BF16)

        o = _dot_nt((q * jnp.exp(b)).astype(BF16), st.astype(BF16))

        blocks = [jnp.zeros((sub, chunk), F32)]
        for i in range(1, nsub):
            bref = b[i * sub - 1:i * sub, :]
            qi = q[i * sub:(i + 1) * sub, :] * jnp.exp(b[i * sub:(i + 1) * sub, :] - bref)
            kj = kk * jnp.exp(jnp.minimum(bref - b, 0.0))
            blocks.append(_dot_nt(qi.astype(BF16), kj.astype(BF16)))
        scores = jnp.where(off_mask, jnp.concatenate(blocks, axis=0), 0.0)

        k3 = kk.reshape(nsub, sub, dk)
        b3 = b.reshape(nsub, sub, dk)
        terms = []
        for s in range(sub):
            ks = jnp.broadcast_to(k3[:, s:s + 1, :], (nsub, sub, dk)).reshape(chunk, dk)
            bs = jnp.broadcast_to(b3[:, s:s + 1, :], (nsub, sub, dk)).reshape(chunk, dk)
            terms.append((q * ks * jnp.exp(jnp.minimum(b - bs, 0.0))).astype(BF16))
        diag = _dot(jnp.concatenate(terms, axis=1), sel)
        scores = scores + jnp.where(diag_mask, diag, 0.0)

        o = o + _dot(scores.astype(BF16), vb)
        o_ref[0, pl.ds(row, chunk), :] = o

        kd = (kk * jnp.exp(b_last - b)).astype(BF16)
        st_ref[...] = jnp.exp(b_last) * st + _dot_tn(vb, kd)
        return 0

    lax.fori_loop(0, q_ref.shape[1] // chunk, chunk_body, 0)


def _hg_core(q, fz, v, lb_logits, layer):
    bsz, seq, width = q.shape
    nhead = width // HG_HEAD
    blk = min(HG_BLOCK, seq)
    chunk = min(HG_CHUNK, blk)
    tok = pl.BlockSpec((1, blk, HG_HEAD), lambda b, hd, n: (b, n, hd))
    return pl.pallas_call(
        functools.partial(_hg_core_kernel, layer=layer, chunk=chunk, sub=HG_SUB),
        out_shape=jax.ShapeDtypeStruct(q.shape, F32),
        grid=(bsz, nhead, seq // blk),
        in_specs=[tok, tok, tok,
                  pl.BlockSpec((lb_logits.shape[0], HG_HEAD), lambda b, hd, n: (0, hd))],
        out_specs=tok,
        scratch_shapes=[pltpu.VMEM((HG_HEAD, HG_HEAD), F32)],
        compiler_params=_cparams(("parallel", "parallel", "arbitrary")),
        name="hgrn2_core",
    )(q, fz, v, lb_logits)


def _gate_out_kernel(o_ref, z_ref, h_ref, gn_ref, wo_ref, gp_ref, out_ref, *, head_norm):
    o = o_ref[0]
    if head_norm:
        parts = []
        for j in range(o.shape[-1] // HG_HEAD):
            oj = o[:, j * HG_HEAD:(j + 1) * HG_HEAD]
            parts.append(oj * lax.rsqrt(jnp.mean(oj * oj, axis=-1, keepdims=True) + NORM_EPS))
        o = jnp.concatenate(parts, axis=-1) * gn_ref[...]
    t = o * _silu(z_ref[0])
    r = _dot(t.astype(BF16), wo_ref[...])
    out_ref[0] = h_ref[0] + _rms(r, gp_ref[...])


def _gate_out(o, z, h, g_head, w_out, g_post, head_norm):
    bsz, seq, d = h.shape
    tm = min(ROW_TILE, seq)
    row = lambda t: t.reshape(1, -1)
    vec = pl.BlockSpec((1, d), lambda b, m: (0, 0))
    tok = pl.BlockSpec((1, tm, d), lambda b, m: (b, m, 0))
    return pl.pallas_call(
        functools.partial(_gate_out_kernel, head_norm=head_norm),
        out_shape=jax.ShapeDtypeStruct(h.shape, F32),
        grid=(bsz, seq // tm),
        in_specs=[tok, tok, tok, vec, pl.BlockSpec((d, d), lambda b, m: (0, 0)), vec],
        out_specs=tok,
        compiler_params=_cparams(("parallel", "parallel")),
        name="gate_out",
    )(o, z, h, row(g_head), w_out.astype(BF16), row(g_post))


def _hg_layer(h, g_pre, g_post, w_in, lb_logits, layer, norm_g, w_out):
    d = h.shape[-1]
    q, fz, v, z = _norm_proj(h, g_pre, w_in.astype(BF16), (d, d, d, d), (0,) * 4)
    o = _hg_core(q, fz, v, lb_logits, layer)
    return _gate_out(o, z, h, norm_g, w_out, g_post, True)


def _at_proj_kernel(h_ref, g_ref, w_ref, bias_ref, pos_ref, invf_ref, q_ref, k_ref, v_ref, z_ref,
                    *, qw, kvw):
    y = _rms(h_ref[0], g_ref[...]).astype(BF16)
    ang = pos_ref[0].astype(F32) * invf_ref[...]
    lane = lax.broadcasted_iota(jnp.int32, ang.shape, 1)
    first = (lane % AT_HEAD) < (AT_HEAD // 2)
    low = lane < AT_HEAD
    cosf = jnp.cos(ang)
    sinf = jnp.where(first, -jnp.sin(ang), jnp.sin(ang))

    def rope(t):
        partner = jnp.where(first, pltpu.roll(t, LANES - AT_HEAD // 2, 1), pltpu.roll(t, AT_HEAD // 2, 1))
        return t * cosf + partner * sinf

    def store_dup(ref, j, t):
        r = pltpu.roll(t, AT_HEAD, 1)
        ref[0, :, (2 * j) * LANES:(2 * j + 1) * LANES] = jnp.where(low, t, r)
        ref[0, :, (2 * j + 1) * LANES:(2 * j + 2) * LANES] = jnp.where(low, r, t)

    for j in range(qw // LANES):
        sl = slice(j * LANES, (j + 1) * LANES)
        q_ref[0, :, sl] = rope(_dot(y, w_ref[:, sl]) + bias_ref[:, sl])
    for j in range(kvw // LANES):
        sl = slice(qw + j * LANES, qw + (j + 1) * LANES)
        store_dup(k_ref, j, rope(_dot(y, w_ref[:, sl]) + bias_ref[:, sl]))
    for j in range(kvw // LANES):
        sl = slice(qw + kvw + j * LANES, qw + kvw + (j + 1) * LANES)
        store_dup(v_ref, j, _dot(y, w_ref[:, sl]) + bias_ref[:, sl])
    off = qw + 2 * kvw
    z_ref[0] = _dot(y, w_ref[:, off:off + qw])


def _at_proj(h, g, w_bf16, bias, positions, qw, kvw):
    bsz, seq, d = h.shape
    tm = min(ROW_TILE, seq)
    half = AT_HEAD // 2
    inv_freq = ROPE_THETA ** (-jnp.arange(0, AT_HEAD, 2, dtype=F32) / AT_HEAD)
    invf = jnp.tile(inv_freq, LANES // half).reshape(1, LANES)
    tokspec = lambda w: pl.BlockSpec((1, tm, w), lambda b, m: (b, m, 0))
    const = lambda shp: pl.BlockSpec(shp, lambda b, m: (0, 0))
    return pl.pallas_call(
        functools.partial(_at_proj_kernel, qw=qw, kvw=kvw),
        out_shape=[jax.ShapeDtypeStruct((bsz, seq, qw), F32), jax.ShapeDtypeStruct((bsz, seq, 2 * kvw), F32),
                   jax.ShapeDtypeStruct((bsz, seq, 2 * kvw), F32), jax.ShapeDtypeStruct((bsz, seq, qw), F32)],
        grid=(bsz, seq // tm),
        in_specs=[tokspec(d), const((1, d)), const(w_bf16.shape), const((1, qw + 2 * kvw)),
                  tokspec(1), const((1, LANES))],
        out_specs=[tokspec(qw), tokspec(2 * kvw), tokspec(2 * kvw), tokspec(qw)],
        compiler_params=_cparams(("parallel", "parallel")),
        name="attn_proj",
    )(h, g.reshape(1, d), w_bf16, bias.reshape(1, -1), positions.reshape(bsz, seq, 1), invf)


def _at_core_kernel(q_ref, kc_ref, kp_ref, vc_ref, vp_ref, sink_ref, o_ref):
    blk = q_ref.shape[1]
    n = pl.program_id(1)
    ngroup = kc_ref.shape[2] // LANES
    qi = lax.broadcasted_iota(jnp.int32, (2 * blk, 4 * blk), 0) % blk
    kj = lax.broadcasted_iota(jnp.int32, (2 * blk, 4 * blk), 1) % (2 * blk)
    dist = qi + blk - kj
    mask = (dist >= 0) & (dist < blk) & ((kj >= blk) | (n > 0))
    low = lax.broadcasted_iota(jnp.int32, (2 * blk, LANES), 1) < AT_HEAD
    top = lax.broadcasted_iota(jnp.int32, (2 * blk, 1), 0) < blk
    ones_lo = jnp.where(low, 1.0, 0.0).astype(BF16)
    ones_hi = jnp.where(low, 0.0, 1.0).astype(BF16)
    scale = AT_HEAD ** -0.5

    scores = []
    for g in range(ngroup):
        sl = slice(g * LANES, (g + 1) * LANES)
        kdup = jnp.concatenate([kp_ref[0, :, sl], kc_ref[0, :, sl]], axis=0)
        rhs = jnp.concatenate([jnp.where(low, kdup, 0.0), jnp.where(low, 0.0, kdup)], axis=0).astype(BF16)
        lhs = jnp.concatenate([q_ref[0, :, (2 * g) * LANES:(2 * g + 1) * LANES],
                               q_ref[0, :, (2 * g + 1) * LANES:(2 * g + 2) * LANES]], axis=0).astype(BF16)
        scores.append(_dot_nt(lhs, rhs))

    probs, sink_terms = [], []
    for g in range(ngroup):
        s = jnp.where(mask, scores[g] * scale, -jnp.inf)
        sink_a = jnp.where(top, sink_ref[0:1, 4 * g:4 * g + 1], sink_ref[0:1, 4 * g + 2:4 * g + 3])
        sink_b = jnp.where(top, sink_ref[0:1, 4 * g + 1:4 * g + 2], sink_ref[0:1, 4 * g + 3:4 * g + 4])
        m_a = jnp.maximum(jnp.max(s[:, :2 * blk], axis=-1, keepdims=True), sink_a)
        m_b = jnp.maximum(jnp.max(s[:, 2 * blk:], axis=-1, keepdims=True), sink_b)
        e = jnp.concatenate([jnp.exp(s[:, :2 * blk] - m_a), jnp.exp(s[:, 2 * blk:] - m_b)], axis=1)
        probs.append(e.astype(BF16))
        sink_terms.append(jnp.where(low, jnp.exp(sink_a - m_a), jnp.exp(sink_b - m_b)))

    for g in range(ngroup):
        sl = slice(g * LANES, (g + 1) * LANES)
        vdup = jnp.concatenate([vp_ref[0, :, sl], vc_ref[0, :, sl]], axis=0)
        va = jnp.where(low, vdup, 0.0).astype(BF16)
        vb = jnp.where(low, 0.0, vdup).astype(BF16)
        rhs = jnp.concatenate([jnp.concatenate([va, ones_lo], axis=1),
                               jnp.concatenate([vb, ones_hi], axis=1)], axis=0)
        out = _dot(probs[g], rhs)
        o = out[:, :LANES] / (out[:, LANES:] + sink_terms[g])
        o_ref[0, :, (2 * g) * LANES:(2 * g + 1) * LANES] = o[:blk]
        o_ref[0, :, (2 * g + 1) * LANES:(2 * g + 2) * LANES] = o[blk:]


def _at_core(q, k_dup, v_dup, sinks):
    bsz, seq, qw = q.shape
    kvw = k_dup.shape[-1]
    blk = AT_BLOCK
    nq = qw // AT_HEAD
    assert nq == 4 * (kvw // LANES) and qw == 2 * kvw
    cur = lambda w: pl.BlockSpec((1, blk, w), lambda b, n: (b, n, 0))
    prev = lambda w: pl.BlockSpec((1, blk, w), lambda b, n: (b, jnp.maximum(n - 1, 0), 0))
    return pl.pallas_call(
        _at_core_kernel,
        out_shape=jax.ShapeDtypeStruct(q.shape, F32),
        grid=(bsz, seq // blk),
        in_specs=[cur(qw), cur(kvw), prev(kvw), cur(kvw), prev(kvw),
                  pl.BlockSpec((1, nq), lambda b, n: (0, 0))],
        out_specs=cur(qw),
        compiler_params=_cparams(("parallel", "arbitrary")),
        name="attn_core",
    )(q, k_dup, k_dup, v_dup, v_dup, sinks.reshape(1, nq))


def _at_layer(h, positions, g_pre, g_post, w_in, b_in, sinks, w_out):
    d = h.shape[-1]
    qw = w_out.shape[0]
    kvw = (b_in.shape[0] - qw) // 2
    q, k, v, z = _at_proj(h, g_pre, w_in.astype(BF16), b_in, positions, qw, kvw)
    o = _at_core(q, k, v, sinks)
    return _gate_out(o, z, h, jnp.ones((d,), F32), w_out, g_post, False)


def kernel(x, positions, norm_pre, norm_post, s5_w_in, s5_lambda_re, s5_lambda_im, s5_log_dt, s5_b_re, s5_b_im, s5_c_re, s5_c_im, s5_d, s5_w_glu, s5_b_glu, s5_w_out, hg_w_in, hg_lb_logits, hg_norm, hg_w_out, at_w_in, at_b_in, at_sinks, at_w_out):
    depth = norm_pre.shape[0]
    h = x
    for i in range(depth):
        kind, j = i % 3, i // 3
        if kind == 0:
            h = _s5_layer(h, norm_pre[i], norm_post[i], s5_w_in[j], s5_lambda_re[j], s5_lambda_im[j],
                          s5_log_dt[j], s5_b_re[j], s5_b_im[j], s5_c_re[j], s5_c_im[j], s5_d[j],
                          s5_w_glu[j], s5_b_glu[j], s5_w_out[j])
        elif kind == 1:
            h = _hg_layer(h, norm_pre[i], norm_post[i], hg_w_in[j], hg_lb_logits, i, hg_norm[j], hg_w_out[j])
        else:
            h = _at_layer(h, positions, norm_pre[i], norm_post[i], at_w_in[j], at_b_in[j], at_sinks[j],
                          at_w_out[j])
    return h
```

```python
import functools
import math

import jax
import jax.numpy as jnp
from jax import lax
from jax.experimental import pallas as pl
from jax.experimental.pallas import tpu as pltpu

F32 = jnp.float32
BF16 = jnp.bfloat16

NORM_EPS = 1e-6
LANES = 128
SUBLANES = 8
VMEM_LIMIT = 56 * 1024 * 1024

S5_GROUP = 16
S5_STATE = 64
S5_TILE_GROUPS = 16
S5_TILE_CH = S5_TILE_GROUPS * S5_GROUP
S5_TILE_ST = S5_TILE_GROUPS * S5_STATE
S5_STEPS = 64
HG_HEAD = 128
HG_CHUNK = 64
HG_BLOCK = 256
AT_HEAD = 64
AT_KV_HEADS = 4
AT_BLOCK = 128
ROPE_THETA = 10000.0

ROW_TILE = 512


def _cparams(sem):
    return pltpu.CompilerParams(dimension_semantics=sem, vmem_limit_bytes=VMEM_LIMIT)


def _rms(x, g):
    return x * lax.rsqrt(jnp.mean(x * x, axis=-1, keepdims=True) + NORM_EPS) * g


def _sigmoid(x):
    return 1.0 / (1.0 + jnp.exp(-x))


def _silu(x):
    return x * _sigmoid(x)


def _gelu_tanh(x):
    c = math.sqrt(2.0 / math.pi)
    return 0.5 * x * (1.0 + jnp.tanh(c * (x + 0.044715 * (x * x * x))))


def _dot(a, b):
    return jnp.dot(a, b, preferred_element_type=F32)


def _dot_nt(a, b):
    return lax.dot_general(a, b, (((1,), (1,)), ((), ())), preferred_element_type=F32)


def _dot_tn(a, b):
    return lax.dot_general(a, b, (((0,), (0,)), ((), ())), preferred_element_type=F32)


def _chunk_row_groups(rows, steps):
    for base in range(0, rows, SUBLANES * steps):
        for r in range(SUBLANES):
            for k0 in range(0, steps, SUBLANES):
                yield base + r * steps + k0, base + k0 * SUBLANES + r


def _norm_proj_kernel(h_ref, g_ref, w_ref, *out_refs, widths, slab_steps):
    y = _rms(h_ref[0], g_ref[...]).astype(BF16)
    off = 0
    for o_ref, width, steps in zip(out_refs, widths, slab_steps):
        r = _dot(y, w_ref[:, off:off + width])
        if steps:
            for nat, cm in _chunk_row_groups(r.shape[0], steps):
                for j in range(width // LANES):
                    o_ref[0, j, pl.ds(cm, SUBLANES, stride=SUBLANES), :] = (
                        r[nat:nat + SUBLANES, j * LANES:(j + 1) * LANES])
        else:
            o_ref[0] = r
        off += width


def _norm_proj(h, g, w_bf16, widths, slab_steps):
    bsz, seq, d = h.shape
    tm = min(ROW_TILE, seq)
    out_shape, out_specs = [], []
    for width, as_slab in zip(widths, slab_steps):
        if as_slab:
            assert tm % (SUBLANES * as_slab) == 0
            ns = width // LANES
            out_shape.append(jax.ShapeDtypeStruct((bsz, ns, seq, LANES), F32))
            out_specs.append(pl.BlockSpec((1, ns, tm, LANES), lambda b, m: (b, 0, m, 0)))
        else:
            out_shape.append(jax.ShapeDtypeStruct((bsz, seq, width), F32))
            out_specs.append(pl.BlockSpec((1, tm, width), lambda b, m: (b, m, 0)))
    return pl.pallas_call(
        functools.partial(_norm_proj_kernel, widths=tuple(widths), slab_steps=tuple(slab_steps)),
        out_shape=out_shape,
        grid=(bsz, seq // tm),
        in_specs=[pl.BlockSpec((1, tm, d), lambda b, m: (b, m, 0)),
                  pl.BlockSpec((1, d), lambda b, m: (0, 0)),
                  pl.BlockSpec(w_bf16.shape, lambda b, m: (0, 0))],
        out_specs=out_specs,
        compiler_params=_cparams(("parallel", "parallel")),
        name="norm_proj",
    )(h, g.reshape(1, d), w_bf16)


def _s5_prep_kernel(lr_ref, li_ref, ldt_ref, br_ref, bi_ref, ar_ref, ai_ref, bbr_ref, bbi_ref):
    lr = lr_ref[...]
    li = li_ref[...]
    dt = jnp.exp(ldt_ref[...])
    mag = jnp.exp(lr * dt)
    ar = mag * jnp.cos(li * dt)
    ai = mag * jnp.sin(li * dt)
    den = lr * lr + li * li
    qr = ((ar - 1.0) * lr + ai * li) / den
    qi = (ai * lr - (ar - 1.0) * li) / den
    br = br_ref[...]
    bi = bi_ref[...]
    ar_ref[...] = ar
    ai_ref[...] = ai
    bbr_ref[...] = qr * br - qi * bi
    bbi_ref[...] = qr * bi + qi * br


def _s5_prep(lam_re, lam_im, log_dt, b_re, b_im):
    g, p, hh = b_re.shape
    rep = lambda t: jnp.repeat(t, hh, axis=1)
    shp = jax.ShapeDtypeStruct((g, p * hh), F32)
    ar, ai, bbr, bbi = pl.pallas_call(
        _s5_prep_kernel, out_shape=[shp, shp, shp, shp], name="s5_prep",
    )(rep(lam_re), rep(lam_im), log_dt.reshape(g, 1), b_re.reshape(g, p * hh), b_im.reshape(g, p * hh))
    return ar[:, ::hh], ai[:, ::hh], bbr.reshape(g, p, hh), bbi.reshape(g, p, hh)


def _s5_core_kernel(u_ref, wb_ref, wc_ref, a_ref, d_ref, y_ref, x_ref, s_ref, carry_ref, *, steps):
    nst = S5_TILE_ST
    nslab = S5_TILE_CH // LANES

    @pl.when(pl.program_id(2) == 0)
    def _():
        carry_ref[...] = jnp.zeros_like(carry_ref)

    u = jnp.concatenate([u_ref[0, j] for j in range(nslab)], axis=-1)
    ub = u.astype(BF16)
    x_ref[:, 0:nst] = _dot(ub, wb_ref[0, :, 0:nst])
    x_ref[:, nst:2 * nst] = _dot(ub, wb_ref[0, :, nst:2 * nst])

    ar1 = a_ref[0, 0:1, :]
    ai1 = a_ref[0, 1:2, :]
    ar = jnp.broadcast_to(ar1, (SUBLANES, nst))
    ai = jnp.broadcast_to(ai1, (SUBLANES, nst))

    def advance(k, sr, si):
        row = pl.multiple_of(k * SUBLANES, SUBLANES)
        xr = x_ref[pl.ds(row, SUBLANES), 0:nst]
        xi = x_ref[pl.ds(row, SUBLANES), nst:2 * nst]
        return ar * sr - ai * si + xr, ar * si + ai * sr + xi

    zero = jnp.zeros((SUBLANES, nst), F32)
    er, ei = lax.fori_loop(0, steps, lambda k, c: advance(k, *c), (zero, zero))

    pr, pi = ar1, ai1
    for _ in range(int(math.log2(steps))):
        pr, pi = pr * pr - pi * pi, 2.0 * pr * pi

    rid = lax.broadcasted_iota(jnp.int32, (SUBLANES, nst), 0)
    cr = carry_ref[0:1, :]
    ci = carry_ref[1:2, :]
    init_r = jnp.zeros((SUBLANES, nst), F32)
    init_i = jnp.zeros((SUBLANES, nst), F32)
    for r in range(SUBLANES):
        init_r = jnp.where(rid == r, jnp.broadcast_to(cr, (SUBLANES, nst)), init_r)
        init_i = jnp.where(rid == r, jnp.broadcast_to(ci, (SUBLANES, nst)), init_i)
        e_r = er[r:r + 1, :]
        e_i = ei[r:r + 1, :]
        cr, ci = pr * cr - pi * ci + e_r, pr * ci + pi * cr + e_i
    carry_ref[0:1, :] = cr
    carry_ref[1:2, :] = ci

    def final_pair(kk, carry):
        r0, i0 = advance(2 * kk, *carry)
        r1, i1 = advance(2 * kk + 1, r0, i0)
        row = pl.multiple_of(kk * 2 * SUBLANES, 2 * SUBLANES)
        s_ref[pl.ds(row, 2 * SUBLANES), 0:nst] = jnp.concatenate([r0, r1], axis=0).astype(BF16)
        s_ref[pl.ds(row, 2 * SUBLANES), nst:2 * nst] = jnp.concatenate([i0, i1], axis=0).astype(BF16)
        return r1, i1

    lax.fori_loop(0, steps // 2, final_pair, (init_r, init_i))

    y = (_dot(s_ref[:, 0:nst], wc_ref[0, 0:nst, :]) + _dot(s_ref[:, nst:2 * nst], wc_ref[0, nst:2 * nst, :])
         + d_ref[0] * u)
    for j in range(nslab):
        y_ref[0, j] = y[:, j * LANES:(j + 1) * LANES]


def _s5_core(u_slab, wb, wc, a_tiles, d_tiles, steps):
    bsz, nslab, seq, _ = u_slab.shape
    ntile = wb.shape[0]
    spt = S5_TILE_CH // LANES
    rows = SUBLANES * steps
    return pl.pallas_call(
        functools.partial(_s5_core_kernel, steps=steps),
        out_shape=jax.ShapeDtypeStruct(u_slab.shape, F32),
        grid=(bsz, ntile, seq // rows),
        in_specs=[pl.BlockSpec((1, spt, rows, LANES), lambda b, i, n: (b, i, n, 0)),
                  pl.BlockSpec((1, S5_TILE_CH, 2 * S5_TILE_ST), lambda b, i, n: (i, 0, 0)),
                  pl.BlockSpec((1, 2 * S5_TILE_ST, S5_TILE_CH), lambda b, i, n: (i, 0, 0)),
                  pl.BlockSpec((1, 2, S5_TILE_ST), lambda b, i, n: (i, 0, 0)),
                  pl.BlockSpec((1, 1, S5_TILE_CH), lambda b, i, n: (i, 0, 0))],
        out_specs=pl.BlockSpec((1, spt, rows, LANES), lambda b, i, n: (b, i, n, 0)),
        scratch_shapes=[pltpu.VMEM((rows, 2 * S5_TILE_ST), F32),
                        pltpu.VMEM((rows, 2 * S5_TILE_ST), BF16),
                        pltpu.VMEM((2, S5_TILE_ST), F32)],
        compiler_params=_cparams(("parallel", "parallel", "arbitrary")),
        name="s5_core",
    )(u_slab, wb, wc, a_tiles, d_tiles)


def _s5_post_kernel(y_ref, z_ref, h_ref, wg_ref, bg_ref, wo_ref, gp_ref, o_ref, *, steps):
    nslab = y_ref.shape[1]
    groups = list(_chunk_row_groups(y_ref.shape[2], steps))
    y = jnp.concatenate(
        [jnp.concatenate([y_ref[0, j, pl.ds(cm, SUBLANES, stride=SUBLANES), :] for _, cm in groups], axis=0)
         for j in range(nslab)], axis=-1)
    y = _gelu_tanh(y)
    y = y * _sigmoid(_dot(y.astype(BF16), wg_ref[...]) + bg_ref[...])
    t = y * _silu(z_ref[0])
    o = _dot(t.astype(BF16), wo_ref[...])
    o_ref[0] = h_ref[0] + _rms(o, gp_ref[...])


def _s5_post(y_slab, z, h, w_glu, b_glu, w_out, g_post, steps):
    bsz, seq, d = h.shape
    nslab = y_slab.shape[1]
    tm = min(ROW_TILE, seq)
    assert tm % (SUBLANES * steps) == 0
    row = lambda t: t.reshape(1, -1)
    vec = pl.BlockSpec((1, d), lambda b, m: (0, 0))
    mat = pl.BlockSpec((d, d), lambda b, m: (0, 0))
    slab = pl.BlockSpec((1, nslab, tm, LANES), lambda b, m: (b, 0, m, 0))
    tok = pl.BlockSpec((1, tm, d), lambda b, m: (b, m, 0))
    return pl.pallas_call(
        functools.partial(_s5_post_kernel, steps=steps),
        out_shape=jax.ShapeDtypeStruct(h.shape, F32),
        grid=(bsz, seq // tm),
        in_specs=[slab, tok, tok, mat, vec, mat, vec],
        out_specs=tok,
        compiler_params=_cparams(("parallel", "parallel")),
        name="s5_post",
    )(y_slab, z, h, w_glu.astype(BF16), row(b_glu), w_out.astype(BF16), row(g_post))


def _s5_layer(h, g_pre, g_post, w_in, lam_re, lam_im, log_dt, b_re, b_im, c_re, c_im, d_skip,
              w_glu, b_glu, w_out):
    bsz, seq, d = h.shape
    ngroup = lam_re.shape[0]
    ntile = ngroup // S5_TILE_GROUPS
    ar, ai, bbr, bbi = _s5_prep(lam_re, lam_im, log_dt, b_re, b_im)
    eye = jnp.eye(S5_TILE_GROUPS, dtype=F32)

    def in_tile(bb):
        t = bb.reshape(ntile, S5_TILE_GROUPS, S5_STATE, S5_GROUP)
        return jnp.einsum('igph,gk->ighkp', t, eye).reshape(ntile, S5_TILE_CH, S5_TILE_ST)

    def out_tile(cc):
        t = cc.reshape(ntile, S5_TILE_GROUPS, S5_GROUP, S5_STATE)
        return jnp.einsum('ighp,gk->ikpgh', t, eye).reshape(ntile, S5_TILE_ST, S5_TILE_CH)

    wb = jnp.concatenate([in_tile(bbr), in_tile(bbi)], axis=2).astype(BF16)
    wc = jnp.concatenate([out_tile(c_re), -out_tile(c_im)], axis=1).astype(BF16)
    a_tiles = jnp.stack([ar.reshape(ntile, S5_TILE_ST), ai.reshape(ntile, S5_TILE_ST)], axis=1)
    d_tiles = d_skip.reshape(ntile, 1, S5_TILE_CH)

    steps = min(S5_STEPS, seq // SUBLANES)
    u_slab, z = _norm_proj(h, g_pre, w_in.astype(BF16), (d, d), (steps, 0))
    y_slab = _s5_core(u_slab, wb, wc, a_tiles, d_tiles, steps)
    return _s5_post(y_slab, z, h, w_glu, b_glu, w_out, g_post, steps)


def _chunk_cumsum(x, chunk):
    pos = lax.broadcasted_iota(jnp.int32, x.shape, 0) % chunk
    sh = 1
    while sh < chunk:
        x = x + jnp.where(pos >= sh, pltpu.roll(x, sh, 0), 0.0)
        sh *= 2
    return x


def _block_mid_rows(x, blk):
    rows, w = x.shape
    half = blk // 2
    if blk >= SUBLANES:
        x3 = x.reshape(rows // blk, blk, w)
        return jnp.broadcast_to(x3[:, half - 1:half, :], (rows // blk, blk, w)).reshape(rows, w)
    pos = lax.broadcasted_iota(jnp.int32, (rows, w), 0) % blk
    out = x
    for p in range(blk):
        d = p - (half - 1)
        if d != 0:
            out = jnp.where(pos == p, pltpu.roll(x, d % rows, 0), out)
    return out


def _hg_core_kernel(q_ref, f_ref, v_ref, lbl_ref, o_ref, ql_ref, kl_ref, qe_ref, kd_ref, vb_ref, g_ref, st_ref,
                    *, layer, chunk):
    rows, width = q_ref.shape[1], q_ref.shape[2]
    pair = 2 * HG_HEAD
    blocks = [2 ** i for i in range(1, int(math.log2(chunk)) + 1)]

    @pl.when(pl.program_id(1) == 0)
    def _():
        st_ref[...] = jnp.zeros_like(st_ref)

    lg = lbl_ref[...]
    ex = jnp.exp(lg - jnp.max(lg, axis=0, keepdims=True))
    p = ex / jnp.sum(ex, axis=0, keepdims=True)
    lb = jnp.zeros((1, width), F32)
    for j in range(1, layer + 1):
        lb = lb + p[j:j + 1, :]

    q = q_ref[0]
    f = lb + (1.0 - lb) * _sigmoid(f_ref[0])
    kk = 1.0 - f
    b = _chunk_cumsum(jnp.log(f), chunk)
    b3 = b.reshape(rows // chunk, chunk, width)
    b_last = b3[:, chunk - 1:chunk, :]
    g_ref[...] = jnp.exp(b_last.reshape(rows // chunk, width))
    qe_ref[...] = (q * jnp.exp(b)).astype(BF16)
    kd_ref[...] = (kk * jnp.exp(jnp.broadcast_to(b_last, b3.shape).reshape(rows, width) - b)).astype(BF16)
    vb_ref[...] = v_ref[0].astype(BF16)
    ql_ref[0] = (q * kk).astype(BF16)
    for li, blk in enumerate(blocks):
        w = jnp.exp(-jnp.abs(b - _block_mid_rows(b, blk)))
        ql_ref[li + 1] = (q * w).astype(BF16)
        kl_ref[li] = (kk * w).astype(BF16)

    t_i = lax.broadcasted_iota(jnp.int32, (chunk, HG_HEAD), 0)
    s_i = lax.broadcasted_iota(jnp.int32, (chunk, HG_HEAD), 1) % chunk
    eye = t_i == s_i
    lmask = [((t_i // blk) == (s_i // blk)) & ((t_i % blk) >= blk // 2) & ((s_i % blk) < blk // 2)
             for blk in blocks]
    lo = lax.broadcasted_iota(jnp.int32, (chunk, pair), 1) < HG_HEAD
    bd = (lax.broadcasted_iota(jnp.int32, (pair, pair), 0) // HG_HEAD
          == lax.broadcasted_iota(jnp.int32, (pair, pair), 1) // HG_HEAD)
    ones_bd = jnp.where(lax.broadcasted_iota(jnp.int32, (pair, HG_HEAD), 0) // HG_HEAD
                        == lax.broadcasted_iota(jnp.int32, (pair, HG_HEAD), 1) // chunk, 1.0, 0.0).astype(BF16)

    def split_heads(t):
        z = jnp.zeros_like(t)
        return jnp.concatenate([jnp.where(lo, t, z), jnp.where(lo, z, t)], axis=0)

    for c in range(rows // chunk):
        rs = slice(c * chunk, (c + 1) * chunk)
        for pr in range(width // pair):
            ls = slice(pr * pair, (pr + 1) * pair)
            sc = jnp.where(eye, _dot(ql_ref[0, rs, ls], ones_bd), 0.0)
            for li in range(len(blocks)):
                sc = jnp.where(lmask[li], _dot_nt(ql_ref[li + 1, rs, ls], split_heads(kl_ref[li, rs, ls])), sc)
            vpair = vb_ref[rs, ls]
            st = st_ref[pr]
            o = _dot(sc.astype(BF16), split_heads(vpair)) + _dot_nt(qe_ref[rs, ls], st.astype(BF16))
            o_ref[0, rs, ls] = o
            upd = _dot_tn(vpair, kd_ref[rs, ls])
            st_ref[pr] = g_ref[c:c + 1, ls] * st + jnp.where(bd, upd, 0.0)


def _hg_core(q, fz, v, lb_logits, layer):
    bsz, seq, width = q.shape
    blk = min(HG_BLOCK, seq)
    chunk = min(HG_CHUNK, blk)
    nlev = int(math.log2(chunk))
    assert HG_HEAD == 2 * chunk and width % (2 * HG_HEAD) == 0
    tok = pl.BlockSpec((1, blk, width), lambda b, n: (b, n, 0))
    return pl.pallas_call(
        functools.partial(_hg_core_kernel, layer=layer, chunk=chunk),
        out_shape=jax.ShapeDtypeStruct(q.shape, F32),
        grid=(bsz, seq // blk),
        in_specs=[tok, tok, tok, pl.BlockSpec(lb_logits.shape, lambda b, n: (0, 0))],
        out_specs=tok,
        scratch_shapes=[pltpu.VMEM((nlev + 1, blk, width), BF16),
                        pltpu.VMEM((nlev, blk, width), BF16),
                        pltpu.VMEM((blk, width), BF16),
                        pltpu.VMEM((blk, width), BF16),
                        pltpu.VMEM((blk, width), BF16),
                        pltpu.VMEM((blk // chunk, width), F32),
                        pltpu.VMEM((width // (2 * HG_HEAD), 2 * HG_HEAD, 2 * HG_HEAD), F32)],
        compiler_params=_cparams(("parallel", "arbitrary")),
        name="hgrn2_core",
    )(q, fz, v, lb_logits)


def _gate_out_kernel(o_ref, z_ref, h_ref, gn_ref, wo_ref, gp_ref, out_ref, *, head_norm):
    o = o_ref[0]
    if head_norm:
        parts = []
        for j in range(o.shape[-1] // HG_HEAD):
            oj = o[:, j * HG_HEAD:(j + 1) * HG_HEAD]
            parts.append(oj * lax.rsqrt(jnp.mean(oj * oj, axis=-1, keepdims=True) + NORM_EPS))
        o = jnp.concatenate(parts, axis=-1) * gn_ref[...]
    t = o * _silu(z_ref[0])
    r = _dot(t.astype(BF16), wo_ref[...])
    out_ref[0] = h_ref[0] + _rms(r, gp_ref[...])


def _gate_out(o, z, h, g_head, w_out, g_post, head_norm):
    bsz, seq, d = h.shape
    tm = min(ROW_TILE, seq)
    row = lambda t: t.reshape(1, -1)
    vec = pl.BlockSpec((1, d), lambda b, m: (0, 0))
    tok = pl.BlockSpec((1, tm, d), lambda b, m: (b, m, 0))
    return pl.pallas_call(
        functools.partial(_gate_out_kernel, head_norm=head_norm),
        out_shape=jax.ShapeDtypeStruct(h.shape, F32),
        grid=(bsz, seq // tm),
        in_specs=[tok, tok, tok, vec, pl.BlockSpec((d, d), lambda b, m: (0, 0)), vec],
        out_specs=tok,
        compiler_params=_cparams(("parallel", "parallel")),
        name="gate_out",
    )(o, z, h, row(g_head), w_out.astype(BF16), row(g_post))


def _hg_layer(h, g_pre, g_post, w_in, lb_logits, layer, norm_g, w_out):
    d = h.shape[-1]
    q, fz, v, z = _norm_proj(h, g_pre, w_in.astype(BF16), (d, d, d, d), (0,) * 4)
    o = _hg_core(q, fz, v, lb_logits, layer)
    return _gate_out(o, z, h, norm_g, w_out, g_post, True)


def _at_proj_kernel(h_ref, g_ref, w_ref, bias_ref, pos_ref, invf_ref, q_ref, k_ref, v_ref, z_ref,
                    *, qw, kvw):
    y = _rms(h_ref[0], g_ref[...]).astype(BF16)
    ang = pos_ref[0].astype(F32) * invf_ref[...]
    lane = lax.broadcasted_iota(jnp.int32, ang.shape, 1)
    first = (lane % AT_HEAD) < (AT_HEAD // 2)
    low = lane < AT_HEAD
    cosf = jnp.cos(ang)
    sinf = jnp.where(first, -jnp.sin(ang), jnp.sin(ang))

    def rope(t):
        partner = jnp.where(first, pltpu.roll(t, LANES - AT_HEAD // 2, 1), pltpu.roll(t, AT_HEAD // 2, 1))
        return t * cosf + partner * sinf

    def store_dup(ref, j, t):
        r = pltpu.roll(t, AT_HEAD, 1)
        ref[0, :, (2 * j) * LANES:(2 * j + 1) * LANES] = jnp.where(low, t, r)
        ref[0, :, (2 * j + 1) * LANES:(2 * j + 2) * LANES] = jnp.where(low, r, t)

    for j in range(qw // LANES):
        sl = slice(j * LANES, (j + 1) * LANES)
        q_ref[0, :, sl] = rope(_dot(y, w_ref[:, sl]) + bias_ref[:, sl])
    for j in range(kvw // LANES):
        sl = slice(qw + j * LANES, qw + (j + 1) * LANES)
        store_dup(k_ref, j, rope(_dot(y, w_ref[:, sl]) + bias_ref[:, sl]))
    for j in range(kvw // LANES):
        sl = slice(qw + kvw + j * LANES, qw + kvw + (j + 1) * LANES)
        store_dup(v_ref, j, _dot(y, w_ref[:, sl]) + bias_ref[:, sl])
    off = qw + 2 * kvw
    z_ref[0] = _dot(y, w_ref[:, off:off + qw])


def _at_proj(h, g, w_bf16, bias, positions, qw, kvw):
    bsz, seq, d = h.shape
    tm = min(ROW_TILE, seq)
    half = AT_HEAD // 2
    inv_freq = ROPE_THETA ** (-jnp.arange(0, AT_HEAD, 2, dtype=F32) / AT_HEAD)
    invf = jnp.tile(inv_freq, LANES // half).reshape(1, LANES)
    tokspec = lambda w: pl.BlockSpec((1, tm, w), lambda b, m: (b, m, 0))
    const = lambda shp: pl.BlockSpec(shp, lambda b, m: (0, 0))
    return pl.pallas_call(
        functools.partial(_at_proj_kernel, qw=qw, kvw=kvw),
        out_shape=[jax.ShapeDtypeStruct((bsz, seq, qw), F32), jax.ShapeDtypeStruct((bsz, seq, 2 * kvw), F32),
                   jax.ShapeDtypeStruct((bsz, seq, 2 * kvw), F32), jax.ShapeDtypeStruct((bsz, seq, qw), F32)],
        grid=(bsz, seq // tm),
        in_specs=[tokspec(d), const((1, d)), const(w_bf16.shape), const((1, qw + 2 * kvw)),
                  tokspec(1), const((1, LANES))],
        out_specs=[tokspec(qw), tokspec(2 * kvw), tokspec(2 * kvw), tokspec(qw)],
        compiler_params=_cparams(("parallel", "parallel")),
        name="attn_proj",
    )(h, g.reshape(1, d), w_bf16, bias.reshape(1, -1), positions.reshape(bsz, seq, 1), invf)


def _at_core_kernel(q_ref, kc_ref, kp_ref, vc_ref, vp_ref, sink_ref, o_ref):
    blk = q_ref.shape[1]
    n = pl.program_id(1)
    ngroup = kc_ref.shape[2] // LANES
    qi = lax.broadcasted_iota(jnp.int32, (2 * blk, 4 * blk), 0) % blk
    kj = lax.broadcasted_iota(jnp.int32, (2 * blk, 4 * blk), 1) % (2 * blk)
    dist = qi + blk - kj
    mask = (dist >= 0) & (dist < blk) & ((kj >= blk) | (n > 0))
    low = lax.broadcasted_iota(jnp.int32, (2 * blk, LANES), 1) < AT_HEAD
    top = lax.broadcasted_iota(jnp.int32, (2 * blk, 1), 0) < blk
    ones_lo = jnp.where(low, 1.0, 0.0).astype(BF16)
    ones_hi = jnp.where(low, 0.0, 1.0).astype(BF16)
    scale = AT_HEAD ** -0.5

    scores = []
    for g in range(ngroup):
        sl = slice(g * LANES, (g + 1) * LANES)
        kdup = jnp.concatenate([kp_ref[0, :, sl], kc_ref[0, :, sl]], axis=0)
        rhs = jnp.concatenate([jnp.where(low, kdup, 0.0), jnp.where(low, 0.0, kdup)], axis=0).astype(BF16)
        lhs = jnp.concatenate([q_ref[0, :, (2 * g) * LANES:(2 * g + 1) * LANES],
                               q_ref[0, :, (2 * g + 1) * LANES:(2 * g + 2) * LANES]], axis=0).astype(BF16)
        scores.append(_dot_nt(lhs, rhs))

    probs, sink_terms = [], []
    for g in range(ngroup):
        s = jnp.where(mask, scores[g] * scale, -jnp.inf)
        sink_a = jnp.where(top, sink_ref[0:1, 4 * g:4 * g + 1], sink_ref[0:1, 4 * g + 2:4 * g + 3])
        sink_b = jnp.where(top, sink_ref[0:1, 4 * g + 1:4 * g + 2], sink_ref[0:1, 4 * g + 3:4 * g + 4])
        m_a = jnp.maximum(jnp.max(s[:, :2 * blk], axis=-1, keepdims=True), sink_a)
        m_b = jnp.maximum(jnp.max(s[:, 2 * blk:], axis=-1, keepdims=True), sink_b)
        e = jnp.concatenate([jnp.exp(s[:, :2 * blk] - m_a), jnp.exp(s[:, 2 * blk:] - m_b)], axis=1)
        probs.append(e.astype(BF16))
        sink_terms.append(jnp.where(low, jnp.exp(sink_a - m_a), jnp.exp(sink_b - m_b)))

    for g in range(ngroup):
        sl = slice(g * LANES, (g + 1) * LANES)
        vdup = jnp.concatenate([vp_ref[0, :, sl], vc_ref[0, :, sl]], axis=0)
        va = jnp.where(low, vdup, 0.0).astype(BF16)
        vb = jnp.where(low, 0.0, vdup).astype(BF16)
        rhs = jnp.concatenate([jnp.concatenate([va, ones_lo], axis=1),
                               jnp.concatenate([vb, ones_hi], axis=1)], axis=0)
        out = _dot(probs[g], rhs)
        o = out[:, :LANES] / (out[:, LANES:] + sink_terms[g])
        o_ref[0, :, (2 * g) * LANES:(2 * g + 1) * LANES] = o[:blk]
        o_ref[0, :, (2 * g + 1) * LANES:(2 * g + 2) * LANES] = o[blk:]


def _at_core(q, k_dup, v_dup, sinks):
    bsz, seq, qw = q.shape
    kvw = k_dup.shape[-1]
    blk = AT_BLOCK
    nq = qw // AT_HEAD
    assert nq == 4 * (kvw // LANES) and qw == 2 * kvw
    cur = lambda w: pl.BlockSpec((1, blk, w), lambda b, n: (b, n, 0))
    prev = lambda w: pl.BlockSpec((1, blk, w), lambda b, n: (b, jnp.maximum(n - 1, 0), 0))
    return pl.pallas_call(
        _at_core_kernel,
        out_shape=jax.ShapeDtypeStruct(q.shape, F32),
        grid=(bsz, seq // blk),
        in_specs=[cur(qw), cur(kvw), prev(kvw), cur(kvw), prev(kvw),
                  pl.BlockSpec((1, nq), lambda b, n: (0, 0))],
        out_specs=cur(qw),
        compiler_params=_cparams(("parallel", "arbitrary")),
        name="attn_core",
    )(q, k_dup, k_dup, v_dup, v_dup, sinks.reshape(1, nq))


def _at_layer(h, positions, g_pre, g_post, w_in, b_in, sinks, w_out):
    d = h.shape[-1]
    qw = w_out.shape[0]
    kvw = (b_in.shape[0] - qw) // 2
    q, k, v, z = _at_proj(h, g_pre, w_in.astype(BF16), b_in, positions, qw, kvw)
    o = _at_core(q, k, v, sinks)
    return _gate_out(o, z, h, jnp.ones((d,), F32), w_out, g_post, False)


def kernel(x, positions, norm_pre, norm_post, s5_w_in, s5_lambda_re, s5_lambda_im, s5_log_dt, s5_b_re, s5_b_im, s5_c_re, s5_c_im, s5_d, s5_w_glu, s5_b_glu, s5_w_out, hg_w_in, hg_lb_logits, hg_norm, hg_w_out, at_w_in, at_b_in, at_sinks, at_w_out):
    depth = norm_pre.shape[0]
    h = x
    for i in range(depth):
        kind, j = i % 3, i // 3
        if kind == 0:
            h = _s5_layer(h, norm_pre[i], norm_post[i], s5_w_in[j], s5_lambda_re[j], s5_lambda_im[j],
                          s5_log_dt[j], s5_b_re[j], s5_b_im[j], s5_c_re[j], s5_c_im[j], s5_d[j],
                          s5_w_glu[j], s5_b_glu[j], s5_w_out[j])
        elif kind == 1:
            h = _hg_layer(h, norm_pre[i], norm_post[i], hg_w_in[j], hg_lb_logits, i, hg_norm[j], hg_w_out[j])
        else:
            h = _at_layer(h, positions, norm_pre[i], norm_post[i], at_w_in[j], at_b_in[j], at_sinks[j],
                          at_w_out[j])
    return h
```

```python
import functools
import math

import jax
import jax.numpy as jnp
from jax import lax
from jax.experimental import pallas as pl
from jax.experimental.pallas import tpu as pltpu

F32 = jnp.float32
BF16 = jnp.bfloat16

NORM_EPS = 1e-6
LANES = 128
SUBLANES = 8
VMEM_LIMIT = 56 * 1024 * 1024

S5_GROUP = 16
S5_STATE = 64
S5_TILE_GROUPS = 16
S5_TILE_CH = S5_TILE_GROUPS * S5_GROUP
S5_TILE_ST = S5_TILE_GROUPS * S5_STATE
S5_STEPS = 64
HG_HEAD = 128
HG_CHUNK = 64
HG_BLOCK = 256
AT_HEAD = 64
AT_KV_HEADS = 4
AT_BLOCK = 128
ROPE_THETA = 10000.0

ROW_TILE = 512


def _cparams(sem):
    return pltpu.CompilerParams(dimension_semantics=sem, vmem_limit_bytes=VMEM_LIMIT)


def _rms(x, g):
    return x * lax.rsqrt(jnp.mean(x * x, axis=-1, keepdims=True) + NORM_EPS) * g


def _sigmoid(x):
    return 1.0 / (1.0 + jnp.exp(-x))


def _silu(x):
    return x * _sigmoid(x)


def _gelu_tanh(x):
    c = math.sqrt(2.0 / math.pi)
    return 0.5 * x * (1.0 + jnp.tanh(c * (x + 0.044715 * (x * x * x))))


def _dot(a, b):
    return jnp.dot(a, b, preferred_element_type=F32)


def _dot_nt(a, b):
    return lax.dot_general(a, b, (((1,), (1,)), ((), ())), preferred_element_type=F32)


def _dot_tn(a, b):
    return lax.dot_general(a, b, (((0,), (0,)), ((), ())), preferred_element_type=F32)


def _chunk_row_groups(rows, steps):
    for base in range(0, rows, SUBLANES * steps):
        for r in range(SUBLANES):
            for k0 in range(0, steps, SUBLANES):
                yield base + r * steps + k0, base + k0 * SUBLANES + r


def _norm_proj_kernel(h_ref, g_ref, w_ref, *out_refs, widths, slab_steps):
    y = _rms(h_ref[0], g_ref[...]).astype(BF16)
    off = 0
    for o_ref, width, steps in zip(out_refs, widths, slab_steps):
        r = _dot(y, w_ref[:, off:off + width])
        if steps:
            for nat, cm in _chunk_row_groups(r.shape[0], steps):
                for j in range(width // LANES):
                    o_ref[0, j, pl.ds(cm, SUBLANES, stride=SUBLANES), :] = (
                        r[nat:nat + SUBLANES, j * LANES:(j + 1) * LANES])
        else:
            o_ref[0] = r
        off += width


def _norm_proj(h, g, w_bf16, widths, slab_steps):
    bsz, seq, d = h.shape
    tm = min(ROW_TILE, seq)
    out_shape, out_specs = [], []
    for width, as_slab in zip(widths, slab_steps):
        if as_slab:
            assert tm % (SUBLANES * as_slab) == 0
            ns = width // LANES
            out_shape.append(jax.ShapeDtypeStruct((bsz, ns, seq, LANES), F32))
            out_specs.append(pl.BlockSpec((1, ns, tm, LANES), lambda b, m: (b, 0, m, 0)))
        else:
            out_shape.append(jax.ShapeDtypeStruct((bsz, seq, width), F32))
            out_specs.append(pl.BlockSpec((1, tm, width), lambda b, m: (b, m, 0)))
    return pl.pallas_call(
        functools.partial(_norm_proj_kernel, widths=tuple(widths), slab_steps=tuple(slab_steps)),
        out_shape=out_shape,
        grid=(bsz, seq // tm),
        in_specs=[pl.BlockSpec((1, tm, d), lambda b, m: (b, m, 0)),
                  pl.BlockSpec((1, d), lambda b, m: (0, 0)),
                  pl.BlockSpec(w_bf16.shape, lambda b, m: (0, 0))],
        out_specs=out_specs,
        compiler_params=_cparams(("parallel", "parallel")),
        name="norm_proj",
    )(h, g.reshape(1, d), w_bf16)


def _s5_prep_kernel(lr_ref, li_ref, ldt_ref, br_ref, bi_ref, ar_ref, ai_ref, bbr_ref, bbi_ref):
    lr = lr_ref[...]
    li = li_ref[...]
    dt = jnp.exp(ldt_ref[...])
    mag = jnp.exp(lr * dt)
    ar = mag * jnp.cos(li * dt)
    ai = mag * jnp.sin(li * dt)
    den = lr * lr + li * li
    qr = ((ar - 1.0) * lr + ai * li) / den
    qi = (ai * lr - (ar - 1.0) * li) / den
    br = br_ref[...]
    bi = bi_ref[...]
    ar_ref[...] = ar
    ai_ref[...] = ai
    bbr_ref[...] = qr * br - qi * bi
    bbi_ref[...] = qr * bi + qi * br


def _s5_prep(lam_re, lam_im, log_dt, b_re, b_im):
    g, p, hh = b_re.shape
    rep = lambda t: jnp.repeat(t, hh, axis=1)
    shp = jax.ShapeDtypeStruct((g, p * hh), F32)
    ar, ai, bbr, bbi = pl.pallas_call(
        _s5_prep_kernel, out_shape=[shp, shp, shp, shp], name="s5_prep",
    )(rep(lam_re), rep(lam_im), log_dt.reshape(g, 1), b_re.reshape(g, p * hh), b_im.reshape(g, p * hh))
    return ar[:, ::hh], ai[:, ::hh], bbr.reshape(g, p, hh), bbi.reshape(g, p, hh)


def _s5_core_kernel(u_ref, wb_ref, wc_ref, a_ref, d_ref, y_ref, x_ref, s_ref, carry_ref, *, steps):
    nst = S5_TILE_ST
    nslab = S5_TILE_CH // LANES
    ntile = wb_ref.shape[0]

    @pl.when(pl.program_id(1) == 0)
    def _():
        carry_ref[...] = jnp.zeros_like(carry_ref)

    def load_u(i):
        return jnp.concatenate([u_ref[0, i * nslab + j] for j in range(nslab)], axis=-1)

    def project_in(i):
        x_ref[i % 2] = _dot(load_u(i).astype(BF16), wb_ref[i])

    def scan(i):
        slot = i % 2
        ar1 = a_ref[i, 0:1, :]
        ai1 = a_ref[i, 1:2, :]
        ar = jnp.broadcast_to(ar1, (SUBLANES, nst))
        ai = jnp.broadcast_to(ai1, (SUBLANES, nst))

        def advance(k, sr, si):
            xr = x_ref[slot, k * SUBLANES:(k + 1) * SUBLANES, 0:nst]
            xi = x_ref[slot, k * SUBLANES:(k + 1) * SUBLANES, nst:2 * nst]
            return ar * sr - ai * si + xr, ar * si + ai * sr + xi

        er = jnp.zeros((SUBLANES, nst), F32)
        ei = jnp.zeros((SUBLANES, nst), F32)
        for k in range(steps):
            er, ei = advance(k, er, ei)

        pr, pi = ar1, ai1
        for _ in range(int(math.log2(steps))):
            pr, pi = pr * pr - pi * pi, 2.0 * pr * pi

        rid = lax.broadcasted_iota(jnp.int32, (SUBLANES, nst), 0)
        cr = carry_ref[i, 0:1, :]
        ci = carry_ref[i, 1:2, :]
        sr = jnp.zeros((SUBLANES, nst), F32)
        si = jnp.zeros((SUBLANES, nst), F32)
        for r in range(SUBLANES):
            sr = jnp.where(rid == r, jnp.broadcast_to(cr, (SUBLANES, nst)), sr)
            si = jnp.where(rid == r, jnp.broadcast_to(ci, (SUBLANES, nst)), si)
            cr, ci = pr * cr - pi * ci + er[r:r + 1, :], pr * ci + pi * cr + ei[r:r + 1, :]
        carry_ref[i, 0:1, :] = cr
        carry_ref[i, 1:2, :] = ci

        for k in range(0, steps, 2):
            r0, i0 = advance(k, sr, si)
            sr, si = advance(k + 1, r0, i0)
            rows2 = slice(k * SUBLANES, (k + 2) * SUBLANES)
            s_ref[slot, rows2, 0:nst] = jnp.concatenate([r0, sr], axis=0).astype(BF16)
            s_ref[slot, rows2, nst:2 * nst] = jnp.concatenate([i0, si], axis=0).astype(BF16)

    def project_out(i):
        y = _dot(s_ref[i % 2], wc_ref[i]) + d_ref[i] * load_u(i)
        for j in range(nslab):
            y_ref[0, i * nslab + j] = y[:, j * LANES:(j + 1) * LANES]

    project_in(0)
    for i in range(ntile):
        if i + 1 < ntile:
            project_in(i + 1)
        scan(i)
        project_out(i)


def _s5_core(u_slab, wb, wc, a_tiles, d_tiles, steps):
    bsz, nslab, seq, _ = u_slab.shape
    ntile = wb.shape[0]
    rows = SUBLANES * steps
    const = lambda shp: pl.BlockSpec(shp, lambda b, n: (0,) * len(shp))
    tok = pl.BlockSpec((1, nslab, rows, LANES), lambda b, n: (b, 0, n, 0))
    return pl.pallas_call(
        functools.partial(_s5_core_kernel, steps=steps),
        out_shape=jax.ShapeDtypeStruct(u_slab.shape, F32),
        grid=(bsz, seq // rows),
        in_specs=[tok, const(wb.shape), const(wc.shape), const(a_tiles.shape), const(d_tiles.shape)],
        out_specs=tok,
        scratch_shapes=[pltpu.VMEM((2, rows, 2 * S5_TILE_ST), F32),
                        pltpu.VMEM((2, rows, 2 * S5_TILE_ST), BF16),
                        pltpu.VMEM((ntile, 2, S5_TILE_ST), F32)],
        compiler_params=_cparams(("parallel", "arbitrary")),
        name="s5_core",
    )(u_slab, wb, wc, a_tiles, d_tiles)


def _s5_post_kernel(y_ref, z_ref, h_ref, wg_ref, bg_ref, wo_ref, gp_ref, o_ref, *, steps):
    nslab = y_ref.shape[1]
    groups = list(_chunk_row_groups(y_ref.shape[2], steps))
    y = jnp.concatenate(
        [jnp.concatenate([y_ref[0, j, pl.ds(cm, SUBLANES, stride=SUBLANES), :] for _, cm in groups], axis=0)
         for j in range(nslab)], axis=-1)
    y = _gelu_tanh(y)
    y = y * _sigmoid(_dot(y.astype(BF16), wg_ref[...]) + bg_ref[...])
    t = y * _silu(z_ref[0])
    o = _dot(t.astype(BF16), wo_ref[...])
    o_ref[0] = h_ref[0] + _rms(o, gp_ref[...])


def _s5_post(y_slab, z, h, w_glu, b_glu, w_out, g_post, steps):
    bsz, seq, d = h.shape
    nslab = y_slab.shape[1]
    tm = min(ROW_TILE, seq)
    assert tm % (SUBLANES * steps) == 0
    row = lambda t: t.reshape(1, -1)
    vec = pl.BlockSpec((1, d), lambda b, m: (0, 0))
    mat = pl.BlockSpec((d, d), lambda b, m: (0, 0))
    slab = pl.BlockSpec((1, nslab, tm, LANES), lambda b, m: (b, 0, m, 0))
    tok = pl.BlockSpec((1, tm, d), lambda b, m: (b, m, 0))
    return pl.pallas_call(
        functools.partial(_s5_post_kernel, steps=steps),
        out_shape=jax.ShapeDtypeStruct(h.shape, F32),
        grid=(bsz, seq // tm),
        in_specs=[slab, tok, tok, mat, vec, mat, vec],
        out_specs=tok,
        compiler_params=_cparams(("parallel", "parallel")),
        name="s5_post",
    )(y_slab, z, h, w_glu.astype(BF16), row(b_glu), w_out.astype(BF16), row(g_post))


def _s5_layer(h, g_pre, g_post, w_in, lam_re, lam_im, log_dt, b_re, b_im, c_re, c_im, d_skip,
              w_glu, b_glu, w_out):
    bsz, seq, d = h.shape
    ngroup = lam_re.shape[0]
    ntile = ngroup // S5_TILE_GROUPS
    ar, ai, bbr, bbi = _s5_prep(lam_re, lam_im, log_dt, b_re, b_im)
    eye = jnp.eye(S5_TILE_GROUPS, dtype=F32)

    def in_tile(bb):
        t = bb.reshape(ntile, S5_TILE_GROUPS, S5_STATE, S5_GROUP)
        return jnp.einsum('igph,gk->ighkp', t, eye).reshape(ntile, S5_TILE_CH, S5_TILE_ST)

    def out_tile(cc):
        t = cc.reshape(ntile, S5_TILE_GROUPS, S5_GROUP, S5_STATE)
        return jnp.einsum('ighp,gk->ikpgh', t, eye).reshape(ntile, S5_TILE_ST, S5_TILE_CH)

    wb = jnp.concatenate([in_tile(bbr), in_tile(bbi)], axis=2).astype(BF16)
    wc = jnp.concatenate([out_tile(c_re), -out_tile(c_im)], axis=1).astype(BF16)
    a_tiles = jnp.stack([ar.reshape(ntile, S5_TILE_ST), ai.reshape(ntile, S5_TILE_ST)], axis=1)
    d_tiles = d_skip.reshape(ntile, 1, S5_TILE_CH)

    steps = min(S5_STEPS, seq // SUBLANES)
    u_slab, z = _norm_proj(h, g_pre, w_in.astype(BF16), (d, d), (steps, 0))
    y_slab = _s5_core(u_slab, wb, wc, a_tiles, d_tiles, steps)
    return _s5_post(y_slab, z, h, w_glu, b_glu, w_out, g_post, steps)


def _chunk_cumsum(x, chunk):
    pos = lax.broadcasted_iota(jnp.int32, x.shape, 0) % chunk
    sh = 1
    while sh < chunk:
        x = x + jnp.where(pos >= sh, pltpu.roll(x, sh, 0), 0.0)
        sh *= 2
    return x


def _block_mid_rows(x, blk):
    rows, w = x.shape
    half = blk // 2
    if blk >= SUBLANES:
        x3 = x.reshape(rows // blk, blk, w)
        return jnp.broadcast_to(x3[:, half - 1:half, :], (rows // blk, blk, w)).reshape(rows, w)
    pos = lax.broadcasted_iota(jnp.int32, (rows, w), 0) % blk
    out = x
    for p in range(blk):
        d = p - (half - 1)
        if d != 0:
            out = jnp.where(pos == p, pltpu.roll(x, d % rows, 0), out)
    return out


def _hg_core_kernel(q_ref, f_ref, v_ref, lbl_ref, o_ref, ql_ref, kl_ref, qe_ref, kd_ref, vb_ref, g_ref, st_ref,
                    *, layer, chunk):
    rows, width = q_ref.shape[1], q_ref.shape[2]
    pair = 2 * HG_HEAD
    blocks = [2 ** i for i in range(1, int(math.log2(chunk)) + 1)]

    @pl.when(pl.program_id(1) == 0)
    def _():
        st_ref[...] = jnp.zeros_like(st_ref)

    lg = lbl_ref[...]
    ex = jnp.exp(lg - jnp.max(lg, axis=0, keepdims=True))
    p = ex / jnp.sum(ex, axis=0, keepdims=True)
    lb = jnp.zeros((1, width), F32)
    for j in range(1, layer + 1):
        lb = lb + p[j:j + 1, :]

    q = q_ref[0]
    f = lb + (1.0 - lb) * _sigmoid(f_ref[0])
    kk = 1.0 - f
    b = _chunk_cumsum(jnp.log(f), chunk)
    b3 = b.reshape(rows // chunk, chunk, width)
    b_last = b3[:, chunk - 1:chunk, :]
    g_ref[...] = jnp.exp(b_last.reshape(rows // chunk, width))
    qe_ref[...] = (q * jnp.exp(b)).astype(BF16)
    kd_ref[...] = (kk * jnp.exp(jnp.broadcast_to(b_last, b3.shape).reshape(rows, width) - b)).astype(BF16)
    vb_ref[...] = v_ref[0].astype(BF16)
    ql_ref[0] = (q * kk).astype(BF16)
    for li, blk in enumerate(blocks):
        w = jnp.exp(-jnp.abs(b - _block_mid_rows(b, blk)))
        ql_ref[li + 1] = (q * w).astype(BF16)
        kl_ref[li] = (kk * w).astype(BF16)

    t_i = lax.broadcasted_iota(jnp.int32, (chunk, HG_HEAD), 0)
    s_i = lax.broadcasted_iota(jnp.int32, (chunk, HG_HEAD), 1) % chunk
    eye = t_i == s_i
    lmask = [((t_i // blk) == (s_i // blk)) & ((t_i % blk) >= blk // 2) & ((s_i % blk) < blk // 2)
             for blk in blocks]
    lo = lax.broadcasted_iota(jnp.int32, (chunk, pair), 1) < HG_HEAD
    bd = (lax.broadcasted_iota(jnp.int32, (pair, pair), 0) // HG_HEAD
          == lax.broadcasted_iota(jnp.int32, (pair, pair), 1) // HG_HEAD)
    ones_bd = jnp.where(lax.broadcasted_iota(jnp.int32, (pair, HG_HEAD), 0) // HG_HEAD
                        == lax.broadcasted_iota(jnp.int32, (pair, HG_HEAD), 1) // chunk, 1.0, 0.0).astype(BF16)

    def split_heads(t):
        z = jnp.zeros_like(t)
        return jnp.concatenate([jnp.where(lo, t, z), jnp.where(lo, z, t)], axis=0)

    for c in range(rows // chunk):
        rs = slice(c * chunk, (c + 1) * chunk)
        for pr in range(width // pair):
            ls = slice(pr * pair, (pr + 1) * pair)
            sc = jnp.where(eye, _dot(ql_ref[0, rs, ls], ones_bd), 0.0)
            for li in range(len(blocks)):
                sc = jnp.where(lmask[li], _dot_nt(ql_ref[li + 1, rs, ls], split_heads(kl_ref[li, rs, ls])), sc)
            vpair = vb_ref[rs, ls]
            st = st_ref[pr]
            o = _dot(sc.astype(BF16), split_heads(vpair)) + _dot_nt(qe_ref[rs, ls], st.astype(BF16))
            o_ref[0, rs, ls] = o
            upd = _dot_tn(vpair, kd_ref[rs, ls])
            st_ref[pr] = g_ref[c:c + 1, ls] * st + jnp.where(bd, upd, 0.0)


def _hg_core(q, fz, v, lb_logits, layer):
    bsz, seq, width = q.shape
    blk = min(HG_BLOCK, seq)
    chunk = min(HG_CHUNK, blk)
    nlev = int(math.log2(chunk))
    assert HG_HEAD == 2 * chunk and width % (2 * HG_HEAD) == 0
    tok = pl.BlockSpec((1, blk, width), lambda b, n: (b, n, 0))
    return pl.pallas_call(
        functools.partial(_hg_core_kernel, layer=layer, chunk=chunk),
        out_shape=jax.ShapeDtypeStruct(q.shape, F32),
        grid=(bsz, seq // blk),
        in_specs=[tok, tok, tok, pl.BlockSpec(lb_logits.shape, lambda b, n: (0, 0))],
        out_specs=tok,
        scratch_shapes=[pltpu.VMEM((nlev + 1, blk, width), BF16),
                        pltpu.VMEM((nlev, blk, width), BF16),
                        pltpu.VMEM((blk, width), BF16),
                        pltpu.VMEM((blk, width), BF16),
                        pltpu.VMEM((blk, width), BF16),
                        pltpu.VMEM((blk // chunk, width), F32),
                        pltpu.VMEM((width // (2 * HG_HEAD), 2 * HG_HEAD, 2 * HG_HEAD), F32)],
        compiler_params=_cparams(("parallel", "arbitrary")),
        name="hgrn2_core",
    )(q, fz, v, lb_logits)


def _gate_out_kernel(o_ref, z_ref, h_ref, gn_ref, wo_ref, gp_ref, out_ref, *, head_norm):
    o = o_ref[0]
    if head_norm:
        parts = []
        for j in range(o.shape[-1] // HG_HEAD):
            oj = o[:, j * HG_HEAD:(j + 1) * HG_HEAD]
            parts.append(oj * lax.rsqrt(jnp.mean(oj * oj, axis=-1, keepdims=True) + NORM_EPS))
        o = jnp.concatenate(parts, axis=-1) * gn_ref[...]
    t = o * _silu(z_ref[0])
    r = _dot(t.astype(BF16), wo_ref[...])
    out_ref[0] = h_ref[0] + _rms(r, gp_ref[...])


def _gate_out(o, z, h, g_head, w_out, g_post, head_norm):
    bsz, seq, d = h.shape
    tm = min(ROW_TILE, seq)
    row = lambda t: t.reshape(1, -1)
    vec = pl.BlockSpec((1, d), lambda b, m: (0, 0))
    tok = pl.BlockSpec((1, tm, d), lambda b, m: (b, m, 0))
    return pl.pallas_call(
        functools.partial(_gate_out_kernel, head_norm=head_norm),
        out_shape=jax.ShapeDtypeStruct(h.shape, F32),
        grid=(bsz, seq // tm),
        in_specs=[tok, tok, tok, vec, pl.BlockSpec((d, d), lambda b, m: (0, 0)), vec],
        out_specs=tok,
        compiler_params=_cparams(("parallel", "parallel")),
        name="gate_out",
    )(o, z, h, row(g_head), w_out.astype(BF16), row(g_post))


def _hg_layer(h, g_pre, g_post, w_in, lb_logits, layer, norm_g, w_out):
    d = h.shape[-1]
    q, fz, v, z = _norm_proj(h, g_pre, w_in.astype(BF16), (d, d, d, d), (0,) * 4)
    o = _hg_core(q, fz, v, lb_logits, layer)
    return _gate_out(o, z, h, norm_g, w_out, g_post, True)


def _at_proj_kernel(h_ref, g_ref, w_ref, bias_ref, pos_ref, invf_ref, q_ref, k_ref, v_ref, z_ref,
                    *, qw, kvw):
    y = _rms(h_ref[0], g_ref[...]).astype(BF16)
    ang = pos_ref[0].astype(F32) * invf_ref[...]
    lane = lax.broadcasted_iota(jnp.int32, ang.shape, 1)
    first = (lane % AT_HEAD) < (AT_HEAD // 2)
    low = lane < AT_HEAD
    cosf = jnp.cos(ang)
    sinf = jnp.where(first, -jnp.sin(ang), jnp.sin(ang))

    def rope(t):
        partner = jnp.where(first, pltpu.roll(t, LANES - AT_HEAD // 2, 1), pltpu.roll(t, AT_HEAD // 2, 1))
        return t * cosf + partner * sinf

    def store_dup(ref, j, t):
        r = pltpu.roll(t, AT_HEAD, 1)
        ref[0, :, (2 * j) * LANES:(2 * j + 1) * LANES] = jnp.where(low, t, r)
        ref[0, :, (2 * j + 1) * LANES:(2 * j + 2) * LANES] = jnp.where(low, r, t)

    for j in range(qw // LANES):
        sl = slice(j * LANES, (j + 1) * LANES)
        q_ref[0, :, sl] = rope(_dot(y, w_ref[:, sl]) + bias_ref[:, sl])
    for j in range(kvw // LANES):
        sl = slice(qw + j * LANES, qw + (j + 1) * LANES)
        store_dup(k_ref, j, rope(_dot(y, w_ref[:, sl]) + bias_ref[:, sl]))
    for j in range(kvw // LANES):
        sl = slice(qw + kvw + j * LANES, qw + kvw + (j + 1) * LANES)
        store_dup(v_ref, j, _dot(y, w_ref[:, sl]) + bias_ref[:, sl])
    off = qw + 2 * kvw
    z_ref[0] = _dot(y, w_ref[:, off:off + qw])


def _at_proj(h, g, w_bf16, bias, positions, qw, kvw):
    bsz, seq, d = h.shape
    tm = min(ROW_TILE, seq)
    half = AT_HEAD // 2
    inv_freq = ROPE_THETA ** (-jnp.arange(0, AT_HEAD, 2, dtype=F32) / AT_HEAD)
    invf = jnp.tile(inv_freq, LANES // half).reshape(1, LANES)
    tokspec = lambda w: pl.BlockSpec((1, tm, w), lambda b, m: (b, m, 0))
    const = lambda shp: pl.BlockSpec(shp, lambda b, m: (0, 0))
    return pl.pallas_call(
        functools.partial(_at_proj_kernel, qw=qw, kvw=kvw),
        out_shape=[jax.ShapeDtypeStruct((bsz, seq, qw), F32), jax.ShapeDtypeStruct((bsz, seq, 2 * kvw), F32),
                   jax.ShapeDtypeStruct((bsz, seq, 2 * kvw), F32), jax.ShapeDtypeStruct((bsz, seq, qw), F32)],
        grid=(bsz, seq // tm),
        in_specs=[tokspec(d), const((1, d)), const(w_bf16.shape), const((1, qw + 2 * kvw)),
                  tokspec(1), const((1, LANES))],
        out_specs=[tokspec(qw), tokspec(2 * kvw), tokspec(2 * kvw), tokspec(qw)],
        compiler_params=_cparams(("parallel", "parallel")),
        name="attn_proj",
    )(h, g.reshape(1, d), w_bf16, bias.reshape(1, -1), positions.reshape(bsz, seq, 1), invf)


def _at_core_kernel(q_ref, kc_ref, kp_ref, vc_ref, vp_ref, sink_ref, o_ref):
    blk = q_ref.shape[1]
    n = pl.program_id(1)
    ngroup = kc_ref.shape[2] // LANES
    qi = lax.broadcasted_iota(jnp.int32, (2 * blk, 4 * blk), 0) % blk
    kj = lax.broadcasted_iota(jnp.int32, (2 * blk, 4 * blk), 1) % (2 * blk)
    dist = qi + blk - kj
    mask = (dist >= 0) & (dist < blk) & ((kj >= blk) | (n > 0))
    low = lax.broadcasted_iota(jnp.int32, (2 * blk, LANES), 1) < AT_HEAD
    top = lax.broadcasted_iota(jnp.int32, (2 * blk, 1), 0) < blk
    ones_lo = jnp.where(low, 1.0, 0.0).astype(BF16)
    ones_hi = jnp.where(low, 0.0, 1.0).astype(BF16)
    scale = AT_HEAD ** -0.5

    scores = []
    for g in range(ngroup):
        sl = slice(g * LANES, (g + 1) * LANES)
        kdup = jnp.concatenate([kp_ref[0, :, sl], kc_ref[0, :, sl]], axis=0)
        rhs = jnp.concatenate([jnp.where(low, kdup, 0.0), jnp.where(low, 0.0, kdup)], axis=0).astype(BF16)
        lhs = jnp.concatenate([q_ref[0, :, (2 * g) * LANES:(2 * g + 1) * LANES],
                               q_ref[0, :, (2 * g + 1) * LANES:(2 * g + 2) * LANES]], axis=0).astype(BF16)
        scores.append(_dot_nt(lhs, rhs))

    probs, sink_terms = [], []
    for g in range(ngroup):
        s = jnp.where(mask, scores[g] * scale, -jnp.inf)
        sink_a = jnp.where(top, sink_ref[0:1, 4 * g:4 * g + 1], sink_ref[0:1, 4 * g + 2:4 * g + 3])
        sink_b = jnp.where(top, sink_ref[0:1, 4 * g + 1:4 * g + 2], sink_ref[0:1, 4 * g + 3:4 * g + 4])
        m_a = jnp.maximum(jnp.max(s[:, :2 * blk], axis=-1, keepdims=True), sink_a)
        m_b = jnp.maximum(jnp.max(s[:, 2 * blk:], axis=-1, keepdims=True), sink_b)
        e = jnp.concatenate([jnp.exp(s[:, :2 * blk] - m_a), jnp.exp(s[:, 2 * blk:] - m_b)], axis=1)
        probs.append(e.astype(BF16))
        sink_terms.append(jnp.where(low, jnp.exp(sink_a - m_a), jnp.exp(sink_b - m_b)))

    for g in range(ngroup):
        sl = slice(g * LANES, (g + 1) * LANES)
        vdup = jnp.concatenate([vp_ref[0, :, sl], vc_ref[0, :, sl]], axis=0)
        va = jnp.where(low, vdup, 0.0).astype(BF16)
        vb = jnp.where(low, 0.0, vdup).astype(BF16)
        rhs = jnp.concatenate([jnp.concatenate([va, ones_lo], axis=1),
                               jnp.concatenate([vb, ones_hi], axis=1)], axis=0)
        out = _dot(probs[g], rhs)
        o = out[:, :LANES] / (out[:, LANES:] + sink_terms[g])
        o_ref[0, :, (2 * g) * LANES:(2 * g + 1) * LANES] = o[:blk]
        o_ref[0, :, (2 * g + 1) * LANES:(2 * g + 2) * LANES] = o[blk:]


def _at_core(q, k_dup, v_dup, sinks):
    bsz, seq, qw = q.shape
    kvw = k_dup.shape[-1]
    blk = AT_BLOCK
    nq = qw // AT_HEAD
    assert nq == 4 * (kvw // LANES) and qw == 2 * kvw
    cur = lambda w: pl.BlockSpec((1, blk, w), lambda b, n: (b, n, 0))
    prev = lambda w: pl.BlockSpec((1, blk, w), lambda b, n: (b, jnp.maximum(n - 1, 0), 0))
    return pl.pallas_call(
        _at_core_kernel,
        out_shape=jax.ShapeDtypeStruct(q.shape, F32),
        grid=(bsz, seq // blk),
        in_specs=[cur(qw), cur(kvw), prev(kvw), cur(kvw), prev(kvw),
                  pl.BlockSpec((1, nq), lambda b, n: (0, 0))],
        out_specs=cur(qw),
        compiler_params=_cparams(("parallel", "arbitrary")),
        name="attn_core",
    )(q, k_dup, k_dup, v_dup, v_dup, sinks.reshape(1, nq))


def _at_layer(h, positions, g_pre, g_post, w_in, b_in, sinks, w_out):
    d = h.shape[-1]
    qw = w_out.shape[0]
    kvw = (b_in.shape[0] - qw) // 2
    q, k, v, z = _at_proj(h, g_pre, w_in.astype(BF16), b_in, positions, qw, kvw)
    o = _at_core(q, k, v, sinks)
    return _gate_out(o, z, h, jnp.ones((d,), F32), w_out, g_post, False)


def kernel(x, positions, norm_pre, norm_post, s5_w_in, s5_lambda_re, s5_lambda_im, s5_log_dt, s5_b_re, s5_b_im, s5_c_re, s5_c_im, s5_d, s5_w_glu, s5_b_glu, s5_w_out, hg_w_in, hg_lb_logits, hg_norm, hg_w_out, at_w_in, at_b_in, at_sinks, at_w_out):
    depth = norm_pre.shape[0]
    h = x
    for i in range(depth):
        kind, j = i % 3, i // 3
        if kind == 0:
            h = _s5_layer(h, norm_pre[i], norm_post[i], s5_w_in[j], s5_lambda_re[j], s5_lambda_im[j],
                          s5_log_dt[j], s5_b_re[j], s5_b_im[j], s5_c_re[j], s5_c_im[j], s5_d[j],
                          s5_w_glu[j], s5_b_glu[j], s5_w_out[j])
        elif kind == 1:
            h = _hg_layer(h, norm_pre[i], norm_post[i], hg_w_in[j], hg_lb_logits, i, hg_norm[j], hg_w_out[j])
        else:
            h = _at_layer(h, positions, norm_pre[i], norm_post[i], at_w_in[j], at_b_in[j], at_sinks[j],
                          at_w_out[j])
    return h
```

```python
import functools
import math

import jax
import jax.numpy as jnp
from jax import lax
from jax.experimental import pallas as pl
from jax.experimental.pallas import tpu as pltpu

F32 = jnp.float32
BF16 = jnp.bfloat16

NORM_EPS = 1e-6
LANES = 128
SUBLANES = 8
VMEM_LIMIT = 56 * 1024 * 1024

S5_GROUP = 16
S5_STATE = 64
S5_TILE_GROUPS = 16
S5_TILE_CH = S5_TILE_GROUPS * S5_GROUP
S5_TILE_ST = S5_TILE_GROUPS * S5_STATE
S5_STEPS = 64
HG_HEAD = 128
HG_CHUNK = 64
HG_BLOCK = 256
AT_HEAD = 64
AT_KV_HEADS = 4
AT_BLOCK = 128
ROPE_THETA = 10000.0

ROW_TILE = 512


def _cparams(sem):
    return pltpu.CompilerParams(dimension_semantics=sem, vmem_limit_bytes=VMEM_LIMIT)


def _rms(x, g):
    return x * lax.rsqrt(jnp.mean(x * x, axis=-1, keepdims=True) + NORM_EPS) * g


def _sigmoid(x):
    return 1.0 / (1.0 + jnp.exp(-x))


def _silu(x):
    return x * _sigmoid(x)


def _gelu_tanh(x):
    c = math.sqrt(2.0 / math.pi)
    return 0.5 * x * (1.0 + jnp.tanh(c * (x + 0.044715 * (x * x * x))))


def _dot(a, b):
    return jnp.dot(a, b, preferred_element_type=F32)


def _dot_nt(a, b):
    return lax.dot_general(a, b, (((1,), (1,)), ((), ())), preferred_element_type=F32)


def _dot_tn(a, b):
    return lax.dot_general(a, b, (((0,), (0,)), ((), ())), preferred_element_type=F32)


def _chunk_row_groups(rows, steps):
    for base in range(0, rows, SUBLANES * steps):
        for r in range(SUBLANES):
            for k0 in range(0, steps, SUBLANES):
                yield base + r * steps + k0, base + k0 * SUBLANES + r


def _norm_proj_kernel(h_ref, g_ref, w_ref, *out_refs, widths, slab_steps):
    y = _rms(h_ref[0], g_ref[...]).astype(BF16)
    off = 0
    for o_ref, width, steps in zip(out_refs, widths, slab_steps):
        r = _dot(y, w_ref[:, off:off + width])
        if steps:
            for nat, cm in _chunk_row_groups(r.shape[0], steps):
                for j in range(width // LANES):
                    o_ref[0, j, pl.ds(cm, SUBLANES, stride=SUBLANES), :] = (
                        r[nat:nat + SUBLANES, j * LANES:(j + 1) * LANES])
        else:
            o_ref[0] = r.astype(o_ref.dtype)
        off += width


def _norm_proj(h, g, w_bf16, widths, slab_steps, dtypes):
    bsz, seq, d = h.shape
    tm = min(ROW_TILE, seq)
    out_shape, out_specs = [], []
    for width, as_slab, dtype in zip(widths, slab_steps, dtypes):
        if as_slab:
            assert tm % (SUBLANES * as_slab) == 0 and dtype == F32
            ns = width // LANES
            out_shape.append(jax.ShapeDtypeStruct((bsz, ns, seq, LANES), F32))
            out_specs.append(pl.BlockSpec((1, ns, tm, LANES), lambda b, m: (b, 0, m, 0)))
        else:
            out_shape.append(jax.ShapeDtypeStruct((bsz, seq, width), dtype))
            out_specs.append(pl.BlockSpec((1, tm, width), lambda b, m: (b, m, 0)))
    return pl.pallas_call(
        functools.partial(_norm_proj_kernel, widths=tuple(widths), slab_steps=tuple(slab_steps)),
        out_shape=out_shape,
        grid=(bsz, seq // tm),
        in_specs=[pl.BlockSpec((1, tm, d), lambda b, m: (b, m, 0)),
                  pl.BlockSpec((1, d), lambda b, m: (0, 0)),
                  pl.BlockSpec(w_bf16.shape, lambda b, m: (0, 0))],
        out_specs=out_specs,
        compiler_params=_cparams(("parallel", "parallel")),
        name="norm_proj",
    )(h, g.reshape(1, d), w_bf16)


def _s5_prep_kernel(lr_ref, li_ref, ldt_ref, br_ref, bi_ref, ar_ref, ai_ref, bbr_ref, bbi_ref):
    lr = lr_ref[...]
    li = li_ref[...]
    dt = jnp.exp(ldt_ref[...])
    mag = jnp.exp(lr * dt)
    ar = mag * jnp.cos(li * dt)
    ai = mag * jnp.sin(li * dt)
    den = lr * lr + li * li
    qr = ((ar - 1.0) * lr + ai * li) / den
    qi = (ai * lr - (ar - 1.0) * li) / den
    br = br_ref[...]
    bi = bi_ref[...]
    ar_ref[...] = ar
    ai_ref[...] = ai
    bbr_ref[...] = qr * br - qi * bi
    bbi_ref[...] = qr * bi + qi * br


def _s5_prep(lam_re, lam_im, log_dt, b_re, b_im):
    g, p, hh = b_re.shape
    rep = lambda t: jnp.repeat(t, hh, axis=1)
    shp = jax.ShapeDtypeStruct((g, p * hh), F32)
    ar, ai, bbr, bbi = pl.pallas_call(
        _s5_prep_kernel, out_shape=[shp, shp, shp, shp], name="s5_prep",
    )(rep(lam_re), rep(lam_im), log_dt.reshape(g, 1), b_re.reshape(g, p * hh), b_im.reshape(g, p * hh))
    return ar[:, ::hh], ai[:, ::hh], bbr.reshape(g, p, hh), bbi.reshape(g, p, hh)


def _s5_core_kernel(u_ref, wb_ref, wc_ref, a_ref, d_ref, y_ref, x_ref, s_ref, carry_ref, *, steps):
    nst = S5_TILE_ST
    nslab = S5_TILE_CH // LANES
    ntile = wb_ref.shape[0]

    @pl.when(pl.program_id(1) == 0)
    def _():
        carry_ref[...] = jnp.zeros_like(carry_ref)

    def load_u(i):
        return jnp.concatenate([u_ref[0, i * nslab + j] for j in range(nslab)], axis=-1)

    def project_in(i):
        x_ref[i % 2] = _dot(load_u(i).astype(BF16), wb_ref[i])

    def scan(i):
        slot = i % 2
        ar1 = a_ref[i, 0:1, :]
        ai1 = a_ref[i, 1:2, :]
        ar = jnp.broadcast_to(ar1, (SUBLANES, nst))
        ai = jnp.broadcast_to(ai1, (SUBLANES, nst))

        def advance(k, sr, si):
            xr = x_ref[slot, k * SUBLANES:(k + 1) * SUBLANES, 0:nst]
            xi = x_ref[slot, k * SUBLANES:(k + 1) * SUBLANES, nst:2 * nst]
            return ar * sr - ai * si + xr, ar * si + ai * sr + xi

        er = jnp.zeros((SUBLANES, nst), F32)
        ei = jnp.zeros((SUBLANES, nst), F32)
        for k in range(steps):
            er, ei = advance(k, er, ei)

        pr, pi = ar1, ai1
        for _ in range(int(math.log2(steps))):
            pr, pi = pr * pr - pi * pi, 2.0 * pr * pi

        rid = lax.broadcasted_iota(jnp.int32, (SUBLANES, nst), 0)
        cr = carry_ref[i, 0:1, :]
        ci = carry_ref[i, 1:2, :]
        sr = jnp.zeros((SUBLANES, nst), F32)
        si = jnp.zeros((SUBLANES, nst), F32)
        for r in range(SUBLANES):
            sr = jnp.where(rid == r, jnp.broadcast_to(cr, (SUBLANES, nst)), sr)
            si = jnp.where(rid == r, jnp.broadcast_to(ci, (SUBLANES, nst)), si)
            cr, ci = pr * cr - pi * ci + er[r:r + 1, :], pr * ci + pi * cr + ei[r:r + 1, :]
        carry_ref[i, 0:1, :] = cr
        carry_ref[i, 1:2, :] = ci

        for k in range(0, steps, 2):
            r0, i0 = advance(k, sr, si)
            sr, si = advance(k + 1, r0, i0)
            rows2 = slice(k * SUBLANES, (k + 2) * SUBLANES)
            s_ref[slot, rows2, 0:nst] = jnp.concatenate([r0, sr], axis=0).astype(BF16)
            s_ref[slot, rows2, nst:2 * nst] = jnp.concatenate([i0, si], axis=0).astype(BF16)

    def project_out(i):
        y = _dot(s_ref[i % 2], wc_ref[i]) + d_ref[i] * load_u(i)
        for j in range(nslab):
            y_ref[0, i * nslab + j] = y[:, j * LANES:(j + 1) * LANES]

    project_in(0)
    for i in range(ntile):
        if i + 1 < ntile:
            project_in(i + 1)
        scan(i)
        project_out(i)


def _s5_core(u_slab, wb, wc, a_tiles, d_tiles, steps):
    bsz, nslab, seq, _ = u_slab.shape
    ntile = wb.shape[0]
    rows = SUBLANES * steps
    const = lambda shp: pl.BlockSpec(shp, lambda b, n: (0,) * len(shp))
    tok = pl.BlockSpec((1, nslab, rows, LANES), lambda b, n: (b, 0, n, 0))
    return pl.pallas_call(
        functools.partial(_s5_core_kernel, steps=steps),
        out_shape=jax.ShapeDtypeStruct(u_slab.shape, F32),
        grid=(bsz, seq // rows),
        in_specs=[tok, const(wb.shape), const(wc.shape), const(a_tiles.shape), const(d_tiles.shape)],
        out_specs=tok,
        scratch_shapes=[pltpu.VMEM((2, rows, 2 * S5_TILE_ST), F32),
                        pltpu.VMEM((2, rows, 2 * S5_TILE_ST), BF16),
                        pltpu.VMEM((ntile, 2, S5_TILE_ST), F32)],
        compiler_params=_cparams(("parallel", "arbitrary")),
        name="s5_core",
    )(u_slab, wb, wc, a_tiles, d_tiles)


def _s5_post_kernel(y_ref, z_ref, h_ref, wg_ref, bg_ref, wo_ref, gp_ref, o_ref, *, steps):
    nslab = y_ref.shape[1]
    groups = list(_chunk_row_groups(y_ref.shape[2], steps))
    y = jnp.concatenate(
        [jnp.concatenate([y_ref[0, j, pl.ds(cm, SUBLANES, stride=SUBLANES), :] for _, cm in groups], axis=0)
         for j in range(nslab)], axis=-1)
    y = _gelu_tanh(y)
    y = y * _sigmoid(_dot(y.astype(BF16), wg_ref[...]) + bg_ref[...])
    t = y * _silu(z_ref[0])
    o = _dot(t.astype(BF16), wo_ref[...])
    o_ref[0] = h_ref[0] + _rms(o, gp_ref[...])


def _s5_post(y_slab, z, h, w_glu, b_glu, w_out, g_post, steps):
    bsz, seq, d = h.shape
    nslab = y_slab.shape[1]
    tm = min(ROW_TILE, seq)
    assert tm % (SUBLANES * steps) == 0
    row = lambda t: t.reshape(1, -1)
    vec = pl.BlockSpec((1, d), lambda b, m: (0, 0))
    mat = pl.BlockSpec((d, d), lambda b, m: (0, 0))
    slab = pl.BlockSpec((1, nslab, tm, LANES), lambda b, m: (b, 0, m, 0))
    tok = pl.BlockSpec((1, tm, d), lambda b, m: (b, m, 0))
    return pl.pallas_call(
        functools.partial(_s5_post_kernel, steps=steps),
        out_shape=jax.ShapeDtypeStruct(h.shape, F32),
        grid=(bsz, seq // tm),
        in_specs=[slab, tok, tok, mat, vec, mat, vec],
        out_specs=tok,
        compiler_params=_cparams(("parallel", "parallel")),
        name="s5_post",
    )(y_slab, z, h, w_glu.astype(BF16), row(b_glu), w_out.astype(BF16), row(g_post))


def _s5_layer(h, g_pre, g_post, w_in, lam_re, lam_im, log_dt, b_re, b_im, c_re, c_im, d_skip,
              w_glu, b_glu, w_out):
    bsz, seq, d = h.shape
    ngroup = lam_re.shape[0]
    ntile = ngroup // S5_TILE_GROUPS
    ar, ai, bbr, bbi = _s5_prep(lam_re, lam_im, log_dt, b_re, b_im)
    eye = jnp.eye(S5_TILE_GROUPS, dtype=F32)

    def in_tile(bb):
        t = bb.reshape(ntile, S5_TILE_GROUPS, S5_STATE, S5_GROUP)
        return jnp.einsum('igph,gk->ighkp', t, eye).reshape(ntile, S5_TILE_CH, S5_TILE_ST)

    def out_tile(cc):
        t = cc.reshape(ntile, S5_TILE_GROUPS, S5_GROUP, S5_STATE)
        return jnp.einsum('ighp,gk->ikpgh', t, eye).reshape(ntile, S5_TILE_ST, S5_TILE_CH)

    wb = jnp.concatenate([in_tile(bbr), in_tile(bbi)], axis=2).astype(BF16)
    wc = jnp.concatenate([out_tile(c_re), -out_tile(c_im)], axis=1).astype(BF16)
    a_tiles = jnp.stack([ar.reshape(ntile, S5_TILE_ST), ai.reshape(ntile, S5_TILE_ST)], axis=1)
    d_tiles = d_skip.reshape(ntile, 1, S5_TILE_CH)

    steps = min(S5_STEPS, seq // SUBLANES)
    u_slab, z = _norm_proj(h, g_pre, w_in.astype(BF16), (d, d), (steps, 0), (F32, F32))
    y_slab = _s5_core(u_slab, wb, wc, a_tiles, d_tiles, steps)
    return _s5_post(y_slab, z, h, w_glu, b_glu, w_out, g_post, steps)


def _chunk_cumsum(x, chunk):
    pos = lax.broadcasted_iota(jnp.int32, x.shape, 0) % chunk
    sh = 1
    while sh < chunk:
        x = x + jnp.where(pos >= sh, pltpu.roll(x, sh, 0), 0.0)
        sh *= 2
    return x


def _block_mid_rows(x, blk):
    rows, w = x.shape
    half = blk // 2
    if blk >= SUBLANES:
        x3 = x.reshape(rows // blk, blk, w)
        return jnp.broadcast_to(x3[:, half - 1:half, :], (rows // blk, blk, w)).reshape(rows, w)
    pos = lax.broadcasted_iota(jnp.int32, (rows, w), 0) % blk
    out = x
    for p in range(blk):
        d = p - (half - 1)
        if d != 0:
            out = jnp.where(pos == p, pltpu.roll(x, d % rows, 0), out)
    return out


def _hg_core_kernel(q_ref, f_ref, v_ref, lbl_ref, o_ref, ql_ref, kl_ref, qe_ref, kd_ref, g_ref, st_ref,
                    *, layer, chunk):
    rows, width = q_ref.shape[1], q_ref.shape[2]
    pair = 2 * HG_HEAD
    blocks = [2 ** i for i in range(1, int(math.log2(chunk)) + 1)]

    @pl.when(pl.program_id(1) == 0)
    def _():
        st_ref[...] = jnp.zeros_like(st_ref)

    lg = lbl_ref[...]
    ex = jnp.exp(lg - jnp.max(lg, axis=0, keepdims=True))
    p = ex / jnp.sum(ex, axis=0, keepdims=True)
    lb = jnp.zeros((1, width), F32)
    for j in range(1, layer + 1):
        lb = lb + p[j:j + 1, :]

    q = q_ref[0]
    f = lb + (1.0 - lb) * _sigmoid(f_ref[0])
    kk = 1.0 - f
    b = _chunk_cumsum(jnp.log(f), chunk)
    b3 = b.reshape(rows // chunk, chunk, width)
    b_last = b3[:, chunk - 1:chunk, :]
    g_ref[...] = jnp.exp(b_last.reshape(rows // chunk, width))
    qe_ref[...] = (q * jnp.exp(b)).astype(BF16)
    kd_ref[...] = (kk * jnp.exp(jnp.broadcast_to(b_last, b3.shape).reshape(rows, width) - b)).astype(BF16)
    ql_ref[0] = (q * kk).astype(BF16)
    for li, blk in enumerate(blocks):
        w = jnp.exp(-jnp.abs(b - _block_mid_rows(b, blk)))
        ql_ref[li + 1] = (q * w).astype(BF16)
        kl_ref[li] = (kk * w).astype(BF16)

    t_i = lax.broadcasted_iota(jnp.int32, (chunk, HG_HEAD), 0)
    s_i = lax.broadcasted_iota(jnp.int32, (chunk, HG_HEAD), 1) % chunk
    eye = t_i == s_i
    lmask = [((t_i // blk) == (s_i // blk)) & ((t_i % blk) >= blk // 2) & ((s_i % blk) < blk // 2)
             for blk in blocks]
    lo = lax.broadcasted_iota(jnp.int32, (chunk, pair), 1) < HG_HEAD
    bd = (lax.broadcasted_iota(jnp.int32, (pair, pair), 0) // HG_HEAD
          == lax.broadcasted_iota(jnp.int32, (pair, pair), 1) // HG_HEAD)
    ones_bd = jnp.where(lax.broadcasted_iota(jnp.int32, (pair, HG_HEAD), 0) // HG_HEAD
                        == lax.broadcasted_iota(jnp.int32, (pair, HG_HEAD), 1) // chunk, 1.0, 0.0).astype(BF16)

    def split_heads(t):
        z = jnp.zeros_like(t)
        return jnp.concatenate([jnp.where(lo, t, z), jnp.where(lo, z, t)], axis=0)

    for c in range(rows // chunk):
        rs = slice(c * chunk, (c + 1) * chunk)
        for pr in range(width // pair):
            ls = slice(pr * pair, (pr + 1) * pair)
            sc = jnp.where(eye, _dot(ql_ref[0, rs, ls], ones_bd), 0.0)
            for li in range(len(blocks)):
                sc = jnp.where(lmask[li], _dot_nt(ql_ref[li + 1, rs, ls], split_heads(kl_ref[li, rs, ls])), sc)
            vpair = v_ref[0, rs, ls]
            st = st_ref[pr]
            o = _dot(sc.astype(BF16), split_heads(vpair)) + _dot_nt(qe_ref[rs, ls], st.astype(BF16))
            o_ref[0, rs, ls] = o
            upd = _dot_tn(vpair, kd_ref[rs, ls])
            st_ref[pr] = g_ref[c:c + 1, ls] * st + jnp.where(bd, upd, 0.0)


def _hg_core(q, fz, v, lb_logits, layer):
    bsz, seq, width = q.shape
    blk = min(HG_BLOCK, seq)
    chunk = min(HG_CHUNK, blk)
    nlev = int(math.log2(chunk))
    assert HG_HEAD == 2 * chunk and width % (2 * HG_HEAD) == 0
    tok = pl.BlockSpec((1, blk, width), lambda b, n: (b, n, 0))
    return pl.pallas_call(
        functools.partial(_hg_core_kernel, layer=layer, chunk=chunk),
        out_shape=jax.ShapeDtypeStruct(q.shape, F32),
        grid=(bsz, seq // blk),
        in_specs=[tok, tok, tok, pl.BlockSpec(lb_logits.shape, lambda b, n: (0, 0))],
        out_specs=tok,
        scratch_shapes=[pltpu.VMEM((nlev + 1, blk, width), BF16),
                        pltpu.VMEM((nlev, blk, width), BF16),
                        pltpu.VMEM((blk, width), BF16),
                        pltpu.VMEM((blk, width), BF16),
                        pltpu.VMEM((blk // chunk, width), F32),
                        pltpu.VMEM((width // (2 * HG_HEAD), 2 * HG_HEAD, 2 * HG_HEAD), F32)],
        compiler_params=_cparams(("parallel", "arbitrary")),
        name="hgrn2_core",
    )(q, fz, v, lb_logits)


def _gate_out_kernel(o_ref, z_ref, h_ref, gn_ref, wo_ref, gp_ref, out_ref, *, head_norm):
    o = o_ref[0]
    if head_norm:
        parts = []
        for j in range(o.shape[-1] // HG_HEAD):
            oj = o[:, j * HG_HEAD:(j + 1) * HG_HEAD]
            parts.append(oj * lax.rsqrt(jnp.mean(oj * oj, axis=-1, keepdims=True) + NORM_EPS))
        o = jnp.concatenate(parts, axis=-1) * gn_ref[...]
    t = o * _silu(z_ref[0])
    r = _dot(t.astype(BF16), wo_ref[...])
    out_ref[0] = h_ref[0] + _rms(r, gp_ref[...])


def _gate_out(o, z, h, g_head, w_out, g_post, head_norm):
    bsz, seq, d = h.shape
    tm = min(ROW_TILE, seq)
    row = lambda t: t.reshape(1, -1)
    vec = pl.BlockSpec((1, d), lambda b, m: (0, 0))
    tok = pl.BlockSpec((1, tm, d), lambda b, m: (b, m, 0))
    return pl.pallas_call(
        functools.partial(_gate_out_kernel, head_norm=head_norm),
        out_shape=jax.ShapeDtypeStruct(h.shape, F32),
        grid=(bsz, seq // tm),
        in_specs=[tok, tok, tok, vec, pl.BlockSpec((d, d), lambda b, m: (0, 0)), vec],
        out_specs=tok,
        compiler_params=_cparams(("parallel", "parallel")),
        name="gate_out",
    )(o, z, h, row(g_head), w_out.astype(BF16), row(g_post))


def _hg_layer(h, g_pre, g_post, w_in, lb_logits, layer, norm_g, w_out):
    d = h.shape[-1]
    q, fz, v, z = _norm_proj(h, g_pre, w_in.astype(BF16), (d, d, d, d), (0,) * 4, (F32, F32, BF16, F32))
    o = _hg_core(q, fz, v, lb_logits, layer)
    return _gate_out(o, z, h, norm_g, w_out, g_post, True)


def _at_proj_kernel(h_ref, g_ref, w_ref, bias_ref, pos_ref, invf_ref, q_ref, k_ref, v_ref, z_ref,
                    *, qw, kvw):
    y = _rms(h_ref[0], g_ref[...]).astype(BF16)
    half = AT_HEAD // 2
    nq4 = LANES // half
    ang = pos_ref[0].astype(F32) * invf_ref[...]
    cos4 = jnp.cos(ang)
    sin4 = jnp.sin(ang)
    lane4 = lax.broadcasted_iota(jnp.int32, ang.shape, 1)

    def spread(t):
        parts = []
        for j in range(nq4):
            m = jnp.where(lane4 // half == j, t, 0.0)
            x = m
            for s in range(1, nq4):
                x = x + pltpu.roll(m, s * half, 1)
            parts.append(x)
        return jnp.concatenate(parts, axis=0)

    lane = lax.broadcasted_iota(jnp.int32, (h_ref.shape[1], LANES), 1)
    first = (lane % AT_HEAD) < half
    low = lane < AT_HEAD
    cosf = spread(cos4)
    sinf = spread(sin4)
    sinf = jnp.where(first, -sinf, sinf)

    def rope(t):
        partner = jnp.where(first, pltpu.roll(t, LANES - AT_HEAD // 2, 1), pltpu.roll(t, AT_HEAD // 2, 1))
        return t * cosf + partner * sinf

    def store_dup(ref, j, t):
        r = pltpu.roll(t, AT_HEAD, 1)
        ref[0, :, (2 * j) * LANES:(2 * j + 1) * LANES] = jnp.where(low, t, r).astype(ref.dtype)
        ref[0, :, (2 * j + 1) * LANES:(2 * j + 2) * LANES] = jnp.where(low, r, t).astype(ref.dtype)

    for j in range(qw // LANES):
        sl = slice(j * LANES, (j + 1) * LANES)
        q_ref[0, :, sl] = rope(_dot(y, w_ref[:, sl]) + bias_ref[:, sl]).astype(q_ref.dtype)
    for j in range(kvw // LANES):
        sl = slice(qw + j * LANES, qw + (j + 1) * LANES)
        store_dup(k_ref, j, rope(_dot(y, w_ref[:, sl]) + bias_ref[:, sl]))
    for j in range(kvw // LANES):
        sl = slice(qw + kvw + j * LANES, qw + kvw + (j + 1) * LANES)
        store_dup(v_ref, j, _dot(y, w_ref[:, sl]) + bias_ref[:, sl])
    off = qw + 2 * kvw
    z_ref[0] = _dot(y, w_ref[:, off:off + qw])


def _at_proj(h, g, w_bf16, bias, positions, qw, kvw):
    bsz, seq, d = h.shape
    tm = min(ROW_TILE, seq)
    half = AT_HEAD // 2
    inv_freq = ROPE_THETA ** (-jnp.arange(0, AT_HEAD, 2, dtype=F32) / AT_HEAD)
    nq4 = LANES // half
    invf = jnp.tile(inv_freq, nq4).reshape(1, LANES)
    pos4 = positions.reshape(bsz, seq // tm, nq4, tm // nq4).transpose(0, 1, 3, 2)
    pos4 = jnp.repeat(pos4, half, axis=-1).reshape(bsz, seq // nq4, LANES)
    tokspec = lambda w: pl.BlockSpec((1, tm, w), lambda b, m: (b, m, 0))
    const = lambda shp: pl.BlockSpec(shp, lambda b, m: (0, 0))
    return pl.pallas_call(
        functools.partial(_at_proj_kernel, qw=qw, kvw=kvw),
        out_shape=[jax.ShapeDtypeStruct((bsz, seq, qw), BF16), jax.ShapeDtypeStruct((bsz, seq, 2 * kvw), BF16),
                   jax.ShapeDtypeStruct((bsz, seq, 2 * kvw), BF16), jax.ShapeDtypeStruct((bsz, seq, qw), F32)],
        grid=(bsz, seq // tm),
        in_specs=[tokspec(d), const((1, d)), const(w_bf16.shape), const((1, qw + 2 * kvw)),
                  pl.BlockSpec((1, tm // nq4, LANES), lambda b, m: (b, m, 0)), const((1, LANES))],
        out_specs=[tokspec(qw), tokspec(2 * kvw), tokspec(2 * kvw), tokspec(qw)],
        compiler_params=_cparams(("parallel", "parallel")),
        name="attn_proj",
    )(h, g.reshape(1, d), w_bf16, bias.reshape(1, -1), pos4, invf)


def _at_core_kernel(q_ref, kc_ref, kp_ref, vc_ref, vp_ref, sink_ref, o_ref):
    blk = q_ref.shape[1]
    n = pl.program_id(1)
    ngroup = kc_ref.shape[2] // LANES
    qi = lax.broadcasted_iota(jnp.int32, (2 * blk, 4 * blk), 0) % blk
    kj = lax.broadcasted_iota(jnp.int32, (2 * blk, 4 * blk), 1) % (2 * blk)
    dist = qi + blk - kj
    mask = (dist >= 0) & (dist < blk) & ((kj >= blk) | (n > 0))
    low = lax.broadcasted_iota(jnp.int32, (2 * blk, LANES), 1) < AT_HEAD
    top = lax.broadcasted_iota(jnp.int32, (2 * blk, 1), 0) < blk
    ones_lo = jnp.where(low, 1.0, 0.0).astype(BF16)
    ones_hi = jnp.where(low, 0.0, 1.0).astype(BF16)
    scale = AT_HEAD ** -0.5

    scores = []
    for g in range(ngroup):
        sl = slice(g * LANES, (g + 1) * LANES)
        kdup = jnp.concatenate([kp_ref[0, :, sl], kc_ref[0, :, sl]], axis=0)
        rhs = jnp.concatenate([jnp.where(low, kdup, 0.0), jnp.where(low, 0.0, kdup)], axis=0).astype(BF16)
        lhs = jnp.concatenate([q_ref[0, :, (2 * g) * LANES:(2 * g + 1) * LANES],
                               q_ref[0, :, (2 * g + 1) * LANES:(2 * g + 2) * LANES]], axis=0).astype(BF16)
        scores.append(_dot_nt(lhs, rhs))

    probs, sink_terms = [], []
    for g in range(ngroup):
        s = jnp.where(mask, scores[g] * scale, -jnp.inf)
        sink_a = jnp.where(top, sink_ref[0:1, 4 * g:4 * g + 1], sink_ref[0:1, 4 * g + 2:4 * g + 3])
        sink_b = jnp.where(top, sink_ref[0:1, 4 * g + 1:4 * g + 2], sink_ref[0:1, 4 * g + 3:4 * g + 4])
        m_a = jnp.maximum(jnp.max(s[:, :2 * blk], axis=-1, keepdims=True), sink_a)
        m_b = jnp.maximum(jnp.max(s[:, 2 * blk:], axis=-1, keepdims=True), sink_b)
        e = jnp.concatenate([jnp.exp(s[:, :2 * blk] - m_a), jnp.exp(s[:, 2 * blk:] - m_b)], axis=1)
        probs.append(e.astype(BF16))
        sink_terms.append(jnp.where(low, jnp.exp(sink_a - m_a), jnp.exp(sink_b - m_b)))

    for g in range(ngroup):
        sl = slice(g * LANES, (g + 1) * LANES)
        vdup = jnp.concatenate([vp_ref[0, :, sl], vc_ref[0, :, sl]], axis=0)
        va = jnp.where(low, vdup, 0.0).astype(BF16)
        vb = jnp.where(low, 0.0, vdup).astype(BF16)
        rhs = jnp.concatenate([jnp.concatenate([va, ones_lo], axis=1),
                               jnp.concatenate([vb, ones_hi], axis=1)], axis=0)
        out = _dot(probs[g], rhs)
        o = out[:, :LANES] / (out[:, LANES:] + sink_terms[g])
        o_ref[0, :, (2 * g) * LANES:(2 * g + 1) * LANES] = o[:blk]
        o_ref[0, :, (2 * g + 1) * LANES:(2 * g + 2) * LANES] = o[blk:]


def _at_core(q, k_dup, v_dup, sinks):
    bsz, seq, qw = q.shape
    kvw = k_dup.shape[-1]
    blk = AT_BLOCK
    nq = qw // AT_HEAD
    assert nq == 4 * (kvw // LANES) and qw == 2 * kvw
    cur = lambda w: pl.BlockSpec((1, blk, w), lambda b, n: (b, n, 0))
    prev = lambda w: pl.BlockSpec((1, blk, w), lambda b, n: (b, jnp.maximum(n - 1, 0), 0))
    return pl.pallas_call(
        _at_core_kernel,
        out_shape=jax.ShapeDtypeStruct(q.shape, F32),
        grid=(bsz, seq // blk),
        in_specs=[cur(qw), cur(kvw), prev(kvw), cur(kvw), prev(kvw),
                  pl.BlockSpec((1, nq), lambda b, n: (0, 0))],
        out_specs=cur(qw),
        compiler_params=_cparams(("parallel", "arbitrary")),
        name="attn_core",
    )(q, k_dup, k_dup, v_dup, v_dup, sinks.reshape(1, nq))


def _at_layer(h, positions, g_pre, g_post, w_in, b_in, sinks, w_out):
    d = h.shape[-1]
    qw = w_out.shape[0]
    kvw = (b_in.shape[0] - qw) // 2
    q, k, v, z = _at_proj(h, g_pre, w_in.astype(BF16), b_in, positions, qw, kvw)
    o = _at_core(q, k, v, sinks)
    return _gate_out(o, z, h, jnp.ones((d,), F32), w_out, g_post, False)


def kernel(x, positions, norm_pre, norm_post, s5_w_in, s5_lambda_re, s5_lambda_im, s5_log_dt, s5_b_re, s5_b_im, s5_c_re, s5_c_im, s5_d, s5_w_glu, s5_b_glu, s5_w_out, hg_w_in, hg_lb_logits, hg_norm, hg_w_out, at_w_in, at_b_in, at_sinks, at_w_out):
    depth = norm_pre.shape[0]
    h = x
    for i in range(depth):
        kind, j = i % 3, i // 3
        if kind == 0:
            h = _s5_layer(h, norm_pre[i], norm_post[i], s5_w_in[j], s5_lambda_re[j], s5_lambda_im[j],
                          s5_log_dt[j], s5_b_re[j], s5_b_im[j], s5_c_re[j], s5_c_im[j], s5_d[j],
                          s5_w_glu[j], s5_b_glu[j], s5_w_out[j])
        elif kind == 1:
            h = _hg_layer(h, norm_pre[i], norm_post[i], hg_w_in[j], hg_lb_logits, i, hg_norm[j], hg_w_out[j])
        else:
            h = _at_layer(h, positions, norm_pre[i], norm_post[i], at_w_in[j], at_b_in[j], at_sinks[j],
                          at_w_out[j])
    return h
```

```python
import functools
import math

import jax
import jax.numpy as jnp
from jax import lax
from jax.experimental import pallas as pl
from jax.experimental.pallas import tpu as pltpu

F32 = jnp.float32
BF16 = jnp.bfloat16

NORM_EPS = 1e-6
LANES = 128
SUBLANES = 8
VMEM_LIMIT = 56 * 1024 * 1024

S5_GROUP = 16
S5_STATE = 64
S5_TILE_GROUPS = 16
S5_TILE_CH = S5_TILE_GROUPS * S5_GROUP
S5_TILE_ST = S5_TILE_GROUPS * S5_STATE
S5_STEPS = 64
HG_HEAD = 128
HG_CHUNK = 64
HG_BLOCK = 256
AT_HEAD = 64
AT_KV_HEADS = 4
AT_BLOCK = 128
ROPE_THETA = 10000.0

ROW_TILE = 512


def _cparams(sem):
    return pltpu.CompilerParams(dimension_semantics=sem, vmem_limit_bytes=VMEM_LIMIT)


def _rms(x, g):
    return x * lax.rsqrt(jnp.mean(x * x, axis=-1, keepdims=True) + NORM_EPS) * g


def _sigmoid(x):
    return 1.0 / (1.0 + jnp.exp(-x))


def _silu(x):
    return x * _sigmoid(x)


def _gelu_tanh(x):
    c = math.sqrt(2.0 / math.pi)
    return 0.5 * x * (1.0 + jnp.tanh(c * (x + 0.044715 * (x * x * x))))


def _dot(a, b):
    return jnp.dot(a, b, preferred_element_type=F32)


def _dot_nt(a, b):
    return lax.dot_general(a, b, (((1,), (1,)), ((), ())), preferred_element_type=F32)


def _dot_tn(a, b):
    return lax.dot_general(a, b, (((0,), (0,)), ((), ())), preferred_element_type=F32)


def _chunk_row_groups(rows, steps):
    for base in range(0, rows, SUBLANES * steps):
        for r in range(SUBLANES):
            for k0 in range(0, steps, SUBLANES):
                yield base + r * steps + k0, base + k0 * SUBLANES + r


def _norm_proj_kernel(h_ref, g_ref, w_ref, *out_refs, widths, slab_steps):
    y = _rms(h_ref[0], g_ref[...]).astype(BF16)
    off = 0
    for o_ref, width, steps in zip(out_refs, widths, slab_steps):
        r = _dot(y, w_ref[:, off:off + width])
        if steps:
            for nat, cm in _chunk_row_groups(r.shape[0], steps):
                for j in range(width // LANES):
                    o_ref[0, j, pl.ds(cm, SUBLANES, stride=SUBLANES), :] = (
                        r[nat:nat + SUBLANES, j * LANES:(j + 1) * LANES])
        else:
            o_ref[0] = r.astype(o_ref.dtype)
        off += width


def _norm_proj(h, g, w_bf16, widths, slab_steps, dtypes):
    bsz, seq, d = h.shape
    tm = min(ROW_TILE, seq)
    out_shape, out_specs = [], []
    for width, as_slab, dtype in zip(widths, slab_steps, dtypes):
        if as_slab:
            assert tm % (SUBLANES * as_slab) == 0 and dtype == F32
            ns = width // LANES
            out_shape.append(jax.ShapeDtypeStruct((bsz, ns, seq, LANES), F32))
            out_specs.append(pl.BlockSpec((1, ns, tm, LANES), lambda b, m: (b, 0, m, 0)))
        else:
            out_shape.append(jax.ShapeDtypeStruct((bsz, seq, width), dtype))
            out_specs.append(pl.BlockSpec((1, tm, width), lambda b, m: (b, m, 0)))
    return pl.pallas_call(
        functools.partial(_norm_proj_kernel, widths=tuple(widths), slab_steps=tuple(slab_steps)),
        out_shape=out_shape,
        grid=(bsz, seq // tm),
        in_specs=[pl.BlockSpec((1, tm, d), lambda b, m: (b, m, 0)),
                  pl.BlockSpec((1, d), lambda b, m: (0, 0)),
                  pl.BlockSpec(w_bf16.shape, lambda b, m: (0, 0))],
        out_specs=out_specs,
        compiler_params=_cparams(("parallel", "parallel")),
        name="norm_proj",
    )(h, g.reshape(1, d), w_bf16)


def _s5_prep_kernel(lr_ref, li_ref, ldt_ref, br_ref, bi_ref, ar_ref, ai_ref, bbr_ref, bbi_ref):
    lr = lr_ref[...]
    li = li_ref[...]
    dt = jnp.exp(ldt_ref[...])
    mag = jnp.exp(lr * dt)
    ar = mag * jnp.cos(li * dt)
    ai = mag * jnp.sin(li * dt)
    den = lr * lr + li * li
    qr = ((ar - 1.0) * lr + ai * li) / den
    qi = (ai * lr - (ar - 1.0) * li) / den
    br = br_ref[...]
    bi = bi_ref[...]
    ar_ref[...] = ar
    ai_ref[...] = ai
    bbr_ref[...] = qr * br - qi * bi
    bbi_ref[...] = qr * bi + qi * br


def _s5_prep(lam_re, lam_im, log_dt, b_re, b_im):
    g, p, hh = b_re.shape
    rep = lambda t: jnp.repeat(t, hh, axis=1)
    shp = jax.ShapeDtypeStruct((g, p * hh), F32)
    ar, ai, bbr, bbi = pl.pallas_call(
        _s5_prep_kernel, out_shape=[shp, shp, shp, shp], name="s5_prep",
    )(rep(lam_re), rep(lam_im), log_dt.reshape(g, 1), b_re.reshape(g, p * hh), b_im.reshape(g, p * hh))
    return ar[:, ::hh], ai[:, ::hh], bbr.reshape(g, p, hh), bbi.reshape(g, p, hh)


def _s5_core_kernel(u_ref, wb_ref, wc_ref, a_ref, d_ref, y_ref, x_ref, s_ref, carry_ref, *, steps):
    nst = S5_TILE_ST
    nslab = S5_TILE_CH // LANES
    ntile = wb_ref.shape[0]

    @pl.when(pl.program_id(1) == 0)
    def _():
        carry_ref[...] = jnp.zeros_like(carry_ref)

    def load_u(i):
        return jnp.concatenate([u_ref[0, i * nslab + j] for j in range(nslab)], axis=-1)

    def project_in(i):
        x_ref[i % 2] = _dot(load_u(i).astype(BF16), wb_ref[i])

    def scan(i):
        slot = i % 2
        ar1 = a_ref[i, 0:1, :]
        ai1 = a_ref[i, 1:2, :]
        ar = jnp.broadcast_to(ar1, (SUBLANES, nst))
        ai = jnp.broadcast_to(ai1, (SUBLANES, nst))

        def advance(k, sr, si):
            xr = x_ref[slot, k * SUBLANES:(k + 1) * SUBLANES, 0:nst]
            xi = x_ref[slot, k * SUBLANES:(k + 1) * SUBLANES, nst:2 * nst]
            return ar * sr - ai * si + xr, ar * si + ai * sr + xi

        er = jnp.zeros((SUBLANES, nst), F32)
        ei = jnp.zeros((SUBLANES, nst), F32)
        for k in range(steps):
            er, ei = advance(k, er, ei)

        pr, pi = ar1, ai1
        for _ in range(int(math.log2(steps))):
            pr, pi = pr * pr - pi * pi, 2.0 * pr * pi

        rid = lax.broadcasted_iota(jnp.int32, (SUBLANES, nst), 0)
        cr = carry_ref[i, 0:1, :]
        ci = carry_ref[i, 1:2, :]
        sr = jnp.zeros((SUBLANES, nst), F32)
        si = jnp.zeros((SUBLANES, nst), F32)
        for r in range(SUBLANES):
            sr = jnp.where(rid == r, jnp.broadcast_to(cr, (SUBLANES, nst)), sr)
            si = jnp.where(rid == r, jnp.broadcast_to(ci, (SUBLANES, nst)), si)
            cr, ci = pr * cr - pi * ci + er[r:r + 1, :], pr * ci + pi * cr + ei[r:r + 1, :]
        carry_ref[i, 0:1, :] = cr
        carry_ref[i, 1:2, :] = ci

        for k in range(0, steps, 2):
            r0, i0 = advance(k, sr, si)
            sr, si = advance(k + 1, r0, i0)
            rows2 = slice(k * SUBLANES, (k + 2) * SUBLANES)
            s_ref[slot, rows2, 0:nst] = jnp.concatenate([r0, sr], axis=0).astype(BF16)
            s_ref[slot, rows2, nst:2 * nst] = jnp.concatenate([i0, si], axis=0).astype(BF16)

    def project_out(i):
        y = _dot(s_ref[i % 2], wc_ref[i]) + d_ref[i] * load_u(i)
        for j in range(nslab):
            y_ref[0, i * nslab + j] = y[:, j * LANES:(j + 1) * LANES]

    project_in(0)
    for i in range(ntile):
        if i + 1 < ntile:
            project_in(i + 1)
        scan(i)
        project_out(i)


def _s5_core(u_slab, wb, wc, a_tiles, d_tiles, layer, steps):
    bsz, nslab, seq, _ = u_slab.shape
    ntile = nslab * LANES // S5_TILE_CH
    rows = SUBLANES * steps
    const = lambda shp: pl.BlockSpec((ntile,) + shp[1:], lambda b, n: (layer,) + (0,) * (len(shp) - 1))
    tok = pl.BlockSpec((1, nslab, rows, LANES), lambda b, n: (b, 0, n, 0))
    return pl.pallas_call(
        functools.partial(_s5_core_kernel, steps=steps),
        out_shape=jax.ShapeDtypeStruct(u_slab.shape, F32),
        grid=(bsz, seq // rows),
        in_specs=[tok, const(wb.shape), const(wc.shape), const(a_tiles.shape), const(d_tiles.shape)],
        out_specs=tok,
        scratch_shapes=[pltpu.VMEM((2, rows, 2 * S5_TILE_ST), F32),
                        pltpu.VMEM((2, rows, 2 * S5_TILE_ST), BF16),
                        pltpu.VMEM((ntile, 2, S5_TILE_ST), F32)],
        compiler_params=_cparams(("parallel", "arbitrary")),
        name="s5_core",
    )(u_slab, wb, wc, a_tiles, d_tiles)


def _s5_post_kernel(y_ref, z_ref, h_ref, wg_ref, bg_ref, wo_ref, gp_ref, o_ref, *, steps):
    nslab = y_ref.shape[1]
    groups = list(_chunk_row_groups(y_ref.shape[2], steps))
    y = jnp.concatenate(
        [jnp.concatenate([y_ref[0, j, pl.ds(cm, SUBLANES, stride=SUBLANES), :] for _, cm in groups], axis=0)
         for j in range(nslab)], axis=-1)
    y = _gelu_tanh(y)
    y = y * _sigmoid(_dot(y.astype(BF16), wg_ref[...]) + bg_ref[...])
    t = y * _silu(z_ref[0])
    o = _dot(t.astype(BF16), wo_ref[...])
    o_ref[0] = h_ref[0] + _rms(o, gp_ref[...])


def _s5_post(y_slab, z, h, w_glu, b_glu, w_out, g_post, steps):
    bsz, seq, d = h.shape
    nslab = y_slab.shape[1]
    tm = min(ROW_TILE, seq)
    assert tm % (SUBLANES * steps) == 0
    row = lambda t: t.reshape(1, -1)
    vec = pl.BlockSpec((1, d), lambda b, m: (0, 0))
    mat = pl.BlockSpec((d, d), lambda b, m: (0, 0))
    slab = pl.BlockSpec((1, nslab, tm, LANES), lambda b, m: (b, 0, m, 0))
    tok = pl.BlockSpec((1, tm, d), lambda b, m: (b, m, 0))
    return pl.pallas_call(
        functools.partial(_s5_post_kernel, steps=steps),
        out_shape=jax.ShapeDtypeStruct(h.shape, F32),
        grid=(bsz, seq // tm),
        in_specs=[slab, tok, tok, mat, vec, mat, vec],
        out_specs=tok,
        compiler_params=_cparams(("parallel", "parallel")),
        name="s5_post",
    )(y_slab, z, h, w_glu.astype(BF16), row(b_glu), w_out.astype(BF16), row(g_post))


def _s5_weights(lam_re, lam_im, log_dt, b_re, b_im, c_re, c_im, d_skip):
    nlayer, ngroup = lam_re.shape[:2]
    flat = lambda t: t.reshape((nlayer * ngroup,) + t.shape[2:])
    ntile = nlayer * ngroup // S5_TILE_GROUPS
    ar, ai, bbr, bbi = _s5_prep(flat(lam_re), flat(lam_im), flat(log_dt), flat(b_re), flat(b_im))
    eye = jnp.eye(S5_TILE_GROUPS, dtype=F32)

    def in_tile(bb):
        t = bb.reshape(ntile, S5_TILE_GROUPS, S5_STATE, S5_GROUP)
        return jnp.einsum('igph,gk->ighkp', t, eye).reshape(ntile, S5_TILE_CH, S5_TILE_ST)

    def out_tile(cc):
        t = cc.reshape(ntile, S5_TILE_GROUPS, S5_GROUP, S5_STATE)
        return jnp.einsum('ighp,gk->ikpgh', t, eye).reshape(ntile, S5_TILE_ST, S5_TILE_CH)

    wb = jnp.concatenate([in_tile(bbr), in_tile(bbi)], axis=2).astype(BF16)
    wc = jnp.concatenate([out_tile(flat(c_re)), -out_tile(flat(c_im))], axis=1).astype(BF16)
    a_tiles = jnp.stack([ar.reshape(ntile, S5_TILE_ST), ai.reshape(ntile, S5_TILE_ST)], axis=1)
    d_tiles = d_skip.reshape(ntile, 1, S5_TILE_CH)
    return wb, wc, a_tiles, d_tiles


def _s5_layer(h, g_pre, g_post, w_in, weights, layer, w_glu, b_glu, w_out):
    bsz, seq, d = h.shape
    steps = min(S5_STEPS, seq // SUBLANES)
    u_slab, z = _norm_proj(h, g_pre, w_in.astype(BF16), (d, d), (steps, 0), (F32, F32))
    y_slab = _s5_core(u_slab, *weights, layer, steps)
    return _s5_post(y_slab, z, h, w_glu, b_glu, w_out, g_post, steps)


def _gate_out_math(o, z, h, g_head, w_out, g_post, head_norm):
    if head_norm:
        parts = []
        for j in range(o.shape[-1] // HG_HEAD):
            oj = o[:, j * HG_HEAD:(j + 1) * HG_HEAD]
            parts.append(oj * lax.rsqrt(jnp.mean(oj * oj, axis=-1, keepdims=True) + NORM_EPS))
        o = jnp.concatenate(parts, axis=-1) * g_head
    t = o * _silu(z)
    return h + _rms(_dot(t.astype(BF16), w_out), g_post)


def _chunk_cumsum(x, chunk):
    pos = lax.broadcasted_iota(jnp.int32, x.shape, 0) % chunk
    sh = 1
    while sh < chunk:
        x = x + jnp.where(pos >= sh, pltpu.roll(x, sh, 0), 0.0)
        sh *= 2
    return x


def _block_mid_rows(x, blk):
    rows, w = x.shape
    half = blk // 2
    if blk >= SUBLANES:
        x3 = x.reshape(rows // blk, blk, w)
        return jnp.broadcast_to(x3[:, half - 1:half, :], (rows // blk, blk, w)).reshape(rows, w)
    pos = lax.broadcasted_iota(jnp.int32, (rows, w), 0) % blk
    out = x
    for p in range(blk):
        d = p - (half - 1)
        if d != 0:
            out = jnp.where(pos == p, pltpu.roll(x, d % rows, 0), out)
    return out


def _hg_core_kernel(q_ref, f_ref, v_ref, lbl_ref, z_ref, h_ref, gn_ref, wo_ref, gp_ref, out_ref,
                    xl_ref, qe_ref, kd_ref, g_ref, st_ref, o_scr, *, layer, chunk):
    rows, width = q_ref.shape[1], q_ref.shape[2]
    pair = 2 * HG_HEAD
    blocks = [2 ** i for i in range(1, int(math.log2(chunk)) + 1)]

    @pl.when(pl.program_id(1) == 0)
    def _():
        st_ref[...] = jnp.zeros_like(st_ref)

    lg = lbl_ref[...]
    ex = jnp.exp(lg - jnp.max(lg, axis=0, keepdims=True))
    p = ex / jnp.sum(ex, axis=0, keepdims=True)
    lb = jnp.zeros((1, width), F32)
    for j in range(1, layer + 1):
        lb = lb + p[j:j + 1, :]

    q = q_ref[0]
    f = lb + (1.0 - lb) * _sigmoid(f_ref[0])
    kk = 1.0 - f
    b = _chunk_cumsum(jnp.log(f), chunk)
    b3 = b.reshape(rows // chunk, chunk, width)
    b_last = b3[:, chunk - 1:chunk, :]
    g_ref[...] = jnp.exp(b_last.reshape(rows // chunk, width))
    qe_ref[...] = (q * jnp.exp(b)).astype(BF16)
    kd_ref[...] = (kk * jnp.exp(jnp.broadcast_to(b_last, b3.shape).reshape(rows, width) - b)).astype(BF16)
    xl_ref[0] = (q * kk).astype(BF16)
    pos = lax.broadcasted_iota(jnp.int32, (rows, width), 0)
    for li, blk in enumerate(blocks):
        w = jnp.exp(-jnp.abs(b - _block_mid_rows(b, blk)))
        xl_ref[li + 1] = (jnp.where(pos % blk >= blk // 2, q, kk) * w).astype(BF16)

    t_i = lax.broadcasted_iota(jnp.int32, (chunk, HG_HEAD), 0)
    s_i = lax.broadcasted_iota(jnp.int32, (chunk, HG_HEAD), 1) % chunk
    eye = t_i == s_i
    lmask = [((t_i // blk) == (s_i // blk)) & ((t_i % blk) >= blk // 2) & ((s_i % blk) < blk // 2)
             for blk in blocks]
    lo = lax.broadcasted_iota(jnp.int32, (chunk, pair), 1) < HG_HEAD
    bd = (lax.broadcasted_iota(jnp.int32, (pair, pair), 0) // HG_HEAD
          == lax.broadcasted_iota(jnp.int32, (pair, pair), 1) // HG_HEAD)
    ones_bd = jnp.where(lax.broadcasted_iota(jnp.int32, (pair, HG_HEAD), 0) // HG_HEAD
                        == lax.broadcasted_iota(jnp.int32, (pair, HG_HEAD), 1) // chunk, 1.0, 0.0).astype(BF16)

    def split_heads(t):
        z = jnp.zeros_like(t)
        return jnp.concatenate([jnp.where(lo, t, z), jnp.where(lo, z, t)], axis=0)

    for c in range(rows // chunk):
        rs = slice(c * chunk, (c + 1) * chunk)
        for pr in range(width // pair):
            ls = slice(pr * pair, (pr + 1) * pair)
            sc = jnp.where(eye, _dot(xl_ref[0, rs, ls], ones_bd), 0.0)
            for li in range(len(blocks)):
                xl = xl_ref[li + 1, rs, ls]
                sc = jnp.where(lmask[li], _dot_nt(xl, split_heads(xl)), sc)
            vpair = v_ref[0, rs, ls]
            st = st_ref[pr]
            o = _dot(sc.astype(BF16), split_heads(vpair)) + _dot_nt(qe_ref[rs, ls], st.astype(BF16))
            o_scr[rs, ls] = o
            upd = _dot_tn(vpair, kd_ref[rs, ls])
            st_ref[pr] = g_ref[c:c + 1, ls] * st + jnp.where(bd, upd, 0.0)

    out_ref[0] = _gate_out_math(o_scr[...], z_ref[0], h_ref[0], gn_ref[...], wo_ref[...], gp_ref[...], True)


def _hg_core(q, fz, v, z, h, lb_logits, layer, norm_g, w_out, g_post):
    bsz, seq, width = q.shape
    blk = min(HG_BLOCK, seq)
    chunk = min(HG_CHUNK, blk)
    nlev = int(math.log2(chunk))
    assert HG_HEAD == 2 * chunk and width % (2 * HG_HEAD) == 0
    tok = pl.BlockSpec((1, blk, width), lambda b, n: (b, n, 0))
    const = lambda shp: pl.BlockSpec(shp, lambda b, n: (0, 0))
    row = lambda t: t.reshape(1, -1)
    return pl.pallas_call(
        functools.partial(_hg_core_kernel, layer=layer, chunk=chunk),
        out_shape=jax.ShapeDtypeStruct(h.shape, F32),
        grid=(bsz, seq // blk),
        in_specs=[tok, tok, tok, const(lb_logits.shape), tok, tok, const((1, width)), const(w_out.shape),
                  const((1, width))],
        out_specs=tok,
        scratch_shapes=[pltpu.VMEM((nlev + 1, blk, width), BF16),
                        pltpu.VMEM((blk, width), BF16),
                        pltpu.VMEM((blk, width), BF16),
                        pltpu.VMEM((blk // chunk, width), F32),
                        pltpu.VMEM((width // (2 * HG_HEAD), 2 * HG_HEAD, 2 * HG_HEAD), F32),
                        pltpu.VMEM((blk, width), F32)],
        compiler_params=_cparams(("parallel", "arbitrary")),
        name="hgrn2_core",
    )(q, fz, v, lb_logits, z, h, row(norm_g), w_out.astype(BF16), row(g_post))


def _hg_layer(h, g_pre, g_post, w_in, lb_logits, layer, norm_g, w_out):
    d = h.shape[-1]
    q, fz, v, z = _norm_proj(h, g_pre, w_in.astype(BF16), (d, d, d, d), (0,) * 4, (F32, F32, BF16, F32))
    return _hg_core(q, fz, v, z, h, lb_logits, layer, norm_g, w_out, g_post)


def _at_proj_kernel(h_ref, g_ref, w_ref, bias_ref, pos_ref, invf_ref, q_ref, k_ref, v_ref, z_ref,
                    *, qw, kvw):
    y = _rms(h_ref[0], g_ref[...]).astype(BF16)
    half = AT_HEAD // 2
    nq4 = LANES // half
    ang = pos_ref[0].astype(F32) * invf_ref[...]
    cos4 = jnp.cos(ang)
    sin4 = jnp.sin(ang)
    lane4 = lax.broadcasted_iota(jnp.int32, ang.shape, 1)

    def spread(t):
        parts = []
        for j in range(nq4):
            m = jnp.where(lane4 // half == j, t, 0.0)
            x = m
            for s in range(1, nq4):
                x = x + pltpu.roll(m, s * half, 1)
            parts.append(x)
        return jnp.concatenate(parts, axis=0)

    lane = lax.broadcasted_iota(jnp.int32, (h_ref.shape[1], LANES), 1)
    first = (lane % AT_HEAD) < half
    low = lane < AT_HEAD
    cosf = spread(cos4)
    sinf = spread(sin4)
    sinf = jnp.where(first, -sinf, sinf)

    def rope(t):
        partner = jnp.where(first, pltpu.roll(t, LANES - AT_HEAD // 2, 1), pltpu.roll(t, AT_HEAD // 2, 1))
        return t * cosf + partner * sinf

    def store_dup(ref, j, t):
        r = pltpu.roll(t, AT_HEAD, 1)
        ref[0, :, (2 * j) * LANES:(2 * j + 1) * LANES] = jnp.where(low, t, r).astype(ref.dtype)
        ref[0, :, (2 * j + 1) * LANES:(2 * j + 2) * LANES] = jnp.where(low, r, t).astype(ref.dtype)

    for j in range(qw // LANES):
        sl = slice(j * LANES, (j + 1) * LANES)
        q_ref[0, :, sl] = rope(_dot(y, w_ref[:, sl]) + bias_ref[:, sl]).astype(q_ref.dtype)
    for j in range(kvw // LANES):
        sl = slice(qw + j * LANES, qw + (j + 1) * LANES)
        store_dup(k_ref, j, rope(_dot(y, w_ref[:, sl]) + bias_ref[:, sl]))
    for j in range(kvw // LANES):
        sl = slice(qw + kvw + j * LANES, qw + kvw + (j + 1) * LANES)
        store_dup(v_ref, j, _dot(y, w_ref[:, sl]) + bias_ref[:, sl])
    off = qw + 2 * kvw
    z_ref[0] = _dot(y, w_ref[:, off:off + qw])


def _at_proj(h, g, w_bf16, bias, positions, qw, kvw):
    bsz, seq, d = h.shape
    tm = min(ROW_TILE, seq)
    half = AT_HEAD // 2
    inv_freq = ROPE_THETA ** (-jnp.arange(0, AT_HEAD, 2, dtype=F32) / AT_HEAD)
    nq4 = LANES // half
    invf = jnp.tile(inv_freq, nq4).reshape(1, LANES)
    pos4 = positions.reshape(bsz, seq // tm, nq4, tm // nq4).transpose(0, 1, 3, 2)
    pos4 = jnp.repeat(pos4, half, axis=-1).reshape(bsz, seq // nq4, LANES)
    tokspec = lambda w: pl.BlockSpec((1, tm, w), lambda b, m: (b, m, 0))
    const = lambda shp: pl.BlockSpec(shp, lambda b, m: (0, 0))
    return pl.pallas_call(
        functools.partial(_at_proj_kernel, qw=qw, kvw=kvw),
        out_shape=[jax.ShapeDtypeStruct((bsz, seq, qw), BF16), jax.ShapeDtypeStruct((bsz, seq, 2 * kvw), BF16),
                   jax.ShapeDtypeStruct((bsz, seq, 2 * kvw), BF16), jax.ShapeDtypeStruct((bsz, seq, qw), F32)],
        grid=(bsz, seq // tm),
        in_specs=[tokspec(d), const((1, d)), const(w_bf16.shape), const((1, qw + 2 * kvw)),
                  pl.BlockSpec((1, tm // nq4, LANES), lambda b, m: (b, m, 0)), const((1, LANES))],
        out_specs=[tokspec(qw), tokspec(2 * kvw), tokspec(2 * kvw), tokspec(qw)],
        compiler_params=_cparams(("parallel", "parallel")),
        name="attn_proj",
    )(h, g.reshape(1, d), w_bf16, bias.reshape(1, -1), pos4, invf)


def _at_core_kernel(q_ref, kc_ref, kp_ref, vc_ref, vp_ref, sink_ref, z_ref, h_ref, wo_ref, gp_ref, out_ref):
    blk = q_ref.shape[1]
    n = pl.program_id(1)
    ngroup = kc_ref.shape[2] // LANES
    qi = lax.broadcasted_iota(jnp.int32, (2 * blk, 4 * blk), 0) % blk
    kj = lax.broadcasted_iota(jnp.int32, (2 * blk, 4 * blk), 1) % (2 * blk)
    dist = qi + blk - kj
    mask = (dist >= 0) & (dist < blk) & ((kj >= blk) | (n > 0))
    low = lax.broadcasted_iota(jnp.int32, (2 * blk, LANES), 1) < AT_HEAD
    top = lax.broadcasted_iota(jnp.int32, (2 * blk, 1), 0) < blk
    ones_lo = jnp.where(low, 1.0, 0.0).astype(BF16)
    ones_hi = jnp.where(low, 0.0, 1.0).astype(BF16)
    scale = AT_HEAD ** -0.5

    scores = []
    for g in range(ngroup):
        sl = slice(g * LANES, (g + 1) * LANES)
        kdup = jnp.concatenate([kp_ref[0, :, sl], kc_ref[0, :, sl]], axis=0)
        rhs = jnp.concatenate([jnp.where(low, kdup, 0.0), jnp.where(low, 0.0, kdup)], axis=0).astype(BF16)
        lhs = jnp.concatenate([q_ref[0, :, (2 * g) * LANES:(2 * g + 1) * LANES],
                               q_ref[0, :, (2 * g + 1) * LANES:(2 * g + 2) * LANES]], axis=0).astype(BF16)
        scores.append(_dot_nt(lhs, rhs))

    probs, sink_terms = [], []
    for g in range(ngroup):
        s = jnp.where(mask, scores[g] * scale, -jnp.inf)
        sink_a = jnp.where(top, sink_ref[0:1, 4 * g:4 * g + 1], sink_ref[0:1, 4 * g + 2:4 * g + 3])
        sink_b = jnp.where(top, sink_ref[0:1, 4 * g + 1:4 * g + 2], sink_ref[0:1, 4 * g + 3:4 * g + 4])
        m_a = jnp.maximum(jnp.max(s[:, :2 * blk], axis=-1, keepdims=True), sink_a)
        m_b = jnp.maximum(jnp.max(s[:, 2 * blk:], axis=-1, keepdims=True), sink_b)
        e = jnp.concatenate([jnp.exp(s[:, :2 * blk] - m_a), jnp.exp(s[:, 2 * blk:] - m_b)], axis=1)
        probs.append(e.astype(BF16))
        sink_terms.append(jnp.where(low, jnp.exp(sink_a - m_a), jnp.exp(sink_b - m_b)))

    o_slabs = []
    for g in range(ngroup):
        sl = slice(g * LANES, (g + 1) * LANES)
        vdup = jnp.concatenate([vp_ref[0, :, sl], vc_ref[0, :, sl]], axis=0)
        va = jnp.where(low, vdup, 0.0).astype(BF16)
        vb = jnp.where(low, 0.0, vdup).astype(BF16)
        rhs = jnp.concatenate([jnp.concatenate([va, ones_lo], axis=1),
                               jnp.concatenate([vb, ones_hi], axis=1)], axis=0)
        out = _dot(probs[g], rhs)
        o = out[:, :LANES] / (out[:, LANES:] + sink_terms[g])
        o_slabs += [o[:blk], o[blk:]]

    out_ref[0] = _gate_out_math(jnp.concatenate(o_slabs, axis=-1), z_ref[0], h_ref[0], None, wo_ref[...],
                                gp_ref[...], False)


def _at_core(q, k_dup, v_dup, sinks, z, h, w_out, g_post):
    bsz, seq, qw = q.shape
    kvw = k_dup.shape[-1]
    d = h.shape[-1]
    blk = AT_BLOCK
    nq = qw // AT_HEAD
    assert nq == 4 * (kvw // LANES) and qw == 2 * kvw
    cur = lambda w: pl.BlockSpec((1, blk, w), lambda b, n: (b, n, 0))
    prev = lambda w: pl.BlockSpec((1, blk, w), lambda b, n: (b, jnp.maximum(n - 1, 0), 0))
    const = lambda shp: pl.BlockSpec(shp, lambda b, n: (0, 0))
    return pl.pallas_call(
        _at_core_kernel,
        out_shape=jax.ShapeDtypeStruct(h.shape, F32),
        grid=(bsz, seq // blk),
        in_specs=[cur(qw), cur(kvw), prev(kvw), cur(kvw), prev(kvw), const((1, nq)),
                  cur(qw), cur(d), const(w_out.shape), const((1, d))],
        out_specs=cur(d),
        compiler_params=_cparams(("parallel", "arbitrary")),
        name="attn_core",
    )(q, k_dup, k_dup, v_dup, v_dup, sinks.reshape(1, nq), z, h, w_out.astype(BF16), g_post.reshape(1, d))


def _at_layer(h, positions, g_pre, g_post, w_in, b_in, sinks, w_out):
    qw = w_out.shape[0]
    kvw = (b_in.shape[0] - qw) // 2
    q, k, v, z = _at_proj(h, g_pre, w_in.astype(BF16), b_in, positions, qw, kvw)
    return _at_core(q, k, v, sinks, z, h, w_out, g_post)


def kernel(x, positions, norm_pre, norm_post, s5_w_in, s5_lambda_re, s5_lambda_im, s5_log_dt, s5_b_re, s5_b_im, s5_c_re, s5_c_im, s5_d, s5_w_glu, s5_b_glu, s5_w_out, hg_w_in, hg_lb_logits, hg_norm, hg_w_out, at_w_in, at_b_in, at_sinks, at_w_out):
    depth = norm_pre.shape[0]
    s5_weights = _s5_weights(s5_lambda_re, s5_lambda_im, s5_log_dt, s5_b_re, s5_b_im, s5_c_re, s5_c_im, s5_d)
    h = x
    for i in range(depth):
        kind, j = i % 3, i // 3
        if kind == 0:
            h = _s5_layer(h, norm_pre[i], norm_post[i], s5_w_in[j], s5_weights, j, s5_w_glu[j], s5_b_glu[j],
                          s5_w_out[j])
        elif kind == 1:
            h = _hg_layer(h, norm_pre[i], norm_post[i], hg_w_in[j], hg_lb_logits, i, hg_norm[j], hg_w_out[j])
        else:
            h = _at_layer(h, positions, norm_pre[i], norm_post[i], at_w_in[j], at_b_in[j], at_sinks[j],
                          at_w_out[j])
    return h
```

```python
import functools
import math

import jax
import jax.numpy as jnp
from jax import lax
from jax.experimental import pallas as pl
from jax.experimental.pallas import tpu as pltpu

F32 = jnp.float32
BF16 = jnp.bfloat16

NORM_EPS = 1e-6
LANES = 128
SUBLANES = 8
VMEM_LIMIT = 56 * 1024 * 1024

S5_GROUP = 16
S5_STATE = 64
S5_TILE_GROUPS = 16
S5_TILE_CH = S5_TILE_GROUPS * S5_GROUP
S5_TILE_ST = S5_TILE_GROUPS * S5_STATE
S5_STEPS = 64
HG_HEAD = 128
HG_CHUNK = 64
HG_BLOCK = 256
AT_HEAD = 64
AT_KV_HEADS = 4
AT_BLOCK = 128
AT_SUBBLOCKS = 4
ROPE_THETA = 10000.0

ROW_TILE = 512


def _cparams(sem):
    return pltpu.CompilerParams(dimension_semantics=sem, vmem_limit_bytes=VMEM_LIMIT)


def _rms(x, g):
    return x * lax.rsqrt(jnp.mean(x * x, axis=-1, keepdims=True) + NORM_EPS) * g


def _sigmoid(x):
    return 1.0 / (1.0 + jnp.exp(-x))


def _sigmoid_gate(x):
    return 0.5 + 0.5 * jnp.tanh(0.5 * x)


def _silu(x):
    return x * _sigmoid_gate(x)


def _gelu_tanh(x):
    c = math.sqrt(2.0 / math.pi)
    return 0.5 * x * (1.0 + jnp.tanh(c * (x + 0.044715 * (x * x * x))))


def _dot(a, b):
    return jnp.dot(a, b, preferred_element_type=F32)


def _dot_nt(a, b):
    return lax.dot_general(a, b, (((1,), (1,)), ((), ())), preferred_element_type=F32)


def _dot_tn(a, b):
    return lax.dot_general(a, b, (((0,), (0,)), ((), ())), preferred_element_type=F32)


def _chunk_row_groups(rows, steps):
    for base in range(0, rows, SUBLANES * steps):
        for r in range(SUBLANES):
            for k0 in range(0, steps, SUBLANES):
                yield base + r * steps + k0, base + k0 * SUBLANES + r


def _norm_proj_kernel(h_ref, g_ref, w_ref, *out_refs, widths, slab_steps):
    y = _rms(h_ref[0], g_ref[...]).astype(BF16)
    off = 0
    for o_ref, width, steps in zip(out_refs, widths, slab_steps):
        r = _dot(y, w_ref[:, off:off + width])
        if steps:
            for nat, cm in _chunk_row_groups(r.shape[0], steps):
                for j in range(width // LANES):
                    o_ref[0, j, pl.ds(cm, SUBLANES, stride=SUBLANES), :] = (
                        r[nat:nat + SUBLANES, j * LANES:(j + 1) * LANES])
        else:
            o_ref[0] = r.astype(o_ref.dtype)
        off += width


def _norm_proj(h, g, w_bf16, widths, slab_steps, dtypes):
    bsz, seq, d = h.shape
    tm = min(ROW_TILE, seq)
    out_shape, out_specs = [], []
    for width, as_slab, dtype in zip(widths, slab_steps, dtypes):
        if as_slab:
            assert tm % (SUBLANES * as_slab) == 0 and dtype == F32
            ns = width // LANES
            out_shape.append(jax.ShapeDtypeStruct((bsz, ns, seq, LANES), F32))
            out_specs.append(pl.BlockSpec((1, ns, tm, LANES), lambda b, m: (b, 0, m, 0)))
        else:
            out_shape.append(jax.ShapeDtypeStruct((bsz, seq, width), dtype))
            out_specs.append(pl.BlockSpec((1, tm, width), lambda b, m: (b, m, 0)))
    return pl.pallas_call(
        functools.partial(_norm_proj_kernel, widths=tuple(widths), slab_steps=tuple(slab_steps)),
        out_shape=out_shape,
        grid=(bsz, seq // tm),
        in_specs=[pl.BlockSpec((1, tm, d), lambda b, m: (b, m, 0)),
                  pl.BlockSpec((1, d), lambda b, m: (0, 0)),
                  pl.BlockSpec(w_bf16.shape, lambda b, m: (0, 0))],
        out_specs=out_specs,
        compiler_params=_cparams(("parallel", "parallel")),
        name="norm_proj",
    )(h, g.reshape(1, d), w_bf16)


def _s5_prep_kernel(lr_ref, li_ref, ldt_ref, br_ref, bi_ref, ar_ref, ai_ref, bbr_ref, bbi_ref):
    lr = lr_ref[...]
    li = li_ref[...]
    dt = jnp.exp(ldt_ref[...])
    mag = jnp.exp(lr * dt)
    ar = mag * jnp.cos(li * dt)
    ai = mag * jnp.sin(li * dt)
    den = lr * lr + li * li
    qr = ((ar - 1.0) * lr + ai * li) / den
    qi = (ai * lr - (ar - 1.0) * li) / den
    br = br_ref[...]
    bi = bi_ref[...]
    ar_ref[...] = ar
    ai_ref[...] = ai
    bbr_ref[...] = qr * br - qi * bi
    bbi_ref[...] = qr * bi + qi * br


def _s5_prep(lam_re, lam_im, log_dt, b_re, b_im):
    g, p, hh = b_re.shape
    rep = lambda t: jnp.repeat(t, hh, axis=1)
    shp = jax.ShapeDtypeStruct((g, p * hh), F32)
    ar, ai, bbr, bbi = pl.pallas_call(
        _s5_prep_kernel, out_shape=[shp, shp, shp, shp], name="s5_prep",
    )(rep(lam_re), rep(lam_im), log_dt.reshape(g, 1), b_re.reshape(g, p * hh), b_im.reshape(g, p * hh))
    return ar[:, ::hh], ai[:, ::hh], bbr.reshape(g, p, hh), bbi.reshape(g, p, hh)


def _s5_core_kernel(u_ref, wb_ref, wc_ref, a_ref, d_ref, y_ref, x_ref, s_ref, carry_ref, *, steps):
    nst = S5_TILE_ST
    nslab = S5_TILE_CH // LANES
    ntile = wb_ref.shape[0]

    @pl.when(pl.program_id(1) == 0)
    def _():
        carry_ref[...] = jnp.zeros_like(carry_ref)

    def load_u(i):
        return jnp.concatenate([u_ref[0, i * nslab + j] for j in range(nslab)], axis=-1)

    def project_in(i):
        x_ref[i % 2] = _dot(load_u(i).astype(BF16), wb_ref[i])

    def scan(i):
        slot = i % 2
        ar1 = a_ref[i, 0:1, :]
        ai1 = a_ref[i, 1:2, :]
        ar = jnp.broadcast_to(ar1, (SUBLANES, nst))
        ai = jnp.broadcast_to(ai1, (SUBLANES, nst))

        def advance(k, sr, si):
            xr = x_ref[slot, k * SUBLANES:(k + 1) * SUBLANES, 0:nst]
            xi = x_ref[slot, k * SUBLANES:(k + 1) * SUBLANES, nst:2 * nst]
            return ar * sr - ai * si + xr, ar * si + ai * sr + xi

        er = jnp.zeros((SUBLANES, nst), F32)
        ei = jnp.zeros((SUBLANES, nst), F32)
        for k in range(steps):
            er, ei = advance(k, er, ei)

        pr, pi = ar1, ai1
        for _ in range(int(math.log2(steps))):
            pr, pi = pr * pr - pi * pi, 2.0 * pr * pi

        rid = lax.broadcasted_iota(jnp.int32, (SUBLANES, nst), 0)
        cr = carry_ref[i, 0:1, :]
        ci = carry_ref[i, 1:2, :]
        sr = jnp.zeros((SUBLANES, nst), F32)
        si = jnp.zeros((SUBLANES, nst), F32)
        for r in range(SUBLANES):
            sr = jnp.where(rid == r, jnp.broadcast_to(cr, (SUBLANES, nst)), sr)
            si = jnp.where(rid == r, jnp.broadcast_to(ci, (SUBLANES, nst)), si)
            cr, ci = pr * cr - pi * ci + er[r:r + 1, :], pr * ci + pi * cr + ei[r:r + 1, :]
        carry_ref[i, 0:1, :] = cr
        carry_ref[i, 1:2, :] = ci

        for k in range(0, steps, 2):
            r0, i0 = advance(k, sr, si)
            sr, si = advance(k + 1, r0, i0)
            rows2 = slice(k * SUBLANES, (k + 2) * SUBLANES)
            s_ref[slot, rows2, 0:nst] = jnp.concatenate([r0, sr], axis=0).astype(BF16)
            s_ref[slot, rows2, nst:2 * nst] = jnp.concatenate([i0, si], axis=0).astype(BF16)

    def project_out(i):
        y = _dot(s_ref[i % 2], wc_ref[i]) + d_ref[i] * load_u(i)
        for j in range(nslab):
            y_ref[0, i * nslab + j] = y[:, j * LANES:(j + 1) * LANES]

    project_in(0)
    for i in range(ntile):
        if i + 1 < ntile:
            project_in(i + 1)
        scan(i)
        project_out(i)


def _s5_core(u_slab, wb, wc, a_tiles, d_tiles, layer, steps):
    bsz, nslab, seq, _ = u_slab.shape
    ntile = nslab * LANES // S5_TILE_CH
    rows = SUBLANES * steps
    const = lambda shp: pl.BlockSpec((ntile,) + shp[1:], lambda b, n: (layer,) + (0,) * (len(shp) - 1))
    tok = pl.BlockSpec((1, nslab, rows, LANES), lambda b, n: (b, 0, n, 0))
    return pl.pallas_call(
        functools.partial(_s5_core_kernel, steps=steps),
        out_shape=jax.ShapeDtypeStruct(u_slab.shape, F32),
        grid=(bsz, seq // rows),
        in_specs=[tok, const(wb.shape), const(wc.shape), const(a_tiles.shape), const(d_tiles.shape)],
        out_specs=tok,
        scratch_shapes=[pltpu.VMEM((2, rows, 2 * S5_TILE_ST), F32),
                        pltpu.VMEM((2, rows, 2 * S5_TILE_ST), BF16),
                        pltpu.VMEM((ntile, 2, S5_TILE_ST), F32)],
        compiler_params=_cparams(("parallel", "arbitrary")),
        name="s5_core",
    )(u_slab, wb, wc, a_tiles, d_tiles)


def _s5_post_kernel(y_ref, z_ref, h_ref, wg_ref, bg_ref, wo_ref, gp_ref, o_ref, *, steps):
    nslab = y_ref.shape[1]
    groups = list(_chunk_row_groups(y_ref.shape[2], steps))
    y = jnp.concatenate(
        [jnp.concatenate([y_ref[0, j, pl.ds(cm, SUBLANES, stride=SUBLANES), :] for _, cm in groups], axis=0)
         for j in range(nslab)], axis=-1)
    y = _gelu_tanh(y)
    y = y * _sigmoid_gate(_dot(y.astype(BF16), wg_ref[...]) + bg_ref[...])
    t = y * _silu(z_ref[0])
    o = _dot(t.astype(BF16), wo_ref[...])
    o_ref[0] = h_ref[0] + _rms(o, gp_ref[...])


def _s5_post(y_slab, z, h, w_glu, b_glu, w_out, g_post, steps):
    bsz, seq, d = h.shape
    nslab = y_slab.shape[1]
    tm = min(ROW_TILE, seq)
    assert tm % (SUBLANES * steps) == 0
    row = lambda t: t.reshape(1, -1)
    vec = pl.BlockSpec((1, d), lambda b, m: (0, 0))
    mat = pl.BlockSpec((d, d), lambda b, m: (0, 0))
    slab = pl.BlockSpec((1, nslab, tm, LANES), lambda b, m: (b, 0, m, 0))
    tok = pl.BlockSpec((1, tm, d), lambda b, m: (b, m, 0))
    return pl.pallas_call(
        functools.partial(_s5_post_kernel, steps=steps),
        out_shape=jax.ShapeDtypeStruct(h.shape, F32),
        grid=(bsz, seq // tm),
        in_specs=[slab, tok, tok, mat, vec, mat, vec],
        out_specs=tok,
        compiler_params=_cparams(("parallel", "parallel")),
        name="s5_post",
    )(y_slab, z, h, w_glu.astype(BF16), row(b_glu), w_out.astype(BF16), row(g_post))


def _s5_weights(lam_re, lam_im, log_dt, b_re, b_im, c_re, c_im, d_skip):
    nlayer, ngroup = lam_re.shape[:2]
    flat = lambda t: t.reshape((nlayer * ngroup,) + t.shape[2:])
    ntile = nlayer * ngroup // S5_TILE_GROUPS
    ar, ai, bbr, bbi = _s5_prep(flat(lam_re), flat(lam_im), flat(log_dt), flat(b_re), flat(b_im))
    eye = jnp.eye(S5_TILE_GROUPS, dtype=F32)

    def in_tile(bb):
        t = bb.reshape(ntile, S5_TILE_GROUPS, S5_STATE, S5_GROUP)
        return jnp.einsum('igph,gk->ighkp', t, eye).reshape(ntile, S5_TILE_CH, S5_TILE_ST)

    def out_tile(cc):
        t = cc.reshape(ntile, S5_TILE_GROUPS, S5_GROUP, S5_STATE)
        return jnp.einsum('ighp,gk->ikpgh', t, eye).reshape(ntile, S5_TILE_ST, S5_TILE_CH)

    wb = jnp.concatenate([in_tile(bbr), in_tile(bbi)], axis=2).astype(BF16)
    wc = jnp.concatenate([out_tile(flat(c_re)), -out_tile(flat(c_im))], axis=1).astype(BF16)
    a_tiles = jnp.stack([ar.reshape(ntile, S5_TILE_ST), ai.reshape(ntile, S5_TILE_ST)], axis=1)
    d_tiles = d_skip.reshape(ntile, 1, S5_TILE_CH)
    return wb, wc, a_tiles, d_tiles


def _s5_layer(h, g_pre, g_post, w_in, weights, layer, w_glu, b_glu, w_out):
    bsz, seq, d = h.shape
    steps = min(S5_STEPS, seq // SUBLANES)
    u_slab, z = _norm_proj(h, g_pre, w_in.astype(BF16), (d, d), (steps, 0), (F32, F32))
    y_slab = _s5_core(u_slab, *weights, layer, steps)
    return _s5_post(y_slab, z, h, w_glu, b_glu, w_out, g_post, steps)


def _gate_out_math(o, z, h, g_head, w_out, g_post, head_norm):
    if head_norm:
        parts = []
        for j in range(o.shape[-1] // HG_HEAD):
            oj = o[:, j * HG_HEAD:(j + 1) * HG_HEAD]
            parts.append(oj * lax.rsqrt(jnp.mean(oj * oj, axis=-1, keepdims=True) + NORM_EPS))
        o = jnp.concatenate(parts, axis=-1) * g_head
    t = o * _silu(z)
    return h + _rms(_dot(t.astype(BF16), w_out), g_post)


def _chunk_cumsum(x, chunk):
    rows, w = x.shape
    per = chunk // SUBLANES
    x3 = x.reshape(rows // SUBLANES, SUBLANES, w)
    pos = lax.broadcasted_iota(jnp.int32, x3.shape, 1)
    sh = 1
    while sh < SUBLANES:
        x3 = x3 + jnp.where(pos >= sh, pltpu.roll(x3, sh, 1), 0.0)
        sh *= 2
    x4 = x3.reshape(rows // chunk, per, SUBLANES, w)
    tot = x4[:, :, SUBLANES - 1:SUBLANES, :]
    pref = [jnp.zeros_like(tot[:, 0:1])]
    for j in range(1, per):
        pref.append(pref[-1] + tot[:, j - 1:j])
    return (x4 + jnp.concatenate(pref, axis=1)).reshape(rows, w)


def _block_mid_rows(x, blk):
    rows, w = x.shape
    half = blk // 2
    if blk >= SUBLANES:
        x3 = x.reshape(rows // blk, blk, w)
        return jnp.broadcast_to(x3[:, half - 1:half, :], (rows // blk, blk, w)).reshape(rows, w)
    pos = lax.broadcasted_iota(jnp.int32, (rows, w), 0) % blk
    out = x
    for p in range(blk):
        d = p - (half - 1)
        if d != 0:
            out = jnp.where(pos == p, pltpu.roll(x, d % rows, 0), out)
    return out


def _hg_core_kernel(q_ref, f_ref, v_ref, lbl_ref, z_ref, h_ref, gn_ref, wo_ref, gp_ref, out_ref,
                    xl_ref, qe_ref, kd_ref, g_ref, st_ref, o_scr, *, layer, chunk):
    rows, width = q_ref.shape[1], q_ref.shape[2]
    pair = 2 * HG_HEAD
    blocks = [2 ** i for i in range(1, int(math.log2(chunk)) + 1)]

    @pl.when(pl.program_id(1) == 0)
    def _():
        st_ref[...] = jnp.zeros_like(st_ref)

    lg = lbl_ref[...]
    ex = jnp.exp(lg - jnp.max(lg, axis=0, keepdims=True))
    p = ex / jnp.sum(ex, axis=0, keepdims=True)
    lb = jnp.zeros((1, width), F32)
    for j in range(1, layer + 1):
        lb = lb + p[j:j + 1, :]

    pos = lax.broadcasted_iota(jnp.int32, (rows, HG_HEAD), 0)
    second = [pos % blk >= blk // 2 for blk in blocks]
    for hd in range(width // HG_HEAD):
        hs = slice(hd * HG_HEAD, (hd + 1) * HG_HEAD)
        q = q_ref[0, :, hs]
        lbh = lb[:, hs]
        f = lbh + (1.0 - lbh) * _sigmoid(f_ref[0, :, hs])
        kk = 1.0 - f
        b = _chunk_cumsum(jnp.log(f), chunk)
        b3 = b.reshape(rows // chunk, chunk, HG_HEAD)
        b_last = b3[:, chunk - 1:chunk, :]
        g_ref[:, hs] = jnp.exp(b_last.reshape(rows // chunk, HG_HEAD))
        qe_ref[:, hs] = (q * jnp.exp(b)).astype(BF16)
        kd_ref[:, hs] = (kk * jnp.exp(jnp.broadcast_to(b_last, b3.shape).reshape(rows, HG_HEAD) - b)).astype(BF16)
        xl_ref[0, :, hs] = (q * kk).astype(BF16)
        for li, blk in enumerate(blocks):
            w = jnp.exp(-jnp.abs(b - _block_mid_rows(b, blk)))
            xl_ref[li + 1, :, hs] = (jnp.where(second[li], q, kk) * w).astype(BF16)

    t_i = lax.broadcasted_iota(jnp.int32, (chunk, HG_HEAD), 0)
    s_i = lax.broadcasted_iota(jnp.int32, (chunk, HG_HEAD), 1) % chunk
    eye = t_i == s_i
    lmask = [((t_i // blk) == (s_i // blk)) & ((t_i % blk) >= blk // 2) & ((s_i % blk) < blk // 2)
             for blk in blocks]
    lo = lax.broadcasted_iota(jnp.int32, (chunk, pair), 1) < HG_HEAD
    bd = (lax.broadcasted_iota(jnp.int32, (pair, pair), 0) // HG_HEAD
          == lax.broadcasted_iota(jnp.int32, (pair, pair), 1) // HG_HEAD)
    ones_bd = jnp.where(lax.broadcasted_iota(jnp.int32, (pair, HG_HEAD), 0) // HG_HEAD
                        == lax.broadcasted_iota(jnp.int32, (pair, HG_HEAD), 1) // chunk, 1.0, 0.0).astype(BF16)

    def split_heads(t):
        z = jnp.zeros_like(t)
        return jnp.concatenate([jnp.where(lo, t, z), jnp.where(lo, z, t)], axis=0)

    for c in range(rows // chunk):
        rs = slice(c * chunk, (c + 1) * chunk)
        for pr in range(width // pair):
            ls = slice(pr * pair, (pr + 1) * pair)
            sc = jnp.where(eye, _dot(xl_ref[0, rs, ls], ones_bd), 0.0)
            for li in range(len(blocks)):
                xl = xl_ref[li + 1, rs, ls]
                sc = jnp.where(lmask[li], _dot_nt(xl, split_heads(xl)), sc)
            vpair = v_ref[0, rs, ls]
            st = st_ref[pr]
            o = _dot(sc.astype(BF16), split_heads(vpair)) + _dot_nt(qe_ref[rs, ls], st.astype(BF16))
            o_scr[rs, ls] = o
            upd = _dot_tn(vpair, kd_ref[rs, ls])
            st_ref[pr] = g_ref[c:c + 1, ls] * st + jnp.where(bd, upd, 0.0)

    out_ref[0] = _gate_out_math(o_scr[...], z_ref[0], h_ref[0], gn_ref[...], wo_ref[...], gp_ref[...], True)


def _hg_core(q, fz, v, z, h, lb_logits, layer, norm_g, w_out, g_post):
    bsz, seq, width = q.shape
    blk = min(HG_BLOCK, seq)
    chunk = min(HG_CHUNK, blk)
    nlev = int(math.log2(chunk))
    assert HG_HEAD == 2 * chunk and width % (2 * HG_HEAD) == 0
    tok = pl.BlockSpec((1, blk, width), lambda b, n: (b, n, 0))
    const = lambda shp: pl.BlockSpec(shp, lambda b, n: (0, 0))
    row = lambda t: t.reshape(1, -1)
    return pl.pallas_call(
        functools.partial(_hg_core_kernel, layer=layer, chunk=chunk),
        out_shape=jax.ShapeDtypeStruct(h.shape, F32),
        grid=(bsz, seq // blk),
        in_specs=[tok, tok, tok, const(lb_logits.shape), tok, tok, const((1, width)), const(w_out.shape),
                  const((1, width))],
        out_specs=tok,
        scratch_shapes=[pltpu.VMEM((nlev + 1, blk, width), BF16),
                        pltpu.VMEM((blk, width), BF16),
                        pltpu.VMEM((blk, width), BF16),
                        pltpu.VMEM((blk // chunk, width), F32),
                        pltpu.VMEM((width // (2 * HG_HEAD), 2 * HG_HEAD, 2 * HG_HEAD), F32),
                        pltpu.VMEM((blk, width), F32)],
        compiler_params=_cparams(("parallel", "arbitrary")),
        name="hgrn2_core",
    )(q, fz, v, lb_logits, z, h, row(norm_g), w_out.astype(BF16), row(g_post))


def _hg_layer(h, g_pre, g_post, w_in, lb_logits, layer, norm_g, w_out):
    d = h.shape[-1]
    q, fz, v, z = _norm_proj(h, g_pre, w_in.astype(BF16), (d, d, d, d), (0,) * 4, (F32, F32, BF16, F32))
    return _hg_core(q, fz, v, z, h, lb_logits, layer, norm_g, w_out, g_post)


def _at_proj_kernel(h_ref, g_ref, w_ref, bias_ref, pos_ref, invf_ref, q_ref, k_ref, v_ref, z_ref,
                    *, qw, kvw):
    y = _rms(h_ref[0], g_ref[...]).astype(BF16)
    half = AT_HEAD // 2
    nq4 = LANES // half
    ang = pos_ref[0].astype(F32) * invf_ref[...]
    cos4 = jnp.cos(ang)
    sin4 = jnp.sin(ang)
    lane4 = lax.broadcasted_iota(jnp.int32, ang.shape, 1)

    def spread(t):
        parts = []
        for j in range(nq4):
            m = jnp.where(lane4 // half == j, t, 0.0)
            x = m
            for s in range(1, nq4):
                x = x + pltpu.roll(m, s * half, 1)
            parts.append(x)
        return jnp.concatenate(parts, axis=0)

    lane = lax.broadcasted_iota(jnp.int32, (h_ref.shape[1], LANES), 1)
    first = (lane % AT_HEAD) < half
    low = lane < AT_HEAD
    cosf = spread(cos4)
    sinf = spread(sin4)
    sinf = jnp.where(first, -sinf, sinf)

    def rope(t):
        partner = jnp.where(first, pltpu.roll(t, LANES - AT_HEAD // 2, 1), pltpu.roll(t, AT_HEAD // 2, 1))
        return t * cosf + partner * sinf

    def store_dup(ref, j, t):
        r = pltpu.roll(t, AT_HEAD, 1)
        ref[0, :, (2 * j) * LANES:(2 * j + 1) * LANES] = jnp.where(low, t, r).astype(ref.dtype)
        ref[0, :, (2 * j + 1) * LANES:(2 * j + 2) * LANES] = jnp.where(low, r, t).astype(ref.dtype)

    for j in range(qw // LANES):
        sl = slice(j * LANES, (j + 1) * LANES)
        q_ref[0, :, sl] = rope(_dot(y, w_ref[:, sl]) + bias_ref[:, sl]).astype(q_ref.dtype)
    for j in range(kvw // LANES):
        sl = slice(qw + j * LANES, qw + (j + 1) * LANES)
        store_dup(k_ref, j, rope(_dot(y, w_ref[:, sl]) + bias_ref[:, sl]))
    for j in range(kvw // LANES):
        sl = slice(qw + kvw + j * LANES, qw + kvw + (j + 1) * LANES)
        store_dup(v_ref, j, _dot(y, w_ref[:, sl]) + bias_ref[:, sl])
    off = qw + 2 * kvw
    z_ref[0] = _dot(y, w_ref[:, off:off + qw])


def _at_proj(h, g, w_bf16, bias, positions, qw, kvw):
    bsz, seq, d = h.shape
    tm = min(ROW_TILE, seq)
    half = AT_HEAD // 2
    inv_freq = ROPE_THETA ** (-jnp.arange(0, AT_HEAD, 2, dtype=F32) / AT_HEAD)
    nq4 = LANES // half
    invf = jnp.tile(inv_freq, nq4).reshape(1, LANES)
    pos4 = positions.reshape(bsz, seq // tm, nq4, tm // nq4).transpose(0, 1, 3, 2)
    pos4 = jnp.repeat(pos4, half, axis=-1).reshape(bsz, seq // nq4, LANES)
    tokspec = lambda w: pl.BlockSpec((1, tm, w), lambda b, m: (b, m, 0))
    const = lambda shp: pl.BlockSpec(shp, lambda b, m: (0, 0))
    return pl.pallas_call(
        functools.partial(_at_proj_kernel, qw=qw, kvw=kvw),
        out_shape=[jax.ShapeDtypeStruct((bsz, seq, qw), BF16), jax.ShapeDtypeStruct((bsz, seq, 2 * kvw), BF16),
                   jax.ShapeDtypeStruct((bsz, seq, 2 * kvw), BF16), jax.ShapeDtypeStruct((bsz, seq, qw), F32)],
        grid=(bsz, seq // tm),
        in_specs=[tokspec(d), const((1, d)), const(w_bf16.shape), const((1, qw + 2 * kvw)),
                  pl.BlockSpec((1, tm // nq4, LANES), lambda b, m: (b, m, 0)), const((1, LANES))],
        out_specs=[tokspec(qw), tokspec(2 * kvw), tokspec(2 * kvw), tokspec(qw)],
        compiler_params=_cparams(("parallel", "parallel")),
        name="attn_proj",
    )(h, g.reshape(1, d), w_bf16, bias.reshape(1, -1), pos4, invf)


def _at_core_kernel(q_ref, kc_ref, kp_ref, vc_ref, vp_ref, sink_ref, z_ref, h_ref, wo_ref, gp_ref, out_ref):
    blk = AT_BLOCK
    nsub = q_ref.shape[1] // blk
    n = pl.program_id(1)
    ngroup = kc_ref.shape[2] // LANES
    qi = lax.broadcasted_iota(jnp.int32, (2 * blk, 4 * blk), 0) % blk
    kj = lax.broadcasted_iota(jnp.int32, (2 * blk, 4 * blk), 1) % (2 * blk)
    dist = qi + blk - kj
    in_window = (dist >= 0) & (dist < blk)
    first_mask = in_window & ((kj >= blk) | (n > 0))
    low = lax.broadcasted_iota(jnp.int32, (2 * blk, LANES), 1) < AT_HEAD
    top = lax.broadcasted_iota(jnp.int32, (2 * blk, 1), 0) < blk
    ones_lo = jnp.where(low, 1.0, 0.0).astype(BF16)
    ones_hi = jnp.where(low, 0.0, 1.0).astype(BF16)
    scale = AT_HEAD ** -0.5
    items = [(j, g) for j in range(nsub) for g in range(ngroup)]

    def band(cur_ref, prev_ref, j, g):
        sl = slice(g * LANES, (g + 1) * LANES)
        prev = prev_ref[0, :, sl] if j == 0 else cur_ref[0, (j - 1) * blk:j * blk, sl]
        return jnp.concatenate([prev, cur_ref[0, j * blk:(j + 1) * blk, sl]], axis=0)

    scores = []
    for j, g in items:
        kdup = band(kc_ref, kp_ref, j, g)
        rhs = jnp.concatenate([jnp.where(low, kdup, 0.0), jnp.where(low, 0.0, kdup)], axis=0)
        rows = slice(j * blk, (j + 1) * blk)
        lhs = jnp.concatenate([q_ref[0, rows, (2 * g) * LANES:(2 * g + 1) * LANES],
                               q_ref[0, rows, (2 * g + 1) * LANES:(2 * g + 2) * LANES]], axis=0)
        scores.append(_dot_nt(lhs, rhs))

    probs, sink_terms = [], []
    for (j, g), sc in zip(items, scores):
        s = jnp.where(first_mask if j == 0 else in_window, sc * scale, -jnp.inf)
        sink_a = jnp.where(top, sink_ref[0:1, 4 * g:4 * g + 1], sink_ref[0:1, 4 * g + 2:4 * g + 3])
        sink_b = jnp.where(top, sink_ref[0:1, 4 * g + 1:4 * g + 2], sink_ref[0:1, 4 * g + 3:4 * g + 4])
        m_a = jnp.maximum(jnp.max(s[:, :2 * blk], axis=-1, keepdims=True), sink_a)
        m_b = jnp.maximum(jnp.max(s[:, 2 * blk:], axis=-1, keepdims=True), sink_b)
        e = jnp.concatenate([jnp.exp(s[:, :2 * blk] - m_a), jnp.exp(s[:, 2 * blk:] - m_b)], axis=1)
        probs.append(e.astype(BF16))
        sink_terms.append(jnp.where(low, jnp.exp(sink_a - m_a), jnp.exp(sink_b - m_b)))

    o_rows = [[] for _ in range(nsub)]
    for (j, g), p, st in zip(items, probs, sink_terms):
        vdup = band(vc_ref, vp_ref, j, g)
        va = jnp.where(low, vdup, 0.0)
        vb = jnp.where(low, 0.0, vdup)
        rhs = jnp.concatenate([jnp.concatenate([va, ones_lo], axis=1),
                               jnp.concatenate([vb, ones_hi], axis=1)], axis=0)
        out = _dot(p, rhs)
        o = out[:, :LANES] / (out[:, LANES:] + st)
        o_rows[j] += [o[:blk], o[blk:]]

    o = jnp.concatenate([jnp.concatenate(slabs, axis=-1) for slabs in o_rows], axis=0)
    out_ref[0] = _gate_out_math(o, z_ref[0], h_ref[0], None, wo_ref[...], gp_ref[...], False)


def _at_core(q, k_dup, v_dup, sinks, z, h, w_out, g_post):
    bsz, seq, qw = q.shape
    kvw = k_dup.shape[-1]
    d = h.shape[-1]
    blk = AT_BLOCK
    nsub = AT_SUBBLOCKS if seq % (AT_SUBBLOCKS * blk) == 0 else 1
    nq = qw // AT_HEAD
    assert nq == 4 * (kvw // LANES) and qw == 2 * kvw
    cur = lambda w: pl.BlockSpec((1, nsub * blk, w), lambda b, n: (b, n, 0))
    prev = lambda w: pl.BlockSpec((1, blk, w), lambda b, n: (b, jnp.maximum(nsub * n - 1, 0), 0))
    const = lambda shp: pl.BlockSpec(shp, lambda b, n: (0, 0))
    return pl.pallas_call(
        _at_core_kernel,
        out_shape=jax.ShapeDtypeStruct(h.shape, F32),
        grid=(bsz, seq // (nsub * blk)),
        in_specs=[cur(qw), cur(kvw), prev(kvw), cur(kvw), prev(kvw), const((1, nq)),
                  cur(qw), cur(d), const(w_out.shape), const((1, d))],
        out_specs=cur(d),
        compiler_params=_cparams(("parallel", "arbitrary")),
        name="attn_core",
    )(q, k_dup, k_dup, v_dup, v_dup, sinks.reshape(1, nq), z, h, w_out.astype(BF16), g_post.reshape(1, d))


def _at_layer(h, positions, g_pre, g_post, w_in, b_in, sinks, w_out):
    qw = w_out.shape[0]
    kvw = (b_in.shape[0] - qw) // 2
    q, k, v, z = _at_proj(h, g_pre, w_in.astype(BF16), b_in, positions, qw, kvw)
    return _at_core(q, k, v, sinks, z, h, w_out, g_post)


def kernel(x, positions, norm_pre, norm_post, s5_w_in, s5_lambda_re, s5_lambda_im, s5_log_dt, s5_b_re, s5_b_im, s5_c_re, s5_c_im, s5_d, s5_w_glu, s5_b_glu, s5_w_out, hg_w_in, hg_lb_logits, hg_norm, hg_w_out, at_w_in, at_b_in, at_sinks, at_w_out):
    depth = norm_pre.shape[0]
    s5_weights = _s5_weights(s5_lambda_re, s5_lambda_im, s5_log_dt, s5_b_re, s5_b_im, s5_c_re, s5_c_im, s5_d)
    h = x
    for i in range(depth):
        kind, j = i % 3, i // 3
        if kind == 0:
            h = _s5_layer(h, norm_pre[i], norm_post[i], s5_w_in[j], s5_weights, j, s5_w_glu[j], s5_b_glu[j],
                          s5_w_out[j])
        elif kind == 1:
            h = _hg_layer(h, norm_pre[i], norm_post[i], hg_w_in[j], hg_lb_logits, i, hg_norm[j], hg_w_out[j])
        else:
            h = _at_layer(h, positions, norm_pre[i], norm_post[i], at_w_in[j], at_b_in[j], at_sinks[j],
                          at_w_out[j])
    return h
```

```python
import functools
import math

import jax
import jax.numpy as jnp
from jax import lax
from jax.experimental import pallas as pl
from jax.experimental.pallas import tpu as pltpu

F32 = jnp.float32
BF16 = jnp.bfloat16

NORM_EPS = 1e-6
LANES = 128
SUBLANES = 8
VMEM_LIMIT = 56 * 1024 * 1024

S5_GROUP = 16
S5_STATE = 64
S5_TILE_GROUPS = 16
S5_TILE_CH = S5_TILE_GROUPS * S5_GROUP
S5_TILE_ST = S5_TILE_GROUPS * S5_STATE
S5_STEPS = 64
HG_HEAD = 128
HG_CHUNK = 64
HG_BLOCK = 256
AT_HEAD = 64
AT_KV_HEADS = 4
AT_BLOCK = 128
AT_SUBBLOCKS = 4
ROPE_THETA = 10000.0

ROW_TILES = (1024, 512, 256)


def _cparams(sem):
    return pltpu.CompilerParams(dimension_semantics=sem, vmem_limit_bytes=VMEM_LIMIT)


def _row_tile(seq, row_bytes, resident_bytes, multiple=SUBLANES):
    budget = VMEM_LIMIT - VMEM_LIMIT // 4
    for tm in ROW_TILES:
        if seq % tm == 0 and tm % multiple == 0 and 2 * (tm * row_bytes + resident_bytes) <= budget:
            return tm
    return seq


def _rms(x, g):
    return x * lax.rsqrt(jnp.mean(x * x, axis=-1, keepdims=True) + NORM_EPS) * g


def _sigmoid(x):
    return 1.0 / (1.0 + jnp.exp(-x))


def _sigmoid_gate(x):
    return 0.5 + 0.5 * jnp.tanh(0.5 * x)


def _silu(x):
    return x * _sigmoid_gate(x)


def _gelu_tanh(x):
    c = math.sqrt(2.0 / math.pi)
    return 0.5 * x * (1.0 + jnp.tanh(c * (x + 0.044715 * (x * x * x))))


def _dot(a, b):
    return jnp.dot(a, b, preferred_element_type=F32)


def _dot_nt(a, b):
    return lax.dot_general(a, b, (((1,), (1,)), ((), ())), preferred_element_type=F32)


def _dot_tn(a, b):
    return lax.dot_general(a, b, (((0,), (0,)), ((), ())), preferred_element_type=F32)


def _chunk_row_groups(rows, steps):
    for base in range(0, rows, SUBLANES * steps):
        for r in range(SUBLANES):
            for k0 in range(0, steps, SUBLANES):
                yield base + r * steps + k0, base + k0 * SUBLANES + r


def _norm_proj_kernel(h_ref, g_ref, w_ref, *out_refs, widths, slab_steps):
    y = _rms(h_ref[0], g_ref[...]).astype(BF16)
    off = 0
    for o_ref, width, steps in zip(out_refs, widths, slab_steps):
        r = _dot(y, w_ref[:, off:off + width])
        if steps:
            for nat, cm in _chunk_row_groups(r.shape[0], steps):
                for j in range(width // LANES):
                    o_ref[0, j, pl.ds(cm, SUBLANES, stride=SUBLANES), :] = (
                        r[nat:nat + SUBLANES, j * LANES:(j + 1) * LANES])
        else:
            o_ref[0] = r.astype(o_ref.dtype)
        off += width


def _norm_proj(h, g, w_bf16, widths, slab_steps, dtypes):
    bsz, seq, d = h.shape
    row_bytes = 4 * d + sum(w * jnp.dtype(t).itemsize for w, t in zip(widths, dtypes))
    tm = _row_tile(seq, row_bytes, w_bf16.size * 2, SUBLANES * max(max(slab_steps), 1))
    out_shape, out_specs = [], []
    for width, as_slab, dtype in zip(widths, slab_steps, dtypes):
        if as_slab:
            assert tm % (SUBLANES * as_slab) == 0 and dtype == F32
            ns = width // LANES
            out_shape.append(jax.ShapeDtypeStruct((bsz, ns, seq, LANES), F32))
            out_specs.append(pl.BlockSpec((1, ns, tm, LANES), lambda b, m: (b, 0, m, 0)))
        else:
            out_shape.append(jax.ShapeDtypeStruct((bsz, seq, width), dtype))
            out_specs.append(pl.BlockSpec((1, tm, width), lambda b, m: (b, m, 0)))
    return pl.pallas_call(
        functools.partial(_norm_proj_kernel, widths=tuple(widths), slab_steps=tuple(slab_steps)),
        out_shape=out_shape,
        grid=(bsz, seq // tm),
        in_specs=[pl.BlockSpec((1, tm, d), lambda b, m: (b, m, 0)),
                  pl.BlockSpec((1, d), lambda b, m: (0, 0)),
                  pl.BlockSpec(w_bf16.shape, lambda b, m: (0, 0))],
        out_specs=out_specs,
        compiler_params=_cparams(("parallel", "parallel")),
        name="norm_proj",
    )(h, g.reshape(1, d), w_bf16)


def _s5_prep_kernel(lr_ref, li_ref, ldt_ref, br_ref, bi_ref, ar_ref, ai_ref, bbr_ref, bbi_ref):
    lr = lr_ref[...]
    li = li_ref[...]
    dt = jnp.exp(ldt_ref[...])
    mag = jnp.exp(lr * dt)
    ar = mag * jnp.cos(li * dt)
    ai = mag * jnp.sin(li * dt)
    den = lr * lr + li * li
    qr = ((ar - 1.0) * lr + ai * li) / den
    qi = (ai * lr - (ar - 1.0) * li) / den
    br = br_ref[...]
    bi = bi_ref[...]
    ar_ref[...] = ar
    ai_ref[...] = ai
    bbr_ref[...] = qr * br - qi * bi
    bbi_ref[...] = qr * bi + qi * br


def _s5_prep(lam_re, lam_im, log_dt, b_re, b_im):
    g, p, hh = b_re.shape
    rep = lambda t: jnp.repeat(t, hh, axis=1)
    shp = jax.ShapeDtypeStruct((g, p * hh), F32)
    ar, ai, bbr, bbi = pl.pallas_call(
        _s5_prep_kernel, out_shape=[shp, shp, shp, shp], name="s5_prep",
    )(rep(lam_re), rep(lam_im), log_dt.reshape(g, 1), b_re.reshape(g, p * hh), b_im.reshape(g, p * hh))
    return ar[:, ::hh], ai[:, ::hh], bbr.reshape(g, p, hh), bbi.reshape(g, p, hh)


def _s5_core_kernel(u_ref, wb_ref, wc_ref, a_ref, d_ref, y_ref, x_ref, s_ref, carry_ref, *, steps):
    nst = S5_TILE_ST
    nslab = S5_TILE_CH // LANES
    ntile = wb_ref.shape[0]

    @pl.when(pl.program_id(1) == 0)
    def _():
        carry_ref[...] = jnp.zeros_like(carry_ref)

    def load_u(i):
        return jnp.concatenate([u_ref[0, i * nslab + j] for j in range(nslab)], axis=-1)

    def project_in(i):
        x_ref[i % 2] = _dot(load_u(i).astype(BF16), wb_ref[i])

    def scan(i):
        slot = i % 2
        ar1 = a_ref[i, 0:1, :]
        ai1 = a_ref[i, 1:2, :]
        ar = jnp.broadcast_to(ar1, (SUBLANES, nst))
        ai = jnp.broadcast_to(ai1, (SUBLANES, nst))

        def advance(k, sr, si):
            xr = x_ref[slot, k * SUBLANES:(k + 1) * SUBLANES, 0:nst]
            xi = x_ref[slot, k * SUBLANES:(k + 1) * SUBLANES, nst:2 * nst]
            return ar * sr - ai * si + xr, ar * si + ai * sr + xi

        er = jnp.zeros((SUBLANES, nst), F32)
        ei = jnp.zeros((SUBLANES, nst), F32)
        for k in range(steps):
            er, ei = advance(k, er, ei)

        pr, pi = ar1, ai1
        for _ in range(int(math.log2(steps))):
            pr, pi = pr * pr - pi * pi, 2.0 * pr * pi

        rid = lax.broadcasted_iota(jnp.int32, (SUBLANES, nst), 0)
        cr = carry_ref[i, 0:1, :]
        ci = carry_ref[i, 1:2, :]
        sr = jnp.zeros((SUBLANES, nst), F32)
        si = jnp.zeros((SUBLANES, nst), F32)
        for r in range(SUBLANES):
            sr = jnp.where(rid == r, jnp.broadcast_to(cr, (SUBLANES, nst)), sr)
            si = jnp.where(rid == r, jnp.broadcast_to(ci, (SUBLANES, nst)), si)
            cr, ci = pr * cr - pi * ci + er[r:r + 1, :], pr * ci + pi * cr + ei[r:r + 1, :]
        carry_ref[i, 0:1, :] = cr
        carry_ref[i, 1:2, :] = ci

        for k in range(0, steps, 2):
            r0, i0 = advance(k, sr, si)
            sr, si = advance(k + 1, r0, i0)
            rows2 = slice(k * SUBLANES, (k + 2) * SUBLANES)
            s_ref[slot, rows2, 0:nst] = jnp.concatenate([r0, sr], axis=0).astype(BF16)
            s_ref[slot, rows2, nst:2 * nst] = jnp.concatenate([i0, si], axis=0).astype(BF16)

    def project_out(i):
        y = _dot(s_ref[i % 2], wc_ref[i]) + d_ref[i] * load_u(i)
        for j in range(nslab):
            y_ref[0, i * nslab + j] = y[:, j * LANES:(j + 1) * LANES]

    project_in(0)
    for i in range(ntile):
        if i + 1 < ntile:
            project_in(i + 1)
        scan(i)
        project_out(i)


def _s5_core(u_slab, wb, wc, a_tiles, d_tiles, layer, steps):
    bsz, nslab, seq, _ = u_slab.shape
    ntile = nslab * LANES // S5_TILE_CH
    rows = SUBLANES * steps
    const = lambda shp: pl.BlockSpec((ntile,) + shp[1:], lambda b, n: (layer,) + (0,) * (len(shp) - 1))
    tok = pl.BlockSpec((1, nslab, rows, LANES), lambda b, n: (b, 0, n, 0))
    return pl.pallas_call(
        functools.partial(_s5_core_kernel, steps=steps),
        out_shape=jax.ShapeDtypeStruct(u_slab.shape, F32),
        grid=(bsz, seq // rows),
        in_specs=[tok, const(wb.shape), const(wc.shape), const(a_tiles.shape), const(d_tiles.shape)],
        out_specs=tok,
        scratch_shapes=[pltpu.VMEM((2, rows, 2 * S5_TILE_ST), F32),
                        pltpu.VMEM((2, rows, 2 * S5_TILE_ST), BF16),
                        pltpu.VMEM((ntile, 2, S5_TILE_ST), F32)],
        compiler_params=_cparams(("parallel", "arbitrary")),
        name="s5_core",
    )(u_slab, wb, wc, a_tiles, d_tiles)


def _s5_post_kernel(y_ref, z_ref, h_ref, wg_ref, bg_ref, wo_ref, gp_ref, o_ref, *, steps):
    nslab = y_ref.shape[1]
    groups = list(_chunk_row_groups(y_ref.shape[2], steps))
    y = jnp.concatenate(
        [jnp.concatenate([y_ref[0, j, pl.ds(cm, SUBLANES, stride=SUBLANES), :] for _, cm in groups], axis=0)
         for j in range(nslab)], axis=-1)
    y = _gelu_tanh(y)
    y = y * _sigmoid_gate(_dot(y.astype(BF16), wg_ref[...]) + bg_ref[...])
    t = y * _silu(z_ref[0])
    o = _dot(t.astype(BF16), wo_ref[...])
    o_ref[0] = h_ref[0] + _rms(o, gp_ref[...])


def _s5_post(y_slab, z, h, w_glu, b_glu, w_out, g_post, steps):
    bsz, seq, d = h.shape
    nslab = y_slab.shape[1]
    tm = _row_tile(seq, 4 * 4 * d, 2 * 2 * d * d, SUBLANES * steps)
    assert tm % (SUBLANES * steps) == 0
    row = lambda t: t.reshape(1, -1)
    vec = pl.BlockSpec((1, d), lambda b, m: (0, 0))
    mat = pl.BlockSpec((d, d), lambda b, m: (0, 0))
    slab = pl.BlockSpec((1, nslab, tm, LANES), lambda b, m: (b, 0, m, 0))
    tok = pl.BlockSpec((1, tm, d), lambda b, m: (b, m, 0))
    return pl.pallas_call(
        functools.partial(_s5_post_kernel, steps=steps),
        out_shape=jax.ShapeDtypeStruct(h.shape, F32),
        grid=(bsz, seq // tm),
        in_specs=[slab, tok, tok, mat, vec, mat, vec],
        out_specs=tok,
        compiler_params=_cparams(("parallel", "parallel")),
        name="s5_post",
    )(y_slab, z, h, w_glu.astype(BF16), row(b_glu), w_out.astype(BF16), row(g_post))


def _s5_weights(lam_re, lam_im, log_dt, b_re, b_im, c_re, c_im, d_skip):
    nlayer, ngroup = lam_re.shape[:2]
    flat = lambda t: t.reshape((nlayer * ngroup,) + t.shape[2:])
    ntile = nlayer * ngroup // S5_TILE_GROUPS
    ar, ai, bbr, bbi = _s5_prep(flat(lam_re), flat(lam_im), flat(log_dt), flat(b_re), flat(b_im))
    eye = jnp.eye(S5_TILE_GROUPS, dtype=F32)

    def in_tile(bb):
        t = bb.reshape(ntile, S5_TILE_GROUPS, S5_STATE, S5_GROUP)
        return jnp.einsum('igph,gk->ighkp', t, eye).reshape(ntile, S5_TILE_CH, S5_TILE_ST)

    def out_tile(cc):
        t = cc.reshape(ntile, S5_TILE_GROUPS, S5_GROUP, S5_STATE)
        return jnp.einsum('ighp,gk->ikpgh', t, eye).reshape(ntile, S5_TILE_ST, S5_TILE_CH)

    wb = jnp.concatenate([in_tile(bbr), in_tile(bbi)], axis=2).astype(BF16)
    wc = jnp.concatenate([out_tile(flat(c_re)), -out_tile(flat(c_im))], axis=1).astype(BF16)
    a_tiles = jnp.stack([ar.reshape(ntile, S5_TILE_ST), ai.reshape(ntile, S5_TILE_ST)], axis=1)
    d_tiles = d_skip.reshape(ntile, 1, S5_TILE_CH)
    return wb, wc, a_tiles, d_tiles


def _s5_layer(h, g_pre, g_post, w_in, weights, layer, w_glu, b_glu, w_out):
    bsz, seq, d = h.shape
    steps = min(S5_STEPS, seq // SUBLANES)
    u_slab, z = _norm_proj(h, g_pre, w_in.astype(BF16), (d, d), (steps, 0), (F32, F32))
    y_slab = _s5_core(u_slab, *weights, layer, steps)
    return _s5_post(y_slab, z, h, w_glu, b_glu, w_out, g_post, steps)


def _gate_out_math(o, z, h, g_head, w_out, g_post, head_norm):
    if head_norm:
        parts = []
        for j in range(o.shape[-1] // HG_HEAD):
            oj = o[:, j * HG_HEAD:(j + 1) * HG_HEAD]
            parts.append(oj * lax.rsqrt(jnp.mean(oj * oj, axis=-1, keepdims=True) + NORM_EPS))
        o = jnp.concatenate(parts, axis=-1) * g_head
    t = o * _silu(z)
    return h + _rms(_dot(t.astype(BF16), w_out), g_post)


def _chunk_cumsum(x, chunk):
    rows, w = x.shape
    per = chunk // SUBLANES
    x3 = x.reshape(rows // SUBLANES, SUBLANES, w)
    pos = lax.broadcasted_iota(jnp.int32, x3.shape, 1)
    sh = 1
    while sh < SUBLANES:
        x3 = x3 + jnp.where(pos >= sh, pltpu.roll(x3, sh, 1), 0.0)
        sh *= 2
    x4 = x3.reshape(rows // chunk, per, SUBLANES, w)
    tot = x4[:, :, SUBLANES - 1:SUBLANES, :]
    pref = [jnp.zeros_like(tot[:, 0:1])]
    for j in range(1, per):
        pref.append(pref[-1] + tot[:, j - 1:j])
    return (x4 + jnp.concatenate(pref, axis=1)).reshape(rows, w)


def _block_mid_rows(x, blk):
    rows, w = x.shape
    half = blk // 2
    if blk >= SUBLANES:
        x3 = x.reshape(rows // blk, blk, w)
        return jnp.broadcast_to(x3[:, half - 1:half, :], (rows // blk, blk, w)).reshape(rows, w)
    pos = lax.broadcasted_iota(jnp.int32, (rows, w), 0) % blk
    out = x
    for p in range(blk):
        d = p - (half - 1)
        if d != 0:
            out = jnp.where(pos == p, pltpu.roll(x, d % rows, 0), out)
    return out


def _hg_core_kernel(q_ref, f_ref, v_ref, lbl_ref, z_ref, h_ref, gn_ref, wo_ref, gp_ref, out_ref,
                    xl_ref, qe_ref, kd_ref, g_ref, st_ref, o_scr, *, layer, chunk):
    rows, width = q_ref.shape[1], q_ref.shape[2]
    pair = 2 * HG_HEAD
    blocks = [2 ** i for i in range(1, int(math.log2(chunk)) + 1)]

    @pl.when(pl.program_id(1) == 0)
    def _():
        st_ref[...] = jnp.zeros_like(st_ref)

    lg = lbl_ref[...]
    ex = jnp.exp(lg - jnp.max(lg, axis=0, keepdims=True))
    p = ex / jnp.sum(ex, axis=0, keepdims=True)
    lb = jnp.zeros((1, width), F32)
    for j in range(1, layer + 1):
        lb = lb + p[j:j + 1, :]

    q = q_ref[0]
    f = lb + (1.0 - lb) * _sigmoid(f_ref[0])
    kk = 1.0 - f
    b = _chunk_cumsum(jnp.log(f), chunk)
    b3 = b.reshape(rows // chunk, chunk, width)
    b_last = b3[:, chunk - 1:chunk, :]
    g_ref[...] = jnp.exp(b_last.reshape(rows // chunk, width))
    qe_ref[...] = (q * jnp.exp(b)).astype(BF16)
    kd_ref[...] = (kk * jnp.exp(jnp.broadcast_to(b_last, b3.shape).reshape(rows, width) - b)).astype(BF16)
    xl_ref[0] = (q * kk).astype(BF16)
    pos = lax.broadcasted_iota(jnp.int32, (rows, width), 0)
    for li, blk in enumerate(blocks):
        w = jnp.exp(-jnp.abs(b - _block_mid_rows(b, blk)))
        xl_ref[li + 1] = (jnp.where(pos % blk >= blk // 2, q, kk) * w).astype(BF16)

    t_i = lax.broadcasted_iota(jnp.int32, (chunk, HG_HEAD), 0)
    s_i = lax.broadcasted_iota(jnp.int32, (chunk, HG_HEAD), 1) % chunk
    eye = t_i == s_i
    lmask = [((t_i // blk) == (s_i // blk)) & ((t_i % blk) >= blk // 2) & ((s_i % blk) < blk // 2)
             for blk in blocks]
    lo = lax.broadcasted_iota(jnp.int32, (chunk, pair), 1) < HG_HEAD
    bd = (lax.broadcasted_iota(jnp.int32, (pair, pair), 0) // HG_HEAD
          == lax.broadcasted_iota(jnp.int32, (pair, pair), 1) // HG_HEAD)
    ones_bd = jnp.where(lax.broadcasted_iota(jnp.int32, (pair, HG_HEAD), 0) // HG_HEAD
                        == lax.broadcasted_iota(jnp.int32, (pair, HG_HEAD), 1) // chunk, 1.0, 0.0).astype(BF16)

    def split_heads(t):
        z = jnp.zeros_like(t)
        return jnp.concatenate([jnp.where(lo, t, z), jnp.where(lo, z, t)], axis=0)

    for c in range(rows // chunk):
        rs = slice(c * chunk, (c + 1) * chunk)
        for pr in range(width // pair):
            ls = slice(pr * pair, (pr + 1) * pair)
            sc = jnp.where(eye, _dot(xl_ref[0, rs, ls], ones_bd), 0.0)
            for li in range(len(blocks)):
                xl = xl_ref[li + 1, rs, ls]
                sc = jnp.where(lmask[li], _dot_nt(xl, split_heads(xl)), sc)
            vpair = v_ref[0, rs, ls]
            st = st_ref[pr]
            o = _dot(sc.astype(BF16), split_heads(vpair)) + _dot_nt(qe_ref[rs, ls], st.astype(BF16))
            o_scr[rs, ls] = o
            upd = _dot_tn(vpair, kd_ref[rs, ls])
            st_ref[pr] = g_ref[c:c + 1, ls] * st + jnp.where(bd, upd, 0.0)

    out_ref[0] = _gate_out_math(o_scr[...], z_ref[0], h_ref[0], gn_ref[...], wo_ref[...], gp_ref[...], True)


def _hg_core(q, fz, v, z, h, lb_logits, layer, norm_g, w_out, g_post):
    bsz, seq, width = q.shape
    blk = min(HG_BLOCK, seq)
    chunk = min(HG_CHUNK, blk)
    nlev = int(math.log2(chunk))
    assert HG_HEAD == 2 * chunk and width % (2 * HG_HEAD) == 0
    tok = pl.BlockSpec((1, blk, width), lambda b, n: (b, n, 0))
    const = lambda shp: pl.BlockSpec(shp, lambda b, n: (0, 0))
    row = lambda t: t.reshape(1, -1)
    return pl.pallas_call(
        functools.partial(_hg_core_kernel, layer=layer, chunk=chunk),
        out_shape=jax.ShapeDtypeStruct(h.shape, F32),
        grid=(bsz, seq // blk),
        in_specs=[tok, tok, tok, const(lb_logits.shape), tok, tok, const((1, width)), const(w_out.shape),
                  const((1, width))],
        out_specs=tok,
        scratch_shapes=[pltpu.VMEM((nlev + 1, blk, width), BF16),
                        pltpu.VMEM((blk, width), BF16),
                        pltpu.VMEM((blk, width), BF16),
                        pltpu.VMEM((blk // chunk, width), F32),
                        pltpu.VMEM((width // (2 * HG_HEAD), 2 * HG_HEAD, 2 * HG_HEAD), F32),
                        pltpu.VMEM((blk, width), F32)],
        compiler_params=_cparams(("parallel", "arbitrary")),
        name="hgrn2_core",
    )(q, fz, v, lb_logits, z, h, row(norm_g), w_out.astype(BF16), row(g_post))


def _hg_layer(h, g_pre, g_post, w_in, lb_logits, layer, norm_g, w_out):
    d = h.shape[-1]
    q, fz, v, z = _norm_proj(h, g_pre, w_in.astype(BF16), (d, d, d, d), (0,) * 4, (F32, F32, BF16, F32))
    return _hg_core(q, fz, v, z, h, lb_logits, layer, norm_g, w_out, g_post)


def _at_proj_kernel(h_ref, g_ref, w_ref, bias_ref, pos_ref, invf_ref, q_ref, k_ref, v_ref, z_ref,
                    *, qw, kvw):
    y = _rms(h_ref[0], g_ref[...]).astype(BF16)
    half = AT_HEAD // 2
    nq4 = LANES // half
    ang = pos_ref[0].astype(F32) * invf_ref[...]
    cos4 = jnp.cos(ang)
    sin4 = jnp.sin(ang)
    lane4 = lax.broadcasted_iota(jnp.int32, ang.shape, 1)

    def spread(t):
        parts = []
        for j in range(nq4):
            m = jnp.where(lane4 // half == j, t, 0.0)
            x = m
            for s in range(1, nq4):
                x = x + pltpu.roll(m, s * half, 1)
            parts.append(x)
        return jnp.concatenate(parts, axis=0)

    lane = lax.broadcasted_iota(jnp.int32, (h_ref.shape[1], LANES), 1)
    first = (lane % AT_HEAD) < half
    low = lane < AT_HEAD
    cosf = spread(cos4)
    sinf = spread(sin4)
    sinf = jnp.where(first, -sinf, sinf)

    def rope(t):
        partner = jnp.where(first, pltpu.roll(t, LANES - AT_HEAD // 2, 1), pltpu.roll(t, AT_HEAD // 2, 1))
        return t * cosf + partner * sinf

    def store_dup(ref, j, t):
        r = pltpu.roll(t, AT_HEAD, 1)
        ref[0, :, (2 * j) * LANES:(2 * j + 1) * LANES] = jnp.where(low, t, r).astype(ref.dtype)
        ref[0, :, (2 * j + 1) * LANES:(2 * j + 2) * LANES] = jnp.where(low, r, t).astype(ref.dtype)

    for j in range(qw // LANES):
        sl = slice(j * LANES, (j + 1) * LANES)
        q_ref[0, :, sl] = rope(_dot(y, w_ref[:, sl]) + bias_ref[:, sl]).astype(q_ref.dtype)
    for j in range(kvw // LANES):
        sl = slice(qw + j * LANES, qw + (j + 1) * LANES)
        store_dup(k_ref, j, rope(_dot(y, w_ref[:, sl]) + bias_ref[:, sl]))
    for j in range(kvw // LANES):
        sl = slice(qw + kvw + j * LANES, qw + kvw + (j + 1) * LANES)
        store_dup(v_ref, j, _dot(y, w_ref[:, sl]) + bias_ref[:, sl])
    off = qw + 2 * kvw
    z_ref[0] = _dot(y, w_ref[:, off:off + qw])


def _at_proj(h, g, w_bf16, bias, positions, qw, kvw):
    bsz, seq, d = h.shape
    tm = _row_tile(seq, 4 * d + 2 * qw + 4 * 2 * kvw + 4 * qw, w_bf16.size * 2, 4 * SUBLANES)
    half = AT_HEAD // 2
    inv_freq = ROPE_THETA ** (-jnp.arange(0, AT_HEAD, 2, dtype=F32) / AT_HEAD)
    nq4 = LANES // half
    invf = jnp.tile(inv_freq, nq4).reshape(1, LANES)
    pos4 = positions.reshape(bsz, seq // tm, nq4, tm // nq4).transpose(0, 1, 3, 2)
    pos4 = jnp.repeat(pos4, half, axis=-1).reshape(bsz, seq // nq4, LANES)
    tokspec = lambda w: pl.BlockSpec((1, tm, w), lambda b, m: (b, m, 0))
    const = lambda shp: pl.BlockSpec(shp, lambda b, m: (0, 0))
    return pl.pallas_call(
        functools.partial(_at_proj_kernel, qw=qw, kvw=kvw),
        out_shape=[jax.ShapeDtypeStruct((bsz, seq, qw), BF16), jax.ShapeDtypeStruct((bsz, seq, 2 * kvw), BF16),
                   jax.ShapeDtypeStruct((bsz, seq, 2 * kvw), BF16), jax.ShapeDtypeStruct((bsz, seq, qw), F32)],
        grid=(bsz, seq // tm),
        in_specs=[tokspec(d), const((1, d)), const(w_bf16.shape), const((1, qw + 2 * kvw)),
                  pl.BlockSpec((1, tm // nq4, LANES), lambda b, m: (b, m, 0)), const((1, LANES))],
        out_specs=[tokspec(qw), tokspec(2 * kvw), tokspec(2 * kvw), tokspec(qw)],
        compiler_params=_cparams(("parallel", "parallel")),
        name="attn_proj",
    )(h, g.reshape(1, d), w_bf16, bias.reshape(1, -1), pos4, invf)


def _at_core_kernel(q_ref, kc_ref, kp_ref, vc_ref, vp_ref, sink_ref, z_ref, h_ref, wo_ref, gp_ref, out_ref):
    blk = AT_BLOCK
    nsub = q_ref.shape[1] // blk
    n = pl.program_id(1)
    ngroup = kc_ref.shape[2] // LANES
    qi = lax.broadcasted_iota(jnp.int32, (2 * blk, 4 * blk), 0) % blk
    kj = lax.broadcasted_iota(jnp.int32, (2 * blk, 4 * blk), 1) % (2 * blk)
    dist = qi + blk - kj
    in_window = (dist >= 0) & (dist < blk)
    first_mask = in_window & ((kj >= blk) | (n > 0))
    low = lax.broadcasted_iota(jnp.int32, (2 * blk, LANES), 1) < AT_HEAD
    top = lax.broadcasted_iota(jnp.int32, (2 * blk, 1), 0) < blk
    ones_lo = jnp.where(low, 1.0, 0.0).astype(BF16)
    ones_hi = jnp.where(low, 0.0, 1.0).astype(BF16)
    scale = AT_HEAD ** -0.5
    items = [(j, g) for j in range(nsub) for g in range(ngroup)]

    def band(cur_ref, prev_ref, j, g):
        sl = slice(g * LANES, (g + 1) * LANES)
        prev = prev_ref[0, :, sl] if j == 0 else cur_ref[0, (j - 1) * blk:j * blk, sl]
        return jnp.concatenate([prev, cur_ref[0, j * blk:(j + 1) * blk, sl]], axis=0)

    scores = []
    for j, g in items:
        kdup = band(kc_ref, kp_ref, j, g)
        rhs = jnp.concatenate([jnp.where(low, kdup, 0.0), jnp.where(low, 0.0, kdup)], axis=0)
        rows = slice(j * blk, (j + 1) * blk)
        lhs = jnp.concatenate([q_ref[0, rows, (2 * g) * LANES:(2 * g + 1) * LANES],
                               q_ref[0, rows, (2 * g + 1) * LANES:(2 * g + 2) * LANES]], axis=0)
        scores.append(_dot_nt(lhs, rhs))

    probs, sink_terms = [], []
    for (j, g), sc in zip(items, scores):
        s = jnp.where(first_mask if j == 0 else in_window, sc * scale, -jnp.inf)
        sink_a = jnp.where(top, sink_ref[0:1, 4 * g:4 * g + 1], sink_ref[0:1, 4 * g + 2:4 * g + 3])
        sink_b = jnp.where(top, sink_ref[0:1, 4 * g + 1:4 * g + 2], sink_ref[0:1, 4 * g + 3:4 * g + 4])
        m_a = jnp.maximum(jnp.max(s[:, :2 * blk], axis=-1, keepdims=True), sink_a)
        m_b = jnp.maximum(jnp.max(s[:, 2 * blk:], axis=-1, keepdims=True), sink_b)
        e = jnp.concatenate([jnp.exp(s[:, :2 * blk] - m_a), jnp.exp(s[:, 2 * blk:] - m_b)], axis=1)
        probs.append(e.astype(BF16))
        sink_terms.append(jnp.where(low, jnp.exp(sink_a - m_a), jnp.exp(sink_b - m_b)))

    o_rows = [[] for _ in range(nsub)]
    for (j, g), p, st in zip(items, probs, sink_terms):
        vdup = band(vc_ref, vp_ref, j, g)
        va = jnp.where(low, vdup, 0.0)
        vb = jnp.where(low, 0.0, vdup)
        rhs = jnp.concatenate([jnp.concatenate([va, ones_lo], axis=1),
                               jnp.concatenate([vb, ones_hi], axis=1)], axis=0)
        out = _dot(p, rhs)
        o = out[:, :LANES] / (out[:, LANES:] + st)
        o_rows[j] += [o[:blk], o[blk:]]

    o = jnp.concatenate([jnp.concatenate(slabs, axis=-1) for slabs in o_rows], axis=0)
    out_ref[0] = _gate_out_math(o, z_ref[0], h_ref[0], None, wo_ref[...], gp_ref[...], False)


def _at_core(q, k_dup, v_dup, sinks, z, h, w_out, g_post):
    bsz, seq, qw = q.shape
    kvw = k_dup.shape[-1]
    d = h.shape[-1]
    blk = AT_BLOCK
    nsub = AT_SUBBLOCKS if seq % (AT_SUBBLOCKS * blk) == 0 else 1
    nq = qw // AT_HEAD
    assert nq == 4 * (kvw // LANES) and qw == 2 * kvw
    cur = lambda w: pl.BlockSpec((1, nsub * blk, w), lambda b, n: (b, n, 0))
    prev = lambda w: pl.BlockSpec((1, blk, w), lambda b, n: (b, jnp.maximum(nsub * n - 1, 0), 0))
    const = lambda shp: pl.BlockSpec(shp, lambda b, n: (0, 0))
    return pl.pallas_call(
        _at_core_kernel,
        out_shape=jax.ShapeDtypeStruct(h.shape, F32),
        grid=(bsz, seq // (nsub * blk)),
        in_specs=[cur(qw), cur(kvw), prev(kvw), cur(kvw), prev(kvw), const((1, nq)),
                  cur(qw), cur(d), const(w_out.shape), const((1, d))],
        out_specs=cur(d),
        compiler_params=_cparams(("parallel", "arbitrary")),
        name="attn_core",
    )(q, k_dup, k_dup, v_dup, v_dup, sinks.reshape(1, nq), z, h, w_out.astype(BF16), g_post.reshape(1, d))


def _at_layer(h, positions, g_pre, g_post, w_in, b_in, sinks, w_out):
    qw = w_out.shape[0]
    kvw = (b_in.shape[0] - qw) // 2
    q, k, v, z = _at_proj(h, g_pre, w_in.astype(BF16), b_in, positions, qw, kvw)
    return _at_core(q, k, v, sinks, z, h, w_out, g_post)


def kernel(x, positions, norm_pre, norm_post, s5_w_in, s5_lambda_re, s5_lambda_im, s5_log_dt, s5_b_re, s5_b_im, s5_c_re, s5_c_im, s5_d, s5_w_glu, s5_b_glu, s5_w_out, hg_w_in, hg_lb_logits, hg_norm, hg_w_out, at_w_in, at_b_in, at_sinks, at_w_out):
    depth = norm_pre.shape[0]
    s5_weights = _s5_weights(s5_lambda_re, s5_lambda_im, s5_log_dt, s5_b_re, s5_b_im, s5_c_re, s5_c_im, s5_d)
    h = x
    for i in range(depth):
        kind, j = i % 3, i // 3
        if kind == 0:
            h = _s5_layer(h, norm_pre[i], norm_post[i], s5_w_in[j], s5_weights, j, s5_w_glu[j], s5_b_glu[j],
                          s5_w_out[j])
        elif kind == 1:
            h = _hg_layer(h, norm_pre[i], norm_post[i], hg_w_in[j], hg_lb_logits, i, hg_norm[j], hg_w_out[j])
        else:
            h = _at_layer(h, positions, norm_pre[i], norm_post[i], at_w_in[j], at_b_in[j], at_sinks[j],
                          at_w_out[j])
    return h
```

```python
import functools
import math

import jax
import jax.numpy as jnp
from jax import lax
from jax.experimental import pallas as pl
from jax.experimental.pallas import tpu as pltpu

F32 = jnp.float32
BF16 = jnp.bfloat16

NORM_EPS = 1e-6
LANES = 128
SUBLANES = 8
VMEM_LIMIT = 56 * 1024 * 1024

S5_GROUP = 16
S5_STATE = 64
S5_TILE_GROUPS = 16
S5_TILE_CH = S5_TILE_GROUPS * S5_GROUP
S5_TILE_ST = S5_TILE_GROUPS * S5_STATE
S5_STEPS = 64
HG_HEAD = 128
HG_CHUNK = 64
HG_BLOCK = 512
HG_MILD_DECAY = 80.0
AT_HEAD = 64
AT_KV_HEADS = 4
AT_BLOCK = 128
AT_SUBBLOCKS = 4
ROPE_THETA = 10000.0

ROW_TILES = (1024, 512, 256)


def _cparams(sem):
    return pltpu.CompilerParams(dimension_semantics=sem, vmem_limit_bytes=VMEM_LIMIT)


def _row_tile(seq, row_bytes, resident_bytes, multiple=SUBLANES):
    budget = VMEM_LIMIT - VMEM_LIMIT // 4
    for tm in ROW_TILES:
        if seq % tm == 0 and tm % multiple == 0 and 2 * (tm * row_bytes + resident_bytes) <= budget:
            return tm
    return seq


def _rms(x, g):
    return x * lax.rsqrt(jnp.mean(x * x, axis=-1, keepdims=True) + NORM_EPS) * g


def _sigmoid(x):
    return 1.0 / (1.0 + jnp.exp(-x))


def _sigmoid_gate(x):
    return 0.5 + 0.5 * jnp.tanh(0.5 * x)


def _silu(x):
    return x * _sigmoid_gate(x)


def _gelu_tanh(x):
    c = math.sqrt(2.0 / math.pi)
    return 0.5 * x * (1.0 + jnp.tanh(c * (x + 0.044715 * (x * x * x))))


def _dot(a, b):
    return jnp.dot(a, b, preferred_element_type=F32)


def _dot_nt(a, b):
    return lax.dot_general(a, b, (((1,), (1,)), ((), ())), preferred_element_type=F32)


def _dot_tn(a, b):
    return lax.dot_general(a, b, (((0,), (0,)), ((), ())), preferred_element_type=F32)


def _chunk_row_groups(rows, steps):
    for base in range(0, rows, SUBLANES * steps):
        for r in range(SUBLANES):
            for k0 in range(0, steps, SUBLANES):
                yield base + r * steps + k0, base + k0 * SUBLANES + r


def _norm_proj_kernel(h_ref, g_ref, w_ref, *out_refs, widths, slab_steps):
    y = _rms(h_ref[0], g_ref[...]).astype(BF16)
    off = 0
    for o_ref, width, steps in zip(out_refs, widths, slab_steps):
        r = _dot(y, w_ref[:, off:off + width])
        if steps:
            for nat, cm in _chunk_row_groups(r.shape[0], steps):
                for j in range(width // LANES):
                    o_ref[0, j, pl.ds(cm, SUBLANES, stride=SUBLANES), :] = (
                        r[nat:nat + SUBLANES, j * LANES:(j + 1) * LANES])
        else:
            o_ref[0] = r.astype(o_ref.dtype)
        off += width


def _norm_proj(h, g, w_bf16, widths, slab_steps, dtypes):
    bsz, seq, d = h.shape
    row_bytes = 4 * d + sum(w * jnp.dtype(t).itemsize for w, t in zip(widths, dtypes))
    tm = _row_tile(seq, row_bytes, w_bf16.size * 2, SUBLANES * max(max(slab_steps), 1))
    out_shape, out_specs = [], []
    for width, as_slab, dtype in zip(widths, slab_steps, dtypes):
        if as_slab:
            assert tm % (SUBLANES * as_slab) == 0 and dtype == F32
            ns = width // LANES
            out_shape.append(jax.ShapeDtypeStruct((bsz, ns, seq, LANES), F32))
            out_specs.append(pl.BlockSpec((1, ns, tm, LANES), lambda b, m: (b, 0, m, 0)))
        else:
            out_shape.append(jax.ShapeDtypeStruct((bsz, seq, width), dtype))
            out_specs.append(pl.BlockSpec((1, tm, width), lambda b, m: (b, m, 0)))
    return pl.pallas_call(
        functools.partial(_norm_proj_kernel, widths=tuple(widths), slab_steps=tuple(slab_steps)),
        out_shape=out_shape,
        grid=(bsz, seq // tm),
        in_specs=[pl.BlockSpec((1, tm, d), lambda b, m: (b, m, 0)),
                  pl.BlockSpec((1, d), lambda b, m: (0, 0)),
                  pl.BlockSpec(w_bf16.shape, lambda b, m: (0, 0))],
        out_specs=out_specs,
        compiler_params=_cparams(("parallel", "parallel")),
        name="norm_proj",
    )(h, g.reshape(1, d), w_bf16)


def _s5_prep_kernel(lr_ref, li_ref, ldt_ref, br_ref, bi_ref, ar_ref, ai_ref, bbr_ref, bbi_ref):
    lr = lr_ref[...]
    li = li_ref[...]
    dt = jnp.exp(ldt_ref[...])
    mag = jnp.exp(lr * dt)
    ar = mag * jnp.cos(li * dt)
    ai = mag * jnp.sin(li * dt)
    den = lr * lr + li * li
    qr = ((ar - 1.0) * lr + ai * li) / den
    qi = (ai * lr - (ar - 1.0) * li) / den
    br = br_ref[...]
    bi = bi_ref[...]
    ar_ref[...] = ar
    ai_ref[...] = ai
    bbr_ref[...] = qr * br - qi * bi
    bbi_ref[...] = qr * bi + qi * br


def _s5_prep(lam_re, lam_im, log_dt, b_re, b_im):
    g, p, hh = b_re.shape
    rep = lambda t: jnp.repeat(t, hh, axis=1)
    shp = jax.ShapeDtypeStruct((g, p * hh), F32)
    ar, ai, bbr, bbi = pl.pallas_call(
        _s5_prep_kernel, out_shape=[shp, shp, shp, shp], name="s5_prep",
    )(rep(lam_re), rep(lam_im), log_dt.reshape(g, 1), b_re.reshape(g, p * hh), b_im.reshape(g, p * hh))
    return ar[:, ::hh], ai[:, ::hh], bbr.reshape(g, p, hh), bbi.reshape(g, p, hh)


def _s5_core_kernel(u_ref, wb_ref, wc_ref, a_ref, d_ref, y_ref, x_ref, s_ref, carry_ref, *, steps):
    nst = S5_TILE_ST
    nslab = S5_TILE_CH // LANES
    ntile = wb_ref.shape[0]

    @pl.when(pl.program_id(1) == 0)
    def _():
        carry_ref[...] = jnp.zeros_like(carry_ref)

    def load_u(i):
        return jnp.concatenate([u_ref[0, i * nslab + j] for j in range(nslab)], axis=-1)

    def project_in(i):
        x_ref[i % 2] = _dot(load_u(i).astype(BF16), wb_ref[i])

    def scan(i):
        slot = i % 2
        ar1 = a_ref[i, 0:1, :]
        ai1 = a_ref[i, 1:2, :]
        ar = jnp.broadcast_to(ar1, (SUBLANES, nst))
        ai = jnp.broadcast_to(ai1, (SUBLANES, nst))

        def advance(k, sr, si):
            xr = x_ref[slot, k * SUBLANES:(k + 1) * SUBLANES, 0:nst]
            xi = x_ref[slot, k * SUBLANES:(k + 1) * SUBLANES, nst:2 * nst]
            return ar * sr - ai * si + xr, ar * si + ai * sr + xi

        er = jnp.zeros((SUBLANES, nst), F32)
        ei = jnp.zeros((SUBLANES, nst), F32)
        for k in range(steps):
            er, ei = advance(k, er, ei)

        pr, pi = ar1, ai1
        for _ in range(int(math.log2(steps))):
            pr, pi = pr * pr - pi * pi, 2.0 * pr * pi

        rid = lax.broadcasted_iota(jnp.int32, (SUBLANES, nst), 0)
        cr = carry_ref[i, 0:1, :]
        ci = carry_ref[i, 1:2, :]
        sr = jnp.zeros((SUBLANES, nst), F32)
        si = jnp.zeros((SUBLANES, nst), F32)
        for r in range(SUBLANES):
            sr = jnp.where(rid == r, jnp.broadcast_to(cr, (SUBLANES, nst)), sr)
            si = jnp.where(rid == r, jnp.broadcast_to(ci, (SUBLANES, nst)), si)
            cr, ci = pr * cr - pi * ci + er[r:r + 1, :], pr * ci + pi * cr + ei[r:r + 1, :]
        carry_ref[i, 0:1, :] = cr
        carry_ref[i, 1:2, :] = ci

        for k in range(0, steps, 2):
            r0, i0 = advance(k, sr, si)
            sr, si = advance(k + 1, r0, i0)
            rows2 = slice(k * SUBLANES, (k + 2) * SUBLANES)
            s_ref[slot, rows2, 0:nst] = jnp.concatenate([r0, sr], axis=0).astype(BF16)
            s_ref[slot, rows2, nst:2 * nst] = jnp.concatenate([i0, si], axis=0).astype(BF16)

    def project_out(i):
        y = _dot(s_ref[i % 2], wc_ref[i]) + d_ref[i] * load_u(i)
        for j in range(nslab):
            y_ref[0, i * nslab + j] = y[:, j * LANES:(j + 1) * LANES]

    project_in(0)
    for i in range(ntile):
        if i + 1 < ntile:
            project_in(i + 1)
        scan(i)
        project_out(i)


def _s5_core(u_slab, wb, wc, a_tiles, d_tiles, layer, steps):
    bsz, nslab, seq, _ = u_slab.shape
    ntile = nslab * LANES // S5_TILE_CH
    rows = SUBLANES * steps
    const = lambda shp: pl.BlockSpec((ntile,) + shp[1:], lambda b, n: (layer,) + (0,) * (len(shp) - 1))
    tok = pl.BlockSpec((1, nslab, rows, LANES), lambda b, n: (b, 0, n, 0))
    return pl.pallas_call(
        functools.partial(_s5_core_kernel, steps=steps),
        out_shape=jax.ShapeDtypeStruct(u_slab.shape, F32),
        grid=(bsz, seq // rows),
        in_specs=[tok, const(wb.shape), const(wc.shape), const(a_tiles.shape), const(d_tiles.shape)],
        out_specs=tok,
        scratch_shapes=[pltpu.VMEM((2, rows, 2 * S5_TILE_ST), F32),
                        pltpu.VMEM((2, rows, 2 * S5_TILE_ST), BF16),
                        pltpu.VMEM((ntile, 2, S5_TILE_ST), F32)],
        compiler_params=_cparams(("parallel", "arbitrary")),
        name="s5_core",
    )(u_slab, wb, wc, a_tiles, d_tiles)


def _s5_post_kernel(y_ref, z_ref, h_ref, wg_ref, bg_ref, wo_ref, gp_ref, o_ref, *, steps):
    nslab = y_ref.shape[1]
    groups = list(_chunk_row_groups(y_ref.shape[2], steps))
    y = jnp.concatenate(
        [jnp.concatenate([y_ref[0, j, pl.ds(cm, SUBLANES, stride=SUBLANES), :] for _, cm in groups], axis=0)
         for j in range(nslab)], axis=-1)
    y = _gelu_tanh(y)
    y = y * _sigmoid_gate(_dot(y.astype(BF16), wg_ref[...]) + bg_ref[...])
    t = y * _silu(z_ref[0])
    o = _dot(t.astype(BF16), wo_ref[...])
    o_ref[0] = h_ref[0] + _rms(o, gp_ref[...])


def _s5_post(y_slab, z, h, w_glu, b_glu, w_out, g_post, steps):
    bsz, seq, d = h.shape
    nslab = y_slab.shape[1]
    tm = _row_tile(seq, 4 * 4 * d, 2 * 2 * d * d, SUBLANES * steps)
    assert tm % (SUBLANES * steps) == 0
    row = lambda t: t.reshape(1, -1)
    vec = pl.BlockSpec((1, d), lambda b, m: (0, 0))
    mat = pl.BlockSpec((d, d), lambda b, m: (0, 0))
    slab = pl.BlockSpec((1, nslab, tm, LANES), lambda b, m: (b, 0, m, 0))
    tok = pl.BlockSpec((1, tm, d), lambda b, m: (b, m, 0))
    return pl.pallas_call(
        functools.partial(_s5_post_kernel, steps=steps),
        out_shape=jax.ShapeDtypeStruct(h.shape, F32),
        grid=(bsz, seq // tm),
        in_specs=[slab, tok, tok, mat, vec, mat, vec],
        out_specs=tok,
        compiler_params=_cparams(("parallel", "parallel")),
        name="s5_post",
    )(y_slab, z, h, w_glu.astype(BF16), row(b_glu), w_out.astype(BF16), row(g_post))


def _s5_weights(lam_re, lam_im, log_dt, b_re, b_im, c_re, c_im, d_skip):
    nlayer, ngroup = lam_re.shape[:2]
    flat = lambda t: t.reshape((nlayer * ngroup,) + t.shape[2:])
    ntile = nlayer * ngroup // S5_TILE_GROUPS
    ar, ai, bbr, bbi = _s5_prep(flat(lam_re), flat(lam_im), flat(log_dt), flat(b_re), flat(b_im))
    eye = jnp.eye(S5_TILE_GROUPS, dtype=F32)

    def in_tile(bb):
        t = bb.reshape(ntile, S5_TILE_GROUPS, S5_STATE, S5_GROUP)
        return jnp.einsum('igph,gk->ighkp', t, eye).reshape(ntile, S5_TILE_CH, S5_TILE_ST)

    def out_tile(cc):
        t = cc.reshape(ntile, S5_TILE_GROUPS, S5_GROUP, S5_STATE)
        return jnp.einsum('ighp,gk->ikpgh', t, eye).reshape(ntile, S5_TILE_ST, S5_TILE_CH)

    wb = jnp.concatenate([in_tile(bbr), in_tile(bbi)], axis=2).astype(BF16)
    wc = jnp.concatenate([out_tile(flat(c_re)), -out_tile(flat(c_im))], axis=1).astype(BF16)
    a_tiles = jnp.stack([ar.reshape(ntile, S5_TILE_ST), ai.reshape(ntile, S5_TILE_ST)], axis=1)
    d_tiles = d_skip.reshape(ntile, 1, S5_TILE_CH)
    return wb, wc, a_tiles, d_tiles


def _s5_layer(h, g_pre, g_post, w_in, weights, layer, w_glu, b_glu, w_out):
    bsz, seq, d = h.shape
    steps = min(S5_STEPS, seq // SUBLANES)
    u_slab, z = _norm_proj(h, g_pre, w_in.astype(BF16), (d, d), (steps, 0), (F32, F32))
    y_slab = _s5_core(u_slab, *weights, layer, steps)
    return _s5_post(y_slab, z, h, w_glu, b_glu, w_out, g_post, steps)


def _gate_out_math(o, z, h, g_head, w_out, g_post, head_norm):
    if head_norm:
        parts = []
        for j in range(o.shape[-1] // HG_HEAD):
            oj = o[:, j * HG_HEAD:(j + 1) * HG_HEAD]
            parts.append(oj * lax.rsqrt(jnp.mean(oj * oj, axis=-1, keepdims=True) + NORM_EPS))
        o = jnp.concatenate(parts, axis=-1) * g_head
    t = o * _silu(z)
    return h + _rms(_dot(t.astype(BF16), w_out), g_post)


def _chunk_cumsum(x, chunk):
    rows, w = x.shape
    per = chunk // SUBLANES
    x3 = x.reshape(rows // SUBLANES, SUBLANES, w)
    pos = lax.broadcasted_iota(jnp.int32, x3.shape, 1)
    sh = 1
    while sh < SUBLANES:
        x3 = x3 + jnp.where(pos >= sh, pltpu.roll(x3, sh, 1), 0.0)
        sh *= 2
    x4 = x3.reshape(rows // chunk, per, SUBLANES, w)
    tot = x4[:, :, SUBLANES - 1:SUBLANES, :]
    pref = [jnp.zeros_like(tot[:, 0:1])]
    for j in range(1, per):
        pref.append(pref[-1] + tot[:, j - 1:j])
    return (x4 + jnp.concatenate(pref, axis=1)).reshape(rows, w)


def _block_mid_rows(x, blk):
    rows, w = x.shape
    half = blk // 2
    if blk >= SUBLANES:
        x3 = x.reshape(rows // blk, blk, w)
        return jnp.broadcast_to(x3[:, half - 1:half, :], (rows // blk, blk, w)).reshape(rows, w)
    pos = lax.broadcasted_iota(jnp.int32, (rows, w), 0) % blk
    out = x
    for p in range(blk):
        d = p - (half - 1)
        if d != 0:
            out = jnp.where(pos == p, pltpu.roll(x, d % rows, 0), out)
    return out


def _hg_core_kernel(q_ref, f_ref, v_ref, lbl_ref, z_ref, h_ref, gn_ref, wo_ref, gp_ref, out_ref,
                    xl_ref, qe_ref, kd_ref, g_ref, st_ref, o_scr, *, layer, chunk):
    rows, width = q_ref.shape[1], q_ref.shape[2]
    pair = 2 * HG_HEAD
    blocks = [2 ** i for i in range(1, int(math.log2(chunk)) + 1)]

    @pl.when(pl.program_id(1) == 0)
    def _():
        st_ref[...] = jnp.zeros_like(st_ref)

    lg = lbl_ref[...]
    ex = jnp.exp(lg - jnp.max(lg, axis=0, keepdims=True))
    p = ex / jnp.sum(ex, axis=0, keepdims=True)
    lb = jnp.zeros((1, width), F32)
    for j in range(1, layer + 1):
        lb = lb + p[j:j + 1, :]

    q = q_ref[0]
    f = lb + (1.0 - lb) * _sigmoid(f_ref[0])
    kk = 1.0 - f
    b = _chunk_cumsum(jnp.log(f), chunk)
    b3 = b.reshape(rows // chunk, chunk, width)
    b_last = b3[:, chunk - 1:chunk, :]

    def state_operands():
        g_ref[...] = jnp.exp(b_last.reshape(rows // chunk, width))
        qe_ref[...] = (q * jnp.exp(b)).astype(BF16)
        kd_ref[...] = (kk * jnp.exp(jnp.broadcast_to(b_last, b3.shape).reshape(rows, width) - b)).astype(BF16)

    t_i = lax.broadcasted_iota(jnp.int32, (chunk, HG_HEAD), 0)
    s_i = lax.broadcasted_iota(jnp.int32, (chunk, HG_HEAD), 1) % chunk
    lo = lax.broadcasted_iota(jnp.int32, (chunk, pair), 1) < HG_HEAD
    bd = (lax.broadcasted_iota(jnp.int32, (pair, pair), 0) // HG_HEAD
          == lax.broadcasted_iota(jnp.int32, (pair, pair), 1) // HG_HEAD)

    def split_heads(t):
        z = jnp.zeros_like(t)
        return jnp.concatenate([jnp.where(lo, t, z), jnp.where(lo, z, t)], axis=0)

    def chunk_pairs():
        for c in range(rows // chunk):
            for pr in range(width // pair):
                yield c, pr, slice(c * chunk, (c + 1) * chunk), slice(pr * pair, (pr + 1) * pair)

    def finish(c, pr, rs, ls, sc):
        vpair = v_ref[0, rs, ls]
        st = st_ref[pr]
        o_scr[rs, ls] = _dot(sc.astype(BF16), split_heads(vpair)) + _dot_nt(qe_ref[rs, ls], st.astype(BF16))
        st_ref[pr] = g_ref[c:c + 1, ls] * st + jnp.where(bd, _dot_tn(vpair, kd_ref[rs, ls]), 0.0)

    mild = jnp.max(-b_last) < HG_MILD_DECAY

    @pl.when(mild)
    def _():
        state_operands()
        xl_ref[0] = (kk * jnp.exp(-b)).astype(BF16)
        causal = s_i <= t_i
        for c, pr, rs, ls in chunk_pairs():
            sc = jnp.where(causal, _dot_nt(qe_ref[rs, ls], split_heads(xl_ref[0, rs, ls])), 0.0)
            finish(c, pr, rs, ls, sc)

    @pl.when(jnp.logical_not(mild))
    def _():
        state_operands()
        xl_ref[0] = (q * kk).astype(BF16)
        pos = lax.broadcasted_iota(jnp.int32, (rows, width), 0)
        for li, blk in enumerate(blocks):
            w = jnp.exp(-jnp.abs(b - _block_mid_rows(b, blk)))
            xl_ref[li + 1] = (jnp.where(pos % blk >= blk // 2, q, kk) * w).astype(BF16)
        eye = t_i == s_i
        lmask = [((t_i // blk) == (s_i // blk)) & ((t_i % blk) >= blk // 2) & ((s_i % blk) < blk // 2)
                 for blk in blocks]
        ones_bd = jnp.where(lax.broadcasted_iota(jnp.int32, (pair, HG_HEAD), 0) // HG_HEAD
                            == lax.broadcasted_iota(jnp.int32, (pair, HG_HEAD), 1) // chunk, 1.0, 0.0).astype(BF16)
        for c, pr, rs, ls in chunk_pairs():
            sc = jnp.where(eye, _dot(xl_ref[0, rs, ls], ones_bd), 0.0)
            for li in range(len(blocks)):
                xl = xl_ref[li + 1, rs, ls]
                sc = jnp.where(lmask[li], _dot_nt(xl, split_heads(xl)), sc)
            finish(c, pr, rs, ls, sc)

    out_ref[0] = _gate_out_math(o_scr[...], z_ref[0], h_ref[0], gn_ref[...], wo_ref[...], gp_ref[...], True)


def _hg_core(q, fz, v, z, h, lb_logits, layer, norm_g, w_out, g_post):
    bsz, seq, width = q.shape
    blk = min(HG_BLOCK, seq)
    chunk = min(HG_CHUNK, blk)
    nlev = int(math.log2(chunk))
    assert HG_HEAD == 2 * chunk and width % (2 * HG_HEAD) == 0
    tok = pl.BlockSpec((1, blk, width), lambda b, n: (b, n, 0))
    const = lambda shp: pl.BlockSpec(shp, lambda b, n: (0, 0))
    row = lambda t: t.reshape(1, -1)
    return pl.pallas_call(
        functools.partial(_hg_core_kernel, layer=layer, chunk=chunk),
        out_shape=jax.ShapeDtypeStruct(h.shape, F32),
        grid=(bsz, seq // blk),
        in_specs=[tok, tok, tok, const(lb_logits.shape), tok, tok, const((1, width)), const(w_out.shape),
                  const((1, width))],
        out_specs=tok,
        scratch_shapes=[pltpu.VMEM((nlev + 1, blk, width), BF16),
                        pltpu.VMEM((blk, width), BF16),
                        pltpu.VMEM((blk, width), BF16),
                        pltpu.VMEM((blk // chunk, width), F32),
                        pltpu.VMEM((width // (2 * HG_HEAD), 2 * HG_HEAD, 2 * HG_HEAD), F32),
                        pltpu.VMEM((blk, width), F32)],
        compiler_params=_cparams(("parallel", "arbitrary")),
        name="hgrn2_core",
    )(q, fz, v, lb_logits, z, h, row(norm_g), w_out.astype(BF16), row(g_post))


def _hg_layer(h, g_pre, g_post, w_in, lb_logits, layer, norm_g, w_out):
    d = h.shape[-1]
    q, fz, v, z = _norm_proj(h, g_pre, w_in.astype(BF16), (d, d, d, d), (0,) * 4, (F32, F32, BF16, F32))
    return _hg_core(q, fz, v, z, h, lb_logits, layer, norm_g, w_out, g_post)


def _at_proj_kernel(h_ref, g_ref, w_ref, bias_ref, pos_ref, invf_ref, q_ref, k_ref, v_ref, z_ref,
                    *, qw, kvw):
    y = _rms(h_ref[0], g_ref[...]).astype(BF16)
    half = AT_HEAD // 2
    nq4 = LANES // half
    ang = pos_ref[0].astype(F32) * invf_ref[...]
    cos4 = jnp.cos(ang)
    sin4 = jnp.sin(ang)
    lane4 = lax.broadcasted_iota(jnp.int32, ang.shape, 1)

    def spread(t):
        parts = []
        for j in range(nq4):
            m = jnp.where(lane4 // half == j, t, 0.0)
            x = m
            for s in range(1, nq4):
                x = x + pltpu.roll(m, s * half, 1)
            parts.append(x)
        return jnp.concatenate(parts, axis=0)

    lane = lax.broadcasted_iota(jnp.int32, (h_ref.shape[1], LANES), 1)
    first = (lane % AT_HEAD) < half
    low = lane < AT_HEAD
    cosf = spread(cos4)
    sinf = spread(sin4)
    sinf = jnp.where(first, -sinf, sinf)

    def rope(t):
        partner = jnp.where(first, pltpu.roll(t, LANES - AT_HEAD // 2, 1), pltpu.roll(t, AT_HEAD // 2, 1))
        return t * cosf + partner * sinf

    def store_dup(ref, j, t):
        r = pltpu.roll(t, AT_HEAD, 1)
        ref[0, :, (2 * j) * LANES:(2 * j + 1) * LANES] = jnp.where(low, t, r).astype(ref.dtype)
        ref[0, :, (2 * j + 1) * LANES:(2 * j + 2) * LANES] = jnp.where(low, r, t).astype(ref.dtype)

    for j in range(qw // LANES):
        sl = slice(j * LANES, (j + 1) * LANES)
        q_ref[0, :, sl] = rope(_dot(y, w_ref[:, sl]) + bias_ref[:, sl]).astype(q_ref.dtype)
    for j in range(kvw // LANES):
        sl = slice(qw + j * LANES, qw + (j + 1) * LANES)
        store_dup(k_ref, j, rope(_dot(y, w_ref[:, sl]) + bias_ref[:, sl]))
    for j in range(kvw // LANES):
        sl = slice(qw + kvw + j * LANES, qw + kvw + (j + 1) * LANES)
        store_dup(v_ref, j, _dot(y, w_ref[:, sl]) + bias_ref[:, sl])
    off = qw + 2 * kvw
    z_ref[0] = _dot(y, w_ref[:, off:off + qw])


def _at_proj(h, g, w_bf16, bias, positions, qw, kvw):
    bsz, seq, d = h.shape
    tm = _row_tile(seq, 4 * d + 2 * qw + 4 * 2 * kvw + 4 * qw, w_bf16.size * 2, 4 * SUBLANES)
    half = AT_HEAD // 2
    inv_freq = ROPE_THETA ** (-jnp.arange(0, AT_HEAD, 2, dtype=F32) / AT_HEAD)
    nq4 = LANES // half
    invf = jnp.tile(inv_freq, nq4).reshape(1, LANES)
    pos4 = positions.reshape(bsz, seq // tm, nq4, tm // nq4).transpose(0, 1, 3, 2)
    pos4 = jnp.repeat(pos4, half, axis=-1).reshape(bsz, seq // nq4, LANES)
    tokspec = lambda w: pl.BlockSpec((1, tm, w), lambda b, m: (b, m, 0))
    const = lambda shp: pl.BlockSpec(shp, lambda b, m: (0, 0))
    return pl.pallas_call(
        functools.partial(_at_proj_kernel, qw=qw, kvw=kvw),
        out_shape=[jax.ShapeDtypeStruct((bsz, seq, qw), BF16), jax.ShapeDtypeStruct((bsz, seq, 2 * kvw), BF16),
                   jax.ShapeDtypeStruct((bsz, seq, 2 * kvw), BF16), jax.ShapeDtypeStruct((bsz, seq, qw), F32)],
        grid=(bsz, seq // tm),
        in_specs=[tokspec(d), const((1, d)), const(w_bf16.shape), const((1, qw + 2 * kvw)),
                  pl.BlockSpec((1, tm // nq4, LANES), lambda b, m: (b, m, 0)), const((1, LANES))],
        out_specs=[tokspec(qw), tokspec(2 * kvw), tokspec(2 * kvw), tokspec(qw)],
        compiler_params=_cparams(("parallel", "parallel")),
        name="attn_proj",
    )(h, g.reshape(1, d), w_bf16, bias.reshape(1, -1), pos4, invf)


def _at_core_kernel(q_ref, kc_ref, kp_ref, vc_ref, vp_ref, sink_ref, z_ref, h_ref, wo_ref, gp_ref, out_ref):
    blk = AT_BLOCK
    nsub = q_ref.shape[1] // blk
    n = pl.program_id(1)
    ngroup = kc_ref.shape[2] // LANES
    qi = lax.broadcasted_iota(jnp.int32, (2 * blk, 4 * blk), 0) % blk
    kj = lax.broadcasted_iota(jnp.int32, (2 * blk, 4 * blk), 1) % (2 * blk)
    dist = qi + blk - kj
    in_window = (dist >= 0) & (dist < blk)
    first_mask = in_window & ((kj >= blk) | (n > 0))
    low = lax.broadcasted_iota(jnp.int32, (2 * blk, LANES), 1) < AT_HEAD
    top = lax.broadcasted_iota(jnp.int32, (2 * blk, 1), 0) < blk
    ones_lo = jnp.where(low, 1.0, 0.0).astype(BF16)
    ones_hi = jnp.where(low, 0.0, 1.0).astype(BF16)
    scale = AT_HEAD ** -0.5
    items = [(j, g) for j in range(nsub) for g in range(ngroup)]

    def band(cur_ref, prev_ref, j, g):
        sl = slice(g * LANES, (g + 1) * LANES)
        prev = prev_ref[0, :, sl] if j == 0 else cur_ref[0, (j - 1) * blk:j * blk, sl]
        return jnp.concatenate([prev, cur_ref[0, j * blk:(j + 1) * blk, sl]], axis=0)

    scores = []
    for j, g in items:
        kdup = band(kc_ref, kp_ref, j, g)
        rhs = jnp.concatenate([jnp.where(low, kdup, 0.0), jnp.where(low, 0.0, kdup)], axis=0)
        rows = slice(j * blk, (j + 1) * blk)
        lhs = jnp.concatenate([q_ref[0, rows, (2 * g) * LANES:(2 * g + 1) * LANES],
                               q_ref[0, rows, (2 * g + 1) * LANES:(2 * g + 2) * LANES]], axis=0)
        scores.append(_dot_nt(lhs, rhs))

    probs, sink_terms = [], []
    for (j, g), sc in zip(items, scores):
        s = jnp.where(first_mask if j == 0 else in_window, sc * scale, -jnp.inf)
        sink_a = jnp.where(top, sink_ref[0:1, 4 * g:4 * g + 1], sink_ref[0:1, 4 * g + 2:4 * g + 3])
        sink_b = jnp.where(top, sink_ref[0:1, 4 * g + 1:4 * g + 2], sink_ref[0:1, 4 * g + 3:4 * g + 4])
        m_a = jnp.maximum(jnp.max(s[:, :2 * blk], axis=-1, keepdims=True), sink_a)
        m_b = jnp.maximum(jnp.max(s[:, 2 * blk:], axis=-1, keepdims=True), sink_b)
        e = jnp.concatenate([jnp.exp(s[:, :2 * blk] - m_a), jnp.exp(s[:, 2 * blk:] - m_b)], axis=1)
        probs.append(e.astype(BF16))
        sink_terms.append(jnp.where(low, jnp.exp(sink_a - m_a), jnp.exp(sink_b - m_b)))

    o_rows = [[] for _ in range(nsub)]
    for (j, g), p, st in zip(items, probs, sink_terms):
        vdup = band(vc_ref, vp_ref, j, g)
        va = jnp.where(low, vdup, 0.0)
        vb = jnp.where(low, 0.0, vdup)
        rhs = jnp.concatenate([jnp.concatenate([va, ones_lo], axis=1),
                               jnp.concatenate([vb, ones_hi], axis=1)], axis=0)
        out = _dot(p, rhs)
        o = out[:, :LANES] / (out[:, LANES:] + st)
        o_rows[j] += [o[:blk], o[blk:]]

    o = jnp.concatenate([jnp.concatenate(slabs, axis=-1) for slabs in o_rows], axis=0)
    out_ref[0] = _gate_out_math(o, z_ref[0], h_ref[0], None, wo_ref[...], gp_ref[...], False)


def _at_core(q, k_dup, v_dup, sinks, z, h, w_out, g_post):
    bsz, seq, qw = q.shape
    kvw = k_dup.shape[-1]
    d = h.shape[-1]
    blk = AT_BLOCK
    nsub = AT_SUBBLOCKS if seq % (AT_SUBBLOCKS * blk) == 0 else 1
    nq = qw // AT_HEAD
    assert nq == 4 * (kvw // LANES) and qw == 2 * kvw
    cur = lambda w: pl.BlockSpec((1, nsub * blk, w), lambda b, n: (b, n, 0))
    prev = lambda w: pl.BlockSpec((1, blk, w), lambda b, n: (b, jnp.maximum(nsub * n - 1, 0), 0))
    const = lambda shp: pl.BlockSpec(shp, lambda b, n: (0, 0))
    return pl.pallas_call(
        _at_core_kernel,
        out_shape=jax.ShapeDtypeStruct(h.shape, F32),
        grid=(bsz, seq // (nsub * blk)),
        in_specs=[cur(qw), cur(kvw), prev(kvw), cur(kvw), prev(kvw), const((1, nq)),
                  cur(qw), cur(d), const(w_out.shape), const((1, d))],
        out_specs=cur(d),
        compiler_params=_cparams(("parallel", "arbitrary")),
        name="attn_core",
    )(q, k_dup, k_dup, v_dup, v_dup, sinks.reshape(1, nq), z, h, w_out.astype(BF16), g_post.reshape(1, d))


def _at_layer(h, positions, g_pre, g_post, w_in, b_in, sinks, w_out):
    qw = w_out.shape[0]
    kvw = (b_in.shape[0] - qw) // 2
    q, k, v, z = _at_proj(h, g_pre, w_in.astype(BF16), b_in, positions, qw, kvw)
    return _at_core(q, k, v, sinks, z, h, w_out, g_post)


def kernel(x, positions, norm_pre, norm_post, s5_w_in, s5_lambda_re, s5_lambda_im, s5_log_dt, s5_b_re, s5_b_im, s5_c_re, s5_c_im, s5_d, s5_w_glu, s5_b_glu, s5_w_out, hg_w_in, hg_lb_logits, hg_norm, hg_w_out, at_w_in, at_b_in, at_sinks, at_w_out):
    depth = norm_pre.shape[0]
    s5_weights = _s5_weights(s5_lambda_re, s5_lambda_im, s5_log_dt, s5_b_re, s5_b_im, s5_c_re, s5_c_im, s5_d)
    h = x
    for i in range(depth):
        kind, j = i % 3, i // 3
        if kind == 0:
            h = _s5_layer(h, norm_pre[i], norm_post[i], s5_w_in[j], s5_weights, j, s5_w_glu[j], s5_b_glu[j],
                          s5_w_out[j])
        elif kind == 1:
            h = _hg_layer(h, norm_pre[i], norm_post[i], hg_w_in[j], hg_lb_logits, i, hg_norm[j], hg_w_out[j])
        else:
            h = _at_layer(h, positions, norm_pre[i], norm_post[i], at_w_in[j], at_b_in[j], at_sinks[j],
                          at_w_out[j])
    return h
```

```python
import functools
import math

import jax
import jax.numpy as jnp
from jax import lax
from jax.experimental import pallas as pl
from jax.experimental.pallas import tpu as pltpu

F32 = jnp.float32
BF16 = jnp.bfloat16

NORM_EPS = 1e-6
LANES = 128
SUBLANES = 8
VMEM_LIMIT = 56 * 1024 * 1024

S5_GROUP = 16
S5_STATE = 64
S5_TILE_GROUPS = 16
S5_TILE_CH = S5_TILE_GROUPS * S5_GROUP
S5_TILE_ST = S5_TILE_GROUPS * S5_STATE
S5_STEPS = 64
HG_HEAD = 128
HG_CHUNK = 64
HG_BLOCK = 512
HG_MILD_DECAY = 80.0
AT_HEAD = 64
AT_KV_HEADS = 4
AT_BLOCK = 128
AT_SUBBLOCKS = 4
AT_WAVE = 16
ROPE_THETA = 10000.0

ROW_TILES = (1024, 512, 256)


def _cparams(sem):
    return pltpu.CompilerParams(dimension_semantics=sem, vmem_limit_bytes=VMEM_LIMIT)


def _row_tile(seq, row_bytes, resident_bytes, multiple=SUBLANES):
    budget = VMEM_LIMIT - VMEM_LIMIT // 4
    for tm in ROW_TILES:
        if seq % tm == 0 and tm % multiple == 0 and 2 * (tm * row_bytes + resident_bytes) <= budget:
            return tm
    return seq


def _rms(x, g):
    return x * lax.rsqrt(jnp.mean(x * x, axis=-1, keepdims=True) + NORM_EPS) * g


def _sigmoid(x):
    return 1.0 / (1.0 + jnp.exp(-x))


def _sigmoid_gate(x):
    return 0.5 + 0.5 * jnp.tanh(0.5 * x)


def _silu(x):
    return x * _sigmoid_gate(x)


def _gelu_tanh(x):
    c = math.sqrt(2.0 / math.pi)
    return 0.5 * x * (1.0 + jnp.tanh(c * (x + 0.044715 * (x * x * x))))


def _dot(a, b):
    return jnp.dot(a, b, preferred_element_type=F32)


def _dot_nt(a, b):
    return lax.dot_general(a, b, (((1,), (1,)), ((), ())), preferred_element_type=F32)


def _dot_tn(a, b):
    return lax.dot_general(a, b, (((0,), (0,)), ((), ())), preferred_element_type=F32)


def _chunk_row_groups(rows, steps):
    for base in range(0, rows, SUBLANES * steps):
        for r in range(SUBLANES):
            for k0 in range(0, steps, SUBLANES):
                yield base + r * steps + k0, base + k0 * SUBLANES + r


def _norm_proj_kernel(h_ref, g_ref, w_ref, *out_refs, widths, slab_steps):
    y = _rms(h_ref[0], g_ref[...]).astype(BF16)
    off = 0
    for o_ref, width, steps in zip(out_refs, widths, slab_steps):
        r = _dot(y, w_ref[:, off:off + width])
        if steps:
            for nat, cm in _chunk_row_groups(r.shape[0], steps):
                for j in range(width // LANES):
                    o_ref[0, j, pl.ds(cm, SUBLANES, stride=SUBLANES), :] = (
                        r[nat:nat + SUBLANES, j * LANES:(j + 1) * LANES])
        else:
            o_ref[0] = r.astype(o_ref.dtype)
        off += width


def _norm_proj(h, g, w_bf16, widths, slab_steps, dtypes):
    bsz, seq, d = h.shape
    row_bytes = 4 * d + sum(w * jnp.dtype(t).itemsize for w, t in zip(widths, dtypes))
    tm = _row_tile(seq, row_bytes, w_bf16.size * 2, SUBLANES * max(max(slab_steps), 1))
    out_shape, out_specs = [], []
    for width, as_slab, dtype in zip(widths, slab_steps, dtypes):
        if as_slab:
            assert tm % (SUBLANES * as_slab) == 0 and dtype == F32
            ns = width // LANES
            out_shape.append(jax.ShapeDtypeStruct((bsz, ns, seq, LANES), F32))
            out_specs.append(pl.BlockSpec((1, ns, tm, LANES), lambda b, m: (b, 0, m, 0)))
        else:
            out_shape.append(jax.ShapeDtypeStruct((bsz, seq, width), dtype))
            out_specs.append(pl.BlockSpec((1, tm, width), lambda b, m: (b, m, 0)))
    return pl.pallas_call(
        functools.partial(_norm_proj_kernel, widths=tuple(widths), slab_steps=tuple(slab_steps)),
        out_shape=out_shape,
        grid=(bsz, seq // tm),
        in_specs=[pl.BlockSpec((1, tm, d), lambda b, m: (b, m, 0)),
                  pl.BlockSpec((1, d), lambda b, m: (0, 0)),
                  pl.BlockSpec(w_bf16.shape, lambda b, m: (0, 0))],
        out_specs=out_specs,
        compiler_params=_cparams(("parallel", "parallel")),
        name="norm_proj",
    )(h, g.reshape(1, d), w_bf16)


def _zoh(lr, li, ldt):
    dt = jnp.exp(ldt)
    mag = jnp.exp(lr * dt)
    ar = mag * jnp.cos(li * dt)
    ai = mag * jnp.sin(li * dt)
    den = lr * lr + li * li
    qr = ((ar - 1.0) * lr + ai * li) / den
    qi = (ai * lr - (ar - 1.0) * li) / den
    return ar, ai, qr, qi


def _s5_tiles_kernel(lr_ref, li_ref, ldt_ref, lrc_ref, lic_ref, ldtc_ref, br_ref, bi_ref, cr_ref, ci_ref,
                     ar_ref, ai_ref, wb_ref, wc_ref):
    ar, ai, _, _ = _zoh(lr_ref[...], li_ref[...], ldt_ref[...])
    ar_ref[...] = ar
    ai_ref[...] = ai
    _, _, qr, qi = _zoh(lrc_ref[...], lic_ref[...], ldtc_ref[...])
    br = br_ref[...]
    bi = bi_ref[...]
    bbr = (qr * br - qi * bi).astype(BF16)
    bbi = (qr * bi + qi * br).astype(BF16)

    ch, st = S5_TILE_CH, S5_TILE_ST
    sel_h = jnp.where(lax.broadcasted_iota(jnp.int32, (ch, S5_GROUP), 0) % S5_GROUP
                      == lax.broadcasted_iota(jnp.int32, (ch, S5_GROUP), 1), 1.0, 0.0).astype(BF16)
    diag_in = (lax.broadcasted_iota(jnp.int32, (ch, st), 0) // S5_GROUP
               == lax.broadcasted_iota(jnp.int32, (ch, st), 1) // S5_STATE)
    wb_ref[0, :, 0:st] = jnp.where(diag_in, _dot_nt(sel_h, bbr), 0.0).astype(BF16)
    wb_ref[0, :, st:2 * st] = jnp.where(diag_in, _dot_nt(sel_h, bbi), 0.0).astype(BF16)
    sel_p = jnp.where(lax.broadcasted_iota(jnp.int32, (st, S5_STATE), 0) % S5_STATE
                      == lax.broadcasted_iota(jnp.int32, (st, S5_STATE), 1), 1.0, 0.0).astype(BF16)
    diag_out = (lax.broadcasted_iota(jnp.int32, (st, ch), 0) // S5_STATE
                == lax.broadcasted_iota(jnp.int32, (st, ch), 1) // S5_GROUP)
    wc_ref[0, 0:st, :] = jnp.where(diag_out, _dot_nt(sel_p, cr_ref[...].astype(BF16)), 0.0).astype(BF16)
    wc_ref[0, st:2 * st, :] = jnp.where(diag_out, -_dot_nt(sel_p, ci_ref[...].astype(BF16)), 0.0).astype(BF16)


def _s5_core_kernel(u_ref, wb_ref, wc_ref, a_ref, d_ref, y_ref, x_ref, s_ref, carry_ref, *, steps):
    nst = S5_TILE_ST
    nslab = S5_TILE_CH // LANES
    ntile = wb_ref.shape[0]

    @pl.when(pl.program_id(1) == 0)
    def _():
        carry_ref[...] = jnp.zeros_like(carry_ref)

    def load_u(i):
        return jnp.concatenate([u_ref[0, i * nslab + j] for j in range(nslab)], axis=-1)

    def project_in(i):
        x_ref[i % 2] = _dot(load_u(i).astype(BF16), wb_ref[i])

    def scan(i):
        slot = i % 2
        ar1 = a_ref[i, 0:1, :]
        ai1 = a_ref[i, 1:2, :]
        ar = jnp.broadcast_to(ar1, (SUBLANES, nst))
        ai = jnp.broadcast_to(ai1, (SUBLANES, nst))

        def advance(k, sr, si):
            xr = x_ref[slot, k * SUBLANES:(k + 1) * SUBLANES, 0:nst]
            xi = x_ref[slot, k * SUBLANES:(k + 1) * SUBLANES, nst:2 * nst]
            return ar * sr - ai * si + xr, ar * si + ai * sr + xi

        er = jnp.zeros((SUBLANES, nst), F32)
        ei = jnp.zeros((SUBLANES, nst), F32)
        for k in range(steps):
            er, ei = advance(k, er, ei)

        pr, pi = ar1, ai1
        for _ in range(int(math.log2(steps))):
            pr, pi = pr * pr - pi * pi, 2.0 * pr * pi

        rid = lax.broadcasted_iota(jnp.int32, (SUBLANES, nst), 0)
        cr = carry_ref[i, 0:1, :]
        ci = carry_ref[i, 1:2, :]
        sr = jnp.zeros((SUBLANES, nst), F32)
        si = jnp.zeros((SUBLANES, nst), F32)
        for r in range(SUBLANES):
            sr = jnp.where(rid == r, jnp.broadcast_to(cr, (SUBLANES, nst)), sr)
            si = jnp.where(rid == r, jnp.broadcast_to(ci, (SUBLANES, nst)), si)
            cr, ci = pr * cr - pi * ci + er[r:r + 1, :], pr * ci + pi * cr + ei[r:r + 1, :]
        carry_ref[i, 0:1, :] = cr
        carry_ref[i, 1:2, :] = ci

        for k in range(0, steps, 2):
            r0, i0 = advance(k, sr, si)
            sr, si = advance(k + 1, r0, i0)
            rows2 = slice(k * SUBLANES, (k + 2) * SUBLANES)
            s_ref[slot, rows2, 0:nst] = jnp.concatenate([r0, sr], axis=0).astype(BF16)
            s_ref[slot, rows2, nst:2 * nst] = jnp.concatenate([i0, si], axis=0).astype(BF16)

    def project_out(i):
        y = _dot(s_ref[i % 2], wc_ref[i]) + d_ref[i] * load_u(i)
        for j in range(nslab):
            y_ref[0, i * nslab + j] = y[:, j * LANES:(j + 1) * LANES]

    project_in(0)
    for i in range(ntile):
        if i + 1 < ntile:
            project_in(i + 1)
        scan(i)
        project_out(i)


def _s5_core(u_slab, wb, wc, a_tiles, d_tiles, layer, steps):
    bsz, nslab, seq, _ = u_slab.shape
    ntile = nslab * LANES // S5_TILE_CH
    rows = SUBLANES * steps
    const = lambda shp: pl.BlockSpec((ntile,) + shp[1:], lambda b, n: (layer,) + (0,) * (len(shp) - 1))
    tok = pl.BlockSpec((1, nslab, rows, LANES), lambda b, n: (b, 0, n, 0))
    return pl.pallas_call(
        functools.partial(_s5_core_kernel, steps=steps),
        out_shape=jax.ShapeDtypeStruct(u_slab.shape, F32),
        grid=(bsz, seq // rows),
        in_specs=[tok, const(wb.shape), const(wc.shape), const(a_tiles.shape), const(d_tiles.shape)],
        out_specs=tok,
        scratch_shapes=[pltpu.VMEM((2, rows, 2 * S5_TILE_ST), F32),
                        pltpu.VMEM((2, rows, 2 * S5_TILE_ST), BF16),
                        pltpu.VMEM((ntile, 2, S5_TILE_ST), F32)],
        compiler_params=_cparams(("parallel", "arbitrary")),
        name="s5_core",
    )(u_slab, wb, wc, a_tiles, d_tiles)


def _s5_post_kernel(y_ref, z_ref, h_ref, wg_ref, bg_ref, wo_ref, gp_ref, o_ref, *, steps):
    nslab = y_ref.shape[1]
    groups = list(_chunk_row_groups(y_ref.shape[2], steps))
    y = jnp.concatenate(
        [jnp.concatenate([y_ref[0, j, pl.ds(cm, SUBLANES, stride=SUBLANES), :] for _, cm in groups], axis=0)
         for j in range(nslab)], axis=-1)
    y = _gelu_tanh(y)
    y = y * _sigmoid_gate(_dot(y.astype(BF16), wg_ref[...]) + bg_ref[...])
    t = y * _silu(z_ref[0])
    o = _dot(t.astype(BF16), wo_ref[...])
    o_ref[0] = h_ref[0] + _rms(o, gp_ref[...])


def _s5_post(y_slab, z, h, w_glu, b_glu, w_out, g_post, steps):
    bsz, seq, d = h.shape
    nslab = y_slab.shape[1]
    tm = _row_tile(seq, 4 * 4 * d, 2 * 2 * d * d, SUBLANES * steps)
    assert tm % (SUBLANES * steps) == 0
    row = lambda t: t.reshape(1, -1)
    vec = pl.BlockSpec((1, d), lambda b, m: (0, 0))
    mat = pl.BlockSpec((d, d), lambda b, m: (0, 0))
    slab = pl.BlockSpec((1, nslab, tm, LANES), lambda b, m: (b, 0, m, 0))
    tok = pl.BlockSpec((1, tm, d), lambda b, m: (b, m, 0))
    return pl.pallas_call(
        functools.partial(_s5_post_kernel, steps=steps),
        out_shape=jax.ShapeDtypeStruct(h.shape, F32),
        grid=(bsz, seq // tm),
        in_specs=[slab, tok, tok, mat, vec, mat, vec],
        out_specs=tok,
        compiler_params=_cparams(("parallel", "parallel")),
        name="s5_post",
    )(y_slab, z, h, w_glu.astype(BF16), row(b_glu), w_out.astype(BF16), row(g_post))


def _s5_weights(lam_re, lam_im, log_dt, b_re, b_im, c_re, c_im, d_skip):
    nlayer, ngroup, nstate, gsize = b_re.shape
    ng = nlayer * ngroup
    ntile = ng // S5_TILE_GROUPS
    tg = S5_TILE_GROUPS
    col = lambda t: t.reshape(ng * nstate, 1)
    ldt = log_dt.reshape(ng, 1)
    grp = lambda w: pl.BlockSpec((tg, w), lambda i: (i, 0))
    rows = lambda r, w: pl.BlockSpec((r, w), lambda i: (i, 0))
    ar, ai, wb, wc = pl.pallas_call(
        _s5_tiles_kernel,
        out_shape=[jax.ShapeDtypeStruct((ng, nstate), F32), jax.ShapeDtypeStruct((ng, nstate), F32),
                   jax.ShapeDtypeStruct((ntile, S5_TILE_CH, 2 * S5_TILE_ST), BF16),
                   jax.ShapeDtypeStruct((ntile, 2 * S5_TILE_ST, S5_TILE_CH), BF16)],
        grid=(ntile,),
        in_specs=[grp(nstate), grp(nstate), grp(1),
                  rows(tg * nstate, 1), rows(tg * nstate, 1), rows(tg * nstate, 1),
                  rows(tg * nstate, gsize), rows(tg * nstate, gsize),
                  rows(tg * gsize, nstate), rows(tg * gsize, nstate)],
        out_specs=[grp(nstate), grp(nstate),
                   pl.BlockSpec((1, S5_TILE_CH, 2 * S5_TILE_ST), lambda i: (i, 0, 0)),
                   pl.BlockSpec((1, 2 * S5_TILE_ST, S5_TILE_CH), lambda i: (i, 0, 0))],
        compiler_params=_cparams(("parallel",)),
        name="s5_tiles",
    )(lam_re.reshape(ng, nstate), lam_im.reshape(ng, nstate), ldt,
      col(lam_re), col(lam_im), col(jnp.broadcast_to(ldt, (ng, nstate))),
      b_re.reshape(ng * nstate, gsize), b_im.reshape(ng * nstate, gsize),
      c_re.reshape(ng * gsize, nstate), c_im.reshape(ng * gsize, nstate))
    a_tiles = jnp.stack([ar.reshape(ntile, S5_TILE_ST), ai.reshape(ntile, S5_TILE_ST)], axis=1)
    d_tiles = d_skip.reshape(ntile, 1, S5_TILE_CH)
    return wb, wc, a_tiles, d_tiles


def _s5_layer(h, g_pre, g_post, w_in, weights, layer, w_glu, b_glu, w_out):
    bsz, seq, d = h.shape
    steps = min(S5_STEPS, seq // SUBLANES)
    u_slab, z = _norm_proj(h, g_pre, w_in.astype(BF16), (d, d), (steps, 0), (F32, F32))
    y_slab = _s5_core(u_slab, *weights, layer, steps)
    return _s5_post(y_slab, z, h, w_glu, b_glu, w_out, g_post, steps)


def _gate_out_math(o, z, h, g_head, w_out, g_post, head_norm):
    if head_norm:
        parts = []
        for j in range(o.shape[-1] // HG_HEAD):
            oj = o[:, j * HG_HEAD:(j + 1) * HG_HEAD]
            parts.append(oj * lax.rsqrt(jnp.mean(oj * oj, axis=-1, keepdims=True) + NORM_EPS))
        o = jnp.concatenate(parts, axis=-1) * g_head
    t = o * _silu(z)
    return h + _rms(_dot(t.astype(BF16), w_out), g_post)


def _chunk_cumsum(x, chunk):
    rows, w = x.shape
    per = chunk // SUBLANES
    x3 = x.reshape(rows // SUBLANES, SUBLANES, w)
    pos = lax.broadcasted_iota(jnp.int32, x3.shape, 1)
    sh = 1
    while sh < SUBLANES:
        x3 = x3 + jnp.where(pos >= sh, pltpu.roll(x3, sh, 1), 0.0)
        sh *= 2
    x4 = x3.reshape(rows // chunk, per, SUBLANES, w)
    tot = x4[:, :, SUBLANES - 1:SUBLANES, :]
    pref = [jnp.zeros_like(tot[:, 0:1])]
    for j in range(1, per):
        pref.append(pref[-1] + tot[:, j - 1:j])
    return (x4 + jnp.concatenate(pref, axis=1)).reshape(rows, w)


def _block_mid_rows(x, blk):
    rows, w = x.shape
    half = blk // 2
    if blk >= SUBLANES:
        x3 = x.reshape(rows // blk, blk, w)
        return jnp.broadcast_to(x3[:, half - 1:half, :], (rows // blk, blk, w)).reshape(rows, w)
    pos = lax.broadcasted_iota(jnp.int32, (rows, w), 0) % blk
    out = x
    for p in range(blk):
        d = p - (half - 1)
        if d != 0:
            out = jnp.where(pos == p, pltpu.roll(x, d % rows, 0), out)
    return out


def _hg_core_kernel(q_ref, f_ref, v_ref, lbl_ref, z_ref, h_ref, gn_ref, wo_ref, gp_ref, out_ref,
                    xl_ref, qe_ref, kd_ref, g_ref, st_ref, o_scr, *, layer, chunk):
    rows, width = q_ref.shape[1], q_ref.shape[2]
    pair = 2 * HG_HEAD
    blocks = [2 ** i for i in range(1, int(math.log2(chunk)) + 1)]

    @pl.when(pl.program_id(1) == 0)
    def _():
        st_ref[...] = jnp.zeros_like(st_ref)

    lg = lbl_ref[...]
    ex = jnp.exp(lg - jnp.max(lg, axis=0, keepdims=True))
    p = ex / jnp.sum(ex, axis=0, keepdims=True)
    lb = jnp.zeros((1, width), F32)
    for j in range(1, layer + 1):
        lb = lb + p[j:j + 1, :]

    q = q_ref[0]
    f = lb + (1.0 - lb) * _sigmoid(f_ref[0])
    kk = 1.0 - f
    b = _chunk_cumsum(jnp.log(f), chunk)
    b3 = b.reshape(rows // chunk, chunk, width)
    b_last = b3[:, chunk - 1:chunk, :]

    def state_operands():
        g_ref[...] = jnp.exp(b_last.reshape(rows // chunk, width))
        qe_ref[...] = (q * jnp.exp(b)).astype(BF16)
        kd_ref[...] = (kk * jnp.exp(jnp.broadcast_to(b_last, b3.shape).reshape(rows, width) - b)).astype(BF16)

    t_i = lax.broadcasted_iota(jnp.int32, (chunk, HG_HEAD), 0)
    s_i = lax.broadcasted_iota(jnp.int32, (chunk, HG_HEAD), 1) % chunk
    lo = lax.broadcasted_iota(jnp.int32, (chunk, pair), 1) < HG_HEAD
    bd = (lax.broadcasted_iota(jnp.int32, (pair, pair), 0) // HG_HEAD
          == lax.broadcasted_iota(jnp.int32, (pair, pair), 1) // HG_HEAD)

    def split_heads(t):
        z = jnp.zeros_like(t)
        return jnp.concatenate([jnp.where(lo, t, z), jnp.where(lo, z, t)], axis=0)

    def chunk_pairs():
        for c in range(rows // chunk):
            for pr in range(width // pair):
                yield c, pr, slice(c * chunk, (c + 1) * chunk), slice(pr * pair, (pr + 1) * pair)

    def finish(c, pr, rs, ls, sc):
        vpair = v_ref[0, rs, ls]
        st = st_ref[pr]
        o_scr[rs, ls] = _dot(sc.astype(BF16), split_heads(vpair)) + _dot_nt(qe_ref[rs, ls], st.astype(BF16))
        st_ref[pr] = g_ref[c:c + 1, ls] * st + jnp.where(bd, _dot_tn(vpair, kd_ref[rs, ls]), 0.0)

    mild = jnp.max(-b_last) < HG_MILD_DECAY

    @pl.when(mild)
    def _():
        state_operands()
        xl_ref[0] = (kk * jnp.exp(-b)).astype(BF16)
        causal = s_i <= t_i
        for c, pr, rs, ls in chunk_pairs():
            sc = jnp.where(causal, _dot_nt(qe_ref[rs, ls], split_heads(xl_ref[0, rs, ls])), 0.0)
            finish(c, pr, rs, ls, sc)

    @pl.when(jnp.logical_not(mild))
    def _():
        state_operands()
        xl_ref[0] = (q * kk).astype(BF16)
        pos = lax.broadcasted_iota(jnp.int32, (rows, width), 0)
        for li, blk in enumerate(blocks):
            w = jnp.exp(-jnp.abs(b - _block_mid_rows(b, blk)))
            xl_ref[li + 1] = (jnp.where(pos % blk >= blk // 2, q, kk) * w).astype(BF16)
        eye = t_i == s_i
        lmask = [((t_i // blk) == (s_i // blk)) & ((t_i % blk) >= blk // 2) & ((s_i % blk) < blk // 2)
                 for blk in blocks]
        ones_bd = jnp.where(lax.broadcasted_iota(jnp.int32, (pair, HG_HEAD), 0) // HG_HEAD
                            == lax.broadcasted_iota(jnp.int32, (pair, HG_HEAD), 1) // chunk, 1.0, 0.0).astype(BF16)
        for c, pr, rs, ls in chunk_pairs():
            sc = jnp.where(eye, _dot(xl_ref[0, rs, ls], ones_bd), 0.0)
            for li in range(len(blocks)):
                xl = xl_ref[li + 1, rs, ls]
                sc = jnp.where(lmask[li], _dot_nt(xl, split_heads(xl)), sc)
            finish(c, pr, rs, ls, sc)

    out_ref[0] = _gate_out_math(o_scr[...], z_ref[0], h_ref[0], gn_ref[...], wo_ref[...], gp_ref[...], True)


def _hg_core(q, fz, v, z, h, lb_logits, layer, norm_g, w_out, g_post):
    bsz, seq, width = q.shape
    blk = min(HG_BLOCK, seq)
    chunk = min(HG_CHUNK, blk)
    nlev = int(math.log2(chunk))
    assert HG_HEAD == 2 * chunk and width % (2 * HG_HEAD) == 0
    tok = pl.BlockSpec((1, blk, width), lambda b, n: (b, n, 0))
    const = lambda shp: pl.BlockSpec(shp, lambda b, n: (0, 0))
    row = lambda t: t.reshape(1, -1)
    return pl.pallas_call(
        functools.partial(_hg_core_kernel, layer=layer, chunk=chunk),
        out_shape=jax.ShapeDtypeStruct(h.shape, F32),
        grid=(bsz, seq // blk),
        in_specs=[tok, tok, tok, const(lb_logits.shape), tok, tok, const((1, width)), const(w_out.shape),
                  const((1, width))],
        out_specs=tok,
        scratch_shapes=[pltpu.VMEM((nlev + 1, blk, width), BF16),
                        pltpu.VMEM((blk, width), BF16),
                        pltpu.VMEM((blk, width), BF16),
                        pltpu.VMEM((blk // chunk, width), F32),
                        pltpu.VMEM((width // (2 * HG_HEAD), 2 * HG_HEAD, 2 * HG_HEAD), F32),
                        pltpu.VMEM((blk, width), F32)],
        compiler_params=_cparams(("parallel", "arbitrary")),
        name="hgrn2_core",
    )(q, fz, v, lb_logits, z, h, row(norm_g), w_out.astype(BF16), row(g_post))


def _hg_layer(h, g_pre, g_post, w_in, lb_logits, layer, norm_g, w_out):
    d = h.shape[-1]
    q, fz, v, z = _norm_proj(h, g_pre, w_in.astype(BF16), (d, d, d, d), (0,) * 4, (F32, F32, BF16, F32))
    return _hg_core(q, fz, v, z, h, lb_logits, layer, norm_g, w_out, g_post)


def _at_proj_kernel(h_ref, g_ref, w_ref, bias_ref, pos_ref, invf_ref, q_ref, k_ref, v_ref, z_ref,
                    *, qw, kvw):
    y = _rms(h_ref[0], g_ref[...]).astype(BF16)
    half = AT_HEAD // 2
    nq4 = LANES // half
    ang = pos_ref[0].astype(F32) * invf_ref[...]
    cos4 = jnp.cos(ang)
    sin4 = jnp.sin(ang)
    lane4 = lax.broadcasted_iota(jnp.int32, ang.shape, 1)

    def spread(t):
        parts = []
        for j in range(nq4):
            m = jnp.where(lane4 // half == j, t, 0.0)
            x = m
            for s in range(1, nq4):
                x = x + pltpu.roll(m, s * half, 1)
            parts.append(x)
        return jnp.concatenate(parts, axis=0)

    lane = lax.broadcasted_iota(jnp.int32, (h_ref.shape[1], LANES), 1)
    first = (lane % AT_HEAD) < half
    low = lane < AT_HEAD
    cosf = spread(cos4)
    sinf = spread(sin4)
    sinf = jnp.where(first, -sinf, sinf)

    def rope(t):
        partner = jnp.where(first, pltpu.roll(t, LANES - AT_HEAD // 2, 1), pltpu.roll(t, AT_HEAD // 2, 1))
        return t * cosf + partner * sinf

    def store_dup(ref, j, t):
        r = pltpu.roll(t, AT_HEAD, 1)
        ref[0, :, (2 * j) * LANES:(2 * j + 1) * LANES] = jnp.where(low, t, r).astype(ref.dtype)
        ref[0, :, (2 * j + 1) * LANES:(2 * j + 2) * LANES] = jnp.where(low, r, t).astype(ref.dtype)

    for j in range(qw // LANES):
        sl = slice(j * LANES, (j + 1) * LANES)
        q_ref[0, :, sl] = (rope(_dot(y, w_ref[:, sl]) + bias_ref[:, sl]) * AT_HEAD ** -0.5).astype(q_ref.dtype)
    for j in range(kvw // LANES):
        sl = slice(qw + j * LANES, qw + (j + 1) * LANES)
        store_dup(k_ref, j, rope(_dot(y, w_ref[:, sl]) + bias_ref[:, sl]))
    for j in range(kvw // LANES):
        sl = slice(qw + kvw + j * LANES, qw + kvw + (j + 1) * LANES)
        store_dup(v_ref, j, _dot(y, w_ref[:, sl]) + bias_ref[:, sl])
    off = qw + 2 * kvw
    z_ref[0] = _dot(y, w_ref[:, off:off + qw])


def _at_proj(h, g, w_bf16, bias, positions, qw, kvw):
    bsz, seq, d = h.shape
    tm = _row_tile(seq, 4 * d + 2 * qw + 4 * 2 * kvw + 4 * qw, w_bf16.size * 2, 4 * SUBLANES)
    half = AT_HEAD // 2
    inv_freq = ROPE_THETA ** (-jnp.arange(0, AT_HEAD, 2, dtype=F32) / AT_HEAD)
    nq4 = LANES // half
    invf = jnp.tile(inv_freq, nq4).reshape(1, LANES)
    pos4 = positions.reshape(bsz, seq // tm, nq4, tm // nq4).transpose(0, 1, 3, 2)
    pos4 = jnp.repeat(pos4, half, axis=-1).reshape(bsz, seq // nq4, LANES)
    tokspec = lambda w: pl.BlockSpec((1, tm, w), lambda b, m: (b, m, 0))
    const = lambda shp: pl.BlockSpec(shp, lambda b, m: (0, 0))
    return pl.pallas_call(
        functools.partial(_at_proj_kernel, qw=qw, kvw=kvw),
        out_shape=[jax.ShapeDtypeStruct((bsz, seq, qw), BF16), jax.ShapeDtypeStruct((bsz, seq, 2 * kvw), BF16),
                   jax.ShapeDtypeStruct((bsz, seq, 2 * kvw), BF16), jax.ShapeDtypeStruct((bsz, seq, qw), F32)],
        grid=(bsz, seq // tm),
        in_specs=[tokspec(d), const((1, d)), const(w_bf16.shape), const((1, qw + 2 * kvw)),
                  pl.BlockSpec((1, tm // nq4, LANES), lambda b, m: (b, m, 0)), const((1, LANES))],
        out_specs=[tokspec(qw), tokspec(2 * kvw), tokspec(2 * kvw), tokspec(qw)],
        compiler_params=_cparams(("parallel", "parallel")),
        name="attn_proj",
    )(h, g.reshape(1, d), w_bf16, bias.reshape(1, -1), pos4, invf)


def _at_core_kernel(q_ref, kc_ref, kp_ref, vc_ref, vp_ref, sink_ref, z_ref, h_ref, wo_ref, gp_ref, out_ref):
    blk = AT_BLOCK
    nsub = q_ref.shape[1] // blk
    n = pl.program_id(1)
    ngroup = kc_ref.shape[2] // LANES
    qi = lax.broadcasted_iota(jnp.int32, (2 * blk, 4 * blk), 0) % blk
    kj = lax.broadcasted_iota(jnp.int32, (2 * blk, 4 * blk), 1) % (2 * blk)
    dist = qi + blk - kj
    in_window = (dist >= 0) & (dist < blk)
    first_mask = in_window & ((kj >= blk) | (n > 0))
    low = lax.broadcasted_iota(jnp.int32, (2 * blk, LANES), 1) < AT_HEAD
    top = lax.broadcasted_iota(jnp.int32, (2 * blk, 1), 0) < blk
    ones_lo = jnp.where(low, 1.0, 0.0).astype(BF16)
    ones_hi = jnp.where(low, 0.0, 1.0).astype(BF16)
    items = [(j, g) for j in range(nsub) for g in range(ngroup)]

    def band(cur_ref, prev_ref, j, g):
        sl = slice(g * LANES, (g + 1) * LANES)
        prev = prev_ref[0, :, sl] if j == 0 else cur_ref[0, (j - 1) * blk:j * blk, sl]
        return jnp.concatenate([prev, cur_ref[0, j * blk:(j + 1) * blk, sl]], axis=0)

    o_rows = [[] for _ in range(nsub)]
    for w0 in range(0, len(items), AT_WAVE):
        wave = items[w0:w0 + AT_WAVE]
        scores = []
        for j, g in wave:
            kdup = band(kc_ref, kp_ref, j, g)
            rhs = jnp.concatenate([jnp.where(low, kdup, 0.0), jnp.where(low, 0.0, kdup)], axis=0)
            rows = slice(j * blk, (j + 1) * blk)
            lhs = jnp.concatenate([q_ref[0, rows, (2 * g) * LANES:(2 * g + 1) * LANES],
                                   q_ref[0, rows, (2 * g + 1) * LANES:(2 * g + 2) * LANES]], axis=0)
            scores.append(_dot_nt(lhs, rhs))

        probs, sink_terms = [], []
        for (j, g), sc in zip(wave, scores):
            s = jnp.where(first_mask if j == 0 else in_window, sc, -jnp.inf)
            sink_a = jnp.where(top, sink_ref[0:1, 4 * g:4 * g + 1], sink_ref[0:1, 4 * g + 2:4 * g + 3])
            sink_b = jnp.where(top, sink_ref[0:1, 4 * g + 1:4 * g + 2], sink_ref[0:1, 4 * g + 3:4 * g + 4])
            m_a = jnp.maximum(jnp.max(s[:, :2 * blk], axis=-1, keepdims=True), sink_a)
            m_b = jnp.maximum(jnp.max(s[:, 2 * blk:], axis=-1, keepdims=True), sink_b)
            e = jnp.concatenate([jnp.exp(s[:, :2 * blk] - m_a), jnp.exp(s[:, 2 * blk:] - m_b)], axis=1)
            probs.append(e.astype(BF16))
            sink_terms.append(jnp.where(low, jnp.exp(sink_a - m_a), jnp.exp(sink_b - m_b)))

        for (j, g), p, st in zip(wave, probs, sink_terms):
            vdup = band(vc_ref, vp_ref, j, g)
            va = jnp.where(low, vdup, 0.0)
            vb = jnp.where(low, 0.0, vdup)
            rhs = jnp.concatenate([jnp.concatenate([va, ones_lo], axis=1),
                                   jnp.concatenate([vb, ones_hi], axis=1)], axis=0)
            out = _dot(p, rhs)
            o = out[:, :LANES] / (out[:, LANES:] + st)
            o_rows[j] += [o[:blk], o[blk:]]

    o = jnp.concatenate([jnp.concatenate(slabs, axis=-1) for slabs in o_rows], axis=0)
    out_ref[0] = _gate_out_math(o, z_ref[0], h_ref[0], None, wo_ref[...], gp_ref[...], False)


def _at_core(q, k_dup, v_dup, sinks, z, h, w_out, g_post):
    bsz, seq, qw = q.shape
    kvw = k_dup.shape[-1]
    d = h.shape[-1]
    blk = AT_BLOCK
    nsub = AT_SUBBLOCKS if seq % (AT_SUBBLOCKS * blk) == 0 else 1
    nq = qw // AT_HEAD
    assert nq == 4 * (kvw // LANES) and qw == 2 * kvw
    cur = lambda w: pl.BlockSpec((1, nsub * blk, w), lambda b, n: (b, n, 0))
    prev = lambda w: pl.BlockSpec((1, blk, w), lambda b, n: (b, jnp.maximum(nsub * n - 1, 0), 0))
    const = lambda shp: pl.BlockSpec(shp, lambda b, n: (0, 0))
    return pl.pallas_call(
        _at_core_kernel,
        out_shape=jax.ShapeDtypeStruct(h.shape, F32),
        grid=(bsz, seq // (nsub * blk)),
        in_specs=[cur(qw), cur(kvw), prev(kvw), cur(kvw), prev(kvw), const((1, nq)),
                  cur(qw), cur(d), const(w_out.shape), const((1, d))],
        out_specs=cur(d),
        compiler_params=_cparams(("parallel", "arbitrary")),
        name="attn_core",
    )(q, k_dup, k_dup, v_dup, v_dup, sinks.reshape(1, nq), z, h, w_out.astype(BF16), g_post.reshape(1, d))


def _at_layer(h, positions, g_pre, g_post, w_in, b_in, sinks, w_out):
    qw = w_out.shape[0]
    kvw = (b_in.shape[0] - qw) // 2
    q, k, v, z = _at_proj(h, g_pre, w_in.astype(BF16), b_in, positions, qw, kvw)
    return _at_core(q, k, v, sinks, z, h, w_out, g_post)


def kernel(x, positions, norm_pre, norm_post, s5_w_in, s5_lambda_re, s5_lambda_im, s5_log_dt, s5_b_re, s5_b_im, s5_c_re, s5_c_im, s5_d, s5_w_glu, s5_b_glu, s5_w_out, hg_w_in, hg_lb_logits, hg_norm, hg_w_out, at_w_in, at_b_in, at_sinks, at_w_out):
    depth = norm_pre.shape[0]
    s5_weights = _s5_weights(s5_lambda_re, s5_lambda_im, s5_log_dt, s5_b_re, s5_b_im, s5_c_re, s5_c_im, s5_d)
    h = x
    for i in range(depth):
        kind, j = i % 3, i // 3
        if kind == 0:
            h = _s5_layer(h, norm_pre[i], norm_post[i], s5_w_in[j], s5_weights, j, s5_w_glu[j], s5_b_glu[j],
                          s5_w_out[j])
        elif kind == 1:
            h = _hg_layer(h, norm_pre[i], norm_post[i], hg_w_in[j], hg_lb_logits, i, hg_norm[j], hg_w_out[j])
        else:
            h = _at_layer(h, positions, norm_pre[i], norm_post[i], at_w_in[j], at_b_in[j], at_sinks[j],
                          at_w_out[j])
    return h
```

```python
import functools
import math

import jax
import jax.numpy as jnp
from jax import lax
from jax.experimental import pallas as pl
from jax.experimental.pallas import tpu as pltpu

F32 = jnp.float32
BF16 = jnp.bfloat16

NORM_EPS = 1e-6
LANES = 128
SUBLANES = 8
VMEM_LIMIT = 56 * 1024 * 1024

S5_GROUP = 16
S5_STATE = 64
S5_TILE_GROUPS = 16
S5_TILE_CH = S5_TILE_GROUPS * S5_GROUP
S5_TILE_ST = S5_TILE_GROUPS * S5_STATE
S5_STEPS = 64
HG_HEAD = 128
HG_CHUNK = 64
HG_BLOCK = 512
HG_MILD_DECAY = 80.0
AT_HEAD = 64
AT_KV_HEADS = 4
AT_BLOCK = 128
AT_SUBBLOCKS = 4
AT_WAVE = 16
ROPE_THETA = 10000.0

ROW_TILES = (1024, 512, 256)


def _cparams(sem):
    return pltpu.CompilerParams(dimension_semantics=sem, vmem_limit_bytes=VMEM_LIMIT)


def _row_tile(seq, row_bytes, resident_bytes, multiple=SUBLANES):
    budget = VMEM_LIMIT - VMEM_LIMIT // 4
    for tm in ROW_TILES:
        if seq % tm == 0 and tm % multiple == 0 and 2 * (tm * row_bytes + resident_bytes) <= budget:
            return tm
    return seq


def _rms(x, g):
    return x * lax.rsqrt(jnp.mean(x * x, axis=-1, keepdims=True) + NORM_EPS) * g


def _sigmoid(x):
    return 1.0 / (1.0 + jnp.exp(-x))


def _sigmoid_gate(x):
    return 0.5 + 0.5 * jnp.tanh(0.5 * x)


def _silu(x):
    return x * _sigmoid_gate(x)


def _gelu_tanh(x):
    c = math.sqrt(2.0 / math.pi)
    return 0.5 * x * (1.0 + jnp.tanh(c * (x + 0.044715 * (x * x * x))))


def _dot(a, b):
    return jnp.dot(a, b, preferred_element_type=F32)


def _dot_nt(a, b):
    return lax.dot_general(a, b, (((1,), (1,)), ((), ())), preferred_element_type=F32)


def _dot_tn(a, b):
    return lax.dot_general(a, b, (((0,), (0,)), ((), ())), preferred_element_type=F32)


def _chunk_row_groups(rows, steps):
    for base in range(0, rows, SUBLANES * steps):
        for r in range(SUBLANES):
            for k0 in range(0, steps, SUBLANES):
                yield base + r * steps + k0, base + k0 * SUBLANES + r


def _norm_proj_kernel(h_ref, g_ref, w_ref, *out_refs, widths, slab_steps):
    y = _rms(h_ref[0], g_ref[...]).astype(BF16)
    off = 0
    for o_ref, width, steps in zip(out_refs, widths, slab_steps):
        r = _dot(y, w_ref[:, off:off + width])
        if steps:
            for nat, cm in _chunk_row_groups(r.shape[0], steps):
                for j in range(width // LANES):
                    o_ref[0, j, pl.ds(cm, SUBLANES, stride=SUBLANES), :] = (
                        r[nat:nat + SUBLANES, j * LANES:(j + 1) * LANES])
        else:
            o_ref[0] = r.astype(o_ref.dtype)
        off += width


def _norm_proj(h, g, w_bf16, widths, slab_steps, dtypes):
    bsz, seq, d = h.shape
    row_bytes = 4 * d + sum(w * jnp.dtype(t).itemsize for w, t in zip(widths, dtypes))
    tm = _row_tile(seq, row_bytes, w_bf16.size * 2, SUBLANES * max(max(slab_steps), 1))
    out_shape, out_specs = [], []
    for width, as_slab, dtype in zip(widths, slab_steps, dtypes):
        if as_slab:
            assert tm % (SUBLANES * as_slab) == 0 and dtype == F32
            ns = width // LANES
            out_shape.append(jax.ShapeDtypeStruct((bsz, ns, seq, LANES), F32))
            out_specs.append(pl.BlockSpec((1, ns, tm, LANES), lambda b, m: (b, 0, m, 0)))
        else:
            out_shape.append(jax.ShapeDtypeStruct((bsz, seq, width), dtype))
            out_specs.append(pl.BlockSpec((1, tm, width), lambda b, m: (b, m, 0)))
    return pl.pallas_call(
        functools.partial(_norm_proj_kernel, widths=tuple(widths), slab_steps=tuple(slab_steps)),
        out_shape=out_shape,
        grid=(bsz, seq // tm),
        in_specs=[pl.BlockSpec((1, tm, d), lambda b, m: (b, m, 0)),
                  pl.BlockSpec((1, d), lambda b, m: (0, 0)),
                  pl.BlockSpec(w_bf16.shape, lambda b, m: (0, 0))],
        out_specs=out_specs,
        compiler_params=_cparams(("parallel", "parallel")),
        name="norm_proj",
    )(h, g.reshape(1, d), w_bf16)


def _zoh(lr, li, ldt):
    dt = jnp.exp(ldt)
    mag = jnp.exp(lr * dt)
    ar = mag * jnp.cos(li * dt)
    ai = mag * jnp.sin(li * dt)
    den = lr * lr + li * li
    qr = ((ar - 1.0) * lr + ai * li) / den
    qi = (ai * lr - (ar - 1.0) * li) / den
    return ar, ai, qr, qi


def _s5_tiles_kernel(lr_ref, li_ref, ldt_ref, lrc_ref, lic_ref, ldtc_ref, br_ref, bi_ref, cr_ref, ci_ref,
                     ar_ref, ai_ref, wb_ref, wc_ref):
    ar, ai, _, _ = _zoh(lr_ref[...], li_ref[...], ldt_ref[...])
    ar_ref[...] = ar
    ai_ref[...] = ai
    _, _, qr, qi = _zoh(lrc_ref[...], lic_ref[...], ldtc_ref[...])
    br = br_ref[...]
    bi = bi_ref[...]
    bbr = (qr * br - qi * bi).astype(BF16)
    bbi = (qr * bi + qi * br).astype(BF16)

    ch, st = S5_TILE_CH, S5_TILE_ST
    sel_h = jnp.where(lax.broadcasted_iota(jnp.int32, (ch, S5_GROUP), 0) % S5_GROUP
                      == lax.broadcasted_iota(jnp.int32, (ch, S5_GROUP), 1), 1.0, 0.0).astype(BF16)
    diag_in = (lax.broadcasted_iota(jnp.int32, (ch, st), 0) // S5_GROUP
               == lax.broadcasted_iota(jnp.int32, (ch, st), 1) // S5_STATE)
    wb_ref[0, :, 0:st] = jnp.where(diag_in, _dot_nt(sel_h, bbr), 0.0).astype(BF16)
    wb_ref[0, :, st:2 * st] = jnp.where(diag_in, _dot_nt(sel_h, bbi), 0.0).astype(BF16)
    sel_p = jnp.where(lax.broadcasted_iota(jnp.int32, (st, S5_STATE), 0) % S5_STATE
                      == lax.broadcasted_iota(jnp.int32, (st, S5_STATE), 1), 1.0, 0.0).astype(BF16)
    diag_out = (lax.broadcasted_iota(jnp.int32, (st, ch), 0) // S5_STATE
                == lax.broadcasted_iota(jnp.int32, (st, ch), 1) // S5_GROUP)
    wc_ref[0, 0:st, :] = jnp.where(diag_out, _dot_nt(sel_p, cr_ref[...].astype(BF16)), 0.0).astype(BF16)
    wc_ref[0, st:2 * st, :] = jnp.where(diag_out, -_dot_nt(sel_p, ci_ref[...].astype(BF16)), 0.0).astype(BF16)


def _s5_core_kernel(u_ref, wb_ref, wc_ref, a_ref, d_ref, y_ref, x_ref, s_ref, carry_ref, *, steps):
    nst = S5_TILE_ST
    nslab = S5_TILE_CH // LANES
    ntile = wb_ref.shape[0]

    @pl.when(pl.program_id(1) == 0)
    def _():
        carry_ref[...] = jnp.zeros_like(carry_ref)

    def load_u(i):
        return jnp.concatenate([u_ref[0, i * nslab + j] for j in range(nslab)], axis=-1)

    def project_in(i):
        x_ref[i % 2] = _dot(load_u(i).astype(BF16), wb_ref[i])

    def scan(i):
        slot = i % 2
        ar1 = a_ref[i, 0:1, :]
        ai1 = a_ref[i, 1:2, :]
        ar = jnp.broadcast_to(ar1, (SUBLANES, nst))
        ai = jnp.broadcast_to(ai1, (SUBLANES, nst))

        def advance(k, sr, si):
            xr = x_ref[slot, k * SUBLANES:(k + 1) * SUBLANES, 0:nst]
            xi = x_ref[slot, k * SUBLANES:(k + 1) * SUBLANES, nst:2 * nst]
            return ar * sr - ai * si + xr, ar * si + ai * sr + xi

        er = jnp.zeros((SUBLANES, nst), F32)
        ei = jnp.zeros((SUBLANES, nst), F32)
        for k in range(steps):
            er, ei = advance(k, er, ei)

        pr, pi = ar1, ai1
        for _ in range(int(math.log2(steps))):
            pr, pi = pr * pr - pi * pi, 2.0 * pr * pi

        rid = lax.broadcasted_iota(jnp.int32, (SUBLANES, nst), 0)
        cr = carry_ref[i, 0:1, :]
        ci = carry_ref[i, 1:2, :]
        sr = jnp.zeros((SUBLANES, nst), F32)
        si = jnp.zeros((SUBLANES, nst), F32)
        for r in range(SUBLANES):
            sr = jnp.where(rid == r, jnp.broadcast_to(cr, (SUBLANES, nst)), sr)
            si = jnp.where(rid == r, jnp.broadcast_to(ci, (SUBLANES, nst)), si)
            cr, ci = pr * cr - pi * ci + er[r:r + 1, :], pr * ci + pi * cr + ei[r:r + 1, :]
        carry_ref[i, 0:1, :] = cr
        carry_ref[i, 1:2, :] = ci

        for k in range(0, steps, 2):
            r0, i0 = advance(k, sr, si)
            sr, si = advance(k + 1, r0, i0)
            rows2 = slice(k * SUBLANES, (k + 2) * SUBLANES)
            s_ref[slot, rows2, 0:nst] = jnp.concatenate([r0, sr], axis=0).astype(BF16)
            s_ref[slot, rows2, nst:2 * nst] = jnp.concatenate([i0, si], axis=0).astype(BF16)

    def project_out(i):
        y = _dot(s_ref[i % 2], wc_ref[i]) + d_ref[i] * load_u(i)
        for j in range(nslab):
            y_ref[0, i * nslab + j] = y[:, j * LANES:(j + 1) * LANES]

    project_in(0)
    for i in range(ntile):
        if i + 1 < ntile:
            project_in(i + 1)
        scan(i)
        project_out(i)


def _s5_core(u_slab, wb, wc, a_tiles, d_tiles, layer, steps):
    bsz, nslab, seq, _ = u_slab.shape
    ntile = nslab * LANES // S5_TILE_CH
    rows = SUBLANES * steps
    const = lambda shp: pl.BlockSpec((ntile,) + shp[1:], lambda b, n: (layer,) + (0,) * (len(shp) - 1))
    tok = pl.BlockSpec((1, nslab, rows, LANES), lambda b, n: (b, 0, n, 0))
    return pl.pallas_call(
        functools.partial(_s5_core_kernel, steps=steps),
        out_shape=jax.ShapeDtypeStruct(u_slab.shape, F32),
        grid=(bsz, seq // rows),
        in_specs=[tok, const(wb.shape), const(wc.shape), const(a_tiles.shape), const(d_tiles.shape)],
        out_specs=tok,
        scratch_shapes=[pltpu.VMEM((2, rows, 2 * S5_TILE_ST), F32),
                        pltpu.VMEM((2, rows, 2 * S5_TILE_ST), BF16),
                        pltpu.VMEM((ntile, 2, S5_TILE_ST), F32)],
        compiler_params=_cparams(("parallel", "arbitrary")),
        name="s5_core",
    )(u_slab, wb, wc, a_tiles, d_tiles)


def _s5_post_kernel(y_ref, z_ref, h_ref, wg_ref, bg_ref, wo_ref, gp_ref, o_ref, *, steps):
    nslab = y_ref.shape[1]
    groups = list(_chunk_row_groups(y_ref.shape[2], steps))
    y = jnp.concatenate(
        [jnp.concatenate([y_ref[0, j, pl.ds(cm, SUBLANES, stride=SUBLANES), :] for _, cm in groups], axis=0)
         for j in range(nslab)], axis=-1)
    y = _gelu_tanh(y)
    y = y * _sigmoid_gate(_dot(y.astype(BF16), wg_ref[...]) + bg_ref[...])
    t = y * _silu(z_ref[0])
    o = _dot(t.astype(BF16), wo_ref[...])
    o_ref[0] = h_ref[0] + _rms(o, gp_ref[...])


def _s5_post(y_slab, z, h, w_glu, b_glu, w_out, g_post, steps):
    bsz, seq, d = h.shape
    nslab = y_slab.shape[1]
    tm = _row_tile(seq, 4 * 4 * d, 2 * 2 * d * d, SUBLANES * steps)
    assert tm % (SUBLANES * steps) == 0
    row = lambda t: t.reshape(1, -1)
    vec = pl.BlockSpec((1, d), lambda b, m: (0, 0))
    mat = pl.BlockSpec((d, d), lambda b, m: (0, 0))
    slab = pl.BlockSpec((1, nslab, tm, LANES), lambda b, m: (b, 0, m, 0))
    tok = pl.BlockSpec((1, tm, d), lambda b, m: (b, m, 0))
    return pl.pallas_call(
        functools.partial(_s5_post_kernel, steps=steps),
        out_shape=jax.ShapeDtypeStruct(h.shape, F32),
        grid=(bsz, seq // tm),
        in_specs=[slab, tok, tok, mat, vec, mat, vec],
        out_specs=tok,
        compiler_params=_cparams(("parallel", "parallel")),
        name="s5_post",
    )(y_slab, z, h, w_glu.astype(BF16), row(b_glu), w_out.astype(BF16), row(g_post))


def _s5_weights(lam_re, lam_im, log_dt, b_re, b_im, c_re, c_im, d_skip):
    nlayer, ngroup, nstate, gsize = b_re.shape
    ng = nlayer * ngroup
    ntile = ng // S5_TILE_GROUPS
    tg = S5_TILE_GROUPS
    col = lambda t: t.reshape(ng * nstate, 1)
    ldt = log_dt.reshape(ng, 1)
    grp = lambda w: pl.BlockSpec((tg, w), lambda i: (i, 0))
    rows = lambda r, w: pl.BlockSpec((r, w), lambda i: (i, 0))
    ar, ai, wb, wc = pl.pallas_call(
        _s5_tiles_kernel,
        out_shape=[jax.ShapeDtypeStruct((ng, nstate), F32), jax.ShapeDtypeStruct((ng, nstate), F32),
                   jax.ShapeDtypeStruct((ntile, S5_TILE_CH, 2 * S5_TILE_ST), BF16),
                   jax.ShapeDtypeStruct((ntile, 2 * S5_TILE_ST, S5_TILE_CH), BF16)],
        grid=(ntile,),
        in_specs=[grp(nstate), grp(nstate), grp(1),
                  rows(tg * nstate, 1), rows(tg * nstate, 1), rows(tg * nstate, 1),
                  rows(tg * nstate, gsize), rows(tg * nstate, gsize),
                  rows(tg * gsize, nstate), rows(tg * gsize, nstate)],
        out_specs=[grp(nstate), grp(nstate),
                   pl.BlockSpec((1, S5_TILE_CH, 2 * S5_TILE_ST), lambda i: (i, 0, 0)),
                   pl.BlockSpec((1, 2 * S5_TILE_ST, S5_TILE_CH), lambda i: (i, 0, 0))],
        compiler_params=_cparams(("parallel",)),
        name="s5_tiles",
    )(lam_re.reshape(ng, nstate), lam_im.reshape(ng, nstate), ldt,
      col(lam_re), col(lam_im), col(jnp.broadcast_to(ldt, (ng, nstate))),
      b_re.reshape(ng * nstate, gsize), b_im.reshape(ng * nstate, gsize),
      c_re.reshape(ng * gsize, nstate), c_im.reshape(ng * gsize, nstate))
    a_tiles = jnp.stack([ar.reshape(ntile, S5_TILE_ST), ai.reshape(ntile, S5_TILE_ST)], axis=1)
    d_tiles = d_skip.reshape(ntile, 1, S5_TILE_CH)
    return wb, wc, a_tiles, d_tiles


def _s5_layer(h, g_pre, g_post, w_in, weights, layer, w_glu, b_glu, w_out):
    bsz, seq, d = h.shape
    steps = min(S5_STEPS, seq // SUBLANES)
    u_slab, z = _norm_proj(h, g_pre, w_in.astype(BF16), (d, d), (steps, 0), (F32, F32))
    y_slab = _s5_core(u_slab, *weights, layer, steps)
    return _s5_post(y_slab, z, h, w_glu, b_glu, w_out, g_post, steps)


def _gate_out_math(o, z, h, g_head, w_out, g_post, head_norm):
    if head_norm:
        parts = []
        for j in range(o.shape[-1] // HG_HEAD):
            oj = o[:, j * HG_HEAD:(j + 1) * HG_HEAD]
            parts.append(oj * lax.rsqrt(jnp.mean(oj * oj, axis=-1, keepdims=True) + NORM_EPS))
        o = jnp.concatenate(parts, axis=-1) * g_head
    t = o * _silu(z)
    return h + _rms(_dot(t.astype(BF16), w_out), g_post)


def _chunk_cumsum(x, chunk):
    rows, w = x.shape
    per = chunk // SUBLANES
    x3 = x.reshape(rows // SUBLANES, SUBLANES, w)
    pos = lax.broadcasted_iota(jnp.int32, x3.shape, 1)
    sh = 1
    while sh < SUBLANES:
        x3 = x3 + jnp.where(pos >= sh, pltpu.roll(x3, sh, 1), 0.0)
        sh *= 2
    x4 = x3.reshape(rows // chunk, per, SUBLANES, w)
    tot = x4[:, :, SUBLANES - 1:SUBLANES, :]
    pref = [jnp.zeros_like(tot[:, 0:1])]
    for j in range(1, per):
        pref.append(pref[-1] + tot[:, j - 1:j])
    return (x4 + jnp.concatenate(pref, axis=1)).reshape(rows, w)


def _block_mid_rows(x, blk):
    rows, w = x.shape
    half = blk // 2
    if blk >= SUBLANES:
        x3 = x.reshape(rows // blk, blk, w)
        return jnp.broadcast_to(x3[:, half - 1:half, :], (rows // blk, blk, w)).reshape(rows, w)
    pos = lax.broadcasted_iota(jnp.int32, (rows, w), 0) % blk
    out = x
    for p in range(blk):
        d = p - (half - 1)
        if d != 0:
            out = jnp.where(pos == p, pltpu.roll(x, d % rows, 0), out)
    return out


def _hg_proj_kernel(h_ref, g_ref, w_ref, lbl_ref, q_ref, kk_ref, b_ref, v_ref, z_ref, dec_ref, *, layer, chunk):
    width = q_ref.shape[2]
    rows = q_ref.shape[1]
    y = _rms(h_ref[0], g_ref[...]).astype(BF16)
    q_ref[0] = _dot(y, w_ref[:, 0:width])
    fz = _dot(y, w_ref[:, width:2 * width])
    v_ref[0] = _dot(y, w_ref[:, 2 * width:3 * width]).astype(v_ref.dtype)
    z_ref[0] = _dot(y, w_ref[:, 3 * width:4 * width])

    lg = lbl_ref[...]
    ex = jnp.exp(lg - jnp.max(lg, axis=0, keepdims=True))
    p = ex / jnp.sum(ex, axis=0, keepdims=True)
    lb = jnp.zeros((1, width), F32)
    for j in range(1, layer + 1):
        lb = lb + p[j:j + 1, :]

    f = lb + (1.0 - lb) * _sigmoid(fz)
    kk_ref[0] = 1.0 - f
    b = _chunk_cumsum(jnp.log(f), chunk)
    b_ref[0] = b
    dec_ref[0] = -b.reshape(rows // chunk, chunk, width)[:, chunk - 1, :]


def _hg_proj(h, g, w_bf16, lb_logits, layer, chunk):
    bsz, seq, d = h.shape
    width = w_bf16.shape[1] // 4
    tm = _row_tile(seq, 4 * d + (4 * 4 + 2) * width, w_bf16.size * 2, SUBLANES * chunk)
    tok = pl.BlockSpec((1, tm, width), lambda b, m: (b, m, 0))
    const = lambda shp: pl.BlockSpec(shp, lambda b, m: (0, 0))
    act = lambda t: jax.ShapeDtypeStruct((bsz, seq, width), t)
    return pl.pallas_call(
        functools.partial(_hg_proj_kernel, layer=layer, chunk=chunk),
        out_shape=[act(F32), act(F32), act(F32), act(BF16), act(F32),
                   jax.ShapeDtypeStruct((bsz, seq // chunk, width), F32)],
        grid=(bsz, seq // tm),
        in_specs=[pl.BlockSpec((1, tm, d), lambda b, m: (b, m, 0)), const((1, d)), const(w_bf16.shape),
                  const(lb_logits.shape)],
        out_specs=[tok, tok, tok, tok, tok, pl.BlockSpec((1, tm // chunk, width), lambda b, m: (b, m, 0))],
        compiler_params=_cparams(("parallel", "parallel")),
        name="hgrn2_proj",
    )(h, g.reshape(1, d), w_bf16, lb_logits)


def _hg_core_kernel(q_ref, kk_ref, b_ref, v_ref, dec_ref, z_ref, h_ref, gn_ref, wo_ref, gp_ref, out_ref,
                    xl_ref, qe_ref, kd_ref, g_ref, st_ref, o_scr, *, chunk):
    rows, width = q_ref.shape[1], q_ref.shape[2]
    pair = 2 * HG_HEAD
    blocks = [2 ** i for i in range(1, int(math.log2(chunk)) + 1)]

    @pl.when(pl.program_id(1) == 0)
    def _():
        st_ref[...] = jnp.zeros_like(st_ref)

    def state_operands():
        q = q_ref[0]
        kk = kk_ref[0]
        b = b_ref[0]
        b3 = b.reshape(rows // chunk, chunk, width)
        b_last = b3[:, chunk - 1:chunk, :]
        g_ref[...] = jnp.exp(b_last.reshape(rows // chunk, width))
        qe_ref[...] = (q * jnp.exp(b)).astype(BF16)
        kd_ref[...] = (kk * jnp.exp(jnp.broadcast_to(b_last, b3.shape).reshape(rows, width) - b)).astype(BF16)
        return q, kk, b

    t_i = lax.broadcasted_iota(jnp.int32, (chunk, HG_HEAD), 0)
    s_i = lax.broadcasted_iota(jnp.int32, (chunk, HG_HEAD), 1) % chunk
    lo = lax.broadcasted_iota(jnp.int32, (chunk, pair), 1) < HG_HEAD
    bd = (lax.broadcasted_iota(jnp.int32, (pair, pair), 0) // HG_HEAD
          == lax.broadcasted_iota(jnp.int32, (pair, pair), 1) // HG_HEAD)

    def split_heads(t):
        z = jnp.zeros_like(t)
        return jnp.concatenate([jnp.where(lo, t, z), jnp.where(lo, z, t)], axis=0)

    def chunk_pairs():
        for c in range(rows // chunk):
            for pr in range(width // pair):
                yield c, pr, slice(c * chunk, (c + 1) * chunk), slice(pr * pair, (pr + 1) * pair)

    def finish(c, pr, rs, ls, sc):
        vpair = v_ref[0, rs, ls]
        st = st_ref[pr]
        o_scr[rs, ls] = _dot(sc.astype(BF16), split_heads(vpair)) + _dot_nt(qe_ref[rs, ls], st.astype(BF16))
        st_ref[pr] = g_ref[c:c + 1, ls] * st + jnp.where(bd, _dot_tn(vpair, kd_ref[rs, ls]), 0.0)

    mild = jnp.max(dec_ref[0]) < HG_MILD_DECAY

    @pl.when(mild)
    def _():
        _, kk, b = state_operands()
        xl_ref[0] = (kk * jnp.exp(-b)).astype(BF16)
        causal = s_i <= t_i
        for c, pr, rs, ls in chunk_pairs():
            sc = jnp.where(causal, _dot_nt(qe_ref[rs, ls], split_heads(xl_ref[0, rs, ls])), 0.0)
            finish(c, pr, rs, ls, sc)

    @pl.when(jnp.logical_not(mild))
    def _():
        q, kk, b = state_operands()
        xl_ref[0] = (q * kk).astype(BF16)
        pos = lax.broadcasted_iota(jnp.int32, (rows, width), 0)
        for li, blk in enumerate(blocks):
            w = jnp.exp(-jnp.abs(b - _block_mid_rows(b, blk)))
            xl_ref[li + 1] = (jnp.where(pos % blk >= blk // 2, q, kk) * w).astype(BF16)
        eye = t_i == s_i
        lmask = [((t_i // blk) == (s_i // blk)) & ((t_i % blk) >= blk // 2) & ((s_i % blk) < blk // 2)
                 for blk in blocks]
        ones_bd = jnp.where(lax.broadcasted_iota(jnp.int32, (pair, HG_HEAD), 0) // HG_HEAD
                            == lax.broadcasted_iota(jnp.int32, (pair, HG_HEAD), 1) // chunk, 1.0, 0.0).astype(BF16)
        for c, pr, rs, ls in chunk_pairs():
            sc = jnp.where(eye, _dot(xl_ref[0, rs, ls], ones_bd), 0.0)
            for li in range(len(blocks)):
                xl = xl_ref[li + 1, rs, ls]
                sc = jnp.where(lmask[li], _dot_nt(xl, split_heads(xl)), sc)
            finish(c, pr, rs, ls, sc)

    out_ref[0] = _gate_out_math(o_scr[...], z_ref[0], h_ref[0], gn_ref[...], wo_ref[...], gp_ref[...], True)


def _hg_core(q, kk, b, v, dec, z, h, norm_g, w_out, g_post, chunk):
    bsz, seq, width = q.shape
    blk = min(HG_BLOCK, seq)
    nlev = int(math.log2(chunk))
    assert HG_HEAD == 2 * chunk and width % (2 * HG_HEAD) == 0 and blk % chunk == 0
    tok = pl.BlockSpec((1, blk, width), lambda b_, n: (b_, n, 0))
    const = lambda shp: pl.BlockSpec(shp, lambda b_, n: (0, 0))
    row = lambda t: t.reshape(1, -1)
    return pl.pallas_call(
        functools.partial(_hg_core_kernel, chunk=chunk),
        out_shape=jax.ShapeDtypeStruct(h.shape, F32),
        grid=(bsz, seq // blk),
        in_specs=[tok, tok, tok, tok, pl.BlockSpec((1, blk // chunk, width), lambda b_, n: (b_, n, 0)),
                  tok, tok, const((1, width)), const(w_out.shape), const((1, width))],
        out_specs=tok,
        scratch_shapes=[pltpu.VMEM((nlev + 1, blk, width), BF16),
                        pltpu.VMEM((blk, width), BF16),
                        pltpu.VMEM((blk, width), BF16),
                        pltpu.VMEM((blk // chunk, width), F32),
                        pltpu.VMEM((width // (2 * HG_HEAD), 2 * HG_HEAD, 2 * HG_HEAD), F32),
                        pltpu.VMEM((blk, width), F32)],
        compiler_params=_cparams(("parallel", "arbitrary")),
        name="hgrn2_core",
    )(q, kk, b, v, dec, z, h, row(norm_g), w_out.astype(BF16), row(g_post))


def _hg_layer(h, g_pre, g_post, w_in, lb_logits, layer, norm_g, w_out):
    chunk = min(HG_CHUNK, h.shape[1])
    q, kk, b, v, z, dec = _hg_proj(h, g_pre, w_in.astype(BF16), lb_logits, layer, chunk)
    return _hg_core(q, kk, b, v, dec, z, h, norm_g, w_out, g_post, chunk)


def _at_proj_kernel(h_ref, g_ref, w_ref, bias_ref, pos_ref, invf_ref, q_ref, k_ref, v_ref, z_ref,
                    *, qw, kvw):
    y = _rms(h_ref[0], g_ref[...]).astype(BF16)
    half = AT_HEAD // 2
    nq4 = LANES // half
    ang = pos_ref[0].astype(F32) * invf_ref[...]
    cos4 = jnp.cos(ang)
    sin4 = jnp.sin(ang)
    lane4 = lax.broadcasted_iota(jnp.int32, ang.shape, 1)

    def spread(t):
        parts = []
        for j in range(nq4):
            m = jnp.where(lane4 // half == j, t, 0.0)
            x = m
            for s in range(1, nq4):
                x = x + pltpu.roll(m, s * half, 1)
            parts.append(x)
        return jnp.concatenate(parts, axis=0)

    lane = lax.broadcasted_iota(jnp.int32, (h_ref.shape[1], LANES), 1)
    first = (lane % AT_HEAD) < half
    low = lane < AT_HEAD
    cosf = spread(cos4)
    sinf = spread(sin4)
    sinf = jnp.where(first, -sinf, sinf)

    def rope(t):
        partner = jnp.where(first, pltpu.roll(t, LANES - AT_HEAD // 2, 1), pltpu.roll(t, AT_HEAD // 2, 1))
        return t * cosf + partner * sinf

    def store_dup(ref, j, t):
        r = pltpu.roll(t, AT_HEAD, 1)
        ref[0, :, (2 * j) * LANES:(2 * j + 1) * LANES] = jnp.where(low, t, r).astype(ref.dtype)
        ref[0, :, (2 * j + 1) * LANES:(2 * j + 2) * LANES] = jnp.where(low, r, t).astype(ref.dtype)

    for j in range(qw // LANES):
        sl = slice(j * LANES, (j + 1) * LANES)
        q_ref[0, :, sl] = (rope(_dot(y, w_ref[:, sl]) + bias_ref[:, sl]) * AT_HEAD ** -0.5).astype(q_ref.dtype)
    for j in range(kvw // LANES):
        sl = slice(qw + j * LANES, qw + (j + 1) * LANES)
        store_dup(k_ref, j, rope(_dot(y, w_ref[:, sl]) + bias_ref[:, sl]))
    for j in range(kvw // LANES):
        sl = slice(qw + kvw + j * LANES, qw + kvw + (j + 1) * LANES)
        store_dup(v_ref, j, _dot(y, w_ref[:, sl]) + bias_ref[:, sl])
    off = qw + 2 * kvw
    z_ref[0] = _dot(y, w_ref[:, off:off + qw])


def _at_proj(h, g, w_bf16, bias, positions, qw, kvw):
    bsz, seq, d = h.shape
    tm = _row_tile(seq, 4 * d + 2 * qw + 4 * 2 * kvw + 4 * qw, w_bf16.size * 2, 4 * SUBLANES)
    half = AT_HEAD // 2
    inv_freq = ROPE_THETA ** (-jnp.arange(0, AT_HEAD, 2, dtype=F32) / AT_HEAD)
    nq4 = LANES // half
    invf = jnp.tile(inv_freq, nq4).reshape(1, LANES)
    pos4 = positions.reshape(bsz, seq // tm, nq4, tm // nq4).transpose(0, 1, 3, 2)
    pos4 = jnp.repeat(pos4, half, axis=-1).reshape(bsz, seq // nq4, LANES)
    tokspec = lambda w: pl.BlockSpec((1, tm, w), lambda b, m: (b, m, 0))
    const = lambda shp: pl.BlockSpec(shp, lambda b, m: (0, 0))
    return pl.pallas_call(
        functools.partial(_at_proj_kernel, qw=qw, kvw=kvw),
        out_shape=[jax.ShapeDtypeStruct((bsz, seq, qw), BF16), jax.ShapeDtypeStruct((bsz, seq, 2 * kvw), BF16),
                   jax.ShapeDtypeStruct((bsz, seq, 2 * kvw), BF16), jax.ShapeDtypeStruct((bsz, seq, qw), F32)],
        grid=(bsz, seq // tm),
        in_specs=[tokspec(d), const((1, d)), const(w_bf16.shape), const((1, qw + 2 * kvw)),
                  pl.BlockSpec((1, tm // nq4, LANES), lambda b, m: (b, m, 0)), const((1, LANES))],
        out_specs=[tokspec(qw), tokspec(2 * kvw), tokspec(2 * kvw), tokspec(qw)],
        compiler_params=_cparams(("parallel", "parallel")),
        name="attn_proj",
    )(h, g.reshape(1, d), w_bf16, bias.reshape(1, -1), pos4, invf)


def _at_core_kernel(q_ref, kc_ref, kp_ref, vc_ref, vp_ref, sink_ref, z_ref, h_ref, wo_ref, gp_ref, out_ref):
    blk = AT_BLOCK
    nsub = q_ref.shape[1] // blk
    n = pl.program_id(1)
    ngroup = kc_ref.shape[2] // LANES
    qi = lax.broadcasted_iota(jnp.int32, (2 * blk, 4 * blk), 0) % blk
    kj = lax.broadcasted_iota(jnp.int32, (2 * blk, 4 * blk), 1) % (2 * blk)
    dist = qi + blk - kj
    in_window = (dist >= 0) & (dist < blk)
    first_mask = in_window & ((kj >= blk) | (n > 0))
    low = lax.broadcasted_iota(jnp.int32, (2 * blk, LANES), 1) < AT_HEAD
    top = lax.broadcasted_iota(jnp.int32, (2 * blk, 1), 0) < blk
    ones_lo = jnp.where(low, 1.0, 0.0).astype(BF16)
    ones_hi = jnp.where(low, 0.0, 1.0).astype(BF16)
    items = [(j, g) for j in range(nsub) for g in range(ngroup)]

    def band(cur_ref, prev_ref, j, g):
        sl = slice(g * LANES, (g + 1) * LANES)
        prev = prev_ref[0, :, sl] if j == 0 else cur_ref[0, (j - 1) * blk:j * blk, sl]
        return jnp.concatenate([prev, cur_ref[0, j * blk:(j + 1) * blk, sl]], axis=0)

    o_rows = [[] for _ in range(nsub)]
    for w0 in range(0, len(items), AT_WAVE):
        wave = items[w0:w0 + AT_WAVE]
        scores = []
        for j, g in wave:
            kdup = band(kc_ref, kp_ref, j, g)
            rhs = jnp.concatenate([jnp.where(low, kdup, 0.0), jnp.where(low, 0.0, kdup)], axis=0)
            rows = slice(j * blk, (j + 1) * blk)
            lhs = jnp.concatenate([q_ref[0, rows, (2 * g) * LANES:(2 * g + 1) * LANES],
                                   q_ref[0, rows, (2 * g + 1) * LANES:(2 * g + 2) * LANES]], axis=0)
            scores.append(_dot_nt(lhs, rhs))

        probs, sink_terms = [], []
        for (j, g), sc in zip(wave, scores):
            s = jnp.where(first_mask if j == 0 else in_window, sc, -jnp.inf)
            sink_a = jnp.where(top, sink_ref[0:1, 4 * g:4 * g + 1], sink_ref[0:1, 4 * g + 2:4 * g + 3])
            sink_b = jnp.where(top, sink_ref[0:1, 4 * g + 1:4 * g + 2], sink_ref[0:1, 4 * g + 3:4 * g + 4])
            m_a = jnp.maximum(jnp.max(s[:, :2 * blk], axis=-1, keepdims=True), sink_a)
            m_b = jnp.maximum(jnp.max(s[:, 2 * blk:], axis=-1, keepdims=True), sink_b)
            e = jnp.concatenate([jnp.exp(s[:, :2 * blk] - m_a), jnp.exp(s[:, 2 * blk:] - m_b)], axis=1)
            probs.append(e.astype(BF16))
            sink_terms.append(jnp.where(low, jnp.exp(sink_a - m_a), jnp.exp(sink_b - m_b)))

        for (j, g), p, st in zip(wave, probs, sink_terms):
            vdup = band(vc_ref, vp_ref, j, g)
            va = jnp.where(low, vdup, 0.0)
            vb = jnp.where(low, 0.0, vdup)
            rhs = jnp.concatenate([jnp.concatenate([va, ones_lo], axis=1),
                                   jnp.concatenate([vb, ones_hi], axis=1)], axis=0)
            out = _dot(p, rhs)
            o = out[:, :LANES] / (out[:, LANES:] + st)
            o_rows[j] += [o[:blk], o[blk:]]

    o = jnp.concatenate([jnp.concatenate(slabs, axis=-1) for slabs in o_rows], axis=0)
    out_ref[0] = _gate_out_math(o, z_ref[0], h_ref[0], None, wo_ref[...], gp_ref[...], False)


def _at_core(q, k_dup, v_dup, sinks, z, h, w_out, g_post):
    bsz, seq, qw = q.shape
    kvw = k_dup.shape[-1]
    d = h.shape[-1]
    blk = AT_BLOCK
    nsub = AT_SUBBLOCKS if seq % (AT_SUBBLOCKS * blk) == 0 else 1
    nq = qw // AT_HEAD
    assert nq == 4 * (kvw // LANES) and qw == 2 * kvw
    cur = lambda w: pl.BlockSpec((1, nsub * blk, w), lambda b, n: (b, n, 0))
    prev = lambda w: pl.BlockSpec((1, blk, w), lambda b, n: (b, jnp.maximum(nsub * n - 1, 0), 0))
    const = lambda shp: pl.BlockSpec(shp, lambda b, n: (0, 0))
    return pl.pallas_call(
        _at_core_kernel,
        out_shape=jax.ShapeDtypeStruct(h.shape, F32),
        grid=(bsz, seq // (nsub * blk)),
        in_specs=[cur(qw), cur(kvw), prev(kvw), cur(kvw), prev(kvw), const((1, nq)),
                  cur(qw), cur(d), const(w_out.shape), const((1, d))],
        out_specs=cur(d),
        compiler_params=_cparams(("parallel", "arbitrary")),
        name="attn_core",
    )(q, k_dup, k_dup, v_dup, v_dup, sinks.reshape(1, nq), z, h, w_out.astype(BF16), g_post.reshape(1, d))


def _at_layer(h, positions, g_pre, g_post, w_in, b_in, sinks, w_out):
    qw = w_out.shape[0]
    kvw = (b_in.shape[0] - qw) // 2
    q, k, v, z = _at_proj(h, g_pre, w_in.astype(BF16), b_in, positions, qw, kvw)
    return _at_core(q, k, v, sinks, z, h, w_out, g_post)


def kernel(x, positions, norm_pre, norm_post, s5_w_in, s5_lambda_re, s5_lambda_im, s5_log_dt, s5_b_re, s5_b_im, s5_c_re, s5_c_im, s5_d, s5_w_glu, s5_b_glu, s5_w_out, hg_w_in, hg_lb_logits, hg_norm, hg_w_out, at_w_in, at_b_in, at_sinks, at_w_out):
    depth = norm_pre.shape[0]
    s5_weights = _s5_weights(s5_lambda_re, s5_lambda_im, s5_log_dt, s5_b_re, s5_b_im, s5_c_re, s5_c_im, s5_d)
    h = x
    for i in range(depth):
        kind, j = i % 3, i // 3
        if kind == 0:
            h = _s5_layer(h, norm_pre[i], norm_post[i], s5_w_in[j], s5_weights, j, s5_w_glu[j], s5_b_glu[j],
                          s5_w_out[j])
        elif kind == 1:
            h = _hg_layer(h, norm_pre[i], norm_post[i], hg_w_in[j], hg_lb_logits, i, hg_norm[j], hg_w_out[j])
        else:
            h = _at_layer(h, positions, norm_pre[i], norm_post[i], at_w_in[j], at_b_in[j], at_sinks[j],
                          at_w_out[j])
    return h
```

```python
import functools
import math

import jax
import jax.numpy as jnp
from jax import lax
from jax.experimental import pallas as pl
from jax.experimental.pallas import tpu as pltpu

F32 = jnp.float32
BF16 = jnp.bfloat16

NORM_EPS = 1e-6
LANES = 128
SUBLANES = 8
VMEM_LIMIT = 56 * 1024 * 1024

S5_GROUP = 16
S5_STATE = 64
S5_TILE_GROUPS = 16
S5_TILE_CH = S5_TILE_GROUPS * S5_GROUP
S5_TILE_ST = S5_TILE_GROUPS * S5_STATE
S5_STEPS = 64
HG_HEAD = 128
HG_CHUNK = 64
HG_BLOCK = 512
HG_MILD_DECAY = 80.0
AT_HEAD = 64
AT_KV_HEADS = 4
AT_BLOCK = 128
AT_SUBBLOCKS = 4
AT_WAVE = 16
ROPE_THETA = 10000.0

ROW_TILES = (1024, 512, 256)


def _cparams(sem):
    return pltpu.CompilerParams(dimension_semantics=sem, vmem_limit_bytes=VMEM_LIMIT)


def _row_tile(seq, row_bytes, resident_bytes, multiple=SUBLANES):
    budget = VMEM_LIMIT - VMEM_LIMIT // 4
    for tm in ROW_TILES:
        if seq % tm == 0 and tm % multiple == 0 and 2 * (tm * row_bytes + resident_bytes) <= budget:
            return tm
    return seq


def _rms(x, g):
    return x * lax.rsqrt(jnp.mean(x * x, axis=-1, keepdims=True) + NORM_EPS) * g


def _sigmoid(x):
    return 1.0 / (1.0 + jnp.exp(-x))


def _sigmoid_gate(x):
    return 0.5 + 0.5 * jnp.tanh(0.5 * x)


def _silu(x):
    return x * _sigmoid_gate(x)


def _gelu_tanh(x):
    c = math.sqrt(2.0 / math.pi)
    return 0.5 * x * (1.0 + jnp.tanh(c * (x + 0.044715 * (x * x * x))))


def _dot(a, b):
    return jnp.dot(a, b, preferred_element_type=F32)


def _dot_nt(a, b):
    return lax.dot_general(a, b, (((1,), (1,)), ((), ())), preferred_element_type=F32)


def _dot_tn(a, b):
    return lax.dot_general(a, b, (((0,), (0,)), ((), ())), preferred_element_type=F32)


def _chunk_row_groups(rows, steps):
    for base in range(0, rows, SUBLANES * steps):
        for r in range(SUBLANES):
            for k0 in range(0, steps, SUBLANES):
                yield base + r * steps + k0, base + k0 * SUBLANES + r


def _norm_proj_kernel(h_ref, g_ref, w_ref, *out_refs, widths, slab_steps):
    y = _rms(h_ref[0], g_ref[...]).astype(BF16)
    off = 0
    for o_ref, width, steps in zip(out_refs, widths, slab_steps):
        r = _dot(y, w_ref[:, off:off + width])
        if steps:
            for nat, cm in _chunk_row_groups(r.shape[0], steps):
                for j in range(width // LANES):
                    o_ref[0, j, pl.ds(cm, SUBLANES, stride=SUBLANES), :] = (
                        r[nat:nat + SUBLANES, j * LANES:(j + 1) * LANES])
        else:
            o_ref[0] = r.astype(o_ref.dtype)
        off += width


def _norm_proj(h, g, w_bf16, widths, slab_steps, dtypes):
    bsz, seq, d = h.shape
    row_bytes = 4 * d + sum(w * jnp.dtype(t).itemsize for w, t in zip(widths, dtypes))
    tm = _row_tile(seq, row_bytes, w_bf16.size * 2, SUBLANES * max(max(slab_steps), 1))
    out_shape, out_specs = [], []
    for width, as_slab, dtype in zip(widths, slab_steps, dtypes):
        if as_slab:
            assert tm % (SUBLANES * as_slab) == 0 and dtype == F32
            ns = width // LANES
            out_shape.append(jax.ShapeDtypeStruct((bsz, ns, seq, LANES), F32))
            out_specs.append(pl.BlockSpec((1, ns, tm, LANES), lambda b, m: (b, 0, m, 0)))
        else:
            out_shape.append(jax.ShapeDtypeStruct((bsz, seq, width), dtype))
            out_specs.append(pl.BlockSpec((1, tm, width), lambda b, m: (b, m, 0)))
    return pl.pallas_call(
        functools.partial(_norm_proj_kernel, widths=tuple(widths), slab_steps=tuple(slab_steps)),
        out_shape=out_shape,
        grid=(bsz, seq // tm),
        in_specs=[pl.BlockSpec((1, tm, d), lambda b, m: (b, m, 0)),
                  pl.BlockSpec((1, d), lambda b, m: (0, 0)),
                  pl.BlockSpec(w_bf16.shape, lambda b, m: (0, 0))],
        out_specs=out_specs,
        compiler_params=_cparams(("parallel", "parallel")),
        name="norm_proj",
    )(h, g.reshape(1, d), w_bf16)


def _zoh(lr, li, ldt):
    dt = jnp.exp(ldt)
    mag = jnp.exp(lr * dt)
    ar = mag * jnp.cos(li * dt)
    ai = mag * jnp.sin(li * dt)
    den = lr * lr + li * li
    qr = ((ar - 1.0) * lr + ai * li) / den
    qi = (ai * lr - (ar - 1.0) * li) / den
    return ar, ai, qr, qi


def _s5_tiles_kernel(lr_ref, li_ref, ldt_ref, lrc_ref, lic_ref, ldtc_ref, br_ref, bi_ref, cr_ref, ci_ref,
                     ar_ref, ai_ref, wb_ref, wc_ref):
    ar, ai, _, _ = _zoh(lr_ref[...], li_ref[...], ldt_ref[...])
    ar_ref[...] = ar
    ai_ref[...] = ai
    _, _, qr, qi = _zoh(lrc_ref[...], lic_ref[...], ldtc_ref[...])
    br = br_ref[...]
    bi = bi_ref[...]
    bbr = (qr * br - qi * bi).astype(BF16)
    bbi = (qr * bi + qi * br).astype(BF16)

    ch, st = S5_TILE_CH, S5_TILE_ST
    sel_h = jnp.where(lax.broadcasted_iota(jnp.int32, (ch, S5_GROUP), 0) % S5_GROUP
                      == lax.broadcasted_iota(jnp.int32, (ch, S5_GROUP), 1), 1.0, 0.0).astype(BF16)
    diag_in = (lax.broadcasted_iota(jnp.int32, (ch, st), 0) // S5_GROUP
               == lax.broadcasted_iota(jnp.int32, (ch, st), 1) // S5_STATE)
    wb_ref[0, :, 0:st] = jnp.where(diag_in, _dot_nt(sel_h, bbr), 0.0).astype(BF16)
    wb_ref[0, :, st:2 * st] = jnp.where(diag_in, _dot_nt(sel_h, bbi), 0.0).astype(BF16)
    sel_p = jnp.where(lax.broadcasted_iota(jnp.int32, (st, S5_STATE), 0) % S5_STATE
                      == lax.broadcasted_iota(jnp.int32, (st, S5_STATE), 1), 1.0, 0.0).astype(BF16)
    diag_out = (lax.broadcasted_iota(jnp.int32, (st, ch), 0) // S5_STATE
                == lax.broadcasted_iota(jnp.int32, (st, ch), 1) // S5_GROUP)
    wc_ref[0, 0:st, :] = jnp.where(diag_out, _dot_nt(sel_p, cr_ref[...].astype(BF16)), 0.0).astype(BF16)
    wc_ref[0, st:2 * st, :] = jnp.where(diag_out, -_dot_nt(sel_p, ci_ref[...].astype(BF16)), 0.0).astype(BF16)


def _s5_core_kernel(u_ref, wb_ref, wc_ref, a_ref, d_ref, y_ref, x_ref, s_ref, carry_ref, *, steps):
    nst = S5_TILE_ST
    nslab = S5_TILE_CH // LANES
    ntile = wb_ref.shape[0]

    @pl.when(pl.program_id(1) == 0)
    def _():
        carry_ref[...] = jnp.zeros_like(carry_ref)

    def load_u(i):
        return jnp.concatenate([u_ref[0, i * nslab + j] for j in range(nslab)], axis=-1)

    def project_in(i):
        x_ref[i % 2] = _dot(load_u(i).astype(BF16), wb_ref[i])

    def scan(i):
        slot = i % 2
        ar1 = a_ref[i, 0:1, :]
        ai1 = a_ref[i, 1:2, :]
        ar = jnp.broadcast_to(ar1, (SUBLANES, nst))
        ai = jnp.broadcast_to(ai1, (SUBLANES, nst))

        def advance(k, sr, si):
            xr = x_ref[slot, k * SUBLANES:(k + 1) * SUBLANES, 0:nst]
            xi = x_ref[slot, k * SUBLANES:(k + 1) * SUBLANES, nst:2 * nst]
            return ar * sr - ai * si + xr, ar * si + ai * sr + xi

        er = jnp.zeros((SUBLANES, nst), F32)
        ei = jnp.zeros((SUBLANES, nst), F32)
        for k in range(steps):
            er, ei = advance(k, er, ei)

        pr, pi = ar1, ai1
        for _ in range(int(math.log2(steps))):
            pr, pi = pr * pr - pi * pi, 2.0 * pr * pi

        rid = lax.broadcasted_iota(jnp.int32, (SUBLANES, nst), 0)
        cr = carry_ref[i, 0:1, :]
        ci = carry_ref[i, 1:2, :]
        sr = jnp.zeros((SUBLANES, nst), F32)
        si = jnp.zeros((SUBLANES, nst), F32)
        for r in range(SUBLANES):
            sr = jnp.where(rid == r, jnp.broadcast_to(cr, (SUBLANES, nst)), sr)
            si = jnp.where(rid == r, jnp.broadcast_to(ci, (SUBLANES, nst)), si)
            cr, ci = pr * cr - pi * ci + er[r:r + 1, :], pr * ci + pi * cr + ei[r:r + 1, :]
        carry_ref[i, 0:1, :] = cr
        carry_ref[i, 1:2, :] = ci

        for k in range(0, steps, 2):
            r0, i0 = advance(k, sr, si)
            sr, si = advance(k + 1, r0, i0)
            rows2 = slice(k * SUBLANES, (k + 2) * SUBLANES)
            s_ref[slot, rows2, 0:nst] = jnp.concatenate([r0, sr], axis=0).astype(BF16)
            s_ref[slot, rows2, nst:2 * nst] = jnp.concatenate([i0, si], axis=0).astype(BF16)

    def project_out(i):
        y = _dot(s_ref[i % 2], wc_ref[i]) + d_ref[i] * load_u(i)
        for j in range(nslab):
            y_ref[0, i * nslab + j] = y[:, j * LANES:(j + 1) * LANES]

    project_in(0)
    for i in range(ntile):
        if i + 1 < ntile:
            project_in(i + 1)
        scan(i)
        project_out(i)


def _s5_core(u_slab, wb, wc, a_tiles, d_tiles, layer, steps):
    bsz, nslab, seq, _ = u_slab.shape
    ntile = nslab * LANES // S5_TILE_CH
    rows = SUBLANES * steps
    const = lambda shp: pl.BlockSpec((ntile,) + shp[1:], lambda b, n: (layer,) + (0,) * (len(shp) - 1))
    tok = pl.BlockSpec((1, nslab, rows, LANES), lambda b, n: (b, 0, n, 0))
    return pl.pallas_call(
        functools.partial(_s5_core_kernel, steps=steps),
        out_shape=jax.ShapeDtypeStruct(u_slab.shape, F32),
        grid=(bsz, seq // rows),
        in_specs=[tok, const(wb.shape), const(wc.shape), const(a_tiles.shape), const(d_tiles.shape)],
        out_specs=tok,
        scratch_shapes=[pltpu.VMEM((2, rows, 2 * S5_TILE_ST), F32),
                        pltpu.VMEM((2, rows, 2 * S5_TILE_ST), BF16),
                        pltpu.VMEM((ntile, 2, S5_TILE_ST), F32)],
        compiler_params=_cparams(("parallel", "arbitrary")),
        name="s5_core",
    )(u_slab, wb, wc, a_tiles, d_tiles)


def _s5_post_kernel(y_ref, z_ref, h_ref, wg_ref, bg_ref, wo_ref, gp_ref, o_ref, *, steps):
    nslab = y_ref.shape[1]
    groups = list(_chunk_row_groups(y_ref.shape[2], steps))
    y = jnp.concatenate(
        [jnp.concatenate([y_ref[0, j, pl.ds(cm, SUBLANES, stride=SUBLANES), :] for _, cm in groups], axis=0)
         for j in range(nslab)], axis=-1)
    y = _gelu_tanh(y)
    y = y * _sigmoid_gate(_dot(y.astype(BF16), wg_ref[...]) + bg_ref[...])
    t = y * _silu(z_ref[0])
    o = _dot(t.astype(BF16), wo_ref[...])
    o_ref[0] = h_ref[0] + _rms(o, gp_ref[...])


def _s5_post(y_slab, z, h, w_glu, b_glu, w_out, g_post, steps):
    bsz, seq, d = h.shape
    nslab = y_slab.shape[1]
    tm = _row_tile(seq, 4 * 4 * d, 2 * 2 * d * d, SUBLANES * steps)
    assert tm % (SUBLANES * steps) == 0
    row = lambda t: t.reshape(1, -1)
    vec = pl.BlockSpec((1, d), lambda b, m: (0, 0))
    mat = pl.BlockSpec((d, d), lambda b, m: (0, 0))
    slab = pl.BlockSpec((1, nslab, tm, LANES), lambda b, m: (b, 0, m, 0))
    tok = pl.BlockSpec((1, tm, d), lambda b, m: (b, m, 0))
    return pl.pallas_call(
        functools.partial(_s5_post_kernel, steps=steps),
        out_shape=jax.ShapeDtypeStruct(h.shape, F32),
        grid=(bsz, seq // tm),
        in_specs=[slab, tok, tok, mat, vec, mat, vec],
        out_specs=tok,
        compiler_params=_cparams(("parallel", "parallel")),
        name="s5_post",
    )(y_slab, z, h, w_glu.astype(BF16), row(b_glu), w_out.astype(BF16), row(g_post))


def _s5_weights(lam_re, lam_im, log_dt, b_re, b_im, c_re, c_im, d_skip):
    nlayer, ngroup, nstate, gsize = b_re.shape
    ng = nlayer * ngroup
    ntile = ng // S5_TILE_GROUPS
    tg = S5_TILE_GROUPS
    col = lambda t: t.reshape(ng * nstate, 1)
    ldt = log_dt.reshape(ng, 1)
    grp = lambda w: pl.BlockSpec((tg, w), lambda i: (i, 0))
    rows = lambda r, w: pl.BlockSpec((r, w), lambda i: (i, 0))
    ar, ai, wb, wc = pl.pallas_call(
        _s5_tiles_kernel,
        out_shape=[jax.ShapeDtypeStruct((ng, nstate), F32), jax.ShapeDtypeStruct((ng, nstate), F32),
                   jax.ShapeDtypeStruct((ntile, S5_TILE_CH, 2 * S5_TILE_ST), BF16),
                   jax.ShapeDtypeStruct((ntile, 2 * S5_TILE_ST, S5_TILE_CH), BF16)],
        grid=(ntile,),
        in_specs=[grp(nstate), grp(nstate), grp(1),
                  rows(tg * nstate, 1), rows(tg * nstate, 1), rows(tg * nstate, 1),
                  rows(tg * nstate, gsize), rows(tg * nstate, gsize),
                  rows(tg * gsize, nstate), rows(tg * gsize, nstate)],
        out_specs=[grp(nstate), grp(nstate),
                   pl.BlockSpec((1, S5_TILE_CH, 2 * S5_TILE_ST), lambda i: (i, 0, 0)),
                   pl.BlockSpec((1, 2 * S5_TILE_ST, S5_TILE_CH), lambda i: (i, 0, 0))],
        compiler_params=_cparams(("parallel",)),
        name="s5_tiles",
    )(lam_re.reshape(ng, nstate), lam_im.reshape(ng, nstate), ldt,
      col(lam_re), col(lam_im), col(jnp.broadcast_to(ldt, (ng, nstate))),
      b_re.reshape(ng * nstate, gsize), b_im.reshape(ng * nstate, gsize),
      c_re.reshape(ng * gsize, nstate), c_im.reshape(ng * gsize, nstate))
    a_tiles = jnp.stack([ar.reshape(ntile, S5_TILE_ST), ai.reshape(ntile, S5_TILE_ST)], axis=1)
    d_tiles = d_skip.reshape(ntile, 1, S5_TILE_CH)
    return wb, wc, a_tiles, d_tiles


def _s5_layer(h, g_pre, g_post, w_in, weights, layer, w_glu, b_glu, w_out):
    bsz, seq, d = h.shape
    steps = min(S5_STEPS, seq // SUBLANES)
    u_slab, z = _norm_proj(h, g_pre, w_in.astype(BF16), (d, d), (steps, 0), (F32, F32))
    y_slab = _s5_core(u_slab, *weights, layer, steps)
    return _s5_post(y_slab, z, h, w_glu, b_glu, w_out, g_post, steps)


def _gate_out_math(o, z, h, g_head, w_out, g_post, head_norm):
    if head_norm:
        parts = []
        for j in range(o.shape[-1] // HG_HEAD):
            oj = o[:, j * HG_HEAD:(j + 1) * HG_HEAD]
            parts.append(oj * lax.rsqrt(jnp.mean(oj * oj, axis=-1, keepdims=True) + NORM_EPS))
        o = jnp.concatenate(parts, axis=-1) * g_head
    t = o * _silu(z)
    return h + _rms(_dot(t.astype(BF16), w_out), g_post)


def _chunk_cumsum(x, chunk):
    rows, w = x.shape
    per = chunk // SUBLANES
    x3 = x.reshape(rows // SUBLANES, SUBLANES, w)
    pos = lax.broadcasted_iota(jnp.int32, x3.shape, 1)
    sh = 1
    while sh < SUBLANES:
        x3 = x3 + jnp.where(pos >= sh, pltpu.roll(x3, sh, 1), 0.0)
        sh *= 2
    x4 = x3.reshape(rows // chunk, per, SUBLANES, w)
    tot = x4[:, :, SUBLANES - 1:SUBLANES, :]
    pref = [jnp.zeros_like(tot[:, 0:1])]
    for j in range(1, per):
        pref.append(pref[-1] + tot[:, j - 1:j])
    return (x4 + jnp.concatenate(pref, axis=1)).reshape(rows, w)


def _block_mid_rows(x, blk):
    rows, w = x.shape
    half = blk // 2
    if blk >= SUBLANES:
        x3 = x.reshape(rows // blk, blk, w)
        return jnp.broadcast_to(x3[:, half - 1:half, :], (rows // blk, blk, w)).reshape(rows, w)
    pos = lax.broadcasted_iota(jnp.int32, (rows, w), 0) % blk
    out = x
    for p in range(blk):
        d = p - (half - 1)
        if d != 0:
            out = jnp.where(pos == p, pltpu.roll(x, d % rows, 0), out)
    return out


def _hg_proj_kernel(h_ref, g_ref, w_ref, lbl_ref, q_ref, kk_ref, b_ref, v_ref, z_ref, dec_ref, *, layer, chunk):
    width = q_ref.shape[2]
    rows = q_ref.shape[1]
    y = _rms(h_ref[0], g_ref[...]).astype(BF16)
    q_ref[0] = _dot(y, w_ref[:, 0:width])
    fz = _dot(y, w_ref[:, width:2 * width])
    v_ref[0] = _dot(y, w_ref[:, 2 * width:3 * width]).astype(v_ref.dtype)
    z_ref[0] = _dot(y, w_ref[:, 3 * width:4 * width])

    lg = lbl_ref[...]
    ex = jnp.exp(lg - jnp.max(lg, axis=0, keepdims=True))
    p = ex / jnp.sum(ex, axis=0, keepdims=True)
    lb = jnp.zeros((1, width), F32)
    for j in range(1, layer + 1):
        lb = lb + p[j:j + 1, :]

    f = lb + (1.0 - lb) * _sigmoid(fz)
    kk_ref[0] = 1.0 - f
    b = _chunk_cumsum(jnp.log(f), chunk)
    b_ref[0] = b
    dec_ref[0] = -b.reshape(rows // chunk, chunk, width)[:, chunk - 1, :]


def _hg_proj(h, g, w_bf16, lb_logits, layer, chunk):
    bsz, seq, d = h.shape
    width = w_bf16.shape[1] // 4
    tm = _row_tile(seq, 4 * d + (4 * 4 + 2) * width, w_bf16.size * 2, SUBLANES * chunk)
    tok = pl.BlockSpec((1, tm, width), lambda b, m: (b, m, 0))
    const = lambda shp: pl.BlockSpec(shp, lambda b, m: (0, 0))
    act = lambda t: jax.ShapeDtypeStruct((bsz, seq, width), t)
    return pl.pallas_call(
        functools.partial(_hg_proj_kernel, layer=layer, chunk=chunk),
        out_shape=[act(F32), act(F32), act(F32), act(BF16), act(F32),
                   jax.ShapeDtypeStruct((bsz, seq // chunk, width), F32)],
        grid=(bsz, seq // tm),
        in_specs=[pl.BlockSpec((1, tm, d), lambda b, m: (b, m, 0)), const((1, d)), const(w_bf16.shape),
                  const(lb_logits.shape)],
        out_specs=[tok, tok, tok, tok, tok, pl.BlockSpec((1, tm // chunk, width), lambda b, m: (b, m, 0))],
        compiler_params=_cparams(("parallel", "parallel")),
        name="hgrn2_proj",
    )(h, g.reshape(1, d), w_bf16, lb_logits)


def _hg_core_kernel(q_ref, kk_ref, b_ref, v_ref, dec_ref, z_ref, h_ref, gn_ref, wo_ref, gp_ref, out_ref,
                    xl_ref, qe_ref, kd_ref, g_ref, st_ref, o_scr, *, chunk):
    rows, width = q_ref.shape[1], q_ref.shape[2]
    pair = 2 * HG_HEAD
    blocks = [2 ** i for i in range(1, int(math.log2(chunk)) + 1)]

    @pl.when(pl.program_id(1) == 0)
    def _():
        st_ref[...] = jnp.zeros_like(st_ref)

    def state_operands():
        q = q_ref[0]
        kk = kk_ref[0]
        b = b_ref[0]
        b3 = b.reshape(rows // chunk, chunk, width)
        b_last = b3[:, chunk - 1:chunk, :]
        g_ref[...] = jnp.exp(b_last.reshape(rows // chunk, width))
        qe_ref[...] = (q * jnp.exp(b)).astype(BF16)
        kd_ref[...] = (kk * jnp.exp(jnp.broadcast_to(b_last, b3.shape).reshape(rows, width) - b)).astype(BF16)
        return q, kk, b

    t_i = lax.broadcasted_iota(jnp.int32, (chunk, HG_HEAD), 0)
    s_i = lax.broadcasted_iota(jnp.int32, (chunk, HG_HEAD), 1) % chunk
    lo = lax.broadcasted_iota(jnp.int32, (chunk, pair), 1) < HG_HEAD
    bd = (lax.broadcasted_iota(jnp.int32, (pair, pair), 0) // HG_HEAD
          == lax.broadcasted_iota(jnp.int32, (pair, pair), 1) // HG_HEAD)

    def split_heads(t):
        z = jnp.zeros_like(t)
        return jnp.concatenate([jnp.where(lo, t, z), jnp.where(lo, z, t)], axis=0)

    def chunk_pairs():
        for c in range(rows // chunk):
            for pr in range(width // pair):
                yield c, pr, slice(c * chunk, (c + 1) * chunk), slice(pr * pair, (pr + 1) * pair)

    def finish(c, pr, rs, ls, sc):
        vpair = v_ref[0, rs, ls]
        st = st_ref[pr]
        o_scr[rs, ls] = _dot(sc.astype(BF16), split_heads(vpair)) + _dot_nt(qe_ref[rs, ls], st.astype(BF16))
        st_ref[pr] = g_ref[c:c + 1, ls] * st + jnp.where(bd, _dot_tn(vpair, kd_ref[rs, ls]), 0.0)

    mild = jnp.max(dec_ref[0]) < HG_MILD_DECAY

    @pl.when(mild)
    def _():
        _, kk, b = state_operands()
        xl_ref[0] = (kk * jnp.exp(-b)).astype(BF16)
        causal = s_i <= t_i
        for c, pr, rs, ls in chunk_pairs():
            sc = jnp.where(causal, _dot_nt(qe_ref[rs, ls], split_heads(xl_ref[0, rs, ls])), 0.0)
            finish(c, pr, rs, ls, sc)

    @pl.when(jnp.logical_not(mild))
    def _():
        q, kk, b = state_operands()
        xl_ref[0] = (q * kk).astype(BF16)
        pos = lax.broadcasted_iota(jnp.int32, (rows, width), 0)
        for li, blk in enumerate(blocks):
            w = jnp.exp(-jnp.abs(b - _block_mid_rows(b, blk)))
            xl_ref[li + 1] = (jnp.where(pos % blk >= blk // 2, q, kk) * w).astype(BF16)
        eye = t_i == s_i
        lmask = [((t_i // blk) == (s_i // blk)) & ((t_i % blk) >= blk // 2) & ((s_i % blk) < blk // 2)
                 for blk in blocks]
        ones_bd = jnp.where(lax.broadcasted_iota(jnp.int32, (pair, HG_HEAD), 0) // HG_HEAD
                            == lax.broadcasted_iota(jnp.int32, (pair, HG_HEAD), 1) // chunk, 1.0, 0.0).astype(BF16)
        for c, pr, rs, ls in chunk_pairs():
            sc = jnp.where(eye, _dot(xl_ref[0, rs, ls], ones_bd), 0.0)
            for li in range(len(blocks)):
                xl = xl_ref[li + 1, rs, ls]
                sc = jnp.where(lmask[li], _dot_nt(xl, split_heads(xl)), sc)
            finish(c, pr, rs, ls, sc)

    out_ref[0] = _gate_out_math(o_scr[...], z_ref[0], h_ref[0], gn_ref[...], wo_ref[...], gp_ref[...], True)


def _hg_core(q, kk, b, v, dec, z, h, norm_g, w_out, g_post, chunk):
    bsz, seq, width = q.shape
    blk = min(HG_BLOCK, seq)
    nlev = int(math.log2(chunk))
    assert HG_HEAD == 2 * chunk and width % (2 * HG_HEAD) == 0 and blk % chunk == 0
    tok = pl.BlockSpec((1, blk, width), lambda b_, n: (b_, n, 0))
    const = lambda shp: pl.BlockSpec(shp, lambda b_, n: (0, 0))
    row = lambda t: t.reshape(1, -1)
    return pl.pallas_call(
        functools.partial(_hg_core_kernel, chunk=chunk),
        out_shape=jax.ShapeDtypeStruct(h.shape, F32),
        grid=(bsz, seq // blk),
        in_specs=[tok, tok, tok, tok, pl.BlockSpec((1, blk // chunk, width), lambda b_, n: (b_, n, 0)),
                  tok, tok, const((1, width)), const(w_out.shape), const((1, width))],
        out_specs=tok,
        scratch_shapes=[pltpu.VMEM((nlev + 1, blk, width), BF16),
                        pltpu.VMEM((blk, width), BF16),
                        pltpu.VMEM((blk, width), BF16),
                        pltpu.VMEM((blk // chunk, width), F32),
                        pltpu.VMEM((width // (2 * HG_HEAD), 2 * HG_HEAD, 2 * HG_HEAD), F32),
                        pltpu.VMEM((blk, width), F32)],
        compiler_params=_cparams(("parallel", "arbitrary")),
        name="hgrn2_core",
    )(q, kk, b, v, dec, z, h, row(norm_g), w_out.astype(BF16), row(g_post))


def _hg_layer(h, g_pre, g_post, w_in, lb_logits, layer, norm_g, w_out):
    chunk = min(HG_CHUNK, h.shape[1])
    q, kk, b, v, z, dec = _hg_proj(h, g_pre, w_in.astype(BF16), lb_logits, layer, chunk)
    return _hg_core(q, kk, b, v, dec, z, h, norm_g, w_out, g_post, chunk)


def _at_proj_kernel(h_ref, g_ref, w_ref, bias_ref, pos_ref, invf_ref, q_ref, k_ref, v_ref, z_ref,
                    *, qw, kvw):
    y = _rms(h_ref[0], g_ref[...]).astype(BF16)
    half = AT_HEAD // 2
    nq4 = LANES // half
    ang = pos_ref[0].astype(F32) * invf_ref[...]
    cos4 = jnp.cos(ang)
    sin4 = jnp.sin(ang)
    lane4 = lax.broadcasted_iota(jnp.int32, ang.shape, 1)

    def spread(t):
        parts = []
        for j in range(nq4):
            m = jnp.where(lane4 // half == j, t, 0.0)
            x = m
            for s in range(1, nq4):
                x = x + pltpu.roll(m, s * half, 1)
            parts.append(x)
        return jnp.concatenate(parts, axis=0)

    lane = lax.broadcasted_iota(jnp.int32, (h_ref.shape[1], LANES), 1)
    first = (lane % AT_HEAD) < half
    low = lane < AT_HEAD
    cosf = spread(cos4)
    sinf = spread(sin4)
    sinf = jnp.where(first, -sinf, sinf)

    def rope(t):
        partner = jnp.where(first, pltpu.roll(t, LANES - AT_HEAD // 2, 1), pltpu.roll(t, AT_HEAD // 2, 1))
        return t * cosf + partner * sinf

    def store_dup(ref, j, t):
        r = pltpu.roll(t, AT_HEAD, 1)
        ref[0, :, (2 * j) * LANES:(2 * j + 1) * LANES] = jnp.where(low, t, r).astype(ref.dtype)
        ref[0, :, (2 * j + 1) * LANES:(2 * j + 2) * LANES] = jnp.where(low, r, t).astype(ref.dtype)

    qkv_w = qw + 2 * kvw
    z_ref[0] = _dot(y, w_ref[:, qkv_w:qkv_w + qw])
    qf = _dot(y, w_ref[:, 0:qw]) + bias_ref[:, 0:qw]
    kvf = _dot(y, w_ref[:, qw:qkv_w]) + bias_ref[:, qw:qkv_w]
    for j in range(qw // LANES):
        sl = slice(j * LANES, (j + 1) * LANES)
        q_ref[0, :, sl] = (rope(qf[:, sl]) * AT_HEAD ** -0.5).astype(q_ref.dtype)
    for j in range(kvw // LANES):
        store_dup(k_ref, j, rope(kvf[:, j * LANES:(j + 1) * LANES]))
        store_dup(v_ref, j, kvf[:, kvw + j * LANES:kvw + (j + 1) * LANES])


def _at_proj(h, g, w_bf16, bias, positions, qw, kvw):
    bsz, seq, d = h.shape
    tm = _row_tile(seq, 4 * d + 2 * qw + 4 * 2 * kvw + 4 * qw, w_bf16.size * 2, 4 * SUBLANES)
    half = AT_HEAD // 2
    inv_freq = ROPE_THETA ** (-jnp.arange(0, AT_HEAD, 2, dtype=F32) / AT_HEAD)
    nq4 = LANES // half
    invf = jnp.tile(inv_freq, nq4).reshape(1, LANES)
    pos4 = positions.reshape(bsz, seq // tm, nq4, tm // nq4).transpose(0, 1, 3, 2)
    pos4 = jnp.repeat(pos4, half, axis=-1).reshape(bsz, seq // nq4, LANES)
    tokspec = lambda w: pl.BlockSpec((1, tm, w), lambda b, m: (b, m, 0))
    const = lambda shp: pl.BlockSpec(shp, lambda b, m: (0, 0))
    return pl.pallas_call(
        functools.partial(_at_proj_kernel, qw=qw, kvw=kvw),
        out_shape=[jax.ShapeDtypeStruct((bsz, seq, qw), BF16), jax.ShapeDtypeStruct((bsz, seq, 2 * kvw), BF16),
                   jax.ShapeDtypeStruct((bsz, seq, 2 * kvw), BF16), jax.ShapeDtypeStruct((bsz, seq, qw), F32)],
        grid=(bsz, seq // tm),
        in_specs=[tokspec(d), const((1, d)), const(w_bf16.shape), const((1, qw + 2 * kvw)),
                  pl.BlockSpec((1, tm // nq4, LANES), lambda b, m: (b, m, 0)), const((1, LANES))],
        out_specs=[tokspec(qw), tokspec(2 * kvw), tokspec(2 * kvw), tokspec(qw)],
        compiler_params=_cparams(("parallel", "parallel")),
        name="attn_proj",
    )(h, g.reshape(1, d), w_bf16, bias.reshape(1, -1), pos4, invf)


def _at_core_kernel(q_ref, kc_ref, kp_ref, vc_ref, vp_ref, sink_ref, z_ref, h_ref, wo_ref, gp_ref, out_ref):
    blk = AT_BLOCK
    nsub = q_ref.shape[1] // blk
    n = pl.program_id(1)
    ngroup = kc_ref.shape[2] // LANES
    qi = lax.broadcasted_iota(jnp.int32, (2 * blk, 4 * blk), 0) % blk
    kj = lax.broadcasted_iota(jnp.int32, (2 * blk, 4 * blk), 1) % (2 * blk)
    dist = qi + blk - kj
    in_window = (dist >= 0) & (dist < blk)
    first_mask = in_window & ((kj >= blk) | (n > 0))
    low = lax.broadcasted_iota(jnp.int32, (2 * blk, LANES), 1) < AT_HEAD
    top = lax.broadcasted_iota(jnp.int32, (2 * blk, 1), 0) < blk
    ones_lo = jnp.where(low, 1.0, 0.0).astype(BF16)
    ones_hi = jnp.where(low, 0.0, 1.0).astype(BF16)
    items = [(j, g) for j in range(nsub) for g in range(ngroup)]

    def band(cur_ref, prev_ref, j, g):
        sl = slice(g * LANES, (g + 1) * LANES)
        prev = prev_ref[0, :, sl] if j == 0 else cur_ref[0, (j - 1) * blk:j * blk, sl]
        return jnp.concatenate([prev, cur_ref[0, j * blk:(j + 1) * blk, sl]], axis=0)

    o_rows = [[] for _ in range(nsub)]
    for w0 in range(0, len(items), AT_WAVE):
        wave = items[w0:w0 + AT_WAVE]
        scores = []
        for j, g in wave:
            kdup = band(kc_ref, kp_ref, j, g)
            rhs = jnp.concatenate([jnp.where(low, kdup, 0.0), jnp.where(low, 0.0, kdup)], axis=0)
            rows = slice(j * blk, (j + 1) * blk)
            lhs = jnp.concatenate([q_ref[0, rows, (2 * g) * LANES:(2 * g + 1) * LANES],
                                   q_ref[0, rows, (2 * g + 1) * LANES:(2 * g + 2) * LANES]], axis=0)
            scores.append(_dot_nt(lhs, rhs))

        probs, sink_terms = [], []
        for (j, g), sc in zip(wave, scores):
            s = jnp.where(first_mask if j == 0 else in_window, sc, -jnp.inf)
            sink_a = jnp.where(top, sink_ref[0:1, 4 * g:4 * g + 1], sink_ref[0:1, 4 * g + 2:4 * g + 3])
            sink_b = jnp.where(top, sink_ref[0:1, 4 * g + 1:4 * g + 2], sink_ref[0:1, 4 * g + 3:4 * g + 4])
            m_a = jnp.maximum(jnp.max(s[:, :2 * blk], axis=-1, keepdims=True), sink_a)
            m_b = jnp.maximum(jnp.max(s[:, 2 * blk:], axis=-1, keepdims=True), sink_b)
            e = jnp.concatenate([jnp.exp(s[:, :2 * blk] - m_a), jnp.exp(s[:, 2 * blk:] - m_b)], axis=1)
            probs.append(e.astype(BF16))
            sink_terms.append(jnp.where(low, jnp.exp(sink_a - m_a), jnp.exp(sink_b - m_b)))

        for (j, g), p, st in zip(wave, probs, sink_terms):
            vdup = band(vc_ref, vp_ref, j, g)
            va = jnp.where(low, vdup, 0.0)
            vb = jnp.where(low, 0.0, vdup)
            rhs = jnp.concatenate([jnp.concatenate([va, ones_lo], axis=1),
                                   jnp.concatenate([vb, ones_hi], axis=1)], axis=0)
            out = _dot(p, rhs)
            o = out[:, :LANES] / (out[:, LANES:] + st)
            o_rows[j] += [o[:blk], o[blk:]]

    o = jnp.concatenate([jnp.concatenate(slabs, axis=-1) for slabs in o_rows], axis=0)
    out_ref[0] = _gate_out_math(o, z_ref[0], h_ref[0], None, wo_ref[...], gp_ref[...], False)


def _at_core(q, k_dup, v_dup, sinks, z, h, w_out, g_post):
    bsz, seq, qw = q.shape
    kvw = k_dup.shape[-1]
    d = h.shape[-1]
    blk = AT_BLOCK
    nsub = AT_SUBBLOCKS if seq % (AT_SUBBLOCKS * blk) == 0 else 1
    nq = qw // AT_HEAD
    assert nq == 4 * (kvw // LANES) and qw == 2 * kvw
    cur = lambda w: pl.BlockSpec((1, nsub * blk, w), lambda b, n: (b, n, 0))
    prev = lambda w: pl.BlockSpec((1, blk, w), lambda b, n: (b, jnp.maximum(nsub * n - 1, 0), 0))
    const = lambda shp: pl.BlockSpec(shp, lambda b, n: (0, 0))
    return pl.pallas_call(
        _at_core_kernel,
        out_shape=jax.ShapeDtypeStruct(h.shape, F32),
        grid=(bsz, seq // (nsub * blk)),
        in_specs=[cur(qw), cur(kvw), prev(kvw), cur(kvw), prev(kvw), const((1, nq)),
                  cur(qw), cur(d), const(w_out.shape), const((1, d))],
        out_specs=cur(d),
        compiler_params=_cparams(("parallel", "arbitrary")),
        name="attn_core",
    )(q, k_dup, k_dup, v_dup, v_dup, sinks.reshape(1, nq), z, h, w_out.astype(BF16), g_post.reshape(1, d))


def _at_layer(h, positions, g_pre, g_post, w_in, b_in, sinks, w_out):
    qw = w_out.shape[0]
    kvw = (b_in.shape[0] - qw) // 2
    q, k, v, z = _at_proj(h, g_pre, w_in.astype(BF16), b_in, positions, qw, kvw)
    return _at_core(q, k, v, sinks, z, h, w_out, g_post)


def kernel(x, positions, norm_pre, norm_post, s5_w_in, s5_lambda_re, s5_lambda_im, s5_log_dt, s5_b_re, s5_b_im, s5_c_re, s5_c_im, s5_d, s5_w_glu, s5_b_glu, s5_w_out, hg_w_in, hg_lb_logits, hg_norm, hg_w_out, at_w_in, at_b_in, at_sinks, at_w_out):
    depth = norm_pre.shape[0]
    s5_weights = _s5_weights(s5_lambda_re, s5_lambda_im, s5_log_dt, s5_b_re, s5_b_im, s5_c_re, s5_c_im, s5_d)
    h = x
    for i in range(depth):
        kind, j = i % 3, i // 3
        if kind == 0:
            h = _s5_layer(h, norm_pre[i], norm_post[i], s5_w_in[j], s5_weights, j, s5_w_glu[j], s5_b_glu[j],
                          s5_w_out[j])
        elif kind == 1:
            h = _hg_layer(h, norm_pre[i], norm_post[i], hg_w_in[j], hg_lb_logits, i, hg_norm[j], hg_w_out[j])
        else:
            h = _at_layer(h, positions, norm_pre[i], norm_post[i], at_w_in[j], at_b_in[j], at_sinks[j],
                          at_w_out[j])
    return h
```

```python
import functools
import math

import jax
import jax.numpy as jnp
from jax import lax
from jax.experimental import pallas as pl
from jax.experimental.pallas import tpu as pltpu

F32 = jnp.float32
BF16 = jnp.bfloat16

NORM_EPS = 1e-6
LANES = 128
SUBLANES = 8
VMEM_LIMIT = 56 * 1024 * 1024

S5_GROUP = 16
S5_STATE = 64
S5_TILE_GROUPS = 16
S5_TILE_CH = S5_TILE_GROUPS * S5_GROUP
S5_TILE_ST = S5_TILE_GROUPS * S5_STATE
S5_STEPS = 64
HG_HEAD = 128
HG_CHUNK = 64
HG_BLOCK = 512
HG_MILD_DECAY = 80.0
AT_HEAD = 64
AT_BLOCK = 128
AT_SUBBLOCKS = 4
AT_WAVE = 16
ROPE_THETA = 10000.0

ROW_TILES = (1024, 512, 256)


def _cparams(sem):
    return pltpu.CompilerParams(dimension_semantics=sem, vmem_limit_bytes=VMEM_LIMIT)


def _row_tile(seq, row_bytes, resident_bytes, multiple=SUBLANES):
    budget = VMEM_LIMIT - VMEM_LIMIT // 4
    for tm in ROW_TILES:
        if seq % tm == 0 and tm % multiple == 0 and 2 * (tm * row_bytes + resident_bytes) <= budget:
            return tm
    return seq


def _rms(x, g):
    return x * lax.rsqrt(jnp.mean(x * x, axis=-1, keepdims=True) + NORM_EPS) * g


def _sigmoid(x):
    return 1.0 / (1.0 + jnp.exp(-x))


def _sigmoid_gate(x):
    return 0.5 + 0.5 * jnp.tanh(0.5 * x)


def _silu(x):
    return x * _sigmoid_gate(x)


def _gelu_tanh(x):
    c = math.sqrt(2.0 / math.pi)
    return 0.5 * x * (1.0 + jnp.tanh(c * (x + 0.044715 * (x * x * x))))


def _dot(a, b):
    return jnp.dot(a, b, preferred_element_type=F32)


def _dot_nt(a, b):
    return lax.dot_general(a, b, (((1,), (1,)), ((), ())), preferred_element_type=F32)


def _dot_tn(a, b):
    return lax.dot_general(a, b, (((0,), (0,)), ((), ())), preferred_element_type=F32)


def _chunk_row_groups(rows, steps):
    for base in range(0, rows, SUBLANES * steps):
        for r in range(SUBLANES):
            for k0 in range(0, steps, SUBLANES):
                yield base + r * steps + k0, base + k0 * SUBLANES + r


def _norm_proj_kernel(h_ref, g_ref, w_ref, *out_refs, widths, slab_steps):
    y = _rms(h_ref[0], g_ref[...]).astype(BF16)
    off = 0
    for o_ref, width, steps in zip(out_refs, widths, slab_steps):
        r = _dot(y, w_ref[:, off:off + width])
        if steps:
            for nat, cm in _chunk_row_groups(r.shape[0], steps):
                for j in range(width // LANES):
                    o_ref[0, j, pl.ds(cm, SUBLANES, stride=SUBLANES), :] = (
                        r[nat:nat + SUBLANES, j * LANES:(j + 1) * LANES])
        else:
            o_ref[0] = r.astype(o_ref.dtype)
        off += width


def _norm_proj(h, g, w_bf16, widths, slab_steps, dtypes):
    bsz, seq, d = h.shape
    row_bytes = 4 * d + sum(w * jnp.dtype(t).itemsize for w, t in zip(widths, dtypes))
    tm = _row_tile(seq, row_bytes, w_bf16.size * 2, SUBLANES * max(max(slab_steps), 1))
    out_shape, out_specs = [], []
    for width, as_slab, dtype in zip(widths, slab_steps, dtypes):
        if as_slab:
            assert tm % (SUBLANES * as_slab) == 0 and dtype == F32
            ns = width // LANES
            out_shape.append(jax.ShapeDtypeStruct((bsz, ns, seq, LANES), F32))
            out_specs.append(pl.BlockSpec((1, ns, tm, LANES), lambda b, m: (b, 0, m, 0)))
        else:
            out_shape.append(jax.ShapeDtypeStruct((bsz, seq, width), dtype))
            out_specs.append(pl.BlockSpec((1, tm, width), lambda b, m: (b, m, 0)))
    return pl.pallas_call(
        functools.partial(_norm_proj_kernel, widths=tuple(widths), slab_steps=tuple(slab_steps)),
        out_shape=out_shape,
        grid=(bsz, seq // tm),
        in_specs=[pl.BlockSpec((1, tm, d), lambda b, m: (b, m, 0)),
                  pl.BlockSpec((1, d), lambda b, m: (0, 0)),
                  pl.BlockSpec(w_bf16.shape, lambda b, m: (0, 0))],
        out_specs=out_specs,
        compiler_params=_cparams(("parallel", "parallel")),
        name="norm_proj",
    )(h, g.reshape(1, d), w_bf16)


def _zoh(lr, li, ldt):
    dt = jnp.exp(ldt)
    mag = jnp.exp(lr * dt)
    ar = mag * jnp.cos(li * dt)
    ai = mag * jnp.sin(li * dt)
    den = lr * lr + li * li
    qr = ((ar - 1.0) * lr + ai * li) / den
    qi = (ai * lr - (ar - 1.0) * li) / den
    return ar, ai, qr, qi


def _s5_tiles_kernel(lr_ref, li_ref, ldt_ref, lrc_ref, lic_ref, ldtc_ref, br_ref, bi_ref, cr_ref, ci_ref,
                     ar_ref, ai_ref, wb_ref, wc_ref):
    ar, ai, _, _ = _zoh(lr_ref[...], li_ref[...], ldt_ref[...])
    ar_ref[...] = ar
    ai_ref[...] = ai
    _, _, qr, qi = _zoh(lrc_ref[...], lic_ref[...], ldtc_ref[...])
    br = br_ref[...]
    bi = bi_ref[...]
    bbr = (qr * br - qi * bi).astype(BF16)
    bbi = (qr * bi + qi * br).astype(BF16)

    ch, st = S5_TILE_CH, S5_TILE_ST
    sel_h = jnp.where(lax.broadcasted_iota(jnp.int32, (ch, S5_GROUP), 0) % S5_GROUP
                      == lax.broadcasted_iota(jnp.int32, (ch, S5_GROUP), 1), 1.0, 0.0).astype(BF16)
    diag_in = (lax.broadcasted_iota(jnp.int32, (ch, st), 0) // S5_GROUP
               == lax.broadcasted_iota(jnp.int32, (ch, st), 1) // S5_STATE)
    wb_ref[0, :, 0:st] = jnp.where(diag_in, _dot_nt(sel_h, bbr), 0.0).astype(BF16)
    wb_ref[0, :, st:2 * st] = jnp.where(diag_in, _dot_nt(sel_h, bbi), 0.0).astype(BF16)
    sel_p = jnp.where(lax.broadcasted_iota(jnp.int32, (st, S5_STATE), 0) % S5_STATE
                      == lax.broadcasted_iota(jnp.int32, (st, S5_STATE), 1), 1.0, 0.0).astype(BF16)
    diag_out = (lax.broadcasted_iota(jnp.int32, (st, ch), 0) // S5_STATE
                == lax.broadcasted_iota(jnp.int32, (st, ch), 1) // S5_GROUP)
    wc_ref[0, 0:st, :] = jnp.where(diag_out, _dot_nt(sel_p, cr_ref[...].astype(BF16)), 0.0).astype(BF16)
    wc_ref[0, st:2 * st, :] = jnp.where(diag_out, -_dot_nt(sel_p, ci_ref[...].astype(BF16)), 0.0).astype(BF16)


def _s5_core_kernel(u_ref, wb_ref, wc_ref, a_ref, d_ref, y_ref, x_ref, s_ref, carry_ref, *, steps):
    nst = S5_TILE_ST
    nslab = S5_TILE_CH // LANES
    ntile = wb_ref.shape[0]

    @pl.when(pl.program_id(1) == 0)
    def _():
        carry_ref[...] = jnp.zeros_like(carry_ref)

    def load_u(i):
        return jnp.concatenate([u_ref[0, i * nslab + j] for j in range(nslab)], axis=-1)

    def project_in(i):
        x_ref[i % 2] = _dot(load_u(i).astype(BF16), wb_ref[i])

    def scan(i):
        slot = i % 2
        ar1 = a_ref[i, 0:1, :]
        ai1 = a_ref[i, 1:2, :]
        ar = jnp.broadcast_to(ar1, (SUBLANES, nst))
        ai = jnp.broadcast_to(ai1, (SUBLANES, nst))

        def advance(k, sr, si):
            xr = x_ref[slot, k * SUBLANES:(k + 1) * SUBLANES, 0:nst]
            xi = x_ref[slot, k * SUBLANES:(k + 1) * SUBLANES, nst:2 * nst]
            return ar * sr - ai * si + xr, ar * si + ai * sr + xi

        er = jnp.zeros((SUBLANES, nst), F32)
        ei = jnp.zeros((SUBLANES, nst), F32)
        for k in range(steps):
            er, ei = advance(k, er, ei)

        pr, pi = ar1, ai1
        for _ in range(int(math.log2(steps))):
            pr, pi = pr * pr - pi * pi, 2.0 * pr * pi

        rid = lax.broadcasted_iota(jnp.int32, (SUBLANES, nst), 0)
        cr = carry_ref[i, 0:1, :]
        ci = carry_ref[i, 1:2, :]
        sr = jnp.zeros((SUBLANES, nst), F32)
        si = jnp.zeros((SUBLANES, nst), F32)
        for r in range(SUBLANES):
            sr = jnp.where(rid == r, jnp.broadcast_to(cr, (SUBLANES, nst)), sr)
            si = jnp.where(rid == r, jnp.broadcast_to(ci, (SUBLANES, nst)), si)
            cr, ci = pr * cr - pi * ci + er[r:r + 1, :], pr * ci + pi * cr + ei[r:r + 1, :]
        carry_ref[i, 0:1, :] = cr
        carry_ref[i, 1:2, :] = ci

        for k in range(0, steps, 2):
            r0, i0 = advance(k, sr, si)
            sr, si = advance(k + 1, r0, i0)
            rows2 = slice(k * SUBLANES, (k + 2) * SUBLANES)
            s_ref[slot, rows2, 0:nst] = jnp.concatenate([r0, sr], axis=0).astype(BF16)
            s_ref[slot, rows2, nst:2 * nst] = jnp.concatenate([i0, si], axis=0).astype(BF16)

    def project_out(i):
        y = _dot(s_ref[i % 2], wc_ref[i]) + d_ref[i] * load_u(i)
        for j in range(nslab):
            y_ref[0, i * nslab + j] = y[:, j * LANES:(j + 1) * LANES]

    project_in(0)
    for i in range(ntile):
        if i + 1 < ntile:
            project_in(i + 1)
        scan(i)
        project_out(i)


def _s5_core(u_slab, wb, wc, a_tiles, d_tiles, layer, steps):
    bsz, nslab, seq, _ = u_slab.shape
    ntile = nslab * LANES // S5_TILE_CH
    rows = SUBLANES * steps
    const = lambda shp: pl.BlockSpec((ntile,) + shp[1:], lambda b, n: (layer,) + (0,) * (len(shp) - 1))
    tok = pl.BlockSpec((1, nslab, rows, LANES), lambda b, n: (b, 0, n, 0))
    return pl.pallas_call(
        functools.partial(_s5_core_kernel, steps=steps),
        out_shape=jax.ShapeDtypeStruct(u_slab.shape, F32),
        grid=(bsz, seq // rows),
        in_specs=[tok, const(wb.shape), const(wc.shape), const(a_tiles.shape), const(d_tiles.shape)],
        out_specs=tok,
        scratch_shapes=[pltpu.VMEM((2, rows, 2 * S5_TILE_ST), F32),
                        pltpu.VMEM((2, rows, 2 * S5_TILE_ST), BF16),
                        pltpu.VMEM((ntile, 2, S5_TILE_ST), F32)],
        compiler_params=_cparams(("parallel", "arbitrary")),
        name="s5_core",
    )(u_slab, wb, wc, a_tiles, d_tiles)


def _s5_post_kernel(y_ref, z_ref, h_ref, wg_ref, bg_ref, wo_ref, gp_ref, o_ref, *, steps):
    nslab = y_ref.shape[1]
    groups = list(_chunk_row_groups(y_ref.shape[2], steps))
    y = jnp.concatenate(
        [jnp.concatenate([y_ref[0, j, pl.ds(cm, SUBLANES, stride=SUBLANES), :] for _, cm in groups], axis=0)
         for j in range(nslab)], axis=-1)
    y = _gelu_tanh(y)
    y = y * _sigmoid_gate(_dot(y.astype(BF16), wg_ref[...]) + bg_ref[...])
    t = y * _silu(z_ref[0])
    o = _dot(t.astype(BF16), wo_ref[...])
    o_ref[0] = h_ref[0] + _rms(o, gp_ref[...])


def _s5_post(y_slab, z, h, w_glu, b_glu, w_out, g_post, steps):
    bsz, seq, d = h.shape
    nslab = y_slab.shape[1]
    tm = _row_tile(seq, 4 * 4 * d, 2 * 2 * d * d, SUBLANES * steps)
    assert tm % (SUBLANES * steps) == 0
    row = lambda t: t.reshape(1, -1)
    vec = pl.BlockSpec((1, d), lambda b, m: (0, 0))
    mat = pl.BlockSpec((d, d), lambda b, m: (0, 0))
    slab = pl.BlockSpec((1, nslab, tm, LANES), lambda b, m: (b, 0, m, 0))
    tok = pl.BlockSpec((1, tm, d), lambda b, m: (b, m, 0))
    return pl.pallas_call(
        functools.partial(_s5_post_kernel, steps=steps),
        out_shape=jax.ShapeDtypeStruct(h.shape, F32),
        grid=(bsz, seq // tm),
        in_specs=[slab, tok, tok, mat, vec, mat, vec],
        out_specs=tok,
        compiler_params=_cparams(("parallel", "parallel")),
        name="s5_post",
    )(y_slab, z, h, w_glu.astype(BF16), row(b_glu), w_out.astype(BF16), row(g_post))


def _s5_weights(lam_re, lam_im, log_dt, b_re, b_im, c_re, c_im, d_skip):
    nlayer, ngroup, nstate, gsize = b_re.shape
    ng = nlayer * ngroup
    ntile = ng // S5_TILE_GROUPS
    tg = S5_TILE_GROUPS
    col = lambda t: t.reshape(ng * nstate, 1)
    ldt = log_dt.reshape(ng, 1)
    grp = lambda w: pl.BlockSpec((tg, w), lambda i: (i, 0))
    rows = lambda r, w: pl.BlockSpec((r, w), lambda i: (i, 0))
    ar, ai, wb, wc = pl.pallas_call(
        _s5_tiles_kernel,
        out_shape=[jax.ShapeDtypeStruct((ng, nstate), F32), jax.ShapeDtypeStruct((ng, nstate), F32),
                   jax.ShapeDtypeStruct((ntile, S5_TILE_CH, 2 * S5_TILE_ST), BF16),
                   jax.ShapeDtypeStruct((ntile, 2 * S5_TILE_ST, S5_TILE_CH), BF16)],
        grid=(ntile,),
        in_specs=[grp(nstate), grp(nstate), grp(1),
                  rows(tg * nstate, 1), rows(tg * nstate, 1), rows(tg * nstate, 1),
                  rows(tg * nstate, gsize), rows(tg * nstate, gsize),
                  rows(tg * gsize, nstate), rows(tg * gsize, nstate)],
        out_specs=[grp(nstate), grp(nstate),
                   pl.BlockSpec((1, S5_TILE_CH, 2 * S5_TILE_ST), lambda i: (i, 0, 0)),
                   pl.BlockSpec((1, 2 * S5_TILE_ST, S5_TILE_CH), lambda i: (i, 0, 0))],
        compiler_params=_cparams(("parallel",)),
        name="s5_tiles",
    )(lam_re.reshape(ng, nstate), lam_im.reshape(ng, nstate), ldt,
      col(lam_re), col(lam_im), col(jnp.broadcast_to(ldt, (ng, nstate))),
      b_re.reshape(ng * nstate, gsize), b_im.reshape(ng * nstate, gsize),
      c_re.reshape(ng * gsize, nstate), c_im.reshape(ng * gsize, nstate))
    a_tiles = jnp.stack([ar.reshape(ntile, S5_TILE_ST), ai.reshape(ntile, S5_TILE_ST)], axis=1)
    d_tiles = d_skip.reshape(ntile, 1, S5_TILE_CH)
    return wb, wc, a_tiles, d_tiles


def _s5_layer(h, g_pre, g_post, w_in, weights, layer, w_glu, b_glu, w_out):
    bsz, seq, d = h.shape
    steps = min(S5_STEPS, seq // SUBLANES)
    u_slab, z = _norm_proj(h, g_pre, w_in.astype(BF16), (d, d), (steps, 0), (F32, F32))
    y_slab = _s5_core(u_slab, *weights, layer, steps)
    return _s5_post(y_slab, z, h, w_glu, b_glu, w_out, g_post, steps)


def _gate_out_math(o, z, h, g_head, w_out, g_post, head_norm):
    if head_norm:
        parts = []
        for j in range(o.shape[-1] // HG_HEAD):
            oj = o[:, j * HG_HEAD:(j + 1) * HG_HEAD]
            parts.append(oj * lax.rsqrt(jnp.mean(oj * oj, axis=-1, keepdims=True) + NORM_EPS))
        o = jnp.concatenate(parts, axis=-1) * g_head
    t = o * _silu(z)
    return h + _rms(_dot(t.astype(BF16), w_out), g_post)


def _chunk_cumsum(x, chunk):
    rows, w = x.shape
    per = chunk // SUBLANES
    x3 = x.reshape(rows // SUBLANES, SUBLANES, w)
    pos = lax.broadcasted_iota(jnp.int32, x3.shape, 1)
    sh = 1
    while sh < SUBLANES:
        x3 = x3 + jnp.where(pos >= sh, pltpu.roll(x3, sh, 1), 0.0)
        sh *= 2
    x4 = x3.reshape(rows // chunk, per, SUBLANES, w)
    tot = x4[:, :, SUBLANES - 1:SUBLANES, :]
    pref = [jnp.zeros_like(tot[:, 0:1])]
    for j in range(1, per):
        pref.append(pref[-1] + tot[:, j - 1:j])
    return (x4 + jnp.concatenate(pref, axis=1)).reshape(rows, w)


def _block_mid_rows(x, blk):
    rows, w = x.shape
    half = blk // 2
    if blk >= SUBLANES:
        x3 = x.reshape(rows // blk, blk, w)
        return jnp.broadcast_to(x3[:, half - 1:half, :], (rows // blk, blk, w)).reshape(rows, w)
    pos = lax.broadcasted_iota(jnp.int32, (rows, w), 0) % blk
    out = x
    for p in range(blk):
        d = p - (half - 1)
        if d != 0:
            out = jnp.where(pos == p, pltpu.roll(x, d % rows, 0), out)
    return out


def _hg_proj_kernel(h_ref, g_ref, w_ref, lbl_ref, q_ref, kk_ref, b_ref, v_ref, z_ref, dec_ref, *, layer, chunk):
    width = q_ref.shape[2]
    rows = q_ref.shape[1]
    y = _rms(h_ref[0], g_ref[...]).astype(BF16)

    lg = lbl_ref[...]
    ex = jnp.exp(lg - jnp.max(lg, axis=0, keepdims=True))
    p = ex / jnp.sum(ex, axis=0, keepdims=True)
    lb = jnp.zeros((1, width), F32)
    for j in range(1, layer + 1):
        lb = lb + p[j:j + 1, :]

    tile = 2 * LANES
    for c0 in range(0, width, tile):
        cs = slice(c0, c0 + tile)
        fz = _dot(y, w_ref[:, width + c0:width + c0 + tile])
        q_ref[0, :, cs] = _dot(y, w_ref[:, c0:c0 + tile])
        v_ref[0, :, cs] = _dot(y, w_ref[:, 2 * width + c0:2 * width + c0 + tile]).astype(v_ref.dtype)
        z_ref[0, :, cs] = _dot(y, w_ref[:, 3 * width + c0:3 * width + c0 + tile])
        f = lb[:, cs] + (1.0 - lb[:, cs]) * _sigmoid(fz)
        kk_ref[0, :, cs] = 1.0 - f
        b = _chunk_cumsum(jnp.log(f), chunk)
        b_ref[0, :, cs] = b
        dec_ref[0, :, cs] = -b.reshape(rows // chunk, chunk, tile)[:, chunk - 1, :]


def _hg_proj(h, g, w_bf16, lb_logits, layer, chunk):
    bsz, seq, d = h.shape
    width = w_bf16.shape[1] // 4
    tm = _row_tile(seq, 4 * d + (4 * 4 + 2) * width, w_bf16.size * 2, SUBLANES * chunk)
    tok = pl.BlockSpec((1, tm, width), lambda b, m: (b, m, 0))
    const = lambda shp: pl.BlockSpec(shp, lambda b, m: (0, 0))
    act = lambda t: jax.ShapeDtypeStruct((bsz, seq, width), t)
    return pl.pallas_call(
        functools.partial(_hg_proj_kernel, layer=layer, chunk=chunk),
        out_shape=[act(F32), act(F32), act(F32), act(BF16), act(F32),
                   jax.ShapeDtypeStruct((bsz, seq // chunk, width), F32)],
        grid=(bsz, seq // tm),
        in_specs=[pl.BlockSpec((1, tm, d), lambda b, m: (b, m, 0)), const((1, d)), const(w_bf16.shape),
                  const(lb_logits.shape)],
        out_specs=[tok, tok, tok, tok, tok, pl.BlockSpec((1, tm // chunk, width), lambda b, m: (b, m, 0))],
        compiler_params=_cparams(("parallel", "parallel")),
        name="hgrn2_proj",
    )(h, g.reshape(1, d), w_bf16, lb_logits)


def _hg_core_kernel(q_ref, kk_ref, b_ref, v_ref, dec_ref, z_ref, h_ref, gn_ref, wo_ref, gp_ref, out_ref,
                    xl_ref, qe_ref, kd_ref, g_ref, st_ref, o_scr, *, chunk):
    rows, width = q_ref.shape[1], q_ref.shape[2]
    pair = 2 * HG_HEAD
    blocks = [2 ** i for i in range(1, int(math.log2(chunk)) + 1)]

    @pl.when(pl.program_id(1) == 0)
    def _():
        st_ref[...] = jnp.zeros_like(st_ref)

    def state_operands():
        q = q_ref[0]
        kk = kk_ref[0]
        b = b_ref[0]
        b3 = b.reshape(rows // chunk, chunk, width)
        b_last = b3[:, chunk - 1:chunk, :]
        g_ref[...] = jnp.exp(b_last.reshape(rows // chunk, width))
        qe_ref[...] = (q * jnp.exp(b)).astype(BF16)
        kd_ref[...] = (kk * jnp.exp(jnp.broadcast_to(b_last, b3.shape).reshape(rows, width) - b)).astype(BF16)
        return q, kk, b

    t_i = lax.broadcasted_iota(jnp.int32, (chunk, HG_HEAD), 0)
    s_i = lax.broadcasted_iota(jnp.int32, (chunk, HG_HEAD), 1) % chunk
    lo = lax.broadcasted_iota(jnp.int32, (chunk, pair), 1) < HG_HEAD
    bd = (lax.broadcasted_iota(jnp.int32, (pair, pair), 0) // HG_HEAD
          == lax.broadcasted_iota(jnp.int32, (pair, pair), 1) // HG_HEAD)

    def split_heads(t):
        z = jnp.zeros_like(t)
        return jnp.concatenate([jnp.where(lo, t, z), jnp.where(lo, z, t)], axis=0)

    def chunk_pairs():
        for c in range(rows // chunk):
            for pr in range(width // pair):
                yield c, pr, slice(c * chunk, (c + 1) * chunk), slice(pr * pair, (pr + 1) * pair)

    def finish(c, pr, rs, ls, sc):
        vpair = v_ref[0, rs, ls]
        st = st_ref[pr]
        o_scr[rs, ls] = _dot(sc.astype(BF16), split_heads(vpair)) + _dot_nt(qe_ref[rs, ls], st.astype(BF16))
        st_ref[pr] = g_ref[c:c + 1, ls] * st + jnp.where(bd, _dot_tn(vpair, kd_ref[rs, ls]), 0.0)

    mild = jnp.max(dec_ref[0]) < HG_MILD_DECAY

    @pl.when(mild)
    def _():
        _, kk, b = state_operands()
        xl_ref[0] = (kk * jnp.exp(-b)).astype(BF16)
        causal = s_i <= t_i
        for c, pr, rs, ls in chunk_pairs():
            sc = jnp.where(causal, _dot_nt(qe_ref[rs, ls], split_heads(xl_ref[0, rs, ls])), 0.0)
            finish(c, pr, rs, ls, sc)

    @pl.when(jnp.logical_not(mild))
    def _():
        q, kk, b = state_operands()
        xl_ref[0] = (q * kk).astype(BF16)
        pos = lax.broadcasted_iota(jnp.int32, (rows, width), 0)
        for li, blk in enumerate(blocks):
            w = jnp.exp(-jnp.abs(b - _block_mid_rows(b, blk)))
            xl_ref[li + 1] = (jnp.where(pos % blk >= blk // 2, q, kk) * w).astype(BF16)
        eye = t_i == s_i
        lmask = [((t_i // blk) == (s_i // blk)) & ((t_i % blk) >= blk // 2) & ((s_i % blk) < blk // 2)
                 for blk in blocks]
        ones_bd = jnp.where(lax.broadcasted_iota(jnp.int32, (pair, HG_HEAD), 0) // HG_HEAD
                            == lax.broadcasted_iota(jnp.int32, (pair, HG_HEAD), 1) // chunk, 1.0, 0.0).astype(BF16)
        for c, pr, rs, ls in chunk_pairs():
            sc = jnp.where(eye, _dot(xl_ref[0, rs, ls], ones_bd), 0.0)
            for li in range(len(blocks)):
                xl = xl_ref[li + 1, rs, ls]
                sc = jnp.where(lmask[li], _dot_nt(xl, split_heads(xl)), sc)
            finish(c, pr, rs, ls, sc)

    out_ref[0] = _gate_out_math(o_scr[...], z_ref[0], h_ref[0], gn_ref[...], wo_ref[...], gp_ref[...], True)


def _hg_core(q, kk, b, v, dec, z, h, norm_g, w_out, g_post, chunk):
    bsz, seq, width = q.shape
    blk = min(HG_BLOCK, seq)
    nlev = int(math.log2(chunk))
    assert HG_HEAD == 2 * chunk and width % (2 * HG_HEAD) == 0 and blk % chunk == 0
    tok = pl.BlockSpec((1, blk, width), lambda b_, n: (b_, n, 0))
    const = lambda shp: pl.BlockSpec(shp, lambda b_, n: (0, 0))
    row = lambda t: t.reshape(1, -1)
    return pl.pallas_call(
        functools.partial(_hg_core_kernel, chunk=chunk),
        out_shape=jax.ShapeDtypeStruct(h.shape, F32),
        grid=(bsz, seq // blk),
        in_specs=[tok, tok, tok, tok, pl.BlockSpec((1, blk // chunk, width), lambda b_, n: (b_, n, 0)),
                  tok, tok, const((1, width)), const(w_out.shape), const((1, width))],
        out_specs=tok,
        scratch_shapes=[pltpu.VMEM((nlev + 1, blk, width), BF16),
                        pltpu.VMEM((blk, width), BF16),
                        pltpu.VMEM((blk, width), BF16),
                        pltpu.VMEM((blk // chunk, width), F32),
                        pltpu.VMEM((width // (2 * HG_HEAD), 2 * HG_HEAD, 2 * HG_HEAD), F32),
                        pltpu.VMEM((blk, width), F32)],
        compiler_params=_cparams(("parallel", "arbitrary")),
        name="hgrn2_core",
    )(q, kk, b, v, dec, z, h, row(norm_g), w_out.astype(BF16), row(g_post))


def _hg_layer(h, g_pre, g_post, w_in, lb_logits, layer, norm_g, w_out):
    chunk = min(HG_CHUNK, h.shape[1])
    q, kk, b, v, z, dec = _hg_proj(h, g_pre, w_in.astype(BF16), lb_logits, layer, chunk)
    return _hg_core(q, kk, b, v, dec, z, h, norm_g, w_out, g_post, chunk)


def _at_proj_kernel(h_ref, g_ref, w_ref, bias_ref, pos_ref, invf_ref, q_ref, k_ref, v_ref, z_ref,
                    *, qw, kvw):
    y = _rms(h_ref[0], g_ref[...]).astype(BF16)
    half = AT_HEAD // 2
    nq4 = LANES // half
    ang = pos_ref[0].astype(F32) * invf_ref[...]
    cos4 = jnp.cos(ang)
    sin4 = jnp.sin(ang)
    lane4 = lax.broadcasted_iota(jnp.int32, ang.shape, 1)

    def spread(t):
        parts = []
        for j in range(nq4):
            m = jnp.where(lane4 // half == j, t, 0.0)
            x = m
            for s in range(1, nq4):
                x = x + pltpu.roll(m, s * half, 1)
            parts.append(x)
        return jnp.concatenate(parts, axis=0)

    lane = lax.broadcasted_iota(jnp.int32, (h_ref.shape[1], LANES), 1)
    first = (lane % AT_HEAD) < half
    low = lane < AT_HEAD
    cosf = spread(cos4)
    sinf = spread(sin4)
    sinf = jnp.where(first, -sinf, sinf)

    def rope(t):
        partner = jnp.where(first, pltpu.roll(t, LANES - AT_HEAD // 2, 1), pltpu.roll(t, AT_HEAD // 2, 1))
        return t * cosf + partner * sinf

    def store_dup(ref, j, t):
        r = pltpu.roll(t, AT_HEAD, 1)
        ref[0, :, (2 * j) * LANES:(2 * j + 1) * LANES] = jnp.where(low, t, r).astype(ref.dtype)
        ref[0, :, (2 * j + 1) * LANES:(2 * j + 2) * LANES] = jnp.where(low, r, t).astype(ref.dtype)

    qkv_w = qw + 2 * kvw
    z_ref[0] = _dot(y, w_ref[:, qkv_w:qkv_w + qw])
    qf = _dot(y, w_ref[:, 0:qw]) + bias_ref[:, 0:qw]
    kvf = _dot(y, w_ref[:, qw:qkv_w]) + bias_ref[:, qw:qkv_w]
    for j in range(qw // LANES):
        sl = slice(j * LANES, (j + 1) * LANES)
        q_ref[0, :, sl] = (rope(qf[:, sl]) * AT_HEAD ** -0.5).astype(q_ref.dtype)
    for j in range(kvw // LANES):
        store_dup(k_ref, j, rope(kvf[:, j * LANES:(j + 1) * LANES]))
        store_dup(v_ref, j, kvf[:, kvw + j * LANES:kvw + (j + 1) * LANES])


def _at_proj(h, g, w_bf16, bias, positions, qw, kvw):
    bsz, seq, d = h.shape
    tm = _row_tile(seq, 4 * d + 2 * qw + 4 * 2 * kvw + 4 * qw, w_bf16.size * 2, 4 * SUBLANES)
    half = AT_HEAD // 2
    inv_freq = ROPE_THETA ** (-jnp.arange(0, AT_HEAD, 2, dtype=F32) / AT_HEAD)
    nq4 = LANES // half
    invf = jnp.tile(inv_freq, nq4).reshape(1, LANES)
    pos4 = positions.reshape(bsz, seq // tm, nq4, tm // nq4).transpose(0, 1, 3, 2)
    pos4 = jnp.repeat(pos4, half, axis=-1).reshape(bsz, seq // nq4, LANES)
    tokspec = lambda w: pl.BlockSpec((1, tm, w), lambda b, m: (b, m, 0))
    const = lambda shp: pl.BlockSpec(shp, lambda b, m: (0, 0))
    return pl.pallas_call(
        functools.partial(_at_proj_kernel, qw=qw, kvw=kvw),
        out_shape=[jax.ShapeDtypeStruct((bsz, seq, qw), BF16), jax.ShapeDtypeStruct((bsz, seq, 2 * kvw), BF16),
                   jax.ShapeDtypeStruct((bsz, seq, 2 * kvw), BF16), jax.ShapeDtypeStruct((bsz, seq, qw), F32)],
        grid=(bsz, seq // tm),
        in_specs=[tokspec(d), const((1, d)), const(w_bf16.shape), const((1, qw + 2 * kvw)),
                  pl.BlockSpec((1, tm // nq4, LANES), lambda b, m: (b, m, 0)), const((1, LANES))],
        out_specs=[tokspec(qw), tokspec(2 * kvw), tokspec(2 * kvw), tokspec(qw)],
        compiler_params=_cparams(("parallel", "parallel")),
        name="attn_proj",
    )(h, g.reshape(1, d), w_bf16, bias.reshape(1, -1), pos4, invf)


def _at_core_kernel(q_ref, kc_ref, kp_ref, vc_ref, vp_ref, sink_ref, z_ref, h_ref, wo_ref, gp_ref, out_ref):
    blk = AT_BLOCK
    nsub = q_ref.shape[1] // blk
    n = pl.program_id(1)
    ngroup = kc_ref.shape[2] // LANES
    qi = lax.broadcasted_iota(jnp.int32, (2 * blk, 4 * blk), 0) % blk
    kj = lax.broadcasted_iota(jnp.int32, (2 * blk, 4 * blk), 1) % (2 * blk)
    dist = qi + blk - kj
    in_window = (dist >= 0) & (dist < blk)
    first_mask = in_window & ((kj >= blk) | (n > 0))
    low = lax.broadcasted_iota(jnp.int32, (2 * blk, LANES), 1) < AT_HEAD
    top = lax.broadcasted_iota(jnp.int32, (2 * blk, 1), 0) < blk
    ones_lo = jnp.where(low, 1.0, 0.0).astype(BF16)
    ones_hi = jnp.where(low, 0.0, 1.0).astype(BF16)
    items = [(j, g) for j in range(nsub) for g in range(ngroup)]

    def band(cur_ref, prev_ref, j, g):
        sl = slice(g * LANES, (g + 1) * LANES)
        prev = prev_ref[0, :, sl] if j == 0 else cur_ref[0, (j - 1) * blk:j * blk, sl]
        return jnp.concatenate([prev, cur_ref[0, j * blk:(j + 1) * blk, sl]], axis=0)

    o_rows = [[] for _ in range(nsub)]
    for w0 in range(0, len(items), AT_WAVE):
        wave = items[w0:w0 + AT_WAVE]
        scores = []
        for j, g in wave:
            kdup = band(kc_ref, kp_ref, j, g)
            rhs = jnp.concatenate([jnp.where(low, kdup, 0.0), jnp.where(low, 0.0, kdup)], axis=0)
            rows = slice(j * blk, (j + 1) * blk)
            lhs = jnp.concatenate([q_ref[0, rows, (2 * g) * LANES:(2 * g + 1) * LANES],
                                   q_ref[0, rows, (2 * g + 1) * LANES:(2 * g + 2) * LANES]], axis=0)
            scores.append(_dot_nt(lhs, rhs))

        probs, sink_terms = [], []
        for (j, g), sc in zip(wave, scores):
            s = jnp.where(first_mask if j == 0 else in_window, sc, -jnp.inf)
            sink_a = jnp.where(top, sink_ref[0:1, 4 * g:4 * g + 1], sink_ref[0:1, 4 * g + 2:4 * g + 3])
            sink_b = jnp.where(top, sink_ref[0:1, 4 * g + 1:4 * g + 2], sink_ref[0:1, 4 * g + 3:4 * g + 4])
            m_a = jnp.maximum(jnp.max(s[:, :2 * blk], axis=-1, keepdims=True), sink_a)
            m_b = jnp.maximum(jnp.max(s[:, 2 * blk:], axis=-1, keepdims=True), sink_b)
            e = jnp.concatenate([jnp.exp(s[:, :2 * blk] - m_a), jnp.exp(s[:, 2 * blk:] - m_b)], axis=1)
            probs.append(e.astype(BF16))
            sink_terms.append(jnp.where(low, jnp.exp(sink_a - m_a), jnp.exp(sink_b - m_b)))

        for (j, g), p, st in zip(wave, probs, sink_terms):
            vdup = band(vc_ref, vp_ref, j, g)
            va = jnp.where(low, vdup, 0.0)
            vb = jnp.where(low, 0.0, vdup)
            rhs = jnp.concatenate([jnp.concatenate([va, ones_lo], axis=1),
                                   jnp.concatenate([vb, ones_hi], axis=1)], axis=0)
            out = _dot(p, rhs)
            o = out[:, :LANES] / (out[:, LANES:] + st)
            o_rows[j] += [o[:blk], o[blk:]]

    o = jnp.concatenate([jnp.concatenate(slabs, axis=-1) for slabs in o_rows], axis=0)
    out_ref[0] = _gate_out_math(o, z_ref[0], h_ref[0], None, wo_ref[...], gp_ref[...], False)


def _at_core(q, k_dup, v_dup, sinks, z, h, w_out, g_post):
    bsz, seq, qw = q.shape
    kvw = k_dup.shape[-1]
    d = h.shape[-1]
    blk = AT_BLOCK
    nsub = AT_SUBBLOCKS if seq % (AT_SUBBLOCKS * blk) == 0 else 1
    nq = qw // AT_HEAD
    assert nq == 4 * (kvw // LANES) and qw == 2 * kvw
    cur = lambda w: pl.BlockSpec((1, nsub * blk, w), lambda b, n: (b, n, 0))
    prev = lambda w: pl.BlockSpec((1, blk, w), lambda b, n: (b, jnp.maximum(nsub * n - 1, 0), 0))
    const = lambda shp: pl.BlockSpec(shp, lambda b, n: (0, 0))
    return pl.pallas_call(
        _at_core_kernel,
        out_shape=jax.ShapeDtypeStruct(h.shape, F32),
        grid=(bsz, seq // (nsub * blk)),
        in_specs=[cur(qw), cur(kvw), prev(kvw), cur(kvw), prev(kvw), const((1, nq)),
                  cur(qw), cur(d), const(w_out.shape), const((1, d))],
        out_specs=cur(d),
        compiler_params=_cparams(("parallel", "arbitrary")),
        name="attn_core",
    )(q, k_dup, k_dup, v_dup, v_dup, sinks.reshape(1, nq), z, h, w_out.astype(BF16), g_post.reshape(1, d))


def _at_layer(h, positions, g_pre, g_post, w_in, b_in, sinks, w_out):
    qw = w_out.shape[0]
    kvw = (b_in.shape[0] - qw) // 2
    q, k, v, z = _at_proj(h, g_pre, w_in.astype(BF16), b_in, positions, qw, kvw)
    return _at_core(q, k, v, sinks, z, h, w_out, g_post)


def kernel(x, positions, norm_pre, norm_post, s5_w_in, s5_lambda_re, s5_lambda_im, s5_log_dt, s5_b_re, s5_b_im, s5_c_re, s5_c_im, s5_d, s5_w_glu, s5_b_glu, s5_w_out, hg_w_in, hg_lb_logits, hg_norm, hg_w_out, at_w_in, at_b_in, at_sinks, at_w_out):
    depth = norm_pre.shape[0]
    s5_weights = _s5_weights(s5_lambda_re, s5_lambda_im, s5_log_dt, s5_b_re, s5_b_im, s5_c_re, s5_c_im, s5_d)
    h = x
    for i in range(depth):
        kind, j = i % 3, i // 3
        if kind == 0:
            h = _s5_layer(h, norm_pre[i], norm_post[i], s5_w_in[j], s5_weights, j, s5_w_glu[j], s5_b_glu[j],
                          s5_w_out[j])
        elif kind == 1:
            h = _hg_layer(h, norm_pre[i], norm_post[i], hg_w_in[j], hg_lb_logits, i, hg_norm[j], hg_w_out[j])
        else:
            h = _at_layer(h, positions, norm_pre[i], norm_post[i], at_w_in[j], at_b_in[j], at_sinks[j],
                          at_w_out[j])
    return h
```

```python
import functools
import math

import jax
import jax.numpy as jnp
from jax import lax
from jax.experimental import pallas as pl
from jax.experimental.pallas import tpu as pltpu

F32 = jnp.float32
BF16 = jnp.bfloat16

NORM_EPS = 1e-6
LANES = 128
SUBLANES = 8
VMEM_LIMIT = 56 * 1024 * 1024

S5_GROUP = 16
S5_STATE = 64
S5_TILE_GROUPS = 16
S5_TILE_CH = S5_TILE_GROUPS * S5_GROUP
S5_TILE_ST = S5_TILE_GROUPS * S5_STATE
S5_STEPS = 64
HG_HEAD = 128
HG_CHUNK = 64
HG_BLOCK = 512
HG_MILD_DECAY = 80.0
AT_HEAD = 64
AT_BLOCK = 128
AT_SUBBLOCKS = 4
AT_WAVE = 16
ROPE_THETA = 10000.0

ROW_TILES = (1024, 512, 256)


def _cparams(sem):
    return pltpu.CompilerParams(dimension_semantics=sem, vmem_limit_bytes=VMEM_LIMIT)


def _row_tile(seq, row_bytes, resident_bytes, multiple=SUBLANES):
    budget = VMEM_LIMIT - VMEM_LIMIT // 4
    for tm in ROW_TILES:
        if seq % tm == 0 and tm % multiple == 0 and 2 * (tm * row_bytes + resident_bytes) <= budget:
            return tm
    return seq


def _rms(x, g):
    return x * lax.rsqrt(jnp.mean(x * x, axis=-1, keepdims=True) + NORM_EPS) * g


def _sigmoid(x):
    return 1.0 / (1.0 + jnp.exp(-x))


def _sigmoid_gate(x):
    return 0.5 + 0.5 * jnp.tanh(0.5 * x)


def _silu(x):
    return x * _sigmoid_gate(x)


def _gelu_tanh(x):
    c = math.sqrt(2.0 / math.pi)
    return 0.5 * x * (1.0 + jnp.tanh(c * (x + 0.044715 * (x * x * x))))


def _dot(a, b):
    return jnp.dot(a, b, preferred_element_type=F32)


def _dot_nt(a, b):
    return lax.dot_general(a, b, (((1,), (1,)), ((), ())), preferred_element_type=F32)


def _dot_tn(a, b):
    return lax.dot_general(a, b, (((0,), (0,)), ((), ())), preferred_element_type=F32)


def _chunk_row_groups(rows, steps):
    for base in range(0, rows, SUBLANES * steps):
        for r in range(SUBLANES):
            for k0 in range(0, steps, SUBLANES):
                yield base + r * steps + k0, base + k0 * SUBLANES + r


def _norm_proj_kernel(h_ref, g_ref, w_ref, *out_refs, widths, slab_steps):
    y = _rms(h_ref[0], g_ref[...]).astype(BF16)
    off = 0
    for o_ref, width, steps in zip(out_refs, widths, slab_steps):
        r = _dot(y, w_ref[:, off:off + width])
        if steps:
            for nat, cm in _chunk_row_groups(r.shape[0], steps):
                for j in range(width // LANES):
                    o_ref[0, j, pl.ds(cm, SUBLANES, stride=SUBLANES), :] = (
                        r[nat:nat + SUBLANES, j * LANES:(j + 1) * LANES])
        else:
            o_ref[0] = r.astype(o_ref.dtype)
        off += width


def _norm_proj(h, g, w_bf16, widths, slab_steps, dtypes):
    bsz, seq, d = h.shape
    row_bytes = 4 * d + sum(w * jnp.dtype(t).itemsize for w, t in zip(widths, dtypes))
    tm = _row_tile(seq, row_bytes, w_bf16.size * 2, SUBLANES * max(max(slab_steps), 1))
    out_shape, out_specs = [], []
    for width, as_slab, dtype in zip(widths, slab_steps, dtypes):
        if as_slab:
            assert tm % (SUBLANES * as_slab) == 0 and dtype == F32
            ns = width // LANES
            out_shape.append(jax.ShapeDtypeStruct((bsz, ns, seq, LANES), F32))
            out_specs.append(pl.BlockSpec((1, ns, tm, LANES), lambda b, m: (b, 0, m, 0)))
        else:
            out_shape.append(jax.ShapeDtypeStruct((bsz, seq, width), dtype))
            out_specs.append(pl.BlockSpec((1, tm, width), lambda b, m: (b, m, 0)))
    return pl.pallas_call(
        functools.partial(_norm_proj_kernel, widths=tuple(widths), slab_steps=tuple(slab_steps)),
        out_shape=out_shape,
        grid=(bsz, seq // tm),
        in_specs=[pl.BlockSpec((1, tm, d), lambda b, m: (b, m, 0)),
                  pl.BlockSpec((1, d), lambda b, m: (0, 0)),
                  pl.BlockSpec(w_bf16.shape, lambda b, m: (0, 0))],
        out_specs=out_specs,
        compiler_params=_cparams(("parallel", "parallel")),
        name="norm_proj",
    )(h, g.reshape(1, d), w_bf16)


def _zoh(lr, li, ldt):
    dt = jnp.exp(ldt)
    mag = jnp.exp(lr * dt)
    ar = mag * jnp.cos(li * dt)
    ai = mag * jnp.sin(li * dt)
    den = lr * lr + li * li
    qr = ((ar - 1.0) * lr + ai * li) / den
    qi = (ai * lr - (ar - 1.0) * li) / den
    return ar, ai, qr, qi


def _s5_tiles_kernel(lr_ref, li_ref, ldt_ref, br_ref, bi_ref, cr_ref, ci_ref, ar_ref, ai_ref, wb_ref, wc_ref):
    ch, st, tg = S5_TILE_CH, S5_TILE_ST, S5_TILE_GROUPS
    ar, ai, qr, qi = _zoh(lr_ref[...], li_ref[...], ldt_ref[...])
    ar_ref[...] = ar
    ai_ref[...] = ai

    sel_p = jnp.where(lax.broadcasted_iota(jnp.int32, (st, S5_STATE), 0) % S5_STATE
                      == lax.broadcasted_iota(jnp.int32, (st, S5_STATE), 1), 1.0, 0.0).astype(BF16)
    own_group = (lax.broadcasted_iota(jnp.int32, (st, tg), 0) // S5_STATE
                 == lax.broadcasted_iota(jnp.int32, (st, tg), 1))

    def to_rows(m):
        hi = m.astype(BF16)
        r1 = m - hi.astype(F32)
        mid = r1.astype(BF16)
        lo = (r1 - mid.astype(F32)).astype(BF16)
        spread = _dot_nt(sel_p, hi) + _dot_nt(sel_p, mid) + _dot_nt(sel_p, lo)
        return jnp.sum(jnp.where(own_group, spread, 0.0), axis=1, keepdims=True)

    qr = to_rows(qr)
    qi = to_rows(qi)
    br = br_ref[...]
    bi = bi_ref[...]
    bbr = (qr * br - qi * bi).astype(BF16)
    bbi = (qr * bi + qi * br).astype(BF16)

    sel_h = jnp.where(lax.broadcasted_iota(jnp.int32, (ch, S5_GROUP), 0) % S5_GROUP
                      == lax.broadcasted_iota(jnp.int32, (ch, S5_GROUP), 1), 1.0, 0.0).astype(BF16)
    diag_in = (lax.broadcasted_iota(jnp.int32, (ch, st), 0) // S5_GROUP
               == lax.broadcasted_iota(jnp.int32, (ch, st), 1) // S5_STATE)
    wb_ref[0, :, 0:st] = jnp.where(diag_in, _dot_nt(sel_h, bbr), 0.0).astype(BF16)
    wb_ref[0, :, st:2 * st] = jnp.where(diag_in, _dot_nt(sel_h, bbi), 0.0).astype(BF16)
    diag_out = (lax.broadcasted_iota(jnp.int32, (st, ch), 0) // S5_STATE
                == lax.broadcasted_iota(jnp.int32, (st, ch), 1) // S5_GROUP)
    wc_ref[0, 0:st, :] = jnp.where(diag_out, _dot_nt(sel_p, cr_ref[...].astype(BF16)), 0.0).astype(BF16)
    wc_ref[0, st:2 * st, :] = jnp.where(diag_out, -_dot_nt(sel_p, ci_ref[...].astype(BF16)), 0.0).astype(BF16)


def _s5_core_kernel(u_ref, wb_ref, wc_ref, a_ref, d_ref, y_ref, x_ref, s_ref, carry_ref, *, steps):
    nst = S5_TILE_ST
    nslab = S5_TILE_CH // LANES
    ntile = wb_ref.shape[0]

    @pl.when(pl.program_id(1) == 0)
    def _():
        carry_ref[...] = jnp.zeros_like(carry_ref)

    def load_u(i):
        return jnp.concatenate([u_ref[0, i * nslab + j] for j in range(nslab)], axis=-1)

    def project_in(i):
        x_ref[i % 2] = _dot(load_u(i).astype(BF16), wb_ref[i])

    def scan(i):
        slot = i % 2
        ar1 = a_ref[i, 0:1, :]
        ai1 = a_ref[i, 1:2, :]
        ar = jnp.broadcast_to(ar1, (SUBLANES, nst))
        ai = jnp.broadcast_to(ai1, (SUBLANES, nst))

        def advance(k, sr, si):
            xr = x_ref[slot, k * SUBLANES:(k + 1) * SUBLANES, 0:nst]
            xi = x_ref[slot, k * SUBLANES:(k + 1) * SUBLANES, nst:2 * nst]
            return ar * sr - ai * si + xr, ar * si + ai * sr + xi

        er = jnp.zeros((SUBLANES, nst), F32)
        ei = jnp.zeros((SUBLANES, nst), F32)
        for k in range(steps):
            er, ei = advance(k, er, ei)

        pr, pi = ar1, ai1
        for _ in range(int(math.log2(steps))):
            pr, pi = pr * pr - pi * pi, 2.0 * pr * pi

        rid = lax.broadcasted_iota(jnp.int32, (SUBLANES, nst), 0)
        cr = carry_ref[i, 0:1, :]
        ci = carry_ref[i, 1:2, :]
        sr = jnp.zeros((SUBLANES, nst), F32)
        si = jnp.zeros((SUBLANES, nst), F32)
        for r in range(SUBLANES):
            sr = jnp.where(rid == r, jnp.broadcast_to(cr, (SUBLANES, nst)), sr)
            si = jnp.where(rid == r, jnp.broadcast_to(ci, (SUBLANES, nst)), si)
            cr, ci = pr * cr - pi * ci + er[r:r + 1, :], pr * ci + pi * cr + ei[r:r + 1, :]
        carry_ref[i, 0:1, :] = cr
        carry_ref[i, 1:2, :] = ci

        for k in range(0, steps, 2):
            r0, i0 = advance(k, sr, si)
            sr, si = advance(k + 1, r0, i0)
            rows2 = slice(k * SUBLANES, (k + 2) * SUBLANES)
            s_ref[slot, rows2, 0:nst] = jnp.concatenate([r0, sr], axis=0).astype(BF16)
            s_ref[slot, rows2, nst:2 * nst] = jnp.concatenate([i0, si], axis=0).astype(BF16)

    def project_out(i):
        y = _dot(s_ref[i % 2], wc_ref[i]) + d_ref[i] * load_u(i)
        for j in range(nslab):
            y_ref[0, i * nslab + j] = y[:, j * LANES:(j + 1) * LANES]

    project_in(0)
    for i in range(ntile):
        if i + 1 < ntile:
            project_in(i + 1)
        scan(i)
        project_out(i)


def _s5_core(u_slab, wb, wc, a_tiles, d_tiles, layer, steps):
    bsz, nslab, seq, _ = u_slab.shape
    ntile = nslab * LANES // S5_TILE_CH
    rows = SUBLANES * steps
    const = lambda shp: pl.BlockSpec((ntile,) + shp[1:], lambda b, n: (layer,) + (0,) * (len(shp) - 1))
    tok = pl.BlockSpec((1, nslab, rows, LANES), lambda b, n: (b, 0, n, 0))
    return pl.pallas_call(
        functools.partial(_s5_core_kernel, steps=steps),
        out_shape=jax.ShapeDtypeStruct(u_slab.shape, F32),
        grid=(bsz, seq // rows),
        in_specs=[tok, const(wb.shape), const(wc.shape), const(a_tiles.shape), const(d_tiles.shape)],
        out_specs=tok,
        scratch_shapes=[pltpu.VMEM((2, rows, 2 * S5_TILE_ST), F32),
                        pltpu.VMEM((2, rows, 2 * S5_TILE_ST), BF16),
                        pltpu.VMEM((ntile, 2, S5_TILE_ST), F32)],
        compiler_params=_cparams(("parallel", "arbitrary")),
        name="s5_core",
    )(u_slab, wb, wc, a_tiles, d_tiles)


def _s5_post_kernel(y_ref, z_ref, h_ref, wg_ref, bg_ref, wo_ref, gp_ref, o_ref, *, steps):
    nslab = y_ref.shape[1]
    groups = list(_chunk_row_groups(y_ref.shape[2], steps))
    y = jnp.concatenate(
        [jnp.concatenate([y_ref[0, j, pl.ds(cm, SUBLANES, stride=SUBLANES), :] for _, cm in groups], axis=0)
         for j in range(nslab)], axis=-1)
    y = _gelu_tanh(y)
    y = y * _sigmoid_gate(_dot(y.astype(BF16), wg_ref[...]) + bg_ref[...])
    t = y * _silu(z_ref[0])
    o = _dot(t.astype(BF16), wo_ref[...])
    o_ref[0] = h_ref[0] + _rms(o, gp_ref[...])


def _s5_post(y_slab, z, h, w_glu, b_glu, w_out, g_post, steps):
    bsz, seq, d = h.shape
    nslab = y_slab.shape[1]
    tm = _row_tile(seq, 4 * 4 * d, 2 * 2 * d * d, SUBLANES * steps)
    assert tm % (SUBLANES * steps) == 0
    row = lambda t: t.reshape(1, -1)
    vec = pl.BlockSpec((1, d), lambda b, m: (0, 0))
    mat = pl.BlockSpec((d, d), lambda b, m: (0, 0))
    slab = pl.BlockSpec((1, nslab, tm, LANES), lambda b, m: (b, 0, m, 0))
    tok = pl.BlockSpec((1, tm, d), lambda b, m: (b, m, 0))
    return pl.pallas_call(
        functools.partial(_s5_post_kernel, steps=steps),
        out_shape=jax.ShapeDtypeStruct(h.shape, F32),
        grid=(bsz, seq // tm),
        in_specs=[slab, tok, tok, mat, vec, mat, vec],
        out_specs=tok,
        compiler_params=_cparams(("parallel", "parallel")),
        name="s5_post",
    )(y_slab, z, h, w_glu.astype(BF16), row(b_glu), w_out.astype(BF16), row(g_post))


def _s5_weights(lam_re, lam_im, log_dt, b_re, b_im, c_re, c_im, d_skip):
    nlayer, ngroup, nstate, gsize = b_re.shape
    ng = nlayer * ngroup
    ntile = ng // S5_TILE_GROUPS
    tg = S5_TILE_GROUPS
    ldt = log_dt.reshape(ng, 1)
    grp = lambda w: pl.BlockSpec((tg, w), lambda i: (i, 0))
    rows = lambda r, w: pl.BlockSpec((r, w), lambda i: (i, 0))
    ar, ai, wb, wc = pl.pallas_call(
        _s5_tiles_kernel,
        out_shape=[jax.ShapeDtypeStruct((ng, nstate), F32), jax.ShapeDtypeStruct((ng, nstate), F32),
                   jax.ShapeDtypeStruct((ntile, S5_TILE_CH, 2 * S5_TILE_ST), BF16),
                   jax.ShapeDtypeStruct((ntile, 2 * S5_TILE_ST, S5_TILE_CH), BF16)],
        grid=(ntile,),
        in_specs=[grp(nstate), grp(nstate), grp(1),
                  rows(tg * nstate, gsize), rows(tg * nstate, gsize),
                  rows(tg * gsize, nstate), rows(tg * gsize, nstate)],
        out_specs=[grp(nstate), grp(nstate),
                   pl.BlockSpec((1, S5_TILE_CH, 2 * S5_TILE_ST), lambda i: (i, 0, 0)),
                   pl.BlockSpec((1, 2 * S5_TILE_ST, S5_TILE_CH), lambda i: (i, 0, 0))],
        compiler_params=_cparams(("parallel",)),
        name="s5_tiles",
    )(lam_re.reshape(ng, nstate), lam_im.reshape(ng, nstate), ldt,
      b_re.reshape(ng * nstate, gsize), b_im.reshape(ng * nstate, gsize),
      c_re.reshape(ng * gsize, nstate), c_im.reshape(ng * gsize, nstate))
    a_tiles = jnp.stack([ar.reshape(ntile, S5_TILE_ST), ai.reshape(ntile, S5_TILE_ST)], axis=1)
    d_tiles = d_skip.reshape(ntile, 1, S5_TILE_CH)
    return wb, wc, a_tiles, d_tiles


def _s5_layer(h, g_pre, g_post, w_in, weights, layer, w_glu, b_glu, w_out):
    bsz, seq, d = h.shape
    steps = min(S5_STEPS, seq // SUBLANES)
    u_slab, z = _norm_proj(h, g_pre, w_in.astype(BF16), (d, d), (steps, 0), (F32, F32))
    y_slab = _s5_core(u_slab, *weights, layer, steps)
    return _s5_post(y_slab, z, h, w_glu, b_glu, w_out, g_post, steps)


def _gate_out_math(o, z, h, g_head, w_out, g_post, head_norm):
    if head_norm:
        parts = []
        for j in range(o.shape[-1] // HG_HEAD):
            oj = o[:, j * HG_HEAD:(j + 1) * HG_HEAD]
            parts.append(oj * lax.rsqrt(jnp.mean(oj * oj, axis=-1, keepdims=True) + NORM_EPS))
        o = jnp.concatenate(parts, axis=-1) * g_head
    t = o * _silu(z)
    return h + _rms(_dot(t.astype(BF16), w_out), g_post)


def _chunk_cumsum(x, chunk):
    rows, w = x.shape
    per = chunk // SUBLANES
    x3 = x.reshape(rows // SUBLANES, SUBLANES, w)
    pos = lax.broadcasted_iota(jnp.int32, x3.shape, 1)
    sh = 1
    while sh < SUBLANES:
        x3 = x3 + jnp.where(pos >= sh, pltpu.roll(x3, sh, 1), 0.0)
        sh *= 2
    x4 = x3.reshape(rows // chunk, per, SUBLANES, w)
    tot = x4[:, :, SUBLANES - 1:SUBLANES, :]
    pref = [jnp.zeros_like(tot[:, 0:1])]
    for j in range(1, per):
        pref.append(pref[-1] + tot[:, j - 1:j])
    return (x4 + jnp.concatenate(pref, axis=1)).reshape(rows, w)


def _block_mid_rows(x, blk):
    rows, w = x.shape
    half = blk // 2
    if blk >= SUBLANES:
        x3 = x.reshape(rows // blk, blk, w)
        return jnp.broadcast_to(x3[:, half - 1:half, :], (rows // blk, blk, w)).reshape(rows, w)
    pos = lax.broadcasted_iota(jnp.int32, (rows, w), 0) % blk
    out = x
    for p in range(blk):
        d = p - (half - 1)
        if d != 0:
            out = jnp.where(pos == p, pltpu.roll(x, d % rows, 0), out)
    return out


def _hg_proj_kernel(h_ref, g_ref, w_ref, lbl_ref, q_ref, kk_ref, b_ref, v_ref, z_ref, dec_ref, *, layer, chunk):
    width = q_ref.shape[2]
    rows = q_ref.shape[1]
    y = _rms(h_ref[0], g_ref[...]).astype(BF16)

    lg = lbl_ref[...]
    ex = jnp.exp(lg - jnp.max(lg, axis=0, keepdims=True))
    p = ex / jnp.sum(ex, axis=0, keepdims=True)
    lb = jnp.zeros((1, width), F32)
    for j in range(1, layer + 1):
        lb = lb + p[j:j + 1, :]

    tile = 2 * LANES
    for c0 in range(0, width, tile):
        cs = slice(c0, c0 + tile)
        fz = _dot(y, w_ref[:, width + c0:width + c0 + tile])
        q_ref[0, :, cs] = _dot(y, w_ref[:, c0:c0 + tile])
        v_ref[0, :, cs] = _dot(y, w_ref[:, 2 * width + c0:2 * width + c0 + tile]).astype(v_ref.dtype)
        z_ref[0, :, cs] = _dot(y, w_ref[:, 3 * width + c0:3 * width + c0 + tile])
        f = lb[:, cs] + (1.0 - lb[:, cs]) * _sigmoid(fz)
        kk_ref[0, :, cs] = 1.0 - f
        b = _chunk_cumsum(jnp.log(f), chunk)
        b_ref[0, :, cs] = b
        dec_ref[0, :, cs] = -b.reshape(rows // chunk, chunk, tile)[:, chunk - 1, :]


def _hg_proj(h, g, w_bf16, lb_logits, layer, chunk):
    bsz, seq, d = h.shape
    width = w_bf16.shape[1] // 4
    tm = _row_tile(seq, 4 * d + (4 * 4 + 2) * width, w_bf16.size * 2, SUBLANES * chunk)
    tok = pl.BlockSpec((1, tm, width), lambda b, m: (b, m, 0))
    const = lambda shp: pl.BlockSpec(shp, lambda b, m: (0, 0))
    act = lambda t: jax.ShapeDtypeStruct((bsz, seq, width), t)
    return pl.pallas_call(
        functools.partial(_hg_proj_kernel, layer=layer, chunk=chunk),
        out_shape=[act(F32), act(F32), act(F32), act(BF16), act(F32),
                   jax.ShapeDtypeStruct((bsz, seq // chunk, width), F32)],
        grid=(bsz, seq // tm),
        in_specs=[pl.BlockSpec((1, tm, d), lambda b, m: (b, m, 0)), const((1, d)), const(w_bf16.shape),
                  const(lb_logits.shape)],
        out_specs=[tok, tok, tok, tok, tok, pl.BlockSpec((1, tm // chunk, width), lambda b, m: (b, m, 0))],
        compiler_params=_cparams(("parallel", "parallel")),
        name="hgrn2_proj",
    )(h, g.reshape(1, d), w_bf16, lb_logits)


def _hg_core_kernel(q_ref, kk_ref, b_ref, v_ref, dec_ref, z_ref, h_ref, gn_ref, wo_ref, gp_ref, out_ref,
                    xl_ref, qe_ref, kd_ref, g_ref, st_ref, o_scr, *, chunk):
    rows, width = q_ref.shape[1], q_ref.shape[2]
    pair = 2 * HG_HEAD
    blocks = [2 ** i for i in range(1, int(math.log2(chunk)) + 1)]

    @pl.when(pl.program_id(1) == 0)
    def _():
        st_ref[...] = jnp.zeros_like(st_ref)

    def state_operands():
        q = q_ref[0]
        kk = kk_ref[0]
        b = b_ref[0]
        b3 = b.reshape(rows // chunk, chunk, width)
        b_last = b3[:, chunk - 1:chunk, :]
        g_ref[...] = jnp.exp(b_last.reshape(rows // chunk, width))
        qe_ref[...] = (q * jnp.exp(b)).astype(BF16)
        kd_ref[...] = (kk * jnp.exp(jnp.broadcast_to(b_last, b3.shape).reshape(rows, width) - b)).astype(BF16)
        return q, kk, b

    t_i = lax.broadcasted_iota(jnp.int32, (chunk, HG_HEAD), 0)
    s_i = lax.broadcasted_iota(jnp.int32, (chunk, HG_HEAD), 1) % chunk
    lo = lax.broadcasted_iota(jnp.int32, (chunk, pair), 1) < HG_HEAD
    bd = (lax.broadcasted_iota(jnp.int32, (pair, pair), 0) // HG_HEAD
          == lax.broadcasted_iota(jnp.int32, (pair, pair), 1) // HG_HEAD)

    def split_heads(t):
        z = jnp.zeros_like(t)
        return jnp.concatenate([jnp.where(lo, t, z), jnp.where(lo, z, t)], axis=0)

    def chunk_pairs():
        for c in range(rows // chunk):
            for pr in range(width // pair):
                yield c, pr, slice(c * chunk, (c + 1) * chunk), slice(pr * pair, (pr + 1) * pair)

    def finish(c, pr, rs, ls, sc):
        vpair = v_ref[0, rs, ls]
        st = st_ref[pr]
        o_scr[rs, ls] = _dot(sc.astype(BF16), split_heads(vpair)) + _dot_nt(qe_ref[rs, ls], st.astype(BF16))
        st_ref[pr] = g_ref[c:c + 1, ls] * st + jnp.where(bd, _dot_tn(vpair, kd_ref[rs, ls]), 0.0)

    mild = jnp.max(dec_ref[0]) < HG_MILD_DECAY

    @pl.when(mild)
    def _():
        _, kk, b = state_operands()
        xl_ref[0] = (kk * jnp.exp(-b)).astype(BF16)
        causal = s_i <= t_i
        for c, pr, rs, ls in chunk_pairs():
            sc = jnp.where(causal, _dot_nt(qe_ref[rs, ls], split_heads(xl_ref[0, rs, ls])), 0.0)
            finish(c, pr, rs, ls, sc)

    @pl.when(jnp.logical_not(mild))
    def _():
        q, kk, b = state_operands()
        xl_ref[0] = (q * kk).astype(BF16)
        pos = lax.broadcasted_iota(jnp.int32, (rows, width), 0)
        for li, blk in enumerate(blocks):
            w = jnp.exp(-jnp.abs(b - _block_mid_rows(b, blk)))
            xl_ref[li + 1] = (jnp.where(pos % blk >= blk // 2, q, kk) * w).astype(BF16)
        eye = t_i == s_i
        lmask = [((t_i // blk) == (s_i // blk)) & ((t_i % blk) >= blk // 2) & ((s_i % blk) < blk // 2)
                 for blk in blocks]
        ones_bd = jnp.where(lax.broadcasted_iota(jnp.int32, (pair, HG_HEAD), 0) // HG_HEAD
                            == lax.broadcasted_iota(jnp.int32, (pair, HG_HEAD), 1) // chunk, 1.0, 0.0).astype(BF16)
        for c, pr, rs, ls in chunk_pairs():
            sc = jnp.where(eye, _dot(xl_ref[0, rs, ls], ones_bd), 0.0)
            for li in range(len(blocks)):
                xl = xl_ref[li + 1, rs, ls]
                sc = jnp.where(lmask[li], _dot_nt(xl, split_heads(xl)), sc)
            finish(c, pr, rs, ls, sc)

    out_ref[0] = _gate_out_math(o_scr[...], z_ref[0], h_ref[0], gn_ref[...], wo_ref[...], gp_ref[...], True)


def _hg_core(q, kk, b, v, dec, z, h, norm_g, w_out, g_post, chunk):
    bsz, seq, width = q.shape
    blk = min(HG_BLOCK, seq)
    nlev = int(math.log2(chunk))
    assert HG_HEAD == 2 * chunk and width % (2 * HG_HEAD) == 0 and blk % chunk == 0
    tok = pl.BlockSpec((1, blk, width), lambda b_, n: (b_, n, 0))
    const = lambda shp: pl.BlockSpec(shp, lambda b_, n: (0, 0))
    row = lambda t: t.reshape(1, -1)
    return pl.pallas_call(
        functools.partial(_hg_core_kernel, chunk=chunk),
        out_shape=jax.ShapeDtypeStruct(h.shape, F32),
        grid=(bsz, seq // blk),
        in_specs=[tok, tok, tok, tok, pl.BlockSpec((1, blk // chunk, width), lambda b_, n: (b_, n, 0)),
                  tok, tok, const((1, width)), const(w_out.shape), const((1, width))],
        out_specs=tok,
        scratch_shapes=[pltpu.VMEM((nlev + 1, blk, width), BF16),
                        pltpu.VMEM((blk, width), BF16),
                        pltpu.VMEM((blk, width), BF16),
                        pltpu.VMEM((blk // chunk, width), F32),
                        pltpu.VMEM((width // (2 * HG_HEAD), 2 * HG_HEAD, 2 * HG_HEAD), F32),
                        pltpu.VMEM((blk, width), F32)],
        compiler_params=_cparams(("parallel", "arbitrary")),
        name="hgrn2_core",
    )(q, kk, b, v, dec, z, h, row(norm_g), w_out.astype(BF16), row(g_post))


def _hg_layer(h, g_pre, g_post, w_in, lb_logits, layer, norm_g, w_out):
    chunk = min(HG_CHUNK, h.shape[1])
    q, kk, b, v, z, dec = _hg_proj(h, g_pre, w_in.astype(BF16), lb_logits, layer, chunk)
    return _hg_core(q, kk, b, v, dec, z, h, norm_g, w_out, g_post, chunk)


def _at_proj_kernel(h_ref, g_ref, w_ref, bias_ref, pos_ref, invf_ref, q_ref, k_ref, v_ref, z_ref,
                    *, qw, kvw):
    y = _rms(h_ref[0], g_ref[...]).astype(BF16)
    half = AT_HEAD // 2
    nq4 = LANES // half
    ang = pos_ref[0].astype(F32) * invf_ref[...]
    cos4 = jnp.cos(ang)
    sin4 = jnp.sin(ang)
    lane4 = lax.broadcasted_iota(jnp.int32, ang.shape, 1)

    def spread(t):
        parts = []
        for j in range(nq4):
            m = jnp.where(lane4 // half == j, t, 0.0)
            x = m
            for s in range(1, nq4):
                x = x + pltpu.roll(m, s * half, 1)
            parts.append(x)
        return jnp.concatenate(parts, axis=0)

    lane = lax.broadcasted_iota(jnp.int32, (h_ref.shape[1], LANES), 1)
    first = (lane % AT_HEAD) < half
    low = lane < AT_HEAD
    cosf = spread(cos4)
    sinf = spread(sin4)
    sinf = jnp.where(first, -sinf, sinf)

    def rope(t):
        partner = jnp.where(first, pltpu.roll(t, LANES - AT_HEAD // 2, 1), pltpu.roll(t, AT_HEAD // 2, 1))
        return t * cosf + partner * sinf

    def store_dup(ref, j, t):
        r = pltpu.roll(t, AT_HEAD, 1)
        ref[0, :, (2 * j) * LANES:(2 * j + 1) * LANES] = jnp.where(low, t, r).astype(ref.dtype)
        ref[0, :, (2 * j + 1) * LANES:(2 * j + 2) * LANES] = jnp.where(low, r, t).astype(ref.dtype)

    qkv_w = qw + 2 * kvw
    z_ref[0] = _dot(y, w_ref[:, qkv_w:qkv_w + qw])
    qf = _dot(y, w_ref[:, 0:qw]) + bias_ref[:, 0:qw]
    kvf = _dot(y, w_ref[:, qw:qkv_w]) + bias_ref[:, qw:qkv_w]
    for j in range(qw // LANES):
        sl = slice(j * LANES, (j + 1) * LANES)
        q_ref[0, :, sl] = (rope(qf[:, sl]) * AT_HEAD ** -0.5).astype(q_ref.dtype)
    for j in range(kvw // LANES):
        store_dup(k_ref, j, rope(kvf[:, j * LANES:(j + 1) * LANES]))
        store_dup(v_ref, j, kvf[:, kvw + j * LANES:kvw + (j + 1) * LANES])


def _at_proj(h, g, w_bf16, bias, positions, qw, kvw):
    bsz, seq, d = h.shape
    tm = _row_tile(seq, 4 * d + 2 * qw + 4 * 2 * kvw + 4 * qw, w_bf16.size * 2, 4 * SUBLANES)
    half = AT_HEAD // 2
    inv_freq = ROPE_THETA ** (-jnp.arange(0, AT_HEAD, 2, dtype=F32) / AT_HEAD)
    nq4 = LANES // half
    invf = jnp.tile(inv_freq, nq4).reshape(1, LANES)
    pos4 = positions.reshape(bsz, seq // tm, nq4, tm // nq4).transpose(0, 1, 3, 2)
    pos4 = jnp.repeat(pos4, half, axis=-1).reshape(bsz, seq // nq4, LANES)
    tokspec = lambda w: pl.BlockSpec((1, tm, w), lambda b, m: (b, m, 0))
    const = lambda shp: pl.BlockSpec(shp, lambda b, m: (0, 0))
    return pl.pallas_call(
        functools.partial(_at_proj_kernel, qw=qw, kvw=kvw),
        out_shape=[jax.ShapeDtypeStruct((bsz, seq, qw), BF16), jax.ShapeDtypeStruct((bsz, seq, 2 * kvw), BF16),
                   jax.ShapeDtypeStruct((bsz, seq, 2 * kvw), BF16), jax.ShapeDtypeStruct((bsz, seq, qw), F32)],
        grid=(bsz, seq // tm),
        in_specs=[tokspec(d), const((1, d)), const(w_bf16.shape), const((1, qw + 2 * kvw)),
                  pl.BlockSpec((1, tm // nq4, LANES), lambda b, m: (b, m, 0)), const((1, LANES))],
        out_specs=[tokspec(qw), tokspec(2 * kvw), tokspec(2 * kvw), tokspec(qw)],
        compiler_params=_cparams(("parallel", "parallel")),
        name="attn_proj",
    )(h, g.reshape(1, d), w_bf16, bias.reshape(1, -1), pos4, invf)


def _at_core_kernel(q_ref, kc_ref, kp_ref, vc_ref, vp_ref, sink_ref, z_ref, h_ref, wo_ref, gp_ref, out_ref):
    blk = AT_BLOCK
    nsub = q_ref.shape[1] // blk
    n = pl.program_id(1)
    ngroup = kc_ref.shape[2] // LANES
    qi = lax.broadcasted_iota(jnp.int32, (2 * blk, 4 * blk), 0) % blk
    kj = lax.broadcasted_iota(jnp.int32, (2 * blk, 4 * blk), 1) % (2 * blk)
    dist = qi + blk - kj
    in_window = (dist >= 0) & (dist < blk)
    first_mask = in_window & ((kj >= blk) | (n > 0))
    low = lax.broadcasted_iota(jnp.int32, (2 * blk, LANES), 1) < AT_HEAD
    top = lax.broadcasted_iota(jnp.int32, (2 * blk, 1), 0) < blk
    ones_lo = jnp.where(low, 1.0, 0.0).astype(BF16)
    ones_hi = jnp.where(low, 0.0, 1.0).astype(BF16)
    items = [(j, g) for j in range(nsub) for g in range(ngroup)]

    def band(cur_ref, prev_ref, j, g):
        sl = slice(g * LANES, (g + 1) * LANES)
        prev = prev_ref[0, :, sl] if j == 0 else cur_ref[0, (j - 1) * blk:j * blk, sl]
        return jnp.concatenate([prev, cur_ref[0, j * blk:(j + 1) * blk, sl]], axis=0)

    o_rows = [[] for _ in range(nsub)]
    for w0 in range(0, len(items), AT_WAVE):
        wave = items[w0:w0 + AT_WAVE]
        scores = []
        for j, g in wave:
            kdup = band(kc_ref, kp_ref, j, g)
            rhs = jnp.concatenate([jnp.where(low, kdup, 0.0), jnp.where(low, 0.0, kdup)], axis=0)
            rows = slice(j * blk, (j + 1) * blk)
            lhs = jnp.concatenate([q_ref[0, rows, (2 * g) * LANES:(2 * g + 1) * LANES],
                                   q_ref[0, rows, (2 * g + 1) * LANES:(2 * g + 2) * LANES]], axis=0)
            scores.append(_dot_nt(lhs, rhs))

        probs, sink_terms = [], []
        for (j, g), sc in zip(wave, scores):
            s = jnp.where(first_mask if j == 0 else in_window, sc, -jnp.inf)
            sink_a = jnp.where(top, sink_ref[0:1, 4 * g:4 * g + 1], sink_ref[0:1, 4 * g + 2:4 * g + 3])
            sink_b = jnp.where(top, sink_ref[0:1, 4 * g + 1:4 * g + 2], sink_ref[0:1, 4 * g + 3:4 * g + 4])
            m_a = jnp.maximum(jnp.max(s[:, :2 * blk], axis=-1, keepdims=True), sink_a)
            m_b = jnp.maximum(jnp.max(s[:, 2 * blk:], axis=-1, keepdims=True), sink_b)
            e = jnp.concatenate([jnp.exp(s[:, :2 * blk] - m_a), jnp.exp(s[:, 2 * blk:] - m_b)], axis=1)
            probs.append(e.astype(BF16))
            sink_terms.append(jnp.where(low, jnp.exp(sink_a - m_a), jnp.exp(sink_b - m_b)))

        for (j, g), p, st in zip(wave, probs, sink_terms):
            vdup = band(vc_ref, vp_ref, j, g)
            va = jnp.where(low, vdup, 0.0)
            vb = jnp.where(low, 0.0, vdup)
            rhs = jnp.concatenate([jnp.concatenate([va, ones_lo], axis=1),
                                   jnp.concatenate([vb, ones_hi], axis=1)], axis=0)
            out = _dot(p, rhs)
            o = out[:, :LANES] / (out[:, LANES:] + st)
            o_rows[j] += [o[:blk], o[blk:]]

    o = jnp.concatenate([jnp.concatenate(slabs, axis=-1) for slabs in o_rows], axis=0)
    out_ref[0] = _gate_out_math(o, z_ref[0], h_ref[0], None, wo_ref[...], gp_ref[...], False)


def _at_core(q, k_dup, v_dup, sinks, z, h, w_out, g_post):
    bsz, seq, qw = q.shape
    kvw = k_dup.shape[-1]
    d = h.shape[-1]
    blk = AT_BLOCK
    nsub = AT_SUBBLOCKS if seq % (AT_SUBBLOCKS * blk) == 0 else 1
    nq = qw // AT_HEAD
    assert nq == 4 * (kvw // LANES) and qw == 2 * kvw
    cur = lambda w: pl.BlockSpec((1, nsub * blk, w), lambda b, n: (b, n, 0))
    prev = lambda w: pl.BlockSpec((1, blk, w), lambda b, n: (b, jnp.maximum(nsub * n - 1, 0), 0))
    const = lambda shp: pl.BlockSpec(shp, lambda b, n: (0, 0))
    return pl.pallas_call(
        _at_core_kernel,
        out_shape=jax.ShapeDtypeStruct(h.shape, F32),
        grid=(bsz, seq // (nsub * blk)),
        in_specs=[cur(qw), cur(kvw), prev(kvw), cur(kvw), prev(kvw), const((1, nq)),
                  cur(qw), cur(d), const(w_out.shape), const((1, d))],
        out_specs=cur(d),
        compiler_params=_cparams(("parallel", "arbitrary")),
        name="attn_core",
    )(q, k_dup, k_dup, v_dup, v_dup, sinks.reshape(1, nq), z, h, w_out.astype(BF16), g_post.reshape(1, d))


def _at_layer(h, positions, g_pre, g_post, w_in, b_in, sinks, w_out):
    qw = w_out.shape[0]
    kvw = (b_in.shape[0] - qw) // 2
    q, k, v, z = _at_proj(h, g_pre, w_in.astype(BF16), b_in, positions, qw, kvw)
    return _at_core(q, k, v, sinks, z, h, w_out, g_post)


def kernel(x, positions, norm_pre, norm_post, s5_w_in, s5_lambda_re, s5_lambda_im, s5_log_dt, s5_b_re, s5_b_im, s5_c_re, s5_c_im, s5_d, s5_w_glu, s5_b_glu, s5_w_out, hg_w_in, hg_lb_logits, hg_norm, hg_w_out, at_w_in, at_b_in, at_sinks, at_w_out):
    depth = norm_pre.shape[0]
    s5_weights = _s5_weights(s5_lambda_re, s5_lambda_im, s5_log_dt, s5_b_re, s5_b_im, s5_c_re, s5_c_im, s5_d)
    h = x
    for i in range(depth):
        kind, j = i % 3, i // 3
        if kind == 0:
            h = _s5_layer(h, norm_pre[i], norm_post[i], s5_w_in[j], s5_weights, j, s5_w_glu[j], s5_b_glu[j],
                          s5_w_out[j])
        elif kind == 1:
            h = _hg_layer(h, norm_pre[i], norm_post[i], hg_w_in[j], hg_lb_logits, i, hg_norm[j], hg_w_out[j])
        else:
            h = _at_layer(h, positions, norm_pre[i], norm_post[i], at_w_in[j], at_b_in[j], at_sinks[j],
                          at_w_out[j])
    return h
```

```python
import functools
import math

import jax
import jax.numpy as jnp
from jax import lax
from jax.experimental import pallas as pl
from jax.experimental.pallas import tpu as pltpu

F32 = jnp.float32
BF16 = jnp.bfloat16

NORM_EPS = 1e-6
LANES = 128
SUBLANES = 8
VMEM_LIMIT = 56 * 1024 * 1024

S5_GROUP = 16
S5_STATE = 64
S5_TILE_GROUPS = 16
S5_TILE_CH = S5_TILE_GROUPS * S5_GROUP
S5_TILE_ST = S5_TILE_GROUPS * S5_STATE
S5_STEPS = 128
HG_HEAD = 128
HG_CHUNK = 64
HG_BLOCK = 512
HG_MILD_DECAY = 80.0
AT_HEAD = 64
AT_BLOCK = 128
AT_SUBBLOCKS = 4
AT_WAVE = 16
ROPE_THETA = 10000.0

ROW_TILES = (1024, 512, 256)


def _cparams(sem):
    return pltpu.CompilerParams(dimension_semantics=sem, vmem_limit_bytes=VMEM_LIMIT)


def _row_tile(seq, row_bytes, resident_bytes, multiple=SUBLANES):
    budget = VMEM_LIMIT - VMEM_LIMIT // 4
    for tm in ROW_TILES:
        if seq % tm == 0 and tm % multiple == 0 and 2 * (tm * row_bytes + resident_bytes) <= budget:
            return tm
    return seq


def _rms(x, g):
    return x * lax.rsqrt(jnp.mean(x * x, axis=-1, keepdims=True) + NORM_EPS) * g


def _sigmoid(x):
    return 1.0 / (1.0 + jnp.exp(-x))


def _sigmoid_gate(x):
    return 0.5 + 0.5 * jnp.tanh(0.5 * x)


def _silu(x):
    return x * _sigmoid_gate(x)


def _gelu_tanh(x):
    c = math.sqrt(2.0 / math.pi)
    return 0.5 * x * (1.0 + jnp.tanh(c * (x + 0.044715 * (x * x * x))))


def _dot(a, b):
    return jnp.dot(a, b, preferred_element_type=F32)


def _dot_nt(a, b):
    return lax.dot_general(a, b, (((1,), (1,)), ((), ())), preferred_element_type=F32)


def _dot_tn(a, b):
    return lax.dot_general(a, b, (((0,), (0,)), ((), ())), preferred_element_type=F32)


def _chunk_row_groups(rows, steps):
    for base in range(0, rows, SUBLANES * steps):
        for r in range(SUBLANES):
            for k0 in range(0, steps, SUBLANES):
                yield base + r * steps + k0, base + k0 * SUBLANES + r


def _norm_proj_kernel(h_ref, g_ref, w_ref, *out_refs, widths, slab_steps):
    y = _rms(h_ref[0], g_ref[...]).astype(BF16)
    off = 0
    for o_ref, width, steps in zip(out_refs, widths, slab_steps):
        r = _dot(y, w_ref[:, off:off + width])
        if steps:
            for nat, cm in _chunk_row_groups(r.shape[0], steps):
                for j in range(width // LANES):
                    o_ref[0, j, pl.ds(cm, SUBLANES, stride=SUBLANES), :] = (
                        r[nat:nat + SUBLANES, j * LANES:(j + 1) * LANES])
        else:
            o_ref[0] = r.astype(o_ref.dtype)
        off += width


def _norm_proj(h, g, w_bf16, widths, slab_steps, dtypes):
    bsz, seq, d = h.shape
    row_bytes = 4 * d + sum(w * jnp.dtype(t).itemsize for w, t in zip(widths, dtypes))
    tm = _row_tile(seq, row_bytes, w_bf16.size * 2, SUBLANES * max(max(slab_steps), 1))
    out_shape, out_specs = [], []
    for width, as_slab, dtype in zip(widths, slab_steps, dtypes):
        if as_slab:
            assert tm % (SUBLANES * as_slab) == 0 and dtype == F32
            ns = width // LANES
            out_shape.append(jax.ShapeDtypeStruct((bsz, ns, seq, LANES), F32))
            out_specs.append(pl.BlockSpec((1, ns, tm, LANES), lambda b, m: (b, 0, m, 0)))
        else:
            out_shape.append(jax.ShapeDtypeStruct((bsz, seq, width), dtype))
            out_specs.append(pl.BlockSpec((1, tm, width), lambda b, m: (b, m, 0)))
    return pl.pallas_call(
        functools.partial(_norm_proj_kernel, widths=tuple(widths), slab_steps=tuple(slab_steps)),
        out_shape=out_shape,
        grid=(bsz, seq // tm),
        in_specs=[pl.BlockSpec((1, tm, d), lambda b, m: (b, m, 0)),
                  pl.BlockSpec((1, d), lambda b, m: (0, 0)),
                  pl.BlockSpec(w_bf16.shape, lambda b, m: (0, 0))],
        out_specs=out_specs,
        compiler_params=_cparams(("parallel", "parallel")),
        name="norm_proj",
    )(h, g.reshape(1, d), w_bf16)


def _zoh(lr, li, ldt):
    dt = jnp.exp(ldt)
    mag = jnp.exp(lr * dt)
    ar = mag * jnp.cos(li * dt)
    ai = mag * jnp.sin(li * dt)
    den = lr * lr + li * li
    qr = ((ar - 1.0) * lr + ai * li) / den
    qi = (ai * lr - (ar - 1.0) * li) / den
    return ar, ai, qr, qi


def _s5_tiles_kernel(lr_ref, li_ref, ldt_ref, br_ref, bi_ref, cr_ref, ci_ref, ar_ref, ai_ref, wb_ref, wc_ref):
    ch, st, tg = S5_TILE_CH, S5_TILE_ST, S5_TILE_GROUPS
    ar, ai, qr, qi = _zoh(lr_ref[...], li_ref[...], ldt_ref[...])
    ar_ref[...] = ar
    ai_ref[...] = ai

    sel_p = jnp.where(lax.broadcasted_iota(jnp.int32, (st, S5_STATE), 0) % S5_STATE
                      == lax.broadcasted_iota(jnp.int32, (st, S5_STATE), 1), 1.0, 0.0).astype(BF16)
    own_group = (lax.broadcasted_iota(jnp.int32, (st, tg), 0) // S5_STATE
                 == lax.broadcasted_iota(jnp.int32, (st, tg), 1))

    def to_rows(m):
        hi = m.astype(BF16)
        r1 = m - hi.astype(F32)
        mid = r1.astype(BF16)
        lo = (r1 - mid.astype(F32)).astype(BF16)
        spread = _dot_nt(sel_p, hi) + _dot_nt(sel_p, mid) + _dot_nt(sel_p, lo)
        return jnp.sum(jnp.where(own_group, spread, 0.0), axis=1, keepdims=True)

    qr = to_rows(qr)
    qi = to_rows(qi)
    br = br_ref[...]
    bi = bi_ref[...]
    bbr = (qr * br - qi * bi).astype(BF16)
    bbi = (qr * bi + qi * br).astype(BF16)

    sel_h = jnp.where(lax.broadcasted_iota(jnp.int32, (ch, S5_GROUP), 0) % S5_GROUP
                      == lax.broadcasted_iota(jnp.int32, (ch, S5_GROUP), 1), 1.0, 0.0).astype(BF16)
    diag_in = (lax.broadcasted_iota(jnp.int32, (ch, st), 0) // S5_GROUP
               == lax.broadcasted_iota(jnp.int32, (ch, st), 1) // S5_STATE)
    wb_ref[0, :, 0:st] = jnp.where(diag_in, _dot_nt(sel_h, bbr), 0.0).astype(BF16)
    wb_ref[0, :, st:2 * st] = jnp.where(diag_in, _dot_nt(sel_h, bbi), 0.0).astype(BF16)
    diag_out = (lax.broadcasted_iota(jnp.int32, (st, ch), 0) // S5_STATE
                == lax.broadcasted_iota(jnp.int32, (st, ch), 1) // S5_GROUP)
    wc_ref[0, 0:st, :] = jnp.where(diag_out, _dot_nt(sel_p, cr_ref[...].astype(BF16)), 0.0).astype(BF16)
    wc_ref[0, st:2 * st, :] = jnp.where(diag_out, -_dot_nt(sel_p, ci_ref[...].astype(BF16)), 0.0).astype(BF16)


def _s5_core_kernel(u_ref, wb_ref, wc_ref, a_ref, d_ref, y_ref, x_ref, s_ref, carry_ref, *, steps):
    nst = S5_TILE_ST
    nslab = S5_TILE_CH // LANES
    ntile = wb_ref.shape[0]

    @pl.when(pl.program_id(1) == 0)
    def _():
        carry_ref[...] = jnp.zeros_like(carry_ref)

    def load_u(i):
        return jnp.concatenate([u_ref[0, i * nslab + j] for j in range(nslab)], axis=-1)

    def project_in(i):
        x_ref[i % 2] = _dot(load_u(i).astype(BF16), wb_ref[i])

    def scan(i):
        slot = i % 2
        ar1 = a_ref[i, 0:1, :]
        ai1 = a_ref[i, 1:2, :]
        ar = jnp.broadcast_to(ar1, (SUBLANES, nst))
        ai = jnp.broadcast_to(ai1, (SUBLANES, nst))

        def advance(k, sr, si):
            xr = x_ref[slot, k * SUBLANES:(k + 1) * SUBLANES, 0:nst]
            xi = x_ref[slot, k * SUBLANES:(k + 1) * SUBLANES, nst:2 * nst]
            return ar * sr - ai * si + xr, ar * si + ai * sr + xi

        er = jnp.zeros((SUBLANES, nst), F32)
        ei = jnp.zeros((SUBLANES, nst), F32)
        for k in range(steps):
            er, ei = advance(k, er, ei)

        pr, pi = ar1, ai1
        for _ in range(int(math.log2(steps))):
            pr, pi = pr * pr - pi * pi, 2.0 * pr * pi

        rid = lax.broadcasted_iota(jnp.int32, (SUBLANES, nst), 0)
        cr = carry_ref[i, 0:1, :]
        ci = carry_ref[i, 1:2, :]
        sr = jnp.zeros((SUBLANES, nst), F32)
        si = jnp.zeros((SUBLANES, nst), F32)
        for r in range(SUBLANES):
            sr = jnp.where(rid == r, jnp.broadcast_to(cr, (SUBLANES, nst)), sr)
            si = jnp.where(rid == r, jnp.broadcast_to(ci, (SUBLANES, nst)), si)
            cr, ci = pr * cr - pi * ci + er[r:r + 1, :], pr * ci + pi * cr + ei[r:r + 1, :]
        carry_ref[i, 0:1, :] = cr
        carry_ref[i, 1:2, :] = ci

        for k in range(0, steps, 2):
            r0, i0 = advance(k, sr, si)
            sr, si = advance(k + 1, r0, i0)
            rows2 = slice(k * SUBLANES, (k + 2) * SUBLANES)
            s_ref[slot, rows2, 0:nst] = jnp.concatenate([r0, sr], axis=0).astype(BF16)
            s_ref[slot, rows2, nst:2 * nst] = jnp.concatenate([i0, si], axis=0).astype(BF16)

    def project_out(i):
        y = _dot(s_ref[i % 2], wc_ref[i]) + d_ref[i] * load_u(i)
        for j in range(nslab):
            y_ref[0, i * nslab + j] = y[:, j * LANES:(j + 1) * LANES]

    project_in(0)
    for i in range(ntile):
        if i + 1 < ntile:
            project_in(i + 1)
        scan(i)
        project_out(i)


def _s5_core(u_slab, wb, wc, a_tiles, d_tiles, layer, steps):
    bsz, nslab, seq, _ = u_slab.shape
    ntile = nslab * LANES // S5_TILE_CH
    rows = SUBLANES * steps
    const = lambda shp: pl.BlockSpec((ntile,) + shp[1:], lambda b, n: (layer,) + (0,) * (len(shp) - 1),
                                     pipeline_mode=pl.Buffered(1))
    tok = pl.BlockSpec((1, nslab, rows, LANES), lambda b, n: (b, 0, n, 0))
    return pl.pallas_call(
        functools.partial(_s5_core_kernel, steps=steps),
        out_shape=jax.ShapeDtypeStruct(u_slab.shape, F32),
        grid=(bsz, seq // rows),
        in_specs=[tok, const(wb.shape), const(wc.shape), const(a_tiles.shape), const(d_tiles.shape)],
        out_specs=tok,
        scratch_shapes=[pltpu.VMEM((2, rows, 2 * S5_TILE_ST), F32),
                        pltpu.VMEM((2, rows, 2 * S5_TILE_ST), BF16),
                        pltpu.VMEM((ntile, 2, S5_TILE_ST), F32)],
        compiler_params=_cparams(("parallel", "arbitrary")),
        name="s5_core",
    )(u_slab, wb, wc, a_tiles, d_tiles)


def _s5_post_kernel(y_ref, z_ref, h_ref, wg_ref, bg_ref, wo_ref, gp_ref, o_ref, *, steps):
    nslab = y_ref.shape[1]
    groups = list(_chunk_row_groups(y_ref.shape[2], steps))
    y = jnp.concatenate(
        [jnp.concatenate([y_ref[0, j, pl.ds(cm, SUBLANES, stride=SUBLANES), :] for _, cm in groups], axis=0)
         for j in range(nslab)], axis=-1)
    y = _gelu_tanh(y)
    y = y * _sigmoid_gate(_dot(y.astype(BF16), wg_ref[...]) + bg_ref[...])
    t = y * _silu(z_ref[0])
    o = _dot(t.astype(BF16), wo_ref[...])
    o_ref[0] = h_ref[0] + _rms(o, gp_ref[...])


def _s5_post(y_slab, z, h, w_glu, b_glu, w_out, g_post, steps):
    bsz, seq, d = h.shape
    nslab = y_slab.shape[1]
    tm = _row_tile(seq, 4 * 4 * d, 2 * 2 * d * d, SUBLANES * steps)
    assert tm % (SUBLANES * steps) == 0
    row = lambda t: t.reshape(1, -1)
    vec = pl.BlockSpec((1, d), lambda b, m: (0, 0))
    mat = pl.BlockSpec((d, d), lambda b, m: (0, 0))
    slab = pl.BlockSpec((1, nslab, tm, LANES), lambda b, m: (b, 0, m, 0))
    tok = pl.BlockSpec((1, tm, d), lambda b, m: (b, m, 0))
    return pl.pallas_call(
        functools.partial(_s5_post_kernel, steps=steps),
        out_shape=jax.ShapeDtypeStruct(h.shape, F32),
        grid=(bsz, seq // tm),
        in_specs=[slab, tok, tok, mat, vec, mat, vec],
        out_specs=tok,
        compiler_params=_cparams(("parallel", "parallel")),
        name="s5_post",
    )(y_slab, z, h, w_glu.astype(BF16), row(b_glu), w_out.astype(BF16), row(g_post))


def _s5_weights(lam_re, lam_im, log_dt, b_re, b_im, c_re, c_im, d_skip):
    nlayer, ngroup, nstate, gsize = b_re.shape
    assert (nstate, gsize) == (S5_STATE, S5_GROUP) and ngroup % S5_TILE_GROUPS == 0
    ng = nlayer * ngroup
    ntile = ng // S5_TILE_GROUPS
    tg = S5_TILE_GROUPS
    ldt = log_dt.reshape(ng, 1)
    grp = lambda w: pl.BlockSpec((tg, w), lambda i: (i, 0))
    rows = lambda r, w: pl.BlockSpec((r, w), lambda i: (i, 0))
    ar, ai, wb, wc = pl.pallas_call(
        _s5_tiles_kernel,
        out_shape=[jax.ShapeDtypeStruct((ng, nstate), F32), jax.ShapeDtypeStruct((ng, nstate), F32),
                   jax.ShapeDtypeStruct((ntile, S5_TILE_CH, 2 * S5_TILE_ST), BF16),
                   jax.ShapeDtypeStruct((ntile, 2 * S5_TILE_ST, S5_TILE_CH), BF16)],
        grid=(ntile,),
        in_specs=[grp(nstate), grp(nstate), grp(1),
                  rows(tg * nstate, gsize), rows(tg * nstate, gsize),
                  rows(tg * gsize, nstate), rows(tg * gsize, nstate)],
        out_specs=[grp(nstate), grp(nstate),
                   pl.BlockSpec((1, S5_TILE_CH, 2 * S5_TILE_ST), lambda i: (i, 0, 0)),
                   pl.BlockSpec((1, 2 * S5_TILE_ST, S5_TILE_CH), lambda i: (i, 0, 0))],
        compiler_params=_cparams(("parallel",)),
        name="s5_tiles",
    )(lam_re.reshape(ng, nstate), lam_im.reshape(ng, nstate), ldt,
      b_re.reshape(ng * nstate, gsize), b_im.reshape(ng * nstate, gsize),
      c_re.reshape(ng * gsize, nstate), c_im.reshape(ng * gsize, nstate))
    a_tiles = jnp.stack([ar.reshape(ntile, S5_TILE_ST), ai.reshape(ntile, S5_TILE_ST)], axis=1)
    d_tiles = d_skip.reshape(ntile, 1, S5_TILE_CH)
    return wb, wc, a_tiles, d_tiles


def _s5_layer(h, g_pre, g_post, w_in, weights, layer, w_glu, b_glu, w_out):
    bsz, seq, d = h.shape
    steps = min(S5_STEPS, seq // SUBLANES)
    u_slab, z = _norm_proj(h, g_pre, w_in.astype(BF16), (d, d), (steps, 0), (F32, F32))
    y_slab = _s5_core(u_slab, *weights, layer, steps)
    return _s5_post(y_slab, z, h, w_glu, b_glu, w_out, g_post, steps)


def _gate_out_math(o, z, h, g_head, w_out, g_post, head_norm):
    if head_norm:
        parts = []
        for j in range(o.shape[-1] // HG_HEAD):
            oj = o[:, j * HG_HEAD:(j + 1) * HG_HEAD]
            parts.append(oj * lax.rsqrt(jnp.mean(oj * oj, axis=-1, keepdims=True) + NORM_EPS))
        o = jnp.concatenate(parts, axis=-1) * g_head
    t = o * _silu(z)
    return h + _rms(_dot(t.astype(BF16), w_out), g_post)


def _chunk_cumsum(x, chunk):
    rows, w = x.shape
    per = chunk // SUBLANES
    x3 = x.reshape(rows // SUBLANES, SUBLANES, w)
    pos = lax.broadcasted_iota(jnp.int32, x3.shape, 1)
    sh = 1
    while sh < SUBLANES:
        x3 = x3 + jnp.where(pos >= sh, pltpu.roll(x3, sh, 1), 0.0)
        sh *= 2
    x4 = x3.reshape(rows // chunk, per, SUBLANES, w)
    tot = x4[:, :, SUBLANES - 1:SUBLANES, :]
    pref = [jnp.zeros_like(tot[:, 0:1])]
    for j in range(1, per):
        pref.append(pref[-1] + tot[:, j - 1:j])
    return (x4 + jnp.concatenate(pref, axis=1)).reshape(rows, w)


def _block_mid_rows(x, blk):
    rows, w = x.shape
    half = blk // 2
    if blk >= SUBLANES:
        x3 = x.reshape(rows // blk, blk, w)
        return jnp.broadcast_to(x3[:, half - 1:half, :], (rows // blk, blk, w)).reshape(rows, w)
    pos = lax.broadcasted_iota(jnp.int32, (rows, w), 0) % blk
    out = x
    for p in range(blk):
        d = p - (half - 1)
        if d != 0:
            out = jnp.where(pos == p, pltpu.roll(x, d % rows, 0), out)
    return out


def _hg_proj_kernel(h_ref, g_ref, w_ref, lbl_ref, q_ref, kk_ref, b_ref, v_ref, z_ref, dec_ref, *, layer, chunk):
    width = q_ref.shape[2]
    rows = q_ref.shape[1]
    y = _rms(h_ref[0], g_ref[...]).astype(BF16)

    lg = lbl_ref[...]
    ex = jnp.exp(lg - jnp.max(lg, axis=0, keepdims=True))
    p = ex / jnp.sum(ex, axis=0, keepdims=True)
    lb = jnp.zeros((1, width), F32)
    for j in range(1, layer + 1):
        lb = lb + p[j:j + 1, :]

    tile = 2 * LANES
    for c0 in range(0, width, tile):
        cs = slice(c0, c0 + tile)
        fz = _dot(y, w_ref[:, width + c0:width + c0 + tile])
        q_ref[0, :, cs] = _dot(y, w_ref[:, c0:c0 + tile])
        v_ref[0, :, cs] = _dot(y, w_ref[:, 2 * width + c0:2 * width + c0 + tile]).astype(v_ref.dtype)
        z_ref[0, :, cs] = _dot(y, w_ref[:, 3 * width + c0:3 * width + c0 + tile])
        f = lb[:, cs] + (1.0 - lb[:, cs]) * _sigmoid(fz)
        kk_ref[0, :, cs] = 1.0 - f
        b = _chunk_cumsum(jnp.log(f), chunk)
        b_ref[0, :, cs] = b
        dec_ref[0, :, cs] = -b.reshape(rows // chunk, chunk, tile)[:, chunk - 1, :]


def _hg_proj(h, g, w_bf16, lb_logits, layer, chunk):
    bsz, seq, d = h.shape
    width = w_bf16.shape[1] // 4
    tm = _row_tile(seq, 4 * d + (4 * 4 + 2) * width, w_bf16.size * 2, SUBLANES * chunk)
    tok = pl.BlockSpec((1, tm, width), lambda b, m: (b, m, 0))
    const = lambda shp: pl.BlockSpec(shp, lambda b, m: (0, 0))
    act = lambda t: jax.ShapeDtypeStruct((bsz, seq, width), t)
    return pl.pallas_call(
        functools.partial(_hg_proj_kernel, layer=layer, chunk=chunk),
        out_shape=[act(F32), act(F32), act(F32), act(BF16), act(F32),
                   jax.ShapeDtypeStruct((bsz, seq // chunk, width), F32)],
        grid=(bsz, seq // tm),
        in_specs=[pl.BlockSpec((1, tm, d), lambda b, m: (b, m, 0)), const((1, d)), const(w_bf16.shape),
                  const(lb_logits.shape)],
        out_specs=[tok, tok, tok, tok, tok, pl.BlockSpec((1, tm // chunk, width), lambda b, m: (b, m, 0))],
        compiler_params=_cparams(("parallel", "parallel")),
        name="hgrn2_proj",
    )(h, g.reshape(1, d), w_bf16, lb_logits)


def _hg_core_kernel(q_ref, kk_ref, b_ref, v_ref, dec_ref, z_ref, h_ref, gn_ref, wo_ref, gp_ref, out_ref,
                    xl_ref, qe_ref, kd_ref, g_ref, st_ref, o_scr, *, chunk):
    rows, width = q_ref.shape[1], q_ref.shape[2]
    pair = 2 * HG_HEAD
    blocks = [2 ** i for i in range(1, int(math.log2(chunk)) + 1)]

    @pl.when(pl.program_id(1) == 0)
    def _():
        st_ref[...] = jnp.zeros_like(st_ref)

    def state_operands():
        q = q_ref[0]
        kk = kk_ref[0]
        b = b_ref[0]
        b3 = b.reshape(rows // chunk, chunk, width)
        b_last = b3[:, chunk - 1:chunk, :]
        g_ref[...] = jnp.exp(b_last.reshape(rows // chunk, width))
        qe_ref[...] = (q * jnp.exp(b)).astype(BF16)
        kd_ref[...] = (kk * jnp.exp(jnp.broadcast_to(b_last, b3.shape).reshape(rows, width) - b)).astype(BF16)
        return q, kk, b

    t_i = lax.broadcasted_iota(jnp.int32, (chunk, HG_HEAD), 0)
    s_i = lax.broadcasted_iota(jnp.int32, (chunk, HG_HEAD), 1) % chunk
    lo = lax.broadcasted_iota(jnp.int32, (chunk, pair), 1) < HG_HEAD
    bd = (lax.broadcasted_iota(jnp.int32, (pair, pair), 0) // HG_HEAD
          == lax.broadcasted_iota(jnp.int32, (pair, pair), 1) // HG_HEAD)

    def split_heads(t):
        z = jnp.zeros_like(t)
        return jnp.concatenate([jnp.where(lo, t, z), jnp.where(lo, z, t)], axis=0)

    def chunk_pairs():
        for c in range(rows // chunk):
            for pr in range(width // pair):
                yield c, pr, slice(c * chunk, (c + 1) * chunk), slice(pr * pair, (pr + 1) * pair)

    def finish(c, pr, rs, ls, sc):
        vpair = v_ref[0, rs, ls]
        st = st_ref[pr]
        o_scr[rs, ls] = _dot(sc.astype(BF16), split_heads(vpair)) + _dot_nt(qe_ref[rs, ls], st.astype(BF16))
        st_ref[pr] = g_ref[c:c + 1, ls] * st + jnp.where(bd, _dot_tn(vpair, kd_ref[rs, ls]), 0.0)

    mild = jnp.max(dec_ref[0]) < HG_MILD_DECAY

    @pl.when(mild)
    def _():
        _, kk, b = state_operands()
        xl_ref[0] = (kk * jnp.exp(-b)).astype(BF16)
        causal = s_i <= t_i
        for c, pr, rs, ls in chunk_pairs():
            sc = jnp.where(causal, _dot_nt(qe_ref[rs, ls], split_heads(xl_ref[0, rs, ls])), 0.0)
            finish(c, pr, rs, ls, sc)

    @pl.when(jnp.logical_not(mild))
    def _():
        q, kk, b = state_operands()
        xl_ref[0] = (q * kk).astype(BF16)
        pos = lax.broadcasted_iota(jnp.int32, (rows, width), 0)
        for li, blk in enumerate(blocks):
            w = jnp.exp(-jnp.abs(b - _block_mid_rows(b, blk)))
            xl_ref[li + 1] = (jnp.where(pos % blk >= blk // 2, q, kk) * w).astype(BF16)
        eye = t_i == s_i
        lmask = [((t_i // blk) == (s_i // blk)) & ((t_i % blk) >= blk // 2) & ((s_i % blk) < blk // 2)
                 for blk in blocks]
        ones_bd = jnp.where(lax.broadcasted_iota(jnp.int32, (pair, HG_HEAD), 0) // HG_HEAD
                            == lax.broadcasted_iota(jnp.int32, (pair, HG_HEAD), 1) // chunk, 1.0, 0.0).astype(BF16)
        for c, pr, rs, ls in chunk_pairs():
            sc = jnp.where(eye, _dot(xl_ref[0, rs, ls], ones_bd), 0.0)
            for li in range(len(blocks)):
                xl = xl_ref[li + 1, rs, ls]
                sc = jnp.where(lmask[li], _dot_nt(xl, split_heads(xl)), sc)
            finish(c, pr, rs, ls, sc)

    out_ref[0] = _gate_out_math(o_scr[...], z_ref[0], h_ref[0], gn_ref[...], wo_ref[...], gp_ref[...], True)


def _hg_core(q, kk, b, v, dec, z, h, norm_g, w_out, g_post, chunk):
    bsz, seq, width = q.shape
    blk = min(HG_BLOCK, seq)
    nlev = int(math.log2(chunk))
    assert HG_HEAD == 2 * chunk and width % (2 * HG_HEAD) == 0 and blk % chunk == 0
    tok = pl.BlockSpec((1, blk, width), lambda b_, n: (b_, n, 0))
    const = lambda shp: pl.BlockSpec(shp, lambda b_, n: (0, 0))
    row = lambda t: t.reshape(1, -1)
    return pl.pallas_call(
        functools.partial(_hg_core_kernel, chunk=chunk),
        out_shape=jax.ShapeDtypeStruct(h.shape, F32),
        grid=(bsz, seq // blk),
        in_specs=[tok, tok, tok, tok, pl.BlockSpec((1, blk // chunk, width), lambda b_, n: (b_, n, 0)),
                  tok, tok, const((1, width)), const(w_out.shape), const((1, width))],
        out_specs=tok,
        scratch_shapes=[pltpu.VMEM((nlev + 1, blk, width), BF16),
                        pltpu.VMEM((blk, width), BF16),
                        pltpu.VMEM((blk, width), BF16),
                        pltpu.VMEM((blk // chunk, width), F32),
                        pltpu.VMEM((width // (2 * HG_HEAD), 2 * HG_HEAD, 2 * HG_HEAD), F32),
                        pltpu.VMEM((blk, width), F32)],
        compiler_params=_cparams(("parallel", "arbitrary")),
        name="hgrn2_core",
    )(q, kk, b, v, dec, z, h, row(norm_g), w_out.astype(BF16), row(g_post))


def _hg_layer(h, g_pre, g_post, w_in, lb_logits, layer, norm_g, w_out):
    chunk = min(HG_CHUNK, h.shape[1])
    q, kk, b, v, z, dec = _hg_proj(h, g_pre, w_in.astype(BF16), lb_logits, layer, chunk)
    return _hg_core(q, kk, b, v, dec, z, h, norm_g, w_out, g_post, chunk)


def _at_proj_kernel(h_ref, g_ref, w_ref, bias_ref, pos_ref, invf_ref, q_ref, k_ref, v_ref, z_ref,
                    *, qw, kvw):
    y = _rms(h_ref[0], g_ref[...]).astype(BF16)
    half = AT_HEAD // 2
    nq4 = LANES // half
    ang = pos_ref[0].astype(F32) * invf_ref[...]
    cos4 = jnp.cos(ang)
    sin4 = jnp.sin(ang)
    lane4 = lax.broadcasted_iota(jnp.int32, ang.shape, 1)

    def spread(t):
        parts = []
        for j in range(nq4):
            m = jnp.where(lane4 // half == j, t, 0.0)
            x = m
            for s in range(1, nq4):
                x = x + pltpu.roll(m, s * half, 1)
            parts.append(x)
        return jnp.concatenate(parts, axis=0)

    lane = lax.broadcasted_iota(jnp.int32, (h_ref.shape[1], LANES), 1)
    first = (lane % AT_HEAD) < half
    low = lane < AT_HEAD
    cosf = spread(cos4)
    sinf = spread(sin4)
    sinf = jnp.where(first, -sinf, sinf)

    def rope(t):
        partner = jnp.where(first, pltpu.roll(t, LANES - AT_HEAD // 2, 1), pltpu.roll(t, AT_HEAD // 2, 1))
        return t * cosf + partner * sinf

    def store_dup(ref, j, t):
        r = pltpu.roll(t, AT_HEAD, 1)
        ref[0, :, (2 * j) * LANES:(2 * j + 1) * LANES] = jnp.where(low, t, r).astype(ref.dtype)
        ref[0, :, (2 * j + 1) * LANES:(2 * j + 2) * LANES] = jnp.where(low, r, t).astype(ref.dtype)

    qkv_w = qw + 2 * kvw
    z_ref[0] = _dot(y, w_ref[:, qkv_w:qkv_w + qw])
    qf = _dot(y, w_ref[:, 0:qw]) + bias_ref[:, 0:qw]
    kvf = _dot(y, w_ref[:, qw:qkv_w]) + bias_ref[:, qw:qkv_w]
    for j in range(qw // LANES):
        sl = slice(j * LANES, (j + 1) * LANES)
        q_ref[0, :, sl] = (rope(qf[:, sl]) * AT_HEAD ** -0.5).astype(q_ref.dtype)
    for j in range(kvw // LANES):
        store_dup(k_ref, j, rope(kvf[:, j * LANES:(j + 1) * LANES]))
        store_dup(v_ref, j, kvf[:, kvw + j * LANES:kvw + (j + 1) * LANES])


def _at_proj(h, g, w_bf16, bias, positions, qw, kvw):
    bsz, seq, d = h.shape
    tm = _row_tile(seq, 4 * d + 2 * qw + 4 * 2 * kvw + 4 * qw, w_bf16.size * 2, 4 * SUBLANES)
    half = AT_HEAD // 2
    inv_freq = ROPE_THETA ** (-jnp.arange(0, AT_HEAD, 2, dtype=F32) / AT_HEAD)
    nq4 = LANES // half
    invf = jnp.tile(inv_freq, nq4).reshape(1, LANES)
    pos4 = positions.reshape(bsz, seq // tm, nq4, tm // nq4).transpose(0, 1, 3, 2)
    pos4 = jnp.repeat(pos4, half, axis=-1).reshape(bsz, seq // nq4, LANES)
    tokspec = lambda w: pl.BlockSpec((1, tm, w), lambda b, m: (b, m, 0))
    const = lambda shp: pl.BlockSpec(shp, lambda b, m: (0, 0))
    return pl.pallas_call(
        functools.partial(_at_proj_kernel, qw=qw, kvw=kvw),
        out_shape=[jax.ShapeDtypeStruct((bsz, seq, qw), BF16), jax.ShapeDtypeStruct((bsz, seq, 2 * kvw), BF16),
                   jax.ShapeDtypeStruct((bsz, seq, 2 * kvw), BF16), jax.ShapeDtypeStruct((bsz, seq, qw), F32)],
        grid=(bsz, seq // tm),
        in_specs=[tokspec(d), const((1, d)), const(w_bf16.shape), const((1, qw + 2 * kvw)),
                  pl.BlockSpec((1, tm // nq4, LANES), lambda b, m: (b, m, 0)), const((1, LANES))],
        out_specs=[tokspec(qw), tokspec(2 * kvw), tokspec(2 * kvw), tokspec(qw)],
        compiler_params=_cparams(("parallel", "parallel")),
        name="attn_proj",
    )(h, g.reshape(1, d), w_bf16, bias.reshape(1, -1), pos4, invf)


def _at_core_kernel(q_ref, kc_ref, kp_ref, vc_ref, vp_ref, sink_ref, z_ref, h_ref, wo_ref, gp_ref, out_ref):
    blk = AT_BLOCK
    nsub = q_ref.shape[1] // blk
    n = pl.program_id(1)
    ngroup = kc_ref.shape[2] // LANES
    qi = lax.broadcasted_iota(jnp.int32, (2 * blk, 4 * blk), 0) % blk
    kj = lax.broadcasted_iota(jnp.int32, (2 * blk, 4 * blk), 1) % (2 * blk)
    dist = qi + blk - kj
    in_window = (dist >= 0) & (dist < blk)
    first_mask = in_window & ((kj >= blk) | (n > 0))
    low = lax.broadcasted_iota(jnp.int32, (2 * blk, LANES), 1) < AT_HEAD
    top = lax.broadcasted_iota(jnp.int32, (2 * blk, 1), 0) < blk
    ones_lo = jnp.where(low, 1.0, 0.0).astype(BF16)
    ones_hi = jnp.where(low, 0.0, 1.0).astype(BF16)
    items = [(j, g) for j in range(nsub) for g in range(ngroup)]

    def band(cur_ref, prev_ref, j, g):
        sl = slice(g * LANES, (g + 1) * LANES)
        prev = prev_ref[0, :, sl] if j == 0 else cur_ref[0, (j - 1) * blk:j * blk, sl]
        return jnp.concatenate([prev, cur_ref[0, j * blk:(j + 1) * blk, sl]], axis=0)

    o_rows = [[] for _ in range(nsub)]
    for w0 in range(0, len(items), AT_WAVE):
        wave = items[w0:w0 + AT_WAVE]
        scores = []
        for j, g in wave:
            kdup = band(kc_ref, kp_ref, j, g)
            rhs = jnp.concatenate([jnp.where(low, kdup, 0.0), jnp.where(low, 0.0, kdup)], axis=0)
            rows = slice(j * blk, (j + 1) * blk)
            lhs = jnp.concatenate([q_ref[0, rows, (2 * g) * LANES:(2 * g + 1) * LANES],
                                   q_ref[0, rows, (2 * g + 1) * LANES:(2 * g + 2) * LANES]], axis=0)
            scores.append(_dot_nt(lhs, rhs))

        probs, sink_terms = [], []
        for (j, g), sc in zip(wave, scores):
            s = jnp.where(first_mask if j == 0 else in_window, sc, -jnp.inf)
            sink_a = jnp.where(top, sink_ref[0:1, 4 * g:4 * g + 1], sink_ref[0:1, 4 * g + 2:4 * g + 3])
            sink_b = jnp.where(top, sink_ref[0:1, 4 * g + 1:4 * g + 2], sink_ref[0:1, 4 * g + 3:4 * g + 4])
            m_a = jnp.maximum(jnp.max(s[:, :2 * blk], axis=-1, keepdims=True), sink_a)
            m_b = jnp.maximum(jnp.max(s[:, 2 * blk:], axis=-1, keepdims=True), sink_b)
            e = jnp.concatenate([jnp.exp(s[:, :2 * blk] - m_a), jnp.exp(s[:, 2 * blk:] - m_b)], axis=1)
            probs.append(e.astype(BF16))
            sink_terms.append(jnp.where(low, jnp.exp(sink_a - m_a), jnp.exp(sink_b - m_b)))

        for (j, g), p, st in zip(wave, probs, sink_terms):
            vdup = band(vc_ref, vp_ref, j, g)
            va = jnp.where(low, vdup, 0.0)
            vb = jnp.where(low, 0.0, vdup)
            rhs = jnp.concatenate([jnp.concatenate([va, ones_lo], axis=1),
                                   jnp.concatenate([vb, ones_hi], axis=1)], axis=0)
            out = _dot(p, rhs)
            o = out[:, :LANES] / (out[:, LANES:] + st)
            o_rows[j] += [o[:blk], o[blk:]]

    o = jnp.concatenate([jnp.concatenate(slabs, axis=-1) for slabs in o_rows], axis=0)
    out_ref[0] = _gate_out_math(o, z_ref[0], h_ref[0], None, wo_ref[...], gp_ref[...], False)


def _at_core(q, k_dup, v_dup, sinks, z, h, w_out, g_post):
    bsz, seq, qw = q.shape
    kvw = k_dup.shape[-1]
    d = h.shape[-1]
    blk = AT_BLOCK
    nsub = AT_SUBBLOCKS if seq % (AT_SUBBLOCKS * blk) == 0 else 1
    nq = qw // AT_HEAD
    assert nq == 4 * (kvw // LANES) and qw == 2 * kvw
    cur = lambda w: pl.BlockSpec((1, nsub * blk, w), lambda b, n: (b, n, 0))
    prev = lambda w: pl.BlockSpec((1, blk, w), lambda b, n: (b, jnp.maximum(nsub * n - 1, 0), 0))
    const = lambda shp: pl.BlockSpec(shp, lambda b, n: (0, 0))
    return pl.pallas_call(
        _at_core_kernel,
        out_shape=jax.ShapeDtypeStruct(h.shape, F32),
        grid=(bsz, seq // (nsub * blk)),
        in_specs=[cur(qw), cur(kvw), prev(kvw), cur(kvw), prev(kvw), const((1, nq)),
                  cur(qw), cur(d), const(w_out.shape), const((1, d))],
        out_specs=cur(d),
        compiler_params=_cparams(("parallel", "arbitrary")),
        name="attn_core",
    )(q, k_dup, k_dup, v_dup, v_dup, sinks.reshape(1, nq), z, h, w_out.astype(BF16), g_post.reshape(1, d))


def _at_layer(h, positions, g_pre, g_post, w_in, b_in, sinks, w_out):
    qw = w_out.shape[0]
    kvw = (b_in.shape[0] - qw) // 2
    q, k, v, z = _at_proj(h, g_pre, w_in.astype(BF16), b_in, positions, qw, kvw)
    return _at_core(q, k, v, sinks, z, h, w_out, g_post)


def kernel(x, positions, norm_pre, norm_post, s5_w_in, s5_lambda_re, s5_lambda_im, s5_log_dt, s5_b_re, s5_b_im, s5_c_re, s5_c_im, s5_d, s5_w_glu, s5_b_glu, s5_w_out, hg_w_in, hg_lb_logits, hg_norm, hg_w_out, at_w_in, at_b_in, at_sinks, at_w_out):
    depth = norm_pre.shape[0]
    s5_weights = _s5_weights(s5_lambda_re, s5_lambda_im, s5_log_dt, s5_b_re, s5_b_im, s5_c_re, s5_c_im, s5_d)
    h = x
    for i in range(depth):
        kind, j = i % 3, i // 3
        if kind == 0:
            h = _s5_layer(h, norm_pre[i], norm_post[i], s5_w_in[j], s5_weights, j, s5_w_glu[j], s5_b_glu[j],
                          s5_w_out[j])
        elif kind == 1:
            h = _hg_layer(h, norm_pre[i], norm_post[i], hg_w_in[j], hg_lb_logits, i, hg_norm[j], hg_w_out[j])
        else:
            h = _at_layer(h, positions, norm_pre[i], norm_post[i], at_w_in[j], at_b_in[j], at_sinks[j],
                          at_w_out[j])
    return h
```

```python
import functools
import math

import jax
import jax.numpy as jnp
from jax import lax
from jax.experimental import pallas as pl
from jax.experimental.pallas import tpu as pltpu

F32 = jnp.float32
BF16 = jnp.bfloat16

NORM_EPS = 1e-6
LANES = 128
SUBLANES = 8
VMEM_LIMIT = 56 * 1024 * 1024

S5_GROUP = 16
S5_STATE = 64
S5_TILE_GROUPS = 16
S5_TILE_CH = S5_TILE_GROUPS * S5_GROUP
S5_TILE_ST = S5_TILE_GROUPS * S5_STATE
S5_STEPS = 128
HG_HEAD = 128
HG_CHUNK = 64
HG_BLOCK = 512
HG_MILD_DECAY = 80.0
AT_HEAD = 64
AT_BLOCK = 128
AT_SUBBLOCKS = 4
AT_WAVE = 16
ROPE_THETA = 10000.0

ROW_TILES = (1024, 512, 256)


def _cparams(sem):
    return pltpu.CompilerParams(dimension_semantics=sem, vmem_limit_bytes=VMEM_LIMIT)


def _row_tile(seq, row_bytes, resident_bytes, multiple=SUBLANES):
    budget = VMEM_LIMIT - VMEM_LIMIT // 4
    for tm in ROW_TILES:
        if seq % tm == 0 and tm % multiple == 0 and 2 * (tm * row_bytes + resident_bytes) <= budget:
            return tm
    return seq


def _rms(x, g):
    return x * lax.rsqrt(jnp.mean(x * x, axis=-1, keepdims=True) + NORM_EPS) * g


def _sigmoid(x):
    return 1.0 / (1.0 + jnp.exp(-x))


def _sigmoid_gate(x):
    return 0.5 + 0.5 * jnp.tanh(0.5 * x)


def _silu(x):
    return x * _sigmoid_gate(x)


def _gelu_tanh(x):
    c = math.sqrt(2.0 / math.pi)
    return 0.5 * x * (1.0 + jnp.tanh(c * (x + 0.044715 * (x * x * x))))


def _dot(a, b):
    return jnp.dot(a, b, preferred_element_type=F32)


def _dot_nt(a, b):
    return lax.dot_general(a, b, (((1,), (1,)), ((), ())), preferred_element_type=F32)


def _dot_tn(a, b):
    return lax.dot_general(a, b, (((0,), (0,)), ((), ())), preferred_element_type=F32)


def _chunk_row_groups(rows, steps):
    for base in range(0, rows, SUBLANES * steps):
        for r in range(SUBLANES):
            for k0 in range(0, steps, SUBLANES):
                yield base + r * steps + k0, base + k0 * SUBLANES + r


def _norm_proj_kernel(h_ref, g_ref, w_ref, *out_refs, widths, slab_steps):
    y = _rms(h_ref[0], g_ref[...]).astype(BF16)
    off = 0
    for o_ref, width, steps in zip(out_refs, widths, slab_steps):
        r = _dot(y, w_ref[0, :, off:off + width])
        if steps:
            for nat, cm in _chunk_row_groups(r.shape[0], steps):
                for j in range(width // LANES):
                    o_ref[0, j, pl.ds(cm, SUBLANES, stride=SUBLANES), :] = (
                        r[nat:nat + SUBLANES, j * LANES:(j + 1) * LANES])
        else:
            o_ref[0] = r.astype(o_ref.dtype)
        off += width


def _norm_proj(h, g, w_bf16, layer, widths, slab_steps, dtypes):
    bsz, seq, d = h.shape
    row_bytes = 4 * d + sum(w * jnp.dtype(t).itemsize for w, t in zip(widths, dtypes))
    tm = _row_tile(seq, row_bytes, w_bf16[0].size * 2, SUBLANES * max(max(slab_steps), 1))
    out_shape, out_specs = [], []
    for width, as_slab, dtype in zip(widths, slab_steps, dtypes):
        if as_slab:
            assert tm % (SUBLANES * as_slab) == 0 and dtype == F32
            ns = width // LANES
            out_shape.append(jax.ShapeDtypeStruct((bsz, ns, seq, LANES), F32))
            out_specs.append(pl.BlockSpec((1, ns, tm, LANES), lambda b, m: (b, 0, m, 0)))
        else:
            out_shape.append(jax.ShapeDtypeStruct((bsz, seq, width), dtype))
            out_specs.append(pl.BlockSpec((1, tm, width), lambda b, m: (b, m, 0)))
    return pl.pallas_call(
        functools.partial(_norm_proj_kernel, widths=tuple(widths), slab_steps=tuple(slab_steps)),
        out_shape=out_shape,
        grid=(bsz, seq // tm),
        in_specs=[pl.BlockSpec((1, tm, d), lambda b, m: (b, m, 0)),
                  pl.BlockSpec((1, d), lambda b, m: (0, 0)),
                  pl.BlockSpec((1,) + w_bf16.shape[1:], lambda b, m: (layer, 0, 0))],
        out_specs=out_specs,
        compiler_params=_cparams(("parallel", "parallel")),
        name="norm_proj",
    )(h, g.reshape(1, d), w_bf16)


def _zoh(lr, li, ldt):
    dt = jnp.exp(ldt)
    mag = jnp.exp(lr * dt)
    ar = mag * jnp.cos(li * dt)
    ai = mag * jnp.sin(li * dt)
    den = lr * lr + li * li
    qr = ((ar - 1.0) * lr + ai * li) / den
    qi = (ai * lr - (ar - 1.0) * li) / den
    return ar, ai, qr, qi


def _s5_tiles_kernel(lr_ref, li_ref, ldt_ref, br_ref, bi_ref, cr_ref, ci_ref, ar_ref, ai_ref, wb_ref, wc_ref):
    ch, st, tg = S5_TILE_CH, S5_TILE_ST, S5_TILE_GROUPS
    ar, ai, qr, qi = _zoh(lr_ref[...], li_ref[...], ldt_ref[...])
    ar_ref[...] = ar
    ai_ref[...] = ai

    sel_p = jnp.where(lax.broadcasted_iota(jnp.int32, (st, S5_STATE), 0) % S5_STATE
                      == lax.broadcasted_iota(jnp.int32, (st, S5_STATE), 1), 1.0, 0.0).astype(BF16)
    own_group = (lax.broadcasted_iota(jnp.int32, (st, tg), 0) // S5_STATE
                 == lax.broadcasted_iota(jnp.int32, (st, tg), 1))

    def to_rows(m):
        hi = m.astype(BF16)
        r1 = m - hi.astype(F32)
        mid = r1.astype(BF16)
        lo = (r1 - mid.astype(F32)).astype(BF16)
        spread = _dot_nt(sel_p, hi) + _dot_nt(sel_p, mid) + _dot_nt(sel_p, lo)
        return jnp.sum(jnp.where(own_group, spread, 0.0), axis=1, keepdims=True)

    qr = to_rows(qr)
    qi = to_rows(qi)
    br = br_ref[...]
    bi = bi_ref[...]
    bbr = (qr * br - qi * bi).astype(BF16)
    bbi = (qr * bi + qi * br).astype(BF16)

    sel_h = jnp.where(lax.broadcasted_iota(jnp.int32, (ch, S5_GROUP), 0) % S5_GROUP
                      == lax.broadcasted_iota(jnp.int32, (ch, S5_GROUP), 1), 1.0, 0.0).astype(BF16)
    diag_in = (lax.broadcasted_iota(jnp.int32, (ch, st), 0) // S5_GROUP
               == lax.broadcasted_iota(jnp.int32, (ch, st), 1) // S5_STATE)
    wb_ref[0, :, 0:st] = jnp.where(diag_in, _dot_nt(sel_h, bbr), 0.0).astype(BF16)
    wb_ref[0, :, st:2 * st] = jnp.where(diag_in, _dot_nt(sel_h, bbi), 0.0).astype(BF16)
    diag_out = (lax.broadcasted_iota(jnp.int32, (st, ch), 0) // S5_STATE
                == lax.broadcasted_iota(jnp.int32, (st, ch), 1) // S5_GROUP)
    wc_ref[0, 0:st, :] = jnp.where(diag_out, _dot_nt(sel_p, cr_ref[...].astype(BF16)), 0.0).astype(BF16)
    wc_ref[0, st:2 * st, :] = jnp.where(diag_out, -_dot_nt(sel_p, ci_ref[...].astype(BF16)), 0.0).astype(BF16)


def _s5_core_kernel(u_ref, wb_ref, wc_ref, a_ref, d_ref, y_ref, x_ref, s_ref, carry_ref, *, steps):
    nst = S5_TILE_ST
    nslab = S5_TILE_CH // LANES
    ntile = wb_ref.shape[0]

    @pl.when(pl.program_id(1) == 0)
    def _():
        carry_ref[...] = jnp.zeros_like(carry_ref)

    def load_u(i):
        return jnp.concatenate([u_ref[0, i * nslab + j] for j in range(nslab)], axis=-1)

    def project_in(i):
        x_ref[i % 2] = _dot(load_u(i).astype(BF16), wb_ref[i])

    def scan(i):
        slot = i % 2
        ar1 = a_ref[i, 0:1, :]
        ai1 = a_ref[i, 1:2, :]
        ar = jnp.broadcast_to(ar1, (SUBLANES, nst))
        ai = jnp.broadcast_to(ai1, (SUBLANES, nst))

        def advance(k, sr, si):
            xr = x_ref[slot, k * SUBLANES:(k + 1) * SUBLANES, 0:nst]
            xi = x_ref[slot, k * SUBLANES:(k + 1) * SUBLANES, nst:2 * nst]
            return ar * sr - ai * si + xr, ar * si + ai * sr + xi

        er = jnp.zeros((SUBLANES, nst), F32)
        ei = jnp.zeros((SUBLANES, nst), F32)
        for k in range(steps):
            er, ei = advance(k, er, ei)

        pr, pi = ar1, ai1
        for _ in range(int(math.log2(steps))):
            pr, pi = pr * pr - pi * pi, 2.0 * pr * pi

        rid = lax.broadcasted_iota(jnp.int32, (SUBLANES, nst), 0)
        cr = carry_ref[i, 0:1, :]
        ci = carry_ref[i, 1:2, :]
        sr = jnp.zeros((SUBLANES, nst), F32)
        si = jnp.zeros((SUBLANES, nst), F32)
        for r in range(SUBLANES):
            sr = jnp.where(rid == r, jnp.broadcast_to(cr, (SUBLANES, nst)), sr)
            si = jnp.where(rid == r, jnp.broadcast_to(ci, (SUBLANES, nst)), si)
            cr, ci = pr * cr - pi * ci + er[r:r + 1, :], pr * ci + pi * cr + ei[r:r + 1, :]
        carry_ref[i, 0:1, :] = cr
        carry_ref[i, 1:2, :] = ci

        for k in range(0, steps, 2):
            r0, i0 = advance(k, sr, si)
            sr, si = advance(k + 1, r0, i0)
            rows2 = slice(k * SUBLANES, (k + 2) * SUBLANES)
            s_ref[slot, rows2, 0:nst] = jnp.concatenate([r0, sr], axis=0).astype(BF16)
            s_ref[slot, rows2, nst:2 * nst] = jnp.concatenate([i0, si], axis=0).astype(BF16)

    def project_out(i):
        y = _dot(s_ref[i % 2], wc_ref[i]) + d_ref[i] * load_u(i)
        for j in range(nslab):
            y_ref[0, i * nslab + j] = y[:, j * LANES:(j + 1) * LANES]

    project_in(0)
    for i in range(ntile):
        if i + 1 < ntile:
            project_in(i + 1)
        scan(i)
        project_out(i)


def _s5_core(u_slab, wb, wc, a_tiles, d_tiles, layer, steps):
    bsz, nslab, seq, _ = u_slab.shape
    ntile = nslab * LANES // S5_TILE_CH
    rows = SUBLANES * steps
    const = lambda shp: pl.BlockSpec((ntile,) + shp[1:], lambda b, n: (layer,) + (0,) * (len(shp) - 1),
                                     pipeline_mode=pl.Buffered(1))
    tok = pl.BlockSpec((1, nslab, rows, LANES), lambda b, n: (b, 0, n, 0))
    return pl.pallas_call(
        functools.partial(_s5_core_kernel, steps=steps),
        out_shape=jax.ShapeDtypeStruct(u_slab.shape, F32),
        grid=(bsz, seq // rows),
        in_specs=[tok, const(wb.shape), const(wc.shape), const(a_tiles.shape), const(d_tiles.shape)],
        out_specs=tok,
        scratch_shapes=[pltpu.VMEM((2, rows, 2 * S5_TILE_ST), F32),
                        pltpu.VMEM((2, rows, 2 * S5_TILE_ST), BF16),
                        pltpu.VMEM((ntile, 2, S5_TILE_ST), F32)],
        compiler_params=_cparams(("parallel", "arbitrary")),
        name="s5_core",
    )(u_slab, wb, wc, a_tiles, d_tiles)


def _s5_post_kernel(y_ref, z_ref, h_ref, wg_ref, bg_ref, wo_ref, gp_ref, o_ref, *, steps):
    nslab = y_ref.shape[1]
    groups = list(_chunk_row_groups(y_ref.shape[2], steps))
    y = jnp.concatenate(
        [jnp.concatenate([y_ref[0, j, pl.ds(cm, SUBLANES, stride=SUBLANES), :] for _, cm in groups], axis=0)
         for j in range(nslab)], axis=-1)
    y = _gelu_tanh(y)
    y = y * _sigmoid_gate(_dot(y.astype(BF16), wg_ref[0]) + bg_ref[...])
    t = y * _silu(z_ref[0])
    o = _dot(t.astype(BF16), wo_ref[0])
    o_ref[0] = h_ref[0] + _rms(o, gp_ref[...])


def _s5_post(y_slab, z, h, w_glu_bf16, b_glu, w_out_bf16, g_post, layer, steps):
    bsz, seq, d = h.shape
    nslab = y_slab.shape[1]
    tm = _row_tile(seq, 4 * 4 * d, 2 * 2 * d * d, SUBLANES * steps)
    assert tm % (SUBLANES * steps) == 0
    row = lambda t: t.reshape(1, -1)
    vec = pl.BlockSpec((1, d), lambda b, m: (0, 0))
    mat = pl.BlockSpec((1, d, d), lambda b, m: (layer, 0, 0))
    slab = pl.BlockSpec((1, nslab, tm, LANES), lambda b, m: (b, 0, m, 0))
    tok = pl.BlockSpec((1, tm, d), lambda b, m: (b, m, 0))
    return pl.pallas_call(
        functools.partial(_s5_post_kernel, steps=steps),
        out_shape=jax.ShapeDtypeStruct(h.shape, F32),
        grid=(bsz, seq // tm),
        in_specs=[slab, tok, tok, mat, vec, mat, vec],
        out_specs=tok,
        compiler_params=_cparams(("parallel", "parallel")),
        name="s5_post",
    )(y_slab, z, h, w_glu_bf16, row(b_glu), w_out_bf16, row(g_post))


def _s5_weights(lam_re, lam_im, log_dt, b_re, b_im, c_re, c_im, d_skip):
    nlayer, ngroup, nstate, gsize = b_re.shape
    assert (nstate, gsize) == (S5_STATE, S5_GROUP) and ngroup % S5_TILE_GROUPS == 0
    ng = nlayer * ngroup
    ntile = ng // S5_TILE_GROUPS
    tg = S5_TILE_GROUPS
    ldt = log_dt.reshape(ng, 1)
    grp = lambda w: pl.BlockSpec((tg, w), lambda i: (i, 0))
    rows = lambda r, w: pl.BlockSpec((r, w), lambda i: (i, 0))
    ar, ai, wb, wc = pl.pallas_call(
        _s5_tiles_kernel,
        out_shape=[jax.ShapeDtypeStruct((ng, nstate), F32), jax.ShapeDtypeStruct((ng, nstate), F32),
                   jax.ShapeDtypeStruct((ntile, S5_TILE_CH, 2 * S5_TILE_ST), BF16),
                   jax.ShapeDtypeStruct((ntile, 2 * S5_TILE_ST, S5_TILE_CH), BF16)],
        grid=(ntile,),
        in_specs=[grp(nstate), grp(nstate), grp(1),
                  rows(tg * nstate, gsize), rows(tg * nstate, gsize),
                  rows(tg * gsize, nstate), rows(tg * gsize, nstate)],
        out_specs=[grp(nstate), grp(nstate),
                   pl.BlockSpec((1, S5_TILE_CH, 2 * S5_TILE_ST), lambda i: (i, 0, 0)),
                   pl.BlockSpec((1, 2 * S5_TILE_ST, S5_TILE_CH), lambda i: (i, 0, 0))],
        compiler_params=_cparams(("parallel",)),
        name="s5_tiles",
    )(lam_re.reshape(ng, nstate), lam_im.reshape(ng, nstate), ldt,
      b_re.reshape(ng * nstate, gsize), b_im.reshape(ng * nstate, gsize),
      c_re.reshape(ng * gsize, nstate), c_im.reshape(ng * gsize, nstate))
    a_tiles = jnp.stack([ar.reshape(ntile, S5_TILE_ST), ai.reshape(ntile, S5_TILE_ST)], axis=1)
    d_tiles = d_skip.reshape(ntile, 1, S5_TILE_CH)
    return wb, wc, a_tiles, d_tiles


def _s5_layer(h, g_pre, g_post, w_in_bf16, weights, layer, w_glu_bf16, b_glu, w_out_bf16):
    bsz, seq, d = h.shape
    steps = min(S5_STEPS, seq // SUBLANES)
    u_slab, z = _norm_proj(h, g_pre, w_in_bf16, layer, (d, d), (steps, 0), (F32, F32))
    y_slab = _s5_core(u_slab, *weights, layer, steps)
    return _s5_post(y_slab, z, h, w_glu_bf16, b_glu, w_out_bf16, g_post, layer, steps)


def _gate_out_math(o, z, h, g_head, w_out, g_post, head_norm):
    if head_norm:
        parts = []
        for j in range(o.shape[-1] // HG_HEAD):
            oj = o[:, j * HG_HEAD:(j + 1) * HG_HEAD]
            parts.append(oj * lax.rsqrt(jnp.mean(oj * oj, axis=-1, keepdims=True) + NORM_EPS))
        o = jnp.concatenate(parts, axis=-1) * g_head
    t = o * _silu(z)
    return h + _rms(_dot(t.astype(BF16), w_out), g_post)


def _chunk_cumsum(x, chunk):
    rows, w = x.shape
    per = chunk // SUBLANES
    x3 = x.reshape(rows // SUBLANES, SUBLANES, w)
    pos = lax.broadcasted_iota(jnp.int32, x3.shape, 1)
    sh = 1
    while sh < SUBLANES:
        x3 = x3 + jnp.where(pos >= sh, pltpu.roll(x3, sh, 1), 0.0)
        sh *= 2
    x4 = x3.reshape(rows // chunk, per, SUBLANES, w)
    tot = x4[:, :, SUBLANES - 1:SUBLANES, :]
    pref = [jnp.zeros_like(tot[:, 0:1])]
    for j in range(1, per):
        pref.append(pref[-1] + tot[:, j - 1:j])
    return (x4 + jnp.concatenate(pref, axis=1)).reshape(rows, w)


def _block_mid_rows(x, blk):
    rows, w = x.shape
    half = blk // 2
    if blk >= SUBLANES:
        x3 = x.reshape(rows // blk, blk, w)
        return jnp.broadcast_to(x3[:, half - 1:half, :], (rows // blk, blk, w)).reshape(rows, w)
    pos = lax.broadcasted_iota(jnp.int32, (rows, w), 0) % blk
    out = x
    for p in range(blk):
        d = p - (half - 1)
        if d != 0:
            out = jnp.where(pos == p, pltpu.roll(x, d % rows, 0), out)
    return out


def _hg_proj_kernel(h_ref, g_ref, w_ref, lbl_ref, q_ref, kk_ref, b_ref, v_ref, z_ref, dec_ref, *, layer, chunk):
    width = q_ref.shape[2]
    rows = q_ref.shape[1]
    y = _rms(h_ref[0], g_ref[...]).astype(BF16)

    lg = lbl_ref[...]
    ex = jnp.exp(lg - jnp.max(lg, axis=0, keepdims=True))
    p = ex / jnp.sum(ex, axis=0, keepdims=True)
    lb = jnp.zeros((1, width), F32)
    for j in range(1, layer + 1):
        lb = lb + p[j:j + 1, :]

    tile = 2 * LANES
    for c0 in range(0, width, tile):
        cs = slice(c0, c0 + tile)
        fz = _dot(y, w_ref[:, width + c0:width + c0 + tile])
        q_ref[0, :, cs] = _dot(y, w_ref[:, c0:c0 + tile])
        v_ref[0, :, cs] = _dot(y, w_ref[:, 2 * width + c0:2 * width + c0 + tile]).astype(v_ref.dtype)
        z_ref[0, :, cs] = _dot(y, w_ref[:, 3 * width + c0:3 * width + c0 + tile])
        f = lb[:, cs] + (1.0 - lb[:, cs]) * _sigmoid(fz)
        kk_ref[0, :, cs] = 1.0 - f
        b = _chunk_cumsum(jnp.log(f), chunk)
        b_ref[0, :, cs] = b
        dec_ref[0, :, cs] = -b.reshape(rows // chunk, chunk, tile)[:, chunk - 1, :]


def _hg_proj(h, g, w_bf16, lb_logits, layer, chunk):
    bsz, seq, d = h.shape
    width = w_bf16.shape[1] // 4
    tm = _row_tile(seq, 4 * d + (4 * 4 + 2) * width, w_bf16.size * 2, SUBLANES * chunk)
    tok = pl.BlockSpec((1, tm, width), lambda b, m: (b, m, 0))
    const = lambda shp: pl.BlockSpec(shp, lambda b, m: (0, 0))
    act = lambda t: jax.ShapeDtypeStruct((bsz, seq, width), t)
    return pl.pallas_call(
        functools.partial(_hg_proj_kernel, layer=layer, chunk=chunk),
        out_shape=[act(F32), act(F32), act(F32), act(BF16), act(F32),
                   jax.ShapeDtypeStruct((bsz, seq // chunk, width), F32)],
        grid=(bsz, seq // tm),
        in_specs=[pl.BlockSpec((1, tm, d), lambda b, m: (b, m, 0)), const((1, d)), const(w_bf16.shape),
                  const(lb_logits.shape)],
        out_specs=[tok, tok, tok, tok, tok, pl.BlockSpec((1, tm // chunk, width), lambda b, m: (b, m, 0))],
        compiler_params=_cparams(("parallel", "parallel")),
        name="hgrn2_proj",
    )(h, g.reshape(1, d), w_bf16, lb_logits)


def _hg_core_kernel(q_ref, kk_ref, b_ref, v_ref, dec_ref, z_ref, h_ref, gn_ref, wo_ref, gp_ref, out_ref,
                    xl_ref, qe_ref, kd_ref, g_ref, st_ref, o_scr, *, chunk):
    rows, width = q_ref.shape[1], q_ref.shape[2]
    pair = 2 * HG_HEAD
    blocks = [2 ** i for i in range(1, int(math.log2(chunk)) + 1)]

    @pl.when(pl.program_id(1) == 0)
    def _():
        st_ref[...] = jnp.zeros_like(st_ref)

    def state_operands():
        q = q_ref[0]
        kk = kk_ref[0]
        b = b_ref[0]
        b3 = b.reshape(rows // chunk, chunk, width)
        b_last = b3[:, chunk - 1:chunk, :]
        g_ref[...] = jnp.exp(b_last.reshape(rows // chunk, width))
        qe_ref[...] = (q * jnp.exp(b)).astype(BF16)
        kd_ref[...] = (kk * jnp.exp(jnp.broadcast_to(b_last, b3.shape).reshape(rows, width) - b)).astype(BF16)
        return q, kk, b

    t_i = lax.broadcasted_iota(jnp.int32, (chunk, HG_HEAD), 0)
    s_i = lax.broadcasted_iota(jnp.int32, (chunk, HG_HEAD), 1) % chunk
    lo = lax.broadcasted_iota(jnp.int32, (chunk, pair), 1) < HG_HEAD
    bd = (lax.broadcasted_iota(jnp.int32, (pair, pair), 0) // HG_HEAD
          == lax.broadcasted_iota(jnp.int32, (pair, pair), 1) // HG_HEAD)

    def split_heads(t):
        z = jnp.zeros_like(t)
        return jnp.concatenate([jnp.where(lo, t, z), jnp.where(lo, z, t)], axis=0)

    def chunk_pairs():
        for c in range(rows // chunk):
            for pr in range(width // pair):
                yield c, pr, slice(c * chunk, (c + 1) * chunk), slice(pr * pair, (pr + 1) * pair)

    def finish(c, pr, rs, ls, sc):
        vpair = v_ref[0, rs, ls]
        st = st_ref[pr]
        o_scr[rs, ls] = _dot(sc.astype(BF16), split_heads(vpair)) + _dot_nt(qe_ref[rs, ls], st.astype(BF16))
        st_ref[pr] = g_ref[c:c + 1, ls] * st + jnp.where(bd, _dot_tn(vpair, kd_ref[rs, ls]), 0.0)

    mild = jnp.max(dec_ref[0]) < HG_MILD_DECAY

    @pl.when(mild)
    def _():
        _, kk, b = state_operands()
        xl_ref[0] = (kk * jnp.exp(-b)).astype(BF16)
        causal = s_i <= t_i
        for c, pr, rs, ls in chunk_pairs():
            sc = jnp.where(causal, _dot_nt(qe_ref[rs, ls], split_heads(xl_ref[0, rs, ls])), 0.0)
            finish(c, pr, rs, ls, sc)

    @pl.when(jnp.logical_not(mild))
    def _():
        q, kk, b = state_operands()
        xl_ref[0] = (q * kk).astype(BF16)
        pos = lax.broadcasted_iota(jnp.int32, (rows, width), 0)
        for li, blk in enumerate(blocks):
            w = jnp.exp(-jnp.abs(b - _block_mid_rows(b, blk)))
            xl_ref[li + 1] = (jnp.where(pos % blk >= blk // 2, q, kk) * w).astype(BF16)
        eye = t_i == s_i
        lmask = [((t_i // blk) == (s_i // blk)) & ((t_i % blk) >= blk // 2) & ((s_i % blk) < blk // 2)
                 for blk in blocks]
        ones_bd = jnp.where(lax.broadcasted_iota(jnp.int32, (pair, HG_HEAD), 0) // HG_HEAD
                            == lax.broadcasted_iota(jnp.int32, (pair, HG_HEAD), 1) // chunk, 1.0, 0.0).astype(BF16)
        for c, pr, rs, ls in chunk_pairs():
            sc = jnp.where(eye, _dot(xl_ref[0, rs, ls], ones_bd), 0.0)
            for li in range(len(blocks)):
                xl = xl_ref[li + 1, rs, ls]
                sc = jnp.where(lmask[li], _dot_nt(xl, split_heads(xl)), sc)
            finish(c, pr, rs, ls, sc)

    out_ref[0] = _gate_out_math(o_scr[...], z_ref[0], h_ref[0], gn_ref[...], wo_ref[...], gp_ref[...], True)


def _hg_core(q, kk, b, v, dec, z, h, norm_g, w_out, g_post, chunk):
    bsz, seq, width = q.shape
    blk = min(HG_BLOCK, seq)
    nlev = int(math.log2(chunk))
    assert HG_HEAD == 2 * chunk and width % (2 * HG_HEAD) == 0 and blk % chunk == 0
    tok = pl.BlockSpec((1, blk, width), lambda b_, n: (b_, n, 0))
    const = lambda shp: pl.BlockSpec(shp, lambda b_, n: (0, 0))
    row = lambda t: t.reshape(1, -1)
    return pl.pallas_call(
        functools.partial(_hg_core_kernel, chunk=chunk),
        out_shape=jax.ShapeDtypeStruct(h.shape, F32),
        grid=(bsz, seq // blk),
        in_specs=[tok, tok, tok, tok, pl.BlockSpec((1, blk // chunk, width), lambda b_, n: (b_, n, 0)),
                  tok, tok, const((1, width)), const(w_out.shape), const((1, width))],
        out_specs=tok,
        scratch_shapes=[pltpu.VMEM((nlev + 1, blk, width), BF16),
                        pltpu.VMEM((blk, width), BF16),
                        pltpu.VMEM((blk, width), BF16),
                        pltpu.VMEM((blk // chunk, width), F32),
                        pltpu.VMEM((width // (2 * HG_HEAD), 2 * HG_HEAD, 2 * HG_HEAD), F32),
                        pltpu.VMEM((blk, width), F32)],
        compiler_params=_cparams(("parallel", "arbitrary")),
        name="hgrn2_core",
    )(q, kk, b, v, dec, z, h, row(norm_g), w_out.astype(BF16), row(g_post))


def _hg_layer(h, g_pre, g_post, w_in, lb_logits, layer, norm_g, w_out):
    chunk = min(HG_CHUNK, h.shape[1])
    q, kk, b, v, z, dec = _hg_proj(h, g_pre, w_in.astype(BF16), lb_logits, layer, chunk)
    return _hg_core(q, kk, b, v, dec, z, h, norm_g, w_out, g_post, chunk)


def _at_proj_kernel(h_ref, g_ref, w_ref, bias_ref, pos_ref, invf_ref, q_ref, k_ref, v_ref, z_ref,
                    *, qw, kvw):
    y = _rms(h_ref[0], g_ref[...]).astype(BF16)
    half = AT_HEAD // 2
    nq4 = LANES // half
    ang = pos_ref[0].astype(F32) * invf_ref[...]
    cos4 = jnp.cos(ang)
    sin4 = jnp.sin(ang)
    lane4 = lax.broadcasted_iota(jnp.int32, ang.shape, 1)

    def spread(t):
        parts = []
        for j in range(nq4):
            m = jnp.where(lane4 // half == j, t, 0.0)
            x = m
            for s in range(1, nq4):
                x = x + pltpu.roll(m, s * half, 1)
            parts.append(x)
        return jnp.concatenate(parts, axis=0)

    lane = lax.broadcasted_iota(jnp.int32, (h_ref.shape[1], LANES), 1)
    first = (lane % AT_HEAD) < half
    low = lane < AT_HEAD
    cosf = spread(cos4)
    sinf = spread(sin4)
    sinf = jnp.where(first, -sinf, sinf)

    def rope(t):
        partner = jnp.where(first, pltpu.roll(t, LANES - AT_HEAD // 2, 1), pltpu.roll(t, AT_HEAD // 2, 1))
        return t * cosf + partner * sinf

    def store_dup(ref, j, t):
        r = pltpu.roll(t, AT_HEAD, 1)
        ref[0, :, (2 * j) * LANES:(2 * j + 1) * LANES] = jnp.where(low, t, r).astype(ref.dtype)
        ref[0, :, (2 * j + 1) * LANES:(2 * j + 2) * LANES] = jnp.where(low, r, t).astype(ref.dtype)

    qkv_w = qw + 2 * kvw
    z_ref[0] = _dot(y, w_ref[:, qkv_w:qkv_w + qw])
    qf = _dot(y, w_ref[:, 0:qw]) + bias_ref[:, 0:qw]
    kvf = _dot(y, w_ref[:, qw:qkv_w]) + bias_ref[:, qw:qkv_w]
    for j in range(qw // LANES):
        sl = slice(j * LANES, (j + 1) * LANES)
        q_ref[0, :, sl] = (rope(qf[:, sl]) * AT_HEAD ** -0.5).astype(q_ref.dtype)
    for j in range(kvw // LANES):
        store_dup(k_ref, j, rope(kvf[:, j * LANES:(j + 1) * LANES]))
        store_dup(v_ref, j, kvf[:, kvw + j * LANES:kvw + (j + 1) * LANES])


def _at_proj(h, g, w_bf16, bias, positions, qw, kvw):
    bsz, seq, d = h.shape
    tm = _row_tile(seq, 4 * d + 2 * qw + 4 * 2 * kvw + 4 * qw, w_bf16.size * 2, 4 * SUBLANES)
    half = AT_HEAD // 2
    inv_freq = ROPE_THETA ** (-jnp.arange(0, AT_HEAD, 2, dtype=F32) / AT_HEAD)
    nq4 = LANES // half
    invf = jnp.tile(inv_freq, nq4).reshape(1, LANES)
    pos4 = positions.reshape(bsz, seq // tm, nq4, tm // nq4).transpose(0, 1, 3, 2)
    pos4 = jnp.repeat(pos4, half, axis=-1).reshape(bsz, seq // nq4, LANES)
    tokspec = lambda w: pl.BlockSpec((1, tm, w), lambda b, m: (b, m, 0))
    const = lambda shp: pl.BlockSpec(shp, lambda b, m: (0, 0))
    return pl.pallas_call(
        functools.partial(_at_proj_kernel, qw=qw, kvw=kvw),
        out_shape=[jax.ShapeDtypeStruct((bsz, seq, qw), BF16), jax.ShapeDtypeStruct((bsz, seq, 2 * kvw), BF16),
                   jax.ShapeDtypeStruct((bsz, seq, 2 * kvw), BF16), jax.ShapeDtypeStruct((bsz, seq, qw), F32)],
        grid=(bsz, seq // tm),
        in_specs=[tokspec(d), const((1, d)), const(w_bf16.shape), const((1, qw + 2 * kvw)),
                  pl.BlockSpec((1, tm // nq4, LANES), lambda b, m: (b, m, 0)), const((1, LANES))],
        out_specs=[tokspec(qw), tokspec(2 * kvw), tokspec(2 * kvw), tokspec(qw)],
        compiler_params=_cparams(("parallel", "parallel")),
        name="attn_proj",
    )(h, g.reshape(1, d), w_bf16, bias.reshape(1, -1), pos4, invf)


def _at_core_kernel(q_ref, kc_ref, kp_ref, vc_ref, vp_ref, sink_ref, z_ref, h_ref, wo_ref, gp_ref, out_ref):
    blk = AT_BLOCK
    nsub = q_ref.shape[1] // blk
    n = pl.program_id(1)
    ngroup = kc_ref.shape[2] // LANES
    qi = lax.broadcasted_iota(jnp.int32, (2 * blk, 4 * blk), 0) % blk
    kj = lax.broadcasted_iota(jnp.int32, (2 * blk, 4 * blk), 1) % (2 * blk)
    dist = qi + blk - kj
    in_window = (dist >= 0) & (dist < blk)
    first_mask = in_window & ((kj >= blk) | (n > 0))
    low = lax.broadcasted_iota(jnp.int32, (2 * blk, LANES), 1) < AT_HEAD
    top = lax.broadcasted_iota(jnp.int32, (2 * blk, 1), 0) < blk
    ones_lo = jnp.where(low, 1.0, 0.0).astype(BF16)
    ones_hi = jnp.where(low, 0.0, 1.0).astype(BF16)
    items = [(j, g) for j in range(nsub) for g in range(ngroup)]

    def band(cur_ref, prev_ref, j, g):
        sl = slice(g * LANES, (g + 1) * LANES)
        prev = prev_ref[0, :, sl] if j == 0 else cur_ref[0, (j - 1) * blk:j * blk, sl]
        return jnp.concatenate([prev, cur_ref[0, j * blk:(j + 1) * blk, sl]], axis=0)

    o_rows = [[] for _ in range(nsub)]
    for w0 in range(0, len(items), AT_WAVE):
        wave = items[w0:w0 + AT_WAVE]
        scores = []
        for j, g in wave:
            kdup = band(kc_ref, kp_ref, j, g)
            rhs = jnp.concatenate([jnp.where(low, kdup, 0.0), jnp.where(low, 0.0, kdup)], axis=0)
            rows = slice(j * blk, (j + 1) * blk)
            lhs = jnp.concatenate([q_ref[0, rows, (2 * g) * LANES:(2 * g + 1) * LANES],
                                   q_ref[0, rows, (2 * g + 1) * LANES:(2 * g + 2) * LANES]], axis=0)
            scores.append(_dot_nt(lhs, rhs))

        probs, sink_terms = [], []
        for (j, g), sc in zip(wave, scores):
            s = jnp.where(first_mask if j == 0 else in_window, sc, -jnp.inf)
            sink_a = jnp.where(top, sink_ref[0:1, 4 * g:4 * g + 1], sink_ref[0:1, 4 * g + 2:4 * g + 3])
            sink_b = jnp.where(top, sink_ref[0:1, 4 * g + 1:4 * g + 2], sink_ref[0:1, 4 * g + 3:4 * g + 4])
            m_a = jnp.maximum(jnp.max(s[:, :2 * blk], axis=-1, keepdims=True), sink_a)
            m_b = jnp.maximum(jnp.max(s[:, 2 * blk:], axis=-1, keepdims=True), sink_b)
            e = jnp.concatenate([jnp.exp(s[:, :2 * blk] - m_a), jnp.exp(s[:, 2 * blk:] - m_b)], axis=1)
            probs.append(e.astype(BF16))
            sink_terms.append(jnp.where(low, jnp.exp(sink_a - m_a), jnp.exp(sink_b - m_b)))

        for (j, g), p, st in zip(wave, probs, sink_terms):
            vdup = band(vc_ref, vp_ref, j, g)
            va = jnp.where(low, vdup, 0.0)
            vb = jnp.where(low, 0.0, vdup)
            rhs = jnp.concatenate([jnp.concatenate([va, ones_lo], axis=1),
                                   jnp.concatenate([vb, ones_hi], axis=1)], axis=0)
            out = _dot(p, rhs)
            o = out[:, :LANES] / (out[:, LANES:] + st)
            o_rows[j] += [o[:blk], o[blk:]]

    o = jnp.concatenate([jnp.concatenate(slabs, axis=-1) for slabs in o_rows], axis=0)
    out_ref[0] = _gate_out_math(o, z_ref[0], h_ref[0], None, wo_ref[...], gp_ref[...], False)


def _at_core(q, k_dup, v_dup, sinks, z, h, w_out, g_post):
    bsz, seq, qw = q.shape
    kvw = k_dup.shape[-1]
    d = h.shape[-1]
    blk = AT_BLOCK
    nsub = AT_SUBBLOCKS if seq % (AT_SUBBLOCKS * blk) == 0 else 1
    nq = qw // AT_HEAD
    assert nq == 4 * (kvw // LANES) and qw == 2 * kvw
    cur = lambda w: pl.BlockSpec((1, nsub * blk, w), lambda b, n: (b, n, 0))
    prev = lambda w: pl.BlockSpec((1, blk, w), lambda b, n: (b, jnp.maximum(nsub * n - 1, 0), 0))
    const = lambda shp: pl.BlockSpec(shp, lambda b, n: (0, 0))
    return pl.pallas_call(
        _at_core_kernel,
        out_shape=jax.ShapeDtypeStruct(h.shape, F32),
        grid=(bsz, seq // (nsub * blk)),
        in_specs=[cur(qw), cur(kvw), prev(kvw), cur(kvw), prev(kvw), const((1, nq)),
                  cur(qw), cur(d), const(w_out.shape), const((1, d))],
        out_specs=cur(d),
        compiler_params=_cparams(("parallel", "arbitrary")),
        name="attn_core",
    )(q, k_dup, k_dup, v_dup, v_dup, sinks.reshape(1, nq), z, h, w_out.astype(BF16), g_post.reshape(1, d))


def _at_layer(h, positions, g_pre, g_post, w_in, b_in, sinks, w_out):
    qw = w_out.shape[0]
    kvw = (b_in.shape[0] - qw) // 2
    q, k, v, z = _at_proj(h, g_pre, w_in.astype(BF16), b_in, positions, qw, kvw)
    return _at_core(q, k, v, sinks, z, h, w_out, g_post)


def kernel(x, positions, norm_pre, norm_post, s5_w_in, s5_lambda_re, s5_lambda_im, s5_log_dt, s5_b_re, s5_b_im, s5_c_re, s5_c_im, s5_d, s5_w_glu, s5_b_glu, s5_w_out, hg_w_in, hg_lb_logits, hg_norm, hg_w_out, at_w_in, at_b_in, at_sinks, at_w_out):
    depth = norm_pre.shape[0]
    s5_weights = _s5_weights(s5_lambda_re, s5_lambda_im, s5_log_dt, s5_b_re, s5_b_im, s5_c_re, s5_c_im, s5_d)
    s5_w_in, s5_w_glu, s5_w_out = (t.astype(BF16) for t in (s5_w_in, s5_w_glu, s5_w_out))
    h = x
    for i in range(depth):
        kind, j = i % 3, i // 3
        if kind == 0:
            h = _s5_layer(h, norm_pre[i], norm_post[i], s5_w_in, s5_weights, j, s5_w_glu, s5_b_glu[j], s5_w_out)
        elif kind == 1:
            h = _hg_layer(h, norm_pre[i], norm_post[i], hg_w_in[j], hg_lb_logits, i, hg_norm[j], hg_w_out[j])
        else:
            h = _at_layer(h, positions, norm_pre[i], norm_post[i], at_w_in[j], at_b_in[j], at_sinks[j],
                          at_w_out[j])
    return h
```

```python
import functools
import math

import jax
import jax.numpy as jnp
from jax import lax
from jax.experimental import pallas as pl
from jax.experimental.pallas import tpu as pltpu

F32 = jnp.float32
BF16 = jnp.bfloat16

NORM_EPS = 1e-6
LANES = 128
SUBLANES = 8
VMEM_LIMIT = 56 * 1024 * 1024

S5_GROUP = 16
S5_STATE = 64
S5_TILE_GROUPS = 16
S5_TILE_CH = S5_TILE_GROUPS * S5_GROUP
S5_TILE_ST = S5_TILE_GROUPS * S5_STATE
S5_STEPS = 128
HG_HEAD = 128
HG_CHUNK = 64
HG_BLOCK = 512
HG_MILD_DECAY = 80.0
AT_HEAD = 64
AT_BLOCK = 128
AT_SUBBLOCKS = 4
AT_WAVE = 16
ROPE_THETA = 10000.0

ROW_TILES = (1024, 512, 256)


def _cparams(sem, n_in=0, fuse=()):
    fusion = [i in fuse for i in range(n_in)] if fuse else None
    return pltpu.CompilerParams(dimension_semantics=sem, vmem_limit_bytes=VMEM_LIMIT, allow_input_fusion=fusion)


def _row_tile(seq, row_bytes, resident_bytes, multiple=SUBLANES):
    budget = VMEM_LIMIT - VMEM_LIMIT // 4
    for tm in ROW_TILES:
        if seq % tm == 0 and tm % multiple == 0 and 2 * (tm * row_bytes + resident_bytes) <= budget:
            return tm
    return seq


def _rms(x, g):
    return x * lax.rsqrt(jnp.mean(x * x, axis=-1, keepdims=True) + NORM_EPS) * g


def _sigmoid(x):
    return 1.0 / (1.0 + jnp.exp(-x))


def _sigmoid_gate(x):
    return 0.5 + 0.5 * jnp.tanh(0.5 * x)


def _silu(x):
    return x * _sigmoid_gate(x)


def _gelu_tanh(x):
    c = math.sqrt(2.0 / math.pi)
    return 0.5 * x * (1.0 + jnp.tanh(c * (x + 0.044715 * (x * x * x))))


def _dot(a, b):
    return jnp.dot(a, b, preferred_element_type=F32)


def _dot_nt(a, b):
    return lax.dot_general(a, b, (((1,), (1,)), ((), ())), preferred_element_type=F32)


def _dot_tn(a, b):
    return lax.dot_general(a, b, (((0,), (0,)), ((), ())), preferred_element_type=F32)


def _chunk_row_groups(rows, steps):
    for base in range(0, rows, SUBLANES * steps):
        for r in range(SUBLANES):
            for k0 in range(0, steps, SUBLANES):
                yield base + r * steps + k0, base + k0 * SUBLANES + r


def _norm_proj_kernel(h_ref, g_ref, w_ref, *out_refs, widths, slab_steps):
    y = _rms(h_ref[0], g_ref[...]).astype(BF16)
    off = 0
    for o_ref, width, steps in zip(out_refs, widths, slab_steps):
        r = _dot(y, w_ref[:, off:off + width])
        if steps:
            for nat, cm in _chunk_row_groups(r.shape[0], steps):
                for j in range(width // LANES):
                    o_ref[0, j, pl.ds(cm, SUBLANES, stride=SUBLANES), :] = (
                        r[nat:nat + SUBLANES, j * LANES:(j + 1) * LANES])
        else:
            o_ref[0] = r.astype(o_ref.dtype)
        off += width


def _norm_proj(h, g, w_bf16, widths, slab_steps, dtypes):
    bsz, seq, d = h.shape
    row_bytes = 4 * d + sum(w * jnp.dtype(t).itemsize for w, t in zip(widths, dtypes))
    tm = _row_tile(seq, row_bytes, w_bf16.size * 2, SUBLANES * max(max(slab_steps), 1))
    out_shape, out_specs = [], []
    for width, as_slab, dtype in zip(widths, slab_steps, dtypes):
        if as_slab:
            assert tm % (SUBLANES * as_slab) == 0 and dtype == F32
            ns = width // LANES
            out_shape.append(jax.ShapeDtypeStruct((bsz, ns, seq, LANES), F32))
            out_specs.append(pl.BlockSpec((1, ns, tm, LANES), lambda b, m: (b, 0, m, 0)))
        else:
            out_shape.append(jax.ShapeDtypeStruct((bsz, seq, width), dtype))
            out_specs.append(pl.BlockSpec((1, tm, width), lambda b, m: (b, m, 0)))
    return pl.pallas_call(
        functools.partial(_norm_proj_kernel, widths=tuple(widths), slab_steps=tuple(slab_steps)),
        out_shape=out_shape,
        grid=(bsz, seq // tm),
        in_specs=[pl.BlockSpec((1, tm, d), lambda b, m: (b, m, 0)),
                  pl.BlockSpec((1, d), lambda b, m: (0, 0)),
                  pl.BlockSpec(w_bf16.shape, lambda b, m: (0, 0))],
        out_specs=out_specs,
        compiler_params=_cparams(("parallel", "parallel"), 3, (2,)),
        name="norm_proj",
    )(h, g.reshape(1, d), w_bf16)


def _zoh(lr, li, ldt):
    dt = jnp.exp(ldt)
    mag = jnp.exp(lr * dt)
    ar = mag * jnp.cos(li * dt)
    ai = mag * jnp.sin(li * dt)
    den = lr * lr + li * li
    qr = ((ar - 1.0) * lr + ai * li) / den
    qi = (ai * lr - (ar - 1.0) * li) / den
    return ar, ai, qr, qi


def _s5_tiles_kernel(lr_ref, li_ref, ldt_ref, br_ref, bi_ref, cr_ref, ci_ref, ar_ref, ai_ref, wb_ref, wc_ref):
    ch, st, tg = S5_TILE_CH, S5_TILE_ST, S5_TILE_GROUPS
    ar, ai, qr, qi = _zoh(lr_ref[...], li_ref[...], ldt_ref[...])
    ar_ref[...] = ar
    ai_ref[...] = ai

    sel_p = jnp.where(lax.broadcasted_iota(jnp.int32, (st, S5_STATE), 0) % S5_STATE
                      == lax.broadcasted_iota(jnp.int32, (st, S5_STATE), 1), 1.0, 0.0).astype(BF16)
    own_group = (lax.broadcasted_iota(jnp.int32, (st, tg), 0) // S5_STATE
                 == lax.broadcasted_iota(jnp.int32, (st, tg), 1))

    def to_rows(m):
        hi = m.astype(BF16)
        r1 = m - hi.astype(F32)
        mid = r1.astype(BF16)
        lo = (r1 - mid.astype(F32)).astype(BF16)
        spread = _dot_nt(sel_p, hi) + _dot_nt(sel_p, mid) + _dot_nt(sel_p, lo)
        return jnp.sum(jnp.where(own_group, spread, 0.0), axis=1, keepdims=True)

    qr = to_rows(qr)
    qi = to_rows(qi)
    br = br_ref[...]
    bi = bi_ref[...]
    bbr = (qr * br - qi * bi).astype(BF16)
    bbi = (qr * bi + qi * br).astype(BF16)

    sel_h = jnp.where(lax.broadcasted_iota(jnp.int32, (ch, S5_GROUP), 0) % S5_GROUP
                      == lax.broadcasted_iota(jnp.int32, (ch, S5_GROUP), 1), 1.0, 0.0).astype(BF16)
    diag_in = (lax.broadcasted_iota(jnp.int32, (ch, st), 0) // S5_GROUP
               == lax.broadcasted_iota(jnp.int32, (ch, st), 1) // S5_STATE)
    wb_ref[0, :, 0:st] = jnp.where(diag_in, _dot_nt(sel_h, bbr), 0.0).astype(BF16)
    wb_ref[0, :, st:2 * st] = jnp.where(diag_in, _dot_nt(sel_h, bbi), 0.0).astype(BF16)
    diag_out = (lax.broadcasted_iota(jnp.int32, (st, ch), 0) // S5_STATE
                == lax.broadcasted_iota(jnp.int32, (st, ch), 1) // S5_GROUP)
    wc_ref[0, 0:st, :] = jnp.where(diag_out, _dot_nt(sel_p, cr_ref[...].astype(BF16)), 0.0).astype(BF16)
    wc_ref[0, st:2 * st, :] = jnp.where(diag_out, -_dot_nt(sel_p, ci_ref[...].astype(BF16)), 0.0).astype(BF16)


def _s5_core_kernel(u_ref, wb_ref, wc_ref, a_ref, d_ref, y_ref, x_ref, s_ref, carry_ref, *, steps):
    nst = S5_TILE_ST
    nslab = S5_TILE_CH // LANES
    ntile = wb_ref.shape[0]

    @pl.when(pl.program_id(1) == 0)
    def _():
        carry_ref[...] = jnp.zeros_like(carry_ref)

    def load_u(i):
        return jnp.concatenate([u_ref[0, i * nslab + j] for j in range(nslab)], axis=-1)

    def project_in(i):
        x_ref[i % 2] = _dot(load_u(i).astype(BF16), wb_ref[i])

    def scan(i):
        slot = i % 2
        ar1 = a_ref[i, 0:1, :]
        ai1 = a_ref[i, 1:2, :]
        ar = jnp.broadcast_to(ar1, (SUBLANES, nst))
        ai = jnp.broadcast_to(ai1, (SUBLANES, nst))

        def advance(k, sr, si):
            xr = x_ref[slot, k * SUBLANES:(k + 1) * SUBLANES, 0:nst]
            xi = x_ref[slot, k * SUBLANES:(k + 1) * SUBLANES, nst:2 * nst]
            return ar * sr - ai * si + xr, ar * si + ai * sr + xi

        er = jnp.zeros((SUBLANES, nst), F32)
        ei = jnp.zeros((SUBLANES, nst), F32)
        for k in range(steps):
            er, ei = advance(k, er, ei)

        pr, pi = ar1, ai1
        for _ in range(int(math.log2(steps))):
            pr, pi = pr * pr - pi * pi, 2.0 * pr * pi

        rid = lax.broadcasted_iota(jnp.int32, (SUBLANES, nst), 0)
        cr = carry_ref[i, 0:1, :]
        ci = carry_ref[i, 1:2, :]
        sr = jnp.zeros((SUBLANES, nst), F32)
        si = jnp.zeros((SUBLANES, nst), F32)
        for r in range(SUBLANES):
            sr = jnp.where(rid == r, jnp.broadcast_to(cr, (SUBLANES, nst)), sr)
            si = jnp.where(rid == r, jnp.broadcast_to(ci, (SUBLANES, nst)), si)
            cr, ci = pr * cr - pi * ci + er[r:r + 1, :], pr * ci + pi * cr + ei[r:r + 1, :]
        carry_ref[i, 0:1, :] = cr
        carry_ref[i, 1:2, :] = ci

        for k in range(0, steps, 2):
            r0, i0 = advance(k, sr, si)
            sr, si = advance(k + 1, r0, i0)
            rows2 = slice(k * SUBLANES, (k + 2) * SUBLANES)
            s_ref[slot, rows2, 0:nst] = jnp.concatenate([r0, sr], axis=0).astype(BF16)
            s_ref[slot, rows2, nst:2 * nst] = jnp.concatenate([i0, si], axis=0).astype(BF16)

    def project_out(i):
        y = _dot(s_ref[i % 2], wc_ref[i]) + d_ref[i] * load_u(i)
        for j in range(nslab):
            y_ref[0, i * nslab + j] = y[:, j * LANES:(j + 1) * LANES]

    project_in(0)
    for i in range(ntile):
        if i + 1 < ntile:
            project_in(i + 1)
        scan(i)
        project_out(i)


def _s5_core(u_slab, wb, wc, a_tiles, d_tiles, layer, steps):
    bsz, nslab, seq, _ = u_slab.shape
    ntile = nslab * LANES // S5_TILE_CH
    rows = SUBLANES * steps
    const = lambda shp: pl.BlockSpec((ntile,) + shp[1:], lambda b, n: (layer,) + (0,) * (len(shp) - 1),
                                     pipeline_mode=pl.Buffered(1))
    tok = pl.BlockSpec((1, nslab, rows, LANES), lambda b, n: (b, 0, n, 0))
    return pl.pallas_call(
        functools.partial(_s5_core_kernel, steps=steps),
        out_shape=jax.ShapeDtypeStruct(u_slab.shape, F32),
        grid=(bsz, seq // rows),
        in_specs=[tok, const(wb.shape), const(wc.shape), const(a_tiles.shape), const(d_tiles.shape)],
        out_specs=tok,
        scratch_shapes=[pltpu.VMEM((2, rows, 2 * S5_TILE_ST), F32),
                        pltpu.VMEM((2, rows, 2 * S5_TILE_ST), BF16),
                        pltpu.VMEM((ntile, 2, S5_TILE_ST), F32)],
        compiler_params=_cparams(("parallel", "arbitrary")),
        name="s5_core",
    )(u_slab, wb, wc, a_tiles, d_tiles)


def _s5_post_kernel(y_ref, z_ref, h_ref, wg_ref, bg_ref, wo_ref, gp_ref, o_ref, *, steps):
    nslab = y_ref.shape[1]
    groups = list(_chunk_row_groups(y_ref.shape[2], steps))
    y = jnp.concatenate(
        [jnp.concatenate([y_ref[0, j, pl.ds(cm, SUBLANES, stride=SUBLANES), :] for _, cm in groups], axis=0)
         for j in range(nslab)], axis=-1)
    y = _gelu_tanh(y)
    y = y * _sigmoid_gate(_dot(y.astype(BF16), wg_ref[...]) + bg_ref[...])
    t = y * _silu(z_ref[0])
    o = _dot(t.astype(BF16), wo_ref[...])
    o_ref[0] = h_ref[0] + _rms(o, gp_ref[...])


def _s5_post(y_slab, z, h, w_glu, b_glu, w_out, g_post, steps):
    bsz, seq, d = h.shape
    nslab = y_slab.shape[1]
    tm = _row_tile(seq, 4 * 4 * d, 2 * 2 * d * d, SUBLANES * steps)
    assert tm % (SUBLANES * steps) == 0
    row = lambda t: t.reshape(1, -1)
    vec = pl.BlockSpec((1, d), lambda b, m: (0, 0))
    mat = pl.BlockSpec((d, d), lambda b, m: (0, 0))
    slab = pl.BlockSpec((1, nslab, tm, LANES), lambda b, m: (b, 0, m, 0))
    tok = pl.BlockSpec((1, tm, d), lambda b, m: (b, m, 0))
    return pl.pallas_call(
        functools.partial(_s5_post_kernel, steps=steps),
        out_shape=jax.ShapeDtypeStruct(h.shape, F32),
        grid=(bsz, seq // tm),
        in_specs=[slab, tok, tok, mat, vec, mat, vec],
        out_specs=tok,
        compiler_params=_cparams(("parallel", "parallel"), 7, (3, 5)),
        name="s5_post",
    )(y_slab, z, h, w_glu.astype(BF16), row(b_glu), w_out.astype(BF16), row(g_post))


def _s5_weights(lam_re, lam_im, log_dt, b_re, b_im, c_re, c_im, d_skip):
    nlayer, ngroup, nstate, gsize = b_re.shape
    assert (nstate, gsize) == (S5_STATE, S5_GROUP) and ngroup % S5_TILE_GROUPS == 0
    ng = nlayer * ngroup
    ntile = ng // S5_TILE_GROUPS
    tg = S5_TILE_GROUPS
    ldt = log_dt.reshape(ng, 1)
    grp = lambda w: pl.BlockSpec((tg, w), lambda i: (i, 0))
    rows = lambda r, w: pl.BlockSpec((r, w), lambda i: (i, 0))
    ar, ai, wb, wc = pl.pallas_call(
        _s5_tiles_kernel,
        out_shape=[jax.ShapeDtypeStruct((ng, nstate), F32), jax.ShapeDtypeStruct((ng, nstate), F32),
                   jax.ShapeDtypeStruct((ntile, S5_TILE_CH, 2 * S5_TILE_ST), BF16),
                   jax.ShapeDtypeStruct((ntile, 2 * S5_TILE_ST, S5_TILE_CH), BF16)],
        grid=(ntile,),
        in_specs=[grp(nstate), grp(nstate), grp(1),
                  rows(tg * nstate, gsize), rows(tg * nstate, gsize),
                  rows(tg * gsize, nstate), rows(tg * gsize, nstate)],
        out_specs=[grp(nstate), grp(nstate),
                   pl.BlockSpec((1, S5_TILE_CH, 2 * S5_TILE_ST), lambda i: (i, 0, 0)),
                   pl.BlockSpec((1, 2 * S5_TILE_ST, S5_TILE_CH), lambda i: (i, 0, 0))],
        compiler_params=_cparams(("parallel",)),
        name="s5_tiles",
    )(lam_re.reshape(ng, nstate), lam_im.reshape(ng, nstate), ldt,
      b_re.reshape(ng * nstate, gsize), b_im.reshape(ng * nstate, gsize),
      c_re.reshape(ng * gsize, nstate), c_im.reshape(ng * gsize, nstate))
    a_tiles = jnp.stack([ar.reshape(ntile, S5_TILE_ST), ai.reshape(ntile, S5_TILE_ST)], axis=1)
    d_tiles = d_skip.reshape(ntile, 1, S5_TILE_CH)
    return wb, wc, a_tiles, d_tiles


def _s5_layer(h, g_pre, g_post, w_in, weights, layer, w_glu, b_glu, w_out):
    bsz, seq, d = h.shape
    steps = min(S5_STEPS, seq // SUBLANES)
    u_slab, z = _norm_proj(h, g_pre, w_in.astype(BF16), (d, d), (steps, 0), (F32, F32))
    y_slab = _s5_core(u_slab, *weights, layer, steps)
    return _s5_post(y_slab, z, h, w_glu, b_glu, w_out, g_post, steps)


def _gate_out_math(o, z, h, g_head, w_out, g_post, head_norm):
    if head_norm:
        parts = []
        for j in range(o.shape[-1] // HG_HEAD):
            oj = o[:, j * HG_HEAD:(j + 1) * HG_HEAD]
            parts.append(oj * lax.rsqrt(jnp.mean(oj * oj, axis=-1, keepdims=True) + NORM_EPS))
        o = jnp.concatenate(parts, axis=-1) * g_head
    t = o * _silu(z)
    return h + _rms(_dot(t.astype(BF16), w_out), g_post)


def _chunk_cumsum(x, chunk):
    rows, w = x.shape
    per = chunk // SUBLANES
    x3 = x.reshape(rows // SUBLANES, SUBLANES, w)
    pos = lax.broadcasted_iota(jnp.int32, x3.shape, 1)
    sh = 1
    while sh < SUBLANES:
        x3 = x3 + jnp.where(pos >= sh, pltpu.roll(x3, sh, 1), 0.0)
        sh *= 2
    x4 = x3.reshape(rows // chunk, per, SUBLANES, w)
    tot = x4[:, :, SUBLANES - 1:SUBLANES, :]
    pref = [jnp.zeros_like(tot[:, 0:1])]
    for j in range(1, per):
        pref.append(pref[-1] + tot[:, j - 1:j])
    return (x4 + jnp.concatenate(pref, axis=1)).reshape(rows, w)


def _block_mid_rows(x, blk):
    rows, w = x.shape
    half = blk // 2
    if blk >= SUBLANES:
        x3 = x.reshape(rows // blk, blk, w)
        return jnp.broadcast_to(x3[:, half - 1:half, :], (rows // blk, blk, w)).reshape(rows, w)
    pos = lax.broadcasted_iota(jnp.int32, (rows, w), 0) % blk
    out = x
    for p in range(blk):
        d = p - (half - 1)
        if d != 0:
            out = jnp.where(pos == p, pltpu.roll(x, d % rows, 0), out)
    return out


def _hg_proj_kernel(h_ref, g_ref, w_ref, lbl_ref, q_ref, kk_ref, b_ref, v_ref, z_ref, dec_ref, *, layer, chunk):
    width = q_ref.shape[2]
    rows = q_ref.shape[1]
    y = _rms(h_ref[0], g_ref[...]).astype(BF16)

    lg = lbl_ref[...]
    ex = jnp.exp(lg - jnp.max(lg, axis=0, keepdims=True))
    p = ex / jnp.sum(ex, axis=0, keepdims=True)
    lb = jnp.zeros((1, width), F32)
    for j in range(1, layer + 1):
        lb = lb + p[j:j + 1, :]

    tile = 2 * LANES
    for c0 in range(0, width, tile):
        cs = slice(c0, c0 + tile)
        fz = _dot(y, w_ref[:, width + c0:width + c0 + tile])
        q_ref[0, :, cs] = _dot(y, w_ref[:, c0:c0 + tile])
        v_ref[0, :, cs] = _dot(y, w_ref[:, 2 * width + c0:2 * width + c0 + tile]).astype(v_ref.dtype)
        z_ref[0, :, cs] = _dot(y, w_ref[:, 3 * width + c0:3 * width + c0 + tile])
        f = lb[:, cs] + (1.0 - lb[:, cs]) * _sigmoid(fz)
        kk_ref[0, :, cs] = 1.0 - f
        b = _chunk_cumsum(jnp.log(f), chunk)
        b_ref[0, :, cs] = b
        dec_ref[0, :, cs] = -b.reshape(rows // chunk, chunk, tile)[:, chunk - 1, :]


def _hg_proj(h, g, w_bf16, lb_logits, layer, chunk):
    bsz, seq, d = h.shape
    width = w_bf16.shape[1] // 4
    tm = _row_tile(seq, 4 * d + (4 * 4 + 2) * width, w_bf16.size * 2, SUBLANES * chunk)
    tok = pl.BlockSpec((1, tm, width), lambda b, m: (b, m, 0))
    const = lambda shp: pl.BlockSpec(shp, lambda b, m: (0, 0))
    act = lambda t: jax.ShapeDtypeStruct((bsz, seq, width), t)
    return pl.pallas_call(
        functools.partial(_hg_proj_kernel, layer=layer, chunk=chunk),
        out_shape=[act(F32), act(F32), act(F32), act(BF16), act(F32),
                   jax.ShapeDtypeStruct((bsz, seq // chunk, width), F32)],
        grid=(bsz, seq // tm),
        in_specs=[pl.BlockSpec((1, tm, d), lambda b, m: (b, m, 0)), const((1, d)), const(w_bf16.shape),
                  const(lb_logits.shape)],
        out_specs=[tok, tok, tok, tok, tok, pl.BlockSpec((1, tm // chunk, width), lambda b, m: (b, m, 0))],
        compiler_params=_cparams(("parallel", "parallel"), 4, (2,)),
        name="hgrn2_proj",
    )(h, g.reshape(1, d), w_bf16, lb_logits)


def _hg_core_kernel(q_ref, kk_ref, b_ref, v_ref, dec_ref, z_ref, h_ref, gn_ref, wo_ref, gp_ref, out_ref,
                    xl_ref, qe_ref, kd_ref, g_ref, st_ref, o_scr, *, chunk):
    rows, width = q_ref.shape[1], q_ref.shape[2]
    pair = 2 * HG_HEAD
    blocks = [2 ** i for i in range(1, int(math.log2(chunk)) + 1)]

    @pl.when(pl.program_id(1) == 0)
    def _():
        st_ref[...] = jnp.zeros_like(st_ref)

    def state_operands():
        q = q_ref[0]
        kk = kk_ref[0]
        b = b_ref[0]
        b3 = b.reshape(rows // chunk, chunk, width)
        b_last = b3[:, chunk - 1:chunk, :]
        g_ref[...] = jnp.exp(b_last.reshape(rows // chunk, width))
        qe_ref[...] = (q * jnp.exp(b)).astype(BF16)
        kd_ref[...] = (kk * jnp.exp(jnp.broadcast_to(b_last, b3.shape).reshape(rows, width) - b)).astype(BF16)
        return q, kk, b

    t_i = lax.broadcasted_iota(jnp.int32, (chunk, HG_HEAD), 0)
    s_i = lax.broadcasted_iota(jnp.int32, (chunk, HG_HEAD), 1) % chunk
    lo = lax.broadcasted_iota(jnp.int32, (chunk, pair), 1) < HG_HEAD
    bd = (lax.broadcasted_iota(jnp.int32, (pair, pair), 0) // HG_HEAD
          == lax.broadcasted_iota(jnp.int32, (pair, pair), 1) // HG_HEAD)

    def split_heads(t):
        z = jnp.zeros_like(t)
        return jnp.concatenate([jnp.where(lo, t, z), jnp.where(lo, z, t)], axis=0)

    def chunk_pairs():
        for c in range(rows // chunk):
            for pr in range(width // pair):
                yield c, pr, slice(c * chunk, (c + 1) * chunk), slice(pr * pair, (pr + 1) * pair)

    def finish(c, pr, rs, ls, sc):
        vpair = v_ref[0, rs, ls]
        st = st_ref[pr]
        o_scr[rs, ls] = _dot(sc.astype(BF16), split_heads(vpair)) + _dot_nt(qe_ref[rs, ls], st.astype(BF16))
        st_ref[pr] = g_ref[c:c + 1, ls] * st + jnp.where(bd, _dot_tn(vpair, kd_ref[rs, ls]), 0.0)

    mild = jnp.max(dec_ref[0]) < HG_MILD_DECAY

    @pl.when(mild)
    def _():
        _, kk, b = state_operands()
        xl_ref[0] = (kk * jnp.exp(-b)).astype(BF16)
        causal = s_i <= t_i
        for c, pr, rs, ls in chunk_pairs():
            sc = jnp.where(causal, _dot_nt(qe_ref[rs, ls], split_heads(xl_ref[0, rs, ls])), 0.0)
            finish(c, pr, rs, ls, sc)

    @pl.when(jnp.logical_not(mild))
    def _():
        q, kk, b = state_operands()
        xl_ref[0] = (q * kk).astype(BF16)
        pos = lax.broadcasted_iota(jnp.int32, (rows, width), 0)
        for li, blk in enumerate(blocks):
            w = jnp.exp(-jnp.abs(b - _block_mid_rows(b, blk)))
            xl_ref[li + 1] = (jnp.where(pos % blk >= blk // 2, q, kk) * w).astype(BF16)
        eye = t_i == s_i
        lmask = [((t_i // blk) == (s_i // blk)) & ((t_i % blk) >= blk // 2) & ((s_i % blk) < blk // 2)
                 for blk in blocks]
        ones_bd = jnp.where(lax.broadcasted_iota(jnp.int32, (pair, HG_HEAD), 0) // HG_HEAD
                            == lax.broadcasted_iota(jnp.int32, (pair, HG_HEAD), 1) // chunk, 1.0, 0.0).astype(BF16)
        for c, pr, rs, ls in chunk_pairs():
            sc = jnp.where(eye, _dot(xl_ref[0, rs, ls], ones_bd), 0.0)
            for li in range(len(blocks)):
                xl = xl_ref[li + 1, rs, ls]
                sc = jnp.where(lmask[li], _dot_nt(xl, split_heads(xl)), sc)
            finish(c, pr, rs, ls, sc)

    out_ref[0] = _gate_out_math(o_scr[...], z_ref[0], h_ref[0], gn_ref[...], wo_ref[...], gp_ref[...], True)


def _hg_core(q, kk, b, v, dec, z, h, norm_g, w_out, g_post, chunk):
    bsz, seq, width = q.shape
    blk = min(HG_BLOCK, seq)
    nlev = int(math.log2(chunk))
    assert HG_HEAD == 2 * chunk and width % (2 * HG_HEAD) == 0 and blk % chunk == 0
    tok = pl.BlockSpec((1, blk, width), lambda b_, n: (b_, n, 0))
    const = lambda shp: pl.BlockSpec(shp, lambda b_, n: (0, 0))
    row = lambda t: t.reshape(1, -1)
    return pl.pallas_call(
        functools.partial(_hg_core_kernel, chunk=chunk),
        out_shape=jax.ShapeDtypeStruct(h.shape, F32),
        grid=(bsz, seq // blk),
        in_specs=[tok, tok, tok, tok, pl.BlockSpec((1, blk // chunk, width), lambda b_, n: (b_, n, 0)),
                  tok, tok, const((1, width)), const(w_out.shape), const((1, width))],
        out_specs=tok,
        scratch_shapes=[pltpu.VMEM((nlev + 1, blk, width), BF16),
                        pltpu.VMEM((blk, width), BF16),
                        pltpu.VMEM((blk, width), BF16),
                        pltpu.VMEM((blk // chunk, width), F32),
                        pltpu.VMEM((width // (2 * HG_HEAD), 2 * HG_HEAD, 2 * HG_HEAD), F32),
                        pltpu.VMEM((blk, width), F32)],
        compiler_params=_cparams(("parallel", "arbitrary"), 10, (8,)),
        name="hgrn2_core",
    )(q, kk, b, v, dec, z, h, row(norm_g), w_out.astype(BF16), row(g_post))


def _hg_layer(h, g_pre, g_post, w_in, lb_logits, layer, norm_g, w_out):
    chunk = min(HG_CHUNK, h.shape[1])
    q, kk, b, v, z, dec = _hg_proj(h, g_pre, w_in.astype(BF16), lb_logits, layer, chunk)
    return _hg_core(q, kk, b, v, dec, z, h, norm_g, w_out, g_post, chunk)


def _at_proj_kernel(h_ref, g_ref, w_ref, bias_ref, pos_ref, invf_ref, q_ref, k_ref, v_ref, z_ref,
                    *, qw, kvw):
    y = _rms(h_ref[0], g_ref[...]).astype(BF16)
    half = AT_HEAD // 2
    nq4 = LANES // half
    ang = pos_ref[0].astype(F32) * invf_ref[...]
    cos4 = jnp.cos(ang)
    sin4 = jnp.sin(ang)
    lane4 = lax.broadcasted_iota(jnp.int32, ang.shape, 1)

    def spread(t):
        parts = []
        for j in range(nq4):
            m = jnp.where(lane4 // half == j, t, 0.0)
            x = m
            for s in range(1, nq4):
                x = x + pltpu.roll(m, s * half, 1)
            parts.append(x)
        return jnp.concatenate(parts, axis=0)

    lane = lax.broadcasted_iota(jnp.int32, (h_ref.shape[1], LANES), 1)
    first = (lane % AT_HEAD) < half
    low = lane < AT_HEAD
    cosf = spread(cos4)
    sinf = spread(sin4)
    sinf = jnp.where(first, -sinf, sinf)

    def rope(t):
        partner = jnp.where(first, pltpu.roll(t, LANES - AT_HEAD // 2, 1), pltpu.roll(t, AT_HEAD // 2, 1))
        return t * cosf + partner * sinf

    def store_dup(ref, j, t):
        r = pltpu.roll(t, AT_HEAD, 1)
        ref[0, :, (2 * j) * LANES:(2 * j + 1) * LANES] = jnp.where(low, t, r).astype(ref.dtype)
        ref[0, :, (2 * j + 1) * LANES:(2 * j + 2) * LANES] = jnp.where(low, r, t).astype(ref.dtype)

    qkv_w = qw + 2 * kvw
    z_ref[0] = _dot(y, w_ref[:, qkv_w:qkv_w + qw])
    qf = _dot(y, w_ref[:, 0:qw]) + bias_ref[:, 0:qw]
    kvf = _dot(y, w_ref[:, qw:qkv_w]) + bias_ref[:, qw:qkv_w]
    for j in range(qw // LANES):
        sl = slice(j * LANES, (j + 1) * LANES)
        q_ref[0, :, sl] = (rope(qf[:, sl]) * AT_HEAD ** -0.5).astype(q_ref.dtype)
    for j in range(kvw // LANES):
        store_dup(k_ref, j, rope(kvf[:, j * LANES:(j + 1) * LANES]))
        store_dup(v_ref, j, kvf[:, kvw + j * LANES:kvw + (j + 1) * LANES])


def _at_proj(h, g, w_bf16, bias, positions, qw, kvw):
    bsz, seq, d = h.shape
    tm = _row_tile(seq, 4 * d + 2 * qw + 4 * 2 * kvw + 4 * qw, w_bf16.size * 2, 4 * SUBLANES)
    half = AT_HEAD // 2
    inv_freq = ROPE_THETA ** (-jnp.arange(0, AT_HEAD, 2, dtype=F32) / AT_HEAD)
    nq4 = LANES // half
    invf = jnp.tile(inv_freq, nq4).reshape(1, LANES)
    pos4 = positions.reshape(bsz, seq // tm, nq4, tm // nq4).transpose(0, 1, 3, 2)
    pos4 = jnp.repeat(pos4, half, axis=-1).reshape(bsz, seq // nq4, LANES)
    tokspec = lambda w: pl.BlockSpec((1, tm, w), lambda b, m: (b, m, 0))
    const = lambda shp: pl.BlockSpec(shp, lambda b, m: (0, 0))
    return pl.pallas_call(
        functools.partial(_at_proj_kernel, qw=qw, kvw=kvw),
        out_shape=[jax.ShapeDtypeStruct((bsz, seq, qw), BF16), jax.ShapeDtypeStruct((bsz, seq, 2 * kvw), BF16),
                   jax.ShapeDtypeStruct((bsz, seq, 2 * kvw), BF16), jax.ShapeDtypeStruct((bsz, seq, qw), F32)],
        grid=(bsz, seq // tm),
        in_specs=[tokspec(d), const((1, d)), const(w_bf16.shape), const((1, qw + 2 * kvw)),
                  pl.BlockSpec((1, tm // nq4, LANES), lambda b, m: (b, m, 0)), const((1, LANES))],
        out_specs=[tokspec(qw), tokspec(2 * kvw), tokspec(2 * kvw), tokspec(qw)],
        compiler_params=_cparams(("parallel", "parallel"), 6, (2,)),
        name="attn_proj",
    )(h, g.reshape(1, d), w_bf16, bias.reshape(1, -1), pos4, invf)


def _at_core_kernel(q_ref, kc_ref, kp_ref, vc_ref, vp_ref, sink_ref, z_ref, h_ref, wo_ref, gp_ref, out_ref):
    blk = AT_BLOCK
    nsub = q_ref.shape[1] // blk
    n = pl.program_id(1)
    ngroup = kc_ref.shape[2] // LANES
    qi = lax.broadcasted_iota(jnp.int32, (2 * blk, 4 * blk), 0) % blk
    kj = lax.broadcasted_iota(jnp.int32, (2 * blk, 4 * blk), 1) % (2 * blk)
    dist = qi + blk - kj
    in_window = (dist >= 0) & (dist < blk)
    first_mask = in_window & ((kj >= blk) | (n > 0))
    low = lax.broadcasted_iota(jnp.int32, (2 * blk, LANES), 1) < AT_HEAD
    top = lax.broadcasted_iota(jnp.int32, (2 * blk, 1), 0) < blk
    ones_lo = jnp.where(low, 1.0, 0.0).astype(BF16)
    ones_hi = jnp.where(low, 0.0, 1.0).astype(BF16)
    items = [(j, g) for j in range(nsub) for g in range(ngroup)]

    def band(cur_ref, prev_ref, j, g):
        sl = slice(g * LANES, (g + 1) * LANES)
        prev = prev_ref[0, :, sl] if j == 0 else cur_ref[0, (j - 1) * blk:j * blk, sl]
        return jnp.concatenate([prev, cur_ref[0, j * blk:(j + 1) * blk, sl]], axis=0)

    o_rows = [[] for _ in range(nsub)]
    for w0 in range(0, len(items), AT_WAVE):
        wave = items[w0:w0 + AT_WAVE]
        scores = []
        for j, g in wave:
            kdup = band(kc_ref, kp_ref, j, g)
            rhs = jnp.concatenate([jnp.where(low, kdup, 0.0), jnp.where(low, 0.0, kdup)], axis=0)
            rows = slice(j * blk, (j + 1) * blk)
            lhs = jnp.concatenate([q_ref[0, rows, (2 * g) * LANES:(2 * g + 1) * LANES],
                                   q_ref[0, rows, (2 * g + 1) * LANES:(2 * g + 2) * LANES]], axis=0)
            scores.append(_dot_nt(lhs, rhs))

        probs, sink_terms = [], []
        for (j, g), sc in zip(wave, scores):
            s = jnp.where(first_mask if j == 0 else in_window, sc, -jnp.inf)
            sink_a = jnp.where(top, sink_ref[0:1, 4 * g:4 * g + 1], sink_ref[0:1, 4 * g + 2:4 * g + 3])
            sink_b = jnp.where(top, sink_ref[0:1, 4 * g + 1:4 * g + 2], sink_ref[0:1, 4 * g + 3:4 * g + 4])
            m_a = jnp.maximum(jnp.max(s[:, :2 * blk], axis=-1, keepdims=True), sink_a)
            m_b = jnp.maximum(jnp.max(s[:, 2 * blk:], axis=-1, keepdims=True), sink_b)
            e = jnp.concatenate([jnp.exp(s[:, :2 * blk] - m_a), jnp.exp(s[:, 2 * blk:] - m_b)], axis=1)
            probs.append(e.astype(BF16))
            sink_terms.append(jnp.where(low, jnp.exp(sink_a - m_a), jnp.exp(sink_b - m_b)))

        for (j, g), p, st in zip(wave, probs, sink_terms):
            vdup = band(vc_ref, vp_ref, j, g)
            va = jnp.where(low, vdup, 0.0)
            vb = jnp.where(low, 0.0, vdup)
            rhs = jnp.concatenate([jnp.concatenate([va, ones_lo], axis=1),
                                   jnp.concatenate([vb, ones_hi], axis=1)], axis=0)
            out = _dot(p, rhs)
            o = out[:, :LANES] / (out[:, LANES:] + st)
            o_rows[j] += [o[:blk], o[blk:]]

    o = jnp.concatenate([jnp.concatenate(slabs, axis=-1) for slabs in o_rows], axis=0)
    out_ref[0] = _gate_out_math(o, z_ref[0], h_ref[0], None, wo_ref[...], gp_ref[...], False)


def _at_core(q, k_dup, v_dup, sinks, z, h, w_out, g_post):
    bsz, seq, qw = q.shape
    kvw = k_dup.shape[-1]
    d = h.shape[-1]
    blk = AT_BLOCK
    nsub = AT_SUBBLOCKS if seq % (AT_SUBBLOCKS * blk) == 0 else 1
    nq = qw // AT_HEAD
    assert nq == 4 * (kvw // LANES) and qw == 2 * kvw
    cur = lambda w: pl.BlockSpec((1, nsub * blk, w), lambda b, n: (b, n, 0))
    prev = lambda w: pl.BlockSpec((1, blk, w), lambda b, n: (b, jnp.maximum(nsub * n - 1, 0), 0))
    const = lambda shp: pl.BlockSpec(shp, lambda b, n: (0, 0))
    return pl.pallas_call(
        _at_core_kernel,
        out_shape=jax.ShapeDtypeStruct(h.shape, F32),
        grid=(bsz, seq // (nsub * blk)),
        in_specs=[cur(qw), cur(kvw), prev(kvw), cur(kvw), prev(kvw), const((1, nq)),
                  cur(qw), cur(d), const(w_out.shape), const((1, d))],
        out_specs=cur(d),
        compiler_params=_cparams(("parallel", "arbitrary"), 10, (8,)),
        name="attn_core",
    )(q, k_dup, k_dup, v_dup, v_dup, sinks.reshape(1, nq), z, h, w_out.astype(BF16), g_post.reshape(1, d))


def _at_layer(h, positions, g_pre, g_post, w_in, b_in, sinks, w_out):
    qw = w_out.shape[0]
    kvw = (b_in.shape[0] - qw) // 2
    q, k, v, z = _at_proj(h, g_pre, w_in.astype(BF16), b_in, positions, qw, kvw)
    return _at_core(q, k, v, sinks, z, h, w_out, g_post)


def kernel(x, positions, norm_pre, norm_post, s5_w_in, s5_lambda_re, s5_lambda_im, s5_log_dt, s5_b_re, s5_b_im, s5_c_re, s5_c_im, s5_d, s5_w_glu, s5_b_glu, s5_w_out, hg_w_in, hg_lb_logits, hg_norm, hg_w_out, at_w_in, at_b_in, at_sinks, at_w_out):
    depth = norm_pre.shape[0]
    s5_weights = _s5_weights(s5_lambda_re, s5_lambda_im, s5_log_dt, s5_b_re, s5_b_im, s5_c_re, s5_c_im, s5_d)
    h = x
    for i in range(depth):
        kind, j = i % 3, i // 3
        if kind == 0:
            h = _s5_layer(h, norm_pre[i], norm_post[i], s5_w_in[j], s5_weights, j, s5_w_glu[j], s5_b_glu[j],
                          s5_w_out[j])
        elif kind == 1:
            h = _hg_layer(h, norm_pre[i], norm_post[i], hg_w_in[j], hg_lb_logits, i, hg_norm[j], hg_w_out[j])
        else:
            h = _at_layer(h, positions, norm_pre[i], norm_post[i], at_w_in[j], at_b_in[j], at_sinks[j],
                          at_w_out[j])
    return h
```
